```python
import math
import jax, jax.numpy as jnp
from jax import lax
import numpy as np

D_MODEL = 1024
BATCH = 8
SEQ = 2048
DEPTH = 4

N_MEM = 256
EPS = 1e-6
DA_HEADS = 8
DA_HEAD_DIM = D_MODEL // (2 * DA_HEADS)
DA_V_DIM = 2 * DA_HEAD_DIM
DA_QK_WIDTH = 2 * DA_HEADS * DA_HEAD_DIM
DA_V_WIDTH = DA_HEADS * DA_V_DIM
ROPE_THETA = 500000.0
ROT_DIM = DA_HEAD_DIM // 4
Q_BLOCK = 128
CONV_WIDTH = D_MODEL
CONV_K = 3
N_BRANCH = 2
SPLITS = tuple(np.cumsum([DA_QK_WIDTH, DA_QK_WIDTH, DA_V_WIDTH, CONV_WIDTH, CONV_WIDTH, CONV_WIDTH]).tolist())
IN_WIDTH = 2 * DA_QK_WIDTH + DA_V_WIDTH + 3 * CONV_WIDTH + N_BRANCH * D_MODEL
MEM_HEADS = 4
MEM_HEAD_DIM = D_MODEL // MEM_HEADS
N_GROUPS = 4
EXPERTS_PER_GROUP = 8
N_EXPERTS = N_GROUPS * EXPERTS_PER_GROUP
TOP_K = 2
D_EXPERT = 3 * D_MODEL // 4
MOE_BLOCK = 128

kernel_name = 'hybrid_gated_conv_diffattn_hmoe_encoder'


def rms_norm(x, g):
    xf = x.astype(jnp.float32)
    y = xf * lax.rsqrt(jnp.mean(xf * xf, axis=-1, keepdims=True) + EPS)
    return (y * g.astype(jnp.float32)).astype(x.dtype)


def rotary_tables(seq_len):
    inv = jnp.float32(ROPE_THETA) ** (-jnp.arange(0, ROT_DIM, 2, dtype=jnp.float32) / ROT_DIM)
    ang = jnp.arange(seq_len, dtype=jnp.float32)[:, None] * inv[None, :]
    return jnp.cos(ang), jnp.sin(ang)


def partial_rotary(x, cos, sin):
    half = ROT_DIM // 2
    xr = x[..., :ROT_DIM].astype(jnp.float32)
    x1, x2 = xr[..., :half], xr[..., half:]
    c, s = cos[:, None, :], sin[:, None, :]
    rot = jnp.concatenate([x1 * c - x2 * s, x2 * c + x1 * s], axis=-1).astype(x.dtype)
    return jnp.concatenate([rot, x[..., ROT_DIM:]], axis=-1)


def diff_attention(q, k, v, lam):
    _, b, s, h, dh = q.shape
    nb = s // Q_BLOCK
    qb = q.reshape(2, b, nb, Q_BLOCK, h, dh).transpose(2, 0, 1, 3, 4, 5)
    scale = dh ** -0.5

    def one_block(qblk):
        sc = jnp.einsum('cbqhd,cbkhd->cbhqk', qblk, k).astype(jnp.float32) * scale
        p = jax.nn.softmax(sc, axis=-1)
        w = p[0] - lam * p[1]
        return jnp.einsum('bhqk,bkhe->bqhe', w.astype(v.dtype), v)

    o = lax.map(one_block, qb)
    return o.transpose(1, 0, 2, 3, 4).reshape(b, s, h, 2 * dh)


def centred_depthwise_conv(u, w):
    c = u.shape[-1]
    return lax.conv_general_dilated(u, w[:, None, :].astype(u.dtype), window_strides=(1,),
                                    padding=[(CONV_K // 2, CONV_K // 2)],
                                    dimension_numbers=('NWC', 'WIO', 'NWC'), feature_group_count=c)


def hybrid_mixer(h, w_in, lq1, lk1, lq2, lk2, subln, conv_w, w_branch, w_o, lambda_init, cos, sin):
    b, s, _ = h.shape
    proj = h @ w_in
    q, k, v, c_b, c_c, c_x, gates = jnp.split(proj, SPLITS, axis=-1)
    q = jnp.moveaxis(q.reshape(b, s, DA_HEADS, 2, DA_HEAD_DIM), 3, 0)
    k = jnp.moveaxis(k.reshape(b, s, DA_HEADS, 2, DA_HEAD_DIM), 3, 0)
    q = partial_rotary(q, cos, sin)
    k = partial_rotary(k, cos, sin)
    v = v.reshape(b, s, DA_HEADS, DA_V_DIM)
    f32 = jnp.float32
    lam = (jnp.exp(jnp.sum(lq1.astype(f32) * lk1.astype(f32)))
           - jnp.exp(jnp.sum(lq2.astype(f32) * lk2.astype(f32))) + lambda_init)
    o = diff_attention(q, k, v, lam)
    o = rms_norm(o, subln) * (1.0 - lambda_init)
    y_attn = o.reshape(b, s, DA_V_WIDTH)
    y_conv = c_b * centred_depthwise_conv(c_c * c_x, conv_w)
    ys = jnp.stack([y_attn, y_conv], axis=2)
    br = jnp.einsum('bsnc,ncd->bsnd', ys, w_branch)
    g = jax.nn.sigmoid(gates.reshape(b, s, N_BRANCH, D_MODEL))
    return jnp.sum(g * br, axis=2) @ w_o


def memory_cross_attention(h, mem_n, w_cq, w_ckv, w_co):
    b, s, _ = h.shape
    m = mem_n.shape[1]
    q = (h @ w_cq).reshape(b, s, MEM_HEADS, MEM_HEAD_DIM)
    kv = (mem_n @ w_ckv).reshape(b, m, 2, MEM_HEADS, MEM_HEAD_DIM)
    k, v = kv[:, :, 0], kv[:, :, 1]
    sc = jnp.einsum('bshd,bmhd->bhsm', q, k).astype(jnp.float32) * (MEM_HEAD_DIM ** -0.5)
    p = jax.nn.softmax(sc, axis=-1)
    o = jnp.einsum('bhsm,bmhd->bshd', p.astype(v.dtype), v).reshape(b, s, MEM_HEADS * MEM_HEAD_DIM)
    return o @ w_co


def routed_experts(ht, expert_idx, expert_w, w_gate, w_up, w_down):
    t, d = ht.shape
    a = t * TOP_K
    flat_e = expert_idx.reshape(a).astype(jnp.int32)
    order = jnp.argsort(flat_e)
    sorted_e = flat_e[order]
    counts = jnp.bincount(flat_e, length=N_EXPERTS)
    padded = (counts + MOE_BLOCK - 1) // MOE_BLOCK * MOE_BLOCK
    pad_end = jnp.cumsum(padded)
    pad_start = pad_end - padded
    start = jnp.cumsum(counts) - counts
    dest = (pad_start[sorted_e] + jnp.arange(a, dtype=jnp.int32) - start[sorted_e]).astype(jnp.int32)
    n_blocks = -(-a // MOE_BLOCK) + N_EXPERTS
    rows = n_blocks * MOE_BLOCK
    row_tok = jnp.zeros((rows,), jnp.int32).at[dest].set((order // TOP_K).astype(jnp.int32))
    block_expert = jnp.minimum(
        jnp.searchsorted(pad_end, jnp.arange(n_blocks, dtype=jnp.int32) * MOE_BLOCK, side='right'),
        N_EXPERTS - 1)
    xb = ht[row_tok].reshape(n_blocks, MOE_BLOCK, d)

    def expert_block(args):
        xblk, e = args
        return (jax.nn.silu(xblk @ w_gate[e]) * (xblk @ w_up[e])) @ w_down[e]

    yb = lax.map(expert_block, (xb, block_expert)).reshape(rows, d)
    dest_flat = jnp.zeros((a,), jnp.int32).at[order].set(dest)
    y = yb[dest_flat].reshape(t, TOP_K, d)
    return jnp.einsum('tk,tkd->td', expert_w.astype(y.dtype), y)


def hierarchical_moe(h, w_rg, w_re, w_gate, w_up, w_down):
    b, s, d = h.shape
    ht = h.reshape(b * s, d)
    g_prob = jax.nn.softmax((ht @ w_rg).astype(jnp.float32), axis=-1)
    g_w, g_idx = lax.top_k(g_prob, 1)
    e_logits = (ht @ w_re).astype(jnp.float32).reshape(-1, N_GROUPS, EXPERTS_PER_GROUP)
    e_logits = jnp.take_along_axis(e_logits, g_idx[:, :, None], axis=1)[:, 0]
    e_prob = jax.nn.softmax(e_logits, axis=-1)
    e_w, e_idx = lax.top_k(e_prob, TOP_K)
    e_w = e_w / jnp.sum(e_w, axis=-1, keepdims=True)
    weights = g_w * e_w
    expert_idx = g_idx * EXPERTS_PER_GROUP + e_idx
    return routed_experts(ht, expert_idx, weights, w_gate, w_up, w_down).reshape(b, s, d)


def setup_inputs(seed: int = 0) -> dict:
    key = jax.random.key(seed)
    ks = jax.random.split(key, 24)
    f32 = jnp.float32
    res = (2.0 * DEPTH) ** -0.5

    def nrm(k, shape, scale):
        return jax.random.normal(k, shape, f32) * scale

    def gain(k, shape):
        return 1.0 + 0.02 * jax.random.normal(k, shape, f32)

    return {
        'x': nrm(ks[0], (BATCH, SEQ, D_MODEL), 1.0),
        'mem': nrm(ks[1], (BATCH, N_MEM, D_MODEL), 1.0),
        'norm_mix': gain(ks[2], (DEPTH, D_MODEL)),
        'w_in': nrm(ks[3], (DEPTH, D_MODEL, IN_WIDTH), D_MODEL ** -0.5),
        'lambda_q1': nrm(ks[4], (DEPTH, DA_HEAD_DIM), 0.1),
        'lambda_k1': nrm(ks[5], (DEPTH, DA_HEAD_DIM), 0.1),
        'lambda_q2': nrm(ks[6], (DEPTH, DA_HEAD_DIM), 0.1),
        'lambda_k2': nrm(ks[7], (DEPTH, DA_HEAD_DIM), 0.1),
        'subln': gain(ks[8], (DEPTH, DA_V_DIM)),
        'conv_w': nrm(ks[9], (DEPTH, CONV_K, CONV_WIDTH), CONV_K ** -0.5),
        'w_branch': nrm(ks[10], (DEPTH, N_BRANCH, D_MODEL, D_MODEL), D_MODEL ** -0.5),
        'w_o': nrm(ks[11], (DEPTH, D_MODEL, D_MODEL), res * D_MODEL ** -0.5),
        'norm_cross': gain(ks[12], (DEPTH, D_MODEL)),
        'norm_mem': gain(ks[13], (D_MODEL,)),
        'w_cq': nrm(ks[14], (DEPTH, D_MODEL, MEM_HEADS * MEM_HEAD_DIM), D_MODEL ** -0.5),
        'w_ckv': nrm(ks[15], (DEPTH, D_MODEL, 2 * MEM_HEADS * MEM_HEAD_DIM), D_MODEL ** -0.5),
        'w_co': nrm(ks[16], (DEPTH, MEM_HEADS * MEM_HEAD_DIM, D_MODEL), res * D_MODEL ** -0.5),
        'norm_ffn': gain(ks[17], (DEPTH, D_MODEL)),
        'w_router_group': nrm(ks[18], (DEPTH, D_MODEL, N_GROUPS), D_MODEL ** -0.5),
        'w_router_expert': nrm(ks[19], (DEPTH, D_MODEL, N_EXPERTS), D_MODEL ** -0.5),
        'w_exp_gate': nrm(ks[20], (DEPTH, N_EXPERTS, D_MODEL, D_EXPERT), D_MODEL ** -0.5),
        'w_exp_up': nrm(ks[21], (DEPTH, N_EXPERTS, D_MODEL, D_EXPERT), D_MODEL ** -0.5),
        'w_exp_down': nrm(ks[22], (DEPTH, N_EXPERTS, D_EXPERT, D_MODEL), res * D_EXPERT ** -0.5),
        'norm_final': gain(ks[23], (D_MODEL,)),
    }


def reference(x, mem, norm_mix, w_in, lambda_q1, lambda_k1, lambda_q2, lambda_k2, subln, conv_w,
              w_branch, w_o, norm_cross, norm_mem, w_cq, w_ckv, w_co, norm_ffn, w_router_group,
              w_router_expert, w_exp_gate, w_exp_up, w_exp_down, norm_final):
    cos, sin = rotary_tables(x.shape[1])
    mem_n = rms_norm(mem, norm_mem)
    for l in range(DEPTH):
        lambda_init = 0.8 - 0.6 * math.exp(-0.3 * l)
        x = x + hybrid_mixer(rms_norm(x, norm_mix[l]), w_in[l], lambda_q1[l], lambda_k1[l],
                             lambda_q2[l], lambda_k2[l], subln[l], conv_w[l], w_branch[l], w_o[l],
                             lambda_init, cos, sin)
        x = x + memory_cross_attention(rms_norm(x, norm_cross[l]), mem_n, w_cq[l], w_ckv[l], w_co[l])
        x = x + hierarchical_moe(rms_norm(x, norm_ffn[l]), w_router_group[l], w_router_expert[l],
                                 w_exp_gate[l], w_exp_up[l], w_exp_down[l])
    return rms_norm(x, norm_final)
```

```python
import functools
import math

import jax
import jax.numpy as jnp
from jax import lax
from jax.experimental import pallas as pl
from jax.experimental.pallas import tpu as pltpu

EPS = 1e-6
DA_HEADS = 8
MEM_HEADS = 4
N_GROUPS = 4
EXPERTS_PER_GROUP = 8
N_EXPERTS = N_GROUPS * EXPERTS_PER_GROUP
TOP_K = 2
ROPE_THETA = 500000.0
LANES = 128
BF16_SUBLANES = 16
EXPERT_LANE0 = N_GROUPS
ROW_TILE = 512
Q_TILE = 512
EXPERT_BLOCK = 256
PROJ_ROW_CHUNK = 256
VMEM_LIMIT = 56 * 1024 * 1024

F32 = jnp.float32
BF16 = jnp.bfloat16
I32 = jnp.int32
NT_DIMS = (((1,), (1,)), ((), ()))


def _rms(x, g):
    return x * lax.rsqrt(jnp.mean(x * x, axis=-1, keepdims=True) + EPS) * g


def _params(sem, vmem=VMEM_LIMIT):
    return pltpu.CompilerParams(dimension_semantics=sem, vmem_limit_bytes=vmem)


def _inproj_kernel(x_ref, g_ref, w_ref, cos_ref, s1_ref, s2_ref, o_ref, h_ref, *, rc, half, q_scale):
    j = pl.program_id(1)
    tm, tn = o_ref.shape

    @pl.when(j == 0)
    def _():
        h_ref[...] = _rms(x_ref[...], g_ref[...]).astype(BF16)

    def run(rot, scale):
        def body(r, carry):
            r0 = pl.multiple_of(r * rc, rc)
            acc = jnp.dot(h_ref[pl.ds(r0, rc), :], w_ref[...], preferred_element_type=F32)
            if not rot:
                o_ref[pl.ds(r0, rc), :] = acc.astype(o_ref.dtype)
                return carry
            c = cos_ref[pl.ds(r0, rc), :]
            s1 = s1_ref[pl.ds(r0, rc), :]
            s2 = s2_ref[pl.ds(r0, rc), :]
            for cc in range(tn // LANES):
                a = acc[:, cc * LANES:(cc + 1) * LANES]
                a = a * c + pltpu.roll(a, half, 1) * s1 + pltpu.roll(a, LANES - half, 1) * s2
                if scale != 1.0:
                    a = a * scale
                o_ref[pl.ds(r0, rc), cc * LANES:(cc + 1) * LANES] = a.astype(o_ref.dtype)
            return carry
        lax.fori_loop(0, tm // rc, body, 0)

    @pl.when(j == 0)
    def _():
        run(True, q_scale)

    @pl.when(j == 1)
    def _():
        run(True, 1.0)

    @pl.when(j >= 2)
    def _():
        run(False, 1.0)


def _inproj(x2, g, w_bf, tabs, seq, dh):
    t, d = x2.shape
    n = w_bf.shape[1]
    tm, tn = seq, d
    rc = min(PROJ_ROW_CHUNK, tm)
    cos_t, s1_t, s2_t = tabs
    tab_spec = pl.BlockSpec((seq, LANES), lambda i, j: (0, 0))
    return pl.pallas_call(
        functools.partial(_inproj_kernel, rc=rc, half=dh // 8, q_scale=dh ** -0.5),
        grid=(t // tm, n // tn),
        in_specs=[pl.BlockSpec((tm, d), lambda i, j: (i, 0)),
                  pl.BlockSpec((1, d), lambda i, j: (0, 0)),
                  pl.BlockSpec((d, tn), lambda i, j: (0, j)),
                  tab_spec, tab_spec, tab_spec],
        out_specs=pl.BlockSpec((tm, tn), lambda i, j: (i, j)),
        out_shape=jax.ShapeDtypeStruct((t, n), BF16),
        scratch_shapes=[pltpu.VMEM((tm, d), BF16)],
        compiler_params=_params(("parallel", "arbitrary")),
        name="inproj",
    )(x2, g, w_bf, cos_t, s1_t, s2_t)


def _rotary_tables(seq, dh):
    rot = dh // 4
    half = rot // 2
    inv = jnp.float32(ROPE_THETA) ** (-jnp.arange(0, rot, 2, dtype=F32) / rot)
    ang = jnp.arange(seq, dtype=F32)[:, None] * inv[None, :]
    cos, sin = jnp.cos(ang), jnp.sin(ang)
    lane = jnp.arange(LANES) % dh
    idx = lane % half
    c_t = jnp.where(lane < rot, cos[:, idx], 1.0)
    s1_t = jnp.where((lane >= half) & (lane < rot), sin[:, idx], 0.0)
    s2_t = jnp.where(lane < half, -sin[:, idx], 0.0)
    return c_t.astype(F32), s1_t.astype(F32), s2_t.astype(F32)


def _attn_kernel(q_ref, k_ref, v_ref, lq1_ref, lk1_ref, lq2_ref, lk2_ref, sub_ref, o_ref, *,
                 dh, lambda_init):
    tq = q_ref.shape[0]
    q = q_ref[...]
    lane = lax.broadcasted_iota(I32, (1, LANES), 1)
    m1 = jnp.where(lane < dh, 1.0, 0.0).astype(BF16)
    m2 = jnp.where(lane >= dh, 1.0, 0.0).astype(BF16)
    qq = jnp.concatenate([q * m1, q * m2], axis=0)
    s = lax.dot_general(qq, k_ref[...], NT_DIMS, preferred_element_type=F32)
    mx = jnp.max(s, axis=-1, keepdims=True)
    p = jnp.exp(s - mx)
    l = jnp.sum(p, axis=-1, keepdims=True)
    o = jnp.dot(p.astype(BF16), v_ref[...], preferred_element_type=F32) / l
    lam = (jnp.exp(jnp.sum(lq1_ref[...] * lk1_ref[...], axis=-1, keepdims=True))
           - jnp.exp(jnp.sum(lq2_ref[...] * lk2_ref[...], axis=-1, keepdims=True)) + lambda_init)
    od = o[:tq] - lam * o[tq:]
    od = _rms(od, sub_ref[...]) * (1.0 - lambda_init)
    o_ref[...] = od.astype(o_ref.dtype)


def _attention(proj, lq1, lk1, lq2, lk2, sub, batch, seq, d, lambda_init):
    t = proj.shape[0]
    nh = d // LANES
    dh = LANES // 2
    tq = min(Q_TILE, seq)
    nq = seq // tq
    vec = pl.BlockSpec((1, dh), lambda b, h, qi: (0, 0))
    return pl.pallas_call(
        functools.partial(_attn_kernel, dh=dh, lambda_init=lambda_init),
        grid=(batch, nh, nq),
        in_specs=[pl.BlockSpec((tq, LANES), lambda b, h, qi: (b * nq + qi, h)),
                  pl.BlockSpec((seq, LANES), lambda b, h, qi: (b, nh + h)),
                  pl.BlockSpec((seq, LANES), lambda b, h, qi: (b, 2 * nh + h)),
                  vec, vec, vec, vec,
                  pl.BlockSpec((1, LANES), lambda b, h, qi: (0, 0))],
        out_specs=pl.BlockSpec((tq, LANES), lambda b, h, qi: (b * nq + qi, h)),
        out_shape=jax.ShapeDtypeStruct((t, d), BF16),
        compiler_params=_params(("parallel", "parallel", "arbitrary")),
        name="diff_attn",
    )(proj, proj, proj, lq1, lk1, lq2, lk2, sub)


def _mixout_kernel(x_ref, ya_ref, cb_ref, cc_ref, cx_ref, g0_ref, g1_ref, pc_ref, px_ref, nc_ref, nx_ref,
                   cw_ref, wb0_ref, wb1_ref, wo_ref, o_ref, *, tiles_per_seq):
    i = pl.program_id(0)
    tm = x_ref.shape[0]
    u = cc_ref[...].astype(F32) * cx_ref[...].astype(F32)
    hr = pc_ref.shape[0]
    u_prev = pc_ref[hr - 1:hr, :].astype(F32) * px_ref[hr - 1:hr, :].astype(F32)
    u_next = nc_ref[0:1, :].astype(F32) * nx_ref[0:1, :].astype(F32)
    pos = i % tiles_per_seq
    u_prev = jnp.where(pos == 0, 0.0, u_prev)
    u_next = jnp.where(pos == tiles_per_seq - 1, 0.0, u_next)
    row = lax.broadcasted_iota(I32, u.shape, 0)
    u_up = jnp.where(row == 0, u_prev, pltpu.roll(u, 1, 0))
    u_dn = jnp.where(row == tm - 1, u_next, pltpu.roll(u, tm - 1, 0))
    conv = cw_ref[0:1, :] * u_up + cw_ref[1:2, :] * u + cw_ref[2:3, :] * u_dn
    y_conv = (cb_ref[...].astype(F32) * conv).astype(BF16)
    br0 = jnp.dot(ya_ref[...], wb0_ref[...], preferred_element_type=F32)
    br1 = jnp.dot(y_conv, wb1_ref[...], preferred_element_type=F32)
    z = jax.nn.sigmoid(g0_ref[...].astype(F32)) * br0 + jax.nn.sigmoid(g1_ref[...].astype(F32)) * br1
    o_ref[...] = x_ref[...] + jnp.dot(z.astype(BF16), wo_ref[...], preferred_element_type=F32)


def _mixer_out(x2, y_attn, proj, conv_w, wb0, wb1, wo, seq):
    t, d = x2.shape
    tm = min(ROW_TILE, seq)
    tps = seq // tm
    hr = BF16_SUBLANES
    nhb = t // hr
    row = lambda c: pl.BlockSpec((tm, d), lambda i: (i, c))
    prev = lambda c: pl.BlockSpec((hr, d), lambda i: (jnp.maximum(i * (tm // hr) - 1, 0), c))
    nxt = lambda c: pl.BlockSpec((hr, d), lambda i: (jnp.minimum((i + 1) * (tm // hr), nhb - 1), c))
    full = lambda r: pl.BlockSpec((r, d), lambda i: (0, 0))
    return pl.pallas_call(
        functools.partial(_mixout_kernel, tiles_per_seq=tps),
        grid=(t // tm,),
        in_specs=[row(0), row(0), row(3), row(4), row(5), row(6), row(7),
                  prev(4), prev(5), nxt(4), nxt(5),
                  full(conv_w.shape[0]), full(d), full(d), full(d)],
        out_specs=row(0),
        out_shape=jax.ShapeDtypeStruct((t, d), F32),
        compiler_params=_params(("parallel",)),
        name="mixer_out",
    )(x2, y_attn, proj, proj, proj, proj, proj, proj, proj, proj, proj, conv_w, wb0, wb1, wo)


def _memkv_kernel(m_ref, g_ref, w_ref, o_ref):
    h = _rms(m_ref[...], g_ref[...]).astype(BF16)
    o_ref[...] = jnp.dot(h, w_ref[...], preferred_element_type=F32).astype(o_ref.dtype)


def _mem_kv(mem2, g, w_bf):
    depth, d, n = w_bf.shape
    rows = mem2.shape[0]
    tn = min(n, 1024)
    return pl.pallas_call(
        _memkv_kernel,
        grid=(depth, n // tn),
        in_specs=[pl.BlockSpec((rows, d), lambda l, j: (0, 0)),
                  pl.BlockSpec((1, d), lambda l, j: (0, 0)),
                  pl.BlockSpec((None, d, tn), lambda l, j: (l, 0, j))],
        out_specs=pl.BlockSpec((None, rows, tn), lambda l, j: (l, 0, j)),
        out_shape=jax.ShapeDtypeStruct((depth, rows, n), BF16),
        compiler_params=_params(("parallel", "parallel")),
        name="mem_kv",
    )(mem2, g, w_bf)


def _cross_kernel(x_ref, g_ref, wq_ref, kv_ref, wo_ref, o_ref, *, heads):
    x = x_ref[...]
    d = x.shape[1]
    hd = d // heads
    h = _rms(x, g_ref[...]).astype(BF16)
    q = (jnp.dot(h, wq_ref[...], preferred_element_type=F32) * (hd ** -0.5)).astype(BF16)
    outs = []
    for hh in range(heads):
        qh = q[:, hh * hd:(hh + 1) * hd]
        kh = kv_ref[:, hh * hd:(hh + 1) * hd]
        vh = kv_ref[:, d + hh * hd:d + (hh + 1) * hd]
        s = lax.dot_general(qh, kh, NT_DIMS, preferred_element_type=F32)
        p = jnp.exp(s - jnp.max(s, axis=-1, keepdims=True))
        l = jnp.sum(p, axis=-1, keepdims=True)
        outs.append((jnp.dot(p.astype(BF16), vh, preferred_element_type=F32) / l).astype(BF16))
    o = jnp.concatenate(outs, axis=1)
    o_ref[...] = x + jnp.dot(o, wo_ref[...], preferred_element_type=F32)


def _cross_attention(x2, g, wq, kv, layer, wo, seq, n_mem):
    t, d = x2.shape
    tm = min(ROW_TILE, seq)
    tps = seq // tm
    return pl.pallas_call(
        functools.partial(_cross_kernel, heads=MEM_HEADS),
        grid=(t // tm,),
        in_specs=[pl.BlockSpec((tm, d), lambda i: (i, 0)),
                  pl.BlockSpec((1, d), lambda i: (0, 0)),
                  pl.BlockSpec((d, d), lambda i: (0, 0)),
                  pl.BlockSpec((None, n_mem, 2 * d), lambda i: (layer, i // tps, 0)),
                  pl.BlockSpec((d, d), lambda i: (0, 0))],
        out_specs=pl.BlockSpec((tm, d), lambda i: (i, 0)),
        out_shape=jax.ShapeDtypeStruct((t, d), F32),
        compiler_params=_params(("parallel",)),
        name="cross_attn",
    )(x2, g, wq, kv, wo)


def _router_kernel(x_ref, g_ref, wr_ref, ri_ref, rw_ref, cnt_ref, carry_ref):
    i = pl.program_id(0)
    tm = x_ref.shape[0]

    @pl.when(i == 0)
    def _():
        carry_ref[...] = jnp.zeros_like(carry_ref)

    h = _rms(x_ref[...], g_ref[...])
    lg = jnp.dot(h, wr_ref[...], preferred_element_type=F32, precision=lax.Precision.HIGHEST)
    lane = lax.broadcasted_iota(I32, lg.shape, 1).astype(F32)
    neg = -jnp.inf

    def first_argmax(v):
        m = jnp.max(v, axis=-1, keepdims=True)
        return m, jnp.min(jnp.where(v == m, lane, float(LANES)), axis=-1, keepdims=True)

    gmask = lane < N_GROUPS
    gl = jnp.where(gmask, lg, neg)
    gmax, gidx = first_argmax(gl)
    g_w = 1.0 / jnp.sum(jnp.where(gmask, jnp.exp(gl - gmax), 0.0), axis=-1, keepdims=True)

    lo = EXPERT_LANE0 + gidx * EXPERTS_PER_GROUP
    emask = (lane >= lo) & (lane < lo + EXPERTS_PER_GROUP)
    el = jnp.where(emask, lg, neg)
    mx1, i1 = first_argmax(el)
    mx2, i2 = first_argmax(jnp.where(lane == i1, neg, el))
    esum = jnp.sum(jnp.where(emask, jnp.exp(el - mx1), 0.0), axis=-1, keepdims=True)
    p1 = 1.0 / esum
    p2 = jnp.exp(mx2 - mx1) / esum
    den = p1 + p2
    w1 = g_w * (p1 / den)
    w2 = g_w * (p2 / den)

    memb = jnp.where((lane == i1) | (lane == i2), 1.0, 0.0)
    r_i = lax.broadcasted_iota(I32, (tm, tm), 0)
    c_i = lax.broadcasted_iota(I32, (tm, tm), 1)
    ltri = jnp.where(c_i < r_i, 1.0, 0.0).astype(BF16)
    tot = jnp.dot(ltri, memb.astype(BF16), preferred_element_type=F32) + carry_ref[...]
    rank1 = jnp.sum(jnp.where(lane == i1, tot, 0.0), axis=-1, keepdims=True).astype(I32)
    rank2 = jnp.sum(jnp.where(lane == i2, tot, 0.0), axis=-1, keepdims=True).astype(I32)
    carry_ref[...] += jnp.sum(memb, axis=0, keepdims=True)

    e1 = (i1 - EXPERT_LANE0).astype(I32)
    e2 = (i2 - EXPERT_LANE0).astype(I32)
    ri_ref[...] = jnp.where(lane == 0, e1,
                            jnp.where(lane == 1, e2, jnp.where(lane == 2, rank1, jnp.where(lane == 3, rank2, 0))))
    rw_ref[...] = jnp.where(lane == 0, w1, jnp.where(lane == 1, w2, 0.0))
    cnt_ref[...] = jnp.broadcast_to(carry_ref[...], cnt_ref.shape)


def _router(x2, g, w_r, seq):
    t, d = x2.shape
    tm = min(ROW_TILE, seq)
    return pl.pallas_call(
        _router_kernel,
        grid=(t // tm,),
        in_specs=[pl.BlockSpec((tm, d), lambda i: (i, 0)),
                  pl.BlockSpec((1, d), lambda i: (0, 0)),
                  pl.BlockSpec((d, LANES), lambda i: (0, 0))],
        out_specs=[pl.BlockSpec((tm, LANES), lambda i: (i, 0)),
                   pl.BlockSpec((tm, LANES), lambda i: (i, 0)),
                   pl.BlockSpec((8, LANES), lambda i: (0, 0))],
        out_shape=[jax.ShapeDtypeStruct((t, LANES), I32),
                   jax.ShapeDtypeStruct((t, LANES), F32),
                   jax.ShapeDtypeStruct((8, LANES), F32)],
        scratch_shapes=[pltpu.VMEM((1, LANES), F32)],
        compiler_params=_params(("arbitrary",)),
        name="moe_router",
    )(x2, g, w_r)


def _dispatch_kernel(dest_ref, x_ref, g_ref, xb_in_ref, xb_ref, h_ref, sem):
    del xb_in_ref
    i = pl.program_id(0)
    tm = x_ref.shape[0]
    h_ref[...] = _rms(x_ref[...], g_ref[...])
    base = i * (TOP_K * tm)

    def body(t, carry):
        for k in range(TOP_K):
            dst = dest_ref[base + TOP_K * t + k]
            pltpu.make_async_copy(h_ref.at[pl.ds(t, 1), :], xb_ref.at[pl.ds(dst, 1), :], sem).start()
        return carry
    lax.fori_loop(0, tm, body, 0, unroll=8)
    for _ in range(TOP_K):
        pltpu.make_async_copy(h_ref, xb_ref.at[pl.ds(0, tm), :], sem).wait()


def _dispatch(dest, x2, g, xb, seq):
    t, d = x2.shape
    tm = min(ROW_TILE, seq)
    return pl.pallas_call(
        _dispatch_kernel,
        grid_spec=pltpu.PrefetchScalarGridSpec(
            num_scalar_prefetch=1,
            grid=(t // tm,),
            in_specs=[pl.BlockSpec((tm, d), lambda i, dest: (i, 0)),
                      pl.BlockSpec((1, d), lambda i, dest: (0, 0)),
                      pl.BlockSpec(memory_space=pl.ANY)],
            out_specs=pl.BlockSpec(memory_space=pl.ANY),
            scratch_shapes=[pltpu.VMEM((tm, d), F32), pltpu.SemaphoreType.DMA]),
        out_shape=jax.ShapeDtypeStruct(xb.shape, xb.dtype),
        input_output_aliases={3: 0},
        compiler_params=_params(("arbitrary",)),
        name="moe_dispatch",
    )(dest, x2, g, xb)


def _expert_kernel(be_ref, na_ref, xb_ref, wg_ref, wu_ref, wd_ref, yb_ref, wgb_ref, wub_ref, wdb_ref):
    i = pl.program_id(0)
    prev = be_ref[jnp.maximum(i - 1, 0)]

    @pl.when(i < na_ref[0])
    def _():
        @pl.when((i == 0) | (be_ref[i] != prev))
        def _():
            wgb_ref[...] = wg_ref[...].astype(BF16)
            wub_ref[...] = wu_ref[...].astype(BF16)
            wdb_ref[...] = wd_ref[...].astype(BF16)

        xb = xb_ref[...].astype(BF16)
        gate = jnp.dot(xb, wgb_ref[...], preferred_element_type=F32)
        up = jnp.dot(xb, wub_ref[...], preferred_element_type=F32)
        act = (jax.nn.silu(gate) * up).astype(BF16)
        yb_ref[...] = jnp.dot(act, wdb_ref[...], preferred_element_type=F32)

    @pl.when(i >= na_ref[0])
    def _():
        yb_ref[...] = jnp.zeros_like(yb_ref)


def _experts(block_expert, n_active, xb, w_gate, w_up, w_down, layer, blk):
    rows, d = xb.shape
    de = w_gate.shape[-1]
    nb = rows // blk

    def row_map(i, be, na):
        return (jnp.minimum(i, na[0] - 1), 0)

    def w_map(i, be, na):
        return (layer, be[jnp.minimum(i, na[0] - 1)], 0, 0)

    return pl.pallas_call(
        _expert_kernel,
        grid_spec=pltpu.PrefetchScalarGridSpec(
            num_scalar_prefetch=2,
            grid=(nb,),
            in_specs=[pl.BlockSpec((blk, d), row_map),
                      pl.BlockSpec((None, None, d, de), w_map),
                      pl.BlockSpec((None, None, d, de), w_map),
                      pl.BlockSpec((None, None, de, d), w_map)],
            out_specs=pl.BlockSpec((blk, d), lambda i, be, na: (i, 0)),
            scratch_shapes=[pltpu.VMEM((d, de), BF16), pltpu.VMEM((d, de), BF16), pltpu.VMEM((de, d), BF16)]),
        out_shape=jax.ShapeDtypeStruct((rows, d), F32),
        compiler_params=_params(("arbitrary",)),
        name="moe_experts",
    )(block_expert, n_active, xb, w_gate, w_up, w_down)


def _combine_kernel(dest_ref, x_ref, rw_ref, yb_ref, gf_ref, o_ref, y_ref, sem, *, final_norm):
    i = pl.program_id(0)
    tm = x_ref.shape[0]
    base = i * (TOP_K * tm)

    def body(t, carry):
        for k in range(TOP_K):
            src = dest_ref[base + TOP_K * t + k]
            pltpu.make_async_copy(yb_ref.at[pl.ds(src, 1), :], y_ref.at[k, pl.ds(t, 1), :], sem).start()
        return carry
    lax.fori_loop(0, tm, body, 0, unroll=8)
    for k in range(TOP_K):
        pltpu.make_async_copy(yb_ref.at[pl.ds(0, tm), :], y_ref.at[k], sem).wait()
    w = rw_ref[...]
    out = x_ref[...]
    for k in range(TOP_K):
        out = out + w[:, k:k + 1] * y_ref[k]
    if final_norm:
        out = _rms(out, gf_ref[...])
    o_ref[...] = out


def _combine(dest, x2, rw, yb, g_final, seq, final_norm):
    t, d = x2.shape
    tm = min(ROW_TILE, seq)
    return pl.pallas_call(
        functools.partial(_combine_kernel, final_norm=final_norm),
        grid_spec=pltpu.PrefetchScalarGridSpec(
            num_scalar_prefetch=1,
            grid=(t // tm,),
            in_specs=[pl.BlockSpec((tm, d), lambda i, dest: (i, 0)),
                      pl.BlockSpec((tm, LANES), lambda i, dest: (i, 0)),
                      pl.BlockSpec(memory_space=pl.ANY),
                      pl.BlockSpec((1, d), lambda i, dest: (0, 0))],
            out_specs=pl.BlockSpec((tm, d), lambda i, dest: (i, 0)),
            scratch_shapes=[pltpu.VMEM((TOP_K, tm, d), F32), pltpu.SemaphoreType.DMA]),
        out_shape=jax.ShapeDtypeStruct((t, d), F32),
        compiler_params=_params(("arbitrary",)),
        name="moe_combine",
    )(dest, x2, rw, yb, g_final)


def _route_plan(ri, counts_f, blk, n_blocks):
    cnt = counts_f[0, EXPERT_LANE0:EXPERT_LANE0 + N_EXPERTS].astype(I32)
    nblk = (cnt + blk - 1) // blk
    cum = jnp.cumsum(nblk)
    pad_start = (cum - nblk) * blk
    e = ri[:, 0:TOP_K]
    rank = ri[:, TOP_K:2 * TOP_K]
    onehot = e[:, :, None] == jnp.arange(N_EXPERTS, dtype=I32)[None, None, :]
    dest = rank + jnp.sum(jnp.where(onehot, pad_start[None, None, :], 0), axis=-1)
    block_expert = jnp.minimum(
        jnp.searchsorted(cum, jnp.arange(n_blocks, dtype=I32), side='right'), N_EXPERTS - 1).astype(I32)
    n_active = cum[-1:].astype(I32)
    return dest.reshape(-1).astype(I32), block_expert, n_active


def kernel(x, mem, norm_mix, w_in, lambda_q1, lambda_k1, lambda_q2, lambda_k2, subln, conv_w, w_branch, w_o,
           norm_cross, norm_mem, w_cq, w_ckv, w_co, norm_ffn, w_router_group, w_router_expert, w_exp_gate,
           w_exp_up, w_exp_down, norm_final):
    batch, seq, d = x.shape
    depth = w_in.shape[0]
    n_mem = mem.shape[1]
    t = batch * seq
    dh = d // (2 * DA_HEADS)
    assert 2 * dh == LANES and d % LANES == 0 and w_in.shape[2] == 8 * d
    assert w_router_group.shape[2] == N_GROUPS and w_router_expert.shape[2] == N_EXPERTS
    blk = min(EXPERT_BLOCK, seq)
    n_blocks = (t * TOP_K) // blk + N_EXPERTS

    tabs = _rotary_tables(seq, dh)
    x2 = x.reshape(t, d)
    kv = _mem_kv(mem.reshape(batch * n_mem, d), norm_mem.reshape(1, d), w_ckv.astype(BF16))
    w_r = jnp.concatenate([w_router_group, w_router_expert,
                           jnp.zeros((depth, d, LANES - N_GROUPS - N_EXPERTS), F32)], axis=-1)
    xb = jnp.zeros((n_blocks * blk, d), F32)

    for l in range(depth):
        lambda_init = 0.8 - 0.6 * math.exp(-0.3 * l)
        proj = _inproj(x2, norm_mix[l].reshape(1, d), w_in[l].astype(BF16), tabs, seq, dh)
        y_attn = _attention(proj, lambda_q1[l].reshape(1, dh), lambda_k1[l].reshape(1, dh),
                            lambda_q2[l].reshape(1, dh), lambda_k2[l].reshape(1, dh),
                            subln[l].reshape(1, LANES), batch, seq, d, lambda_init)
        x2 = _mixer_out(x2, y_attn, proj, conv_w[l], w_branch[l, 0].astype(BF16), w_branch[l, 1].astype(BF16),
                        w_o[l].astype(BF16), seq)
        x2 = _cross_attention(x2, norm_cross[l].reshape(1, d), w_cq[l].astype(BF16), kv, l,
                              w_co[l].astype(BF16), seq, n_mem)
        g_ffn = norm_ffn[l].reshape(1, d)
        ri, rw, counts = _router(x2, g_ffn, w_r[l], seq)
        dest, block_expert, n_active = _route_plan(ri, counts, blk, n_blocks)
        xb = _dispatch(dest, x2, g_ffn, xb, seq)
        yb = _experts(block_expert, n_active, xb, w_exp_gate, w_exp_up, w_exp_down, l, blk)
        x2 = _combine(dest, x2, rw, yb, norm_final.reshape(1, d), seq, l == depth - 1)
    return x2.reshape(batch, seq, d)
```

```python
import functools
import math

import jax
import jax.numpy as jnp
from jax import lax
from jax.experimental import pallas as pl
from jax.experimental.pallas import tpu as pltpu

EPS = 1e-6
DA_HEADS = 8
MEM_HEADS = 4
N_GROUPS = 4
EXPERTS_PER_GROUP = 8
N_EXPERTS = N_GROUPS * EXPERTS_PER_GROUP
TOP_K = 2
ROPE_THETA = 500000.0
LANES = 128
BF16_SUBLANES = 16
EXPERT_LANE0 = N_GROUPS
ROW_TILE = 512
Q_TILE = 512
EXPERT_BLOCK = 256
PROJ_ROW_CHUNK = 256
ATTN_ROW_CHUNK = 256
VMEM_LIMIT = 56 * 1024 * 1024

F32 = jnp.float32
BF16 = jnp.bfloat16
I32 = jnp.int32
NT_DIMS = (((1,), (1,)), ((), ()))


def _rms(x, g):
    return x * lax.rsqrt(jnp.mean(x * x, axis=-1, keepdims=True) + EPS) * g


def _params(sem, vmem=VMEM_LIMIT):
    return pltpu.CompilerParams(dimension_semantics=sem, vmem_limit_bytes=vmem)


def _inproj_kernel(x_ref, g_ref, w_ref, cos_ref, s1_ref, s2_ref, o_ref, h_ref, *, rc, half, q_scale):
    j = pl.program_id(1)
    tm, tn = o_ref.shape

    @pl.when(j == 0)
    def _():
        h_ref[...] = _rms(x_ref[...], g_ref[...]).astype(BF16)

    def run(rot, scale):
        def body(r, carry):
            r0 = pl.multiple_of(r * rc, rc)
            acc = jnp.dot(h_ref[pl.ds(r0, rc), :], w_ref[...], preferred_element_type=F32)
            if not rot:
                o_ref[pl.ds(r0, rc), :] = acc.astype(o_ref.dtype)
                return carry
            c = cos_ref[pl.ds(r0, rc), :]
            s1 = s1_ref[pl.ds(r0, rc), :]
            s2 = s2_ref[pl.ds(r0, rc), :]
            for cc in range(tn // LANES):
                a = acc[:, cc * LANES:(cc + 1) * LANES]
                a = a * c + pltpu.roll(a, half, 1) * s1 + pltpu.roll(a, LANES - half, 1) * s2
                if scale != 1.0:
                    a = a * scale
                o_ref[pl.ds(r0, rc), cc * LANES:(cc + 1) * LANES] = a.astype(o_ref.dtype)
            return carry
        lax.fori_loop(0, tm // rc, body, 0)

    @pl.when(j == 0)
    def _():
        run(True, q_scale)

    @pl.when(j == 1)
    def _():
        run(True, 1.0)

    @pl.when(j >= 2)
    def _():
        run(False, 1.0)


def _inproj(x2, g, w_bf, tabs, seq, dh):
    t, d = x2.shape
    n = w_bf.shape[1]
    tm, tn = seq, d
    rc = min(PROJ_ROW_CHUNK, tm)
    cos_t, s1_t, s2_t = tabs
    tab_spec = pl.BlockSpec((seq, LANES), lambda i, j: (0, 0))
    return pl.pallas_call(
        functools.partial(_inproj_kernel, rc=rc, half=dh // 8, q_scale=dh ** -0.5),
        grid=(t // tm, n // tn),
        in_specs=[pl.BlockSpec((tm, d), lambda i, j: (i, 0)),
                  pl.BlockSpec((1, d), lambda i, j: (0, 0)),
                  pl.BlockSpec((d, tn), lambda i, j: (0, j)),
                  tab_spec, tab_spec, tab_spec],
        out_specs=pl.BlockSpec((tm, tn), lambda i, j: (i, j)),
        out_shape=jax.ShapeDtypeStruct((t, n), BF16),
        scratch_shapes=[pltpu.VMEM((tm, d), BF16)],
        compiler_params=_params(("parallel", "arbitrary")),
        name="inproj",
    )(x2, g, w_bf, cos_t, s1_t, s2_t)


def _rotary_tables(seq, dh):
    rot = dh // 4
    half = rot // 2
    inv = jnp.float32(ROPE_THETA) ** (-jnp.arange(0, rot, 2, dtype=F32) / rot)
    ang = jnp.arange(seq, dtype=F32)[:, None] * inv[None, :]
    cos, sin = jnp.cos(ang), jnp.sin(ang)
    lane = jnp.arange(LANES) % dh
    idx = lane % half
    c_t = jnp.where(lane < rot, cos[:, idx], 1.0)
    s1_t = jnp.where((lane >= half) & (lane < rot), sin[:, idx], 0.0)
    s2_t = jnp.where(lane < half, -sin[:, idx], 0.0)
    return c_t.astype(F32), s1_t.astype(F32), s2_t.astype(F32)


def _attn_kernel(q_ref, k_ref, v_ref, lq1_ref, lk1_ref, lq2_ref, lk2_ref, sub_ref, o_ref,
                 s_ref, p_ref, oacc_ref, *, dh, lambda_init, rc):
    tq = q_ref.shape[0]
    cpm = tq // rc
    n_chunks = 2 * cpm
    lane = lax.broadcasted_iota(I32, (1, LANES), 1)
    masks = (jnp.where(lane < dh, 1.0, 0.0).astype(BF16), jnp.where(lane >= dh, 1.0, 0.0).astype(BF16))

    def scores(c):
        r0 = (c % cpm) * rc
        qc = q_ref[r0:r0 + rc, :] * masks[c // cpm]
        s_ref[c % 2] = lax.dot_general(qc, k_ref[...], NT_DIMS, preferred_element_type=F32)

    def softmax_pv(c):
        s = s_ref[c % 2]
        p = jnp.exp(s - jnp.max(s, axis=-1, keepdims=True))
        l = jnp.sum(p, axis=-1, keepdims=True)
        p_ref[c % 2] = p.astype(BF16)
        oacc_ref[c * rc:(c + 1) * rc, :] = jnp.dot(p_ref[c % 2], v_ref[...], preferred_element_type=F32) / l

    scores(0)
    for c in range(n_chunks):
        if c + 1 < n_chunks:
            scores(c + 1)
        softmax_pv(c)
    lam = (jnp.exp(jnp.sum(lq1_ref[...] * lk1_ref[...], axis=-1, keepdims=True))
           - jnp.exp(jnp.sum(lq2_ref[...] * lk2_ref[...], axis=-1, keepdims=True)) + lambda_init)
    od = oacc_ref[0:tq, :] - lam * oacc_ref[tq:2 * tq, :]
    od = _rms(od, sub_ref[...]) * (1.0 - lambda_init)
    o_ref[...] = od.astype(o_ref.dtype)


def _attention(proj, lq1, lk1, lq2, lk2, sub, batch, seq, d, lambda_init):
    t = proj.shape[0]
    nh = d // LANES
    dh = LANES // 2
    tq = min(Q_TILE, seq)
    nq = seq // tq
    rc = min(ATTN_ROW_CHUNK, tq)
    vec = pl.BlockSpec((1, dh), lambda b, h, qi: (0, 0))
    return pl.pallas_call(
        functools.partial(_attn_kernel, dh=dh, lambda_init=lambda_init, rc=rc),
        grid=(batch, nh, nq),
        scratch_shapes=[pltpu.VMEM((2, rc, seq), F32), pltpu.VMEM((2, rc, seq), BF16),
                        pltpu.VMEM((2 * tq, LANES), F32)],
        in_specs=[pl.BlockSpec((tq, LANES), lambda b, h, qi: (b * nq + qi, h)),
                  pl.BlockSpec((seq, LANES), lambda b, h, qi: (b, nh + h)),
                  pl.BlockSpec((seq, LANES), lambda b, h, qi: (b, 2 * nh + h)),
                  vec, vec, vec, vec,
                  pl.BlockSpec((1, LANES), lambda b, h, qi: (0, 0))],
        out_specs=pl.BlockSpec((tq, LANES), lambda b, h, qi: (b * nq + qi, h)),
        out_shape=jax.ShapeDtypeStruct((t, d), BF16),
        compiler_params=_params(("parallel", "parallel", "arbitrary")),
        name="diff_attn",
    )(proj, proj, proj, lq1, lk1, lq2, lk2, sub)


def _mixout_kernel(x_ref, ya_ref, cb_ref, cc_ref, cx_ref, g0_ref, g1_ref, pc_ref, px_ref, nc_ref, nx_ref,
                   cw_ref, wb0_ref, wb1_ref, wo_ref, o_ref, *, tiles_per_seq):
    i = pl.program_id(0)
    tm = x_ref.shape[0]
    u = cc_ref[...].astype(F32) * cx_ref[...].astype(F32)
    hr = pc_ref.shape[0]
    u_prev = pc_ref[hr - 1:hr, :].astype(F32) * px_ref[hr - 1:hr, :].astype(F32)
    u_next = nc_ref[0:1, :].astype(F32) * nx_ref[0:1, :].astype(F32)
    pos = i % tiles_per_seq
    u_prev = jnp.where(pos == 0, 0.0, u_prev)
    u_next = jnp.where(pos == tiles_per_seq - 1, 0.0, u_next)
    row = lax.broadcasted_iota(I32, u.shape, 0)
    u_up = jnp.where(row == 0, u_prev, pltpu.roll(u, 1, 0))
    u_dn = jnp.where(row == tm - 1, u_next, pltpu.roll(u, tm - 1, 0))
    conv = cw_ref[0:1, :] * u_up + cw_ref[1:2, :] * u + cw_ref[2:3, :] * u_dn
    y_conv = (cb_ref[...].astype(F32) * conv).astype(BF16)
    br0 = jnp.dot(ya_ref[...], wb0_ref[...], preferred_element_type=F32)
    br1 = jnp.dot(y_conv, wb1_ref[...], preferred_element_type=F32)
    z = jax.nn.sigmoid(g0_ref[...].astype(F32)) * br0 + jax.nn.sigmoid(g1_ref[...].astype(F32)) * br1
    o_ref[...] = x_ref[...] + jnp.dot(z.astype(BF16), wo_ref[...], preferred_element_type=F32)


def _mixer_out(x2, y_attn, proj, conv_w, wb0, wb1, wo, seq):
    t, d = x2.shape
    tm = min(ROW_TILE, seq)
    tps = seq // tm
    hr = BF16_SUBLANES
    nhb = t // hr
    row = lambda c: pl.BlockSpec((tm, d), lambda i: (i, c))
    prev = lambda c: pl.BlockSpec((hr, d), lambda i: (jnp.maximum(i * (tm // hr) - 1, 0), c))
    nxt = lambda c: pl.BlockSpec((hr, d), lambda i: (jnp.minimum((i + 1) * (tm // hr), nhb - 1), c))
    full = lambda r: pl.BlockSpec((r, d), lambda i: (0, 0))
    return pl.pallas_call(
        functools.partial(_mixout_kernel, tiles_per_seq=tps),
        grid=(t // tm,),
        in_specs=[row(0), row(0), row(3), row(4), row(5), row(6), row(7),
                  prev(4), prev(5), nxt(4), nxt(5),
                  full(conv_w.shape[0]), full(d), full(d), full(d)],
        out_specs=row(0),
        out_shape=jax.ShapeDtypeStruct((t, d), F32),
        compiler_params=_params(("parallel",)),
        name="mixer_out",
    )(x2, y_attn, proj, proj, proj, proj, proj, proj, proj, proj, proj, conv_w, wb0, wb1, wo)


def _memkv_kernel(m_ref, g_ref, w_ref, o_ref):
    h = _rms(m_ref[...], g_ref[...]).astype(BF16)
    o_ref[...] = jnp.dot(h, w_ref[...], preferred_element_type=F32).astype(o_ref.dtype)


def _mem_kv(mem2, g, w_bf):
    depth, d, n = w_bf.shape
    rows = mem2.shape[0]
    tn = min(n, 1024)
    return pl.pallas_call(
        _memkv_kernel,
        grid=(depth, n // tn),
        in_specs=[pl.BlockSpec((rows, d), lambda l, j: (0, 0)),
                  pl.BlockSpec((1, d), lambda l, j: (0, 0)),
                  pl.BlockSpec((None, d, tn), lambda l, j: (l, 0, j))],
        out_specs=pl.BlockSpec((None, rows, tn), lambda l, j: (l, 0, j)),
        out_shape=jax.ShapeDtypeStruct((depth, rows, n), BF16),
        compiler_params=_params(("parallel", "parallel")),
        name="mem_kv",
    )(mem2, g, w_bf)


def _cross_kernel(x_ref, g_ref, wq_ref, kv_ref, wo_ref, o_ref, *, heads):
    x = x_ref[...]
    d = x.shape[1]
    hd = d // heads
    h = _rms(x, g_ref[...]).astype(BF16)
    q = (jnp.dot(h, wq_ref[...], preferred_element_type=F32) * (hd ** -0.5)).astype(BF16)
    outs = []
    for hh in range(heads):
        qh = q[:, hh * hd:(hh + 1) * hd]
        kh = kv_ref[:, hh * hd:(hh + 1) * hd]
        vh = kv_ref[:, d + hh * hd:d + (hh + 1) * hd]
        s = lax.dot_general(qh, kh, NT_DIMS, preferred_element_type=F32)
        p = jnp.exp(s - jnp.max(s, axis=-1, keepdims=True))
        l = jnp.sum(p, axis=-1, keepdims=True)
        outs.append((jnp.dot(p.astype(BF16), vh, preferred_element_type=F32) / l).astype(BF16))
    o = jnp.concatenate(outs, axis=1)
    o_ref[...] = x + jnp.dot(o, wo_ref[...], preferred_element_type=F32)


def _cross_attention(x2, g, wq, kv, layer, wo, seq, n_mem):
    t, d = x2.shape
    tm = min(ROW_TILE, seq)
    tps = seq // tm
    return pl.pallas_call(
        functools.partial(_cross_kernel, heads=MEM_HEADS),
        grid=(t // tm,),
        in_specs=[pl.BlockSpec((tm, d), lambda i: (i, 0)),
                  pl.BlockSpec((1, d), lambda i: (0, 0)),
                  pl.BlockSpec((d, d), lambda i: (0, 0)),
                  pl.BlockSpec((None, n_mem, 2 * d), lambda i: (layer, i // tps, 0)),
                  pl.BlockSpec((d, d), lambda i: (0, 0))],
        out_specs=pl.BlockSpec((tm, d), lambda i: (i, 0)),
        out_shape=jax.ShapeDtypeStruct((t, d), F32),
        compiler_params=_params(("parallel",)),
        name="cross_attn",
    )(x2, g, wq, kv, wo)


def _router_kernel(x_ref, g_ref, wr_ref, ri_ref, rw_ref, cnt_ref, carry_ref):
    i = pl.program_id(0)
    tm = x_ref.shape[0]

    @pl.when(i == 0)
    def _():
        carry_ref[...] = jnp.zeros_like(carry_ref)

    h = _rms(x_ref[...], g_ref[...])
    lg = jnp.dot(h, wr_ref[...], preferred_element_type=F32, precision=lax.Precision.HIGHEST)
    lane = lax.broadcasted_iota(I32, lg.shape, 1).astype(F32)
    neg = -jnp.inf

    def first_argmax(v):
        m = jnp.max(v, axis=-1, keepdims=True)
        return m, jnp.min(jnp.where(v == m, lane, float(LANES)), axis=-1, keepdims=True)

    gmask = lane < N_GROUPS
    gl = jnp.where(gmask, lg, neg)
    gmax, gidx = first_argmax(gl)
    g_w = 1.0 / jnp.sum(jnp.where(gmask, jnp.exp(gl - gmax), 0.0), axis=-1, keepdims=True)

    lo = EXPERT_LANE0 + gidx * EXPERTS_PER_GROUP
    emask = (lane >= lo) & (lane < lo + EXPERTS_PER_GROUP)
    el = jnp.where(emask, lg, neg)
    mx1, i1 = first_argmax(el)
    mx2, i2 = first_argmax(jnp.where(lane == i1, neg, el))
    esum = jnp.sum(jnp.where(emask, jnp.exp(el - mx1), 0.0), axis=-1, keepdims=True)
    p1 = 1.0 / esum
    p2 = jnp.exp(mx2 - mx1) / esum
    den = p1 + p2
    w1 = g_w * (p1 / den)
    w2 = g_w * (p2 / den)

    memb = jnp.where((lane == i1) | (lane == i2), 1.0, 0.0)
    r_i = lax.broadcasted_iota(I32, (tm, tm), 0)
    c_i = lax.broadcasted_iota(I32, (tm, tm), 1)
    ltri = jnp.where(c_i < r_i, 1.0, 0.0).astype(BF16)
    tot = jnp.dot(ltri, memb.astype(BF16), preferred_element_type=F32) + carry_ref[...]
    rank1 = jnp.sum(jnp.where(lane == i1, tot, 0.0), axis=-1, keepdims=True).astype(I32)
    rank2 = jnp.sum(jnp.where(lane == i2, tot, 0.0), axis=-1, keepdims=True).astype(I32)
    carry_ref[...] += jnp.sum(memb, axis=0, keepdims=True)

    e1 = (i1 - EXPERT_LANE0).astype(I32)
    e2 = (i2 - EXPERT_LANE0).astype(I32)
    ri_ref[...] = jnp.where(lane == 0, e1,
                            jnp.where(lane == 1, e2, jnp.where(lane == 2, rank1, jnp.where(lane == 3, rank2, 0))))
    rw_ref[...] = jnp.where(lane == 0, w1, jnp.where(lane == 1, w2, 0.0))
    cnt_ref[...] = jnp.broadcast_to(carry_ref[...], cnt_ref.shape)


def _router(x2, g, w_r, seq):
    t, d = x2.shape
    tm = min(ROW_TILE, seq)
    return pl.pallas_call(
        _router_kernel,
        grid=(t // tm,),
        in_specs=[pl.BlockSpec((tm, d), lambda i: (i, 0)),
                  pl.BlockSpec((1, d), lambda i: (0, 0)),
                  pl.BlockSpec((d, LANES), lambda i: (0, 0))],
        out_specs=[pl.BlockSpec((tm, LANES), lambda i: (i, 0)),
                   pl.BlockSpec((tm, LANES), lambda i: (i, 0)),
                   pl.BlockSpec((8, LANES), lambda i: (0, 0))],
        out_shape=[jax.ShapeDtypeStruct((t, LANES), I32),
                   jax.ShapeDtypeStruct((t, LANES), F32),
                   jax.ShapeDtypeStruct((8, LANES), F32)],
        scratch_shapes=[pltpu.VMEM((1, LANES), F32)],
        compiler_params=_params(("arbitrary",)),
        name="moe_router",
    )(x2, g, w_r)


def _dispatch_kernel(dest_ref, x_ref, g_ref, xb_in_ref, xb_ref, h_ref, sem):
    del xb_in_ref
    i = pl.program_id(0)
    tm = x_ref.shape[0]
    h_ref[...] = _rms(x_ref[...], g_ref[...])
    base = i * (TOP_K * tm)

    def body(t, carry):
        for k in range(TOP_K):
            dst = dest_ref[base + TOP_K * t + k]
            pltpu.make_async_copy(h_ref.at[pl.ds(t, 1), :], xb_ref.at[pl.ds(dst, 1), :], sem).start()
        return carry
    lax.fori_loop(0, tm, body, 0, unroll=8)
    for _ in range(TOP_K):
        pltpu.make_async_copy(h_ref, xb_ref.at[pl.ds(0, tm), :], sem).wait()


def _dispatch(dest, x2, g, xb, seq):
    t, d = x2.shape
    tm = min(ROW_TILE, seq)
    return pl.pallas_call(
        _dispatch_kernel,
        grid_spec=pltpu.PrefetchScalarGridSpec(
            num_scalar_prefetch=1,
            grid=(t // tm,),
            in_specs=[pl.BlockSpec((tm, d), lambda i, dest: (i, 0)),
                      pl.BlockSpec((1, d), lambda i, dest: (0, 0)),
                      pl.BlockSpec(memory_space=pl.ANY)],
            out_specs=pl.BlockSpec(memory_space=pl.ANY),
            scratch_shapes=[pltpu.VMEM((tm, d), F32), pltpu.SemaphoreType.DMA]),
        out_shape=jax.ShapeDtypeStruct(xb.shape, xb.dtype),
        input_output_aliases={3: 0},
        compiler_params=_params(("arbitrary",)),
        name="moe_dispatch",
    )(dest, x2, g, xb)


def _expert_kernel(be_ref, na_ref, xb_ref, wg_ref, wu_ref, wd_ref, yb_ref, wgb_ref, wub_ref, wdb_ref):
    i = pl.program_id(0)
    prev = be_ref[jnp.maximum(i - 1, 0)]

    @pl.when(i < na_ref[0])
    def _():
        @pl.when((i == 0) | (be_ref[i] != prev))
        def _():
            wgb_ref[...] = wg_ref[...].astype(BF16)
            wub_ref[...] = wu_ref[...].astype(BF16)
            wdb_ref[...] = wd_ref[...].astype(BF16)

        xb = xb_ref[...].astype(BF16)
        gate = jnp.dot(xb, wgb_ref[...], preferred_element_type=F32)
        up = jnp.dot(xb, wub_ref[...], preferred_element_type=F32)
        act = (jax.nn.silu(gate) * up).astype(BF16)
        yb_ref[...] = jnp.dot(act, wdb_ref[...], preferred_element_type=F32)

    @pl.when(i >= na_ref[0])
    def _():
        yb_ref[...] = jnp.zeros_like(yb_ref)


def _experts(block_expert, n_active, xb, w_gate, w_up, w_down, layer, blk):
    rows, d = xb.shape
    de = w_gate.shape[-1]
    nb = rows // blk

    def row_map(i, be, na):
        return (jnp.minimum(i, na[0] - 1), 0)

    def w_map(i, be, na):
        return (layer, be[jnp.minimum(i, na[0] - 1)], 0, 0)

    return pl.pallas_call(
        _expert_kernel,
        grid_spec=pltpu.PrefetchScalarGridSpec(
            num_scalar_prefetch=2,
            grid=(nb,),
            in_specs=[pl.BlockSpec((blk, d), row_map),
                      pl.BlockSpec((None, None, d, de), w_map),
                      pl.BlockSpec((None, None, d, de), w_map),
                      pl.BlockSpec((None, None, de, d), w_map)],
            out_specs=pl.BlockSpec((blk, d), lambda i, be, na: (i, 0)),
            scratch_shapes=[pltpu.VMEM((d, de), BF16), pltpu.VMEM((d, de), BF16), pltpu.VMEM((de, d), BF16)]),
        out_shape=jax.ShapeDtypeStruct((rows, d), F32),
        compiler_params=_params(("arbitrary",)),
        name="moe_experts",
    )(block_expert, n_active, xb, w_gate, w_up, w_down)


def _combine_kernel(dest_ref, x_ref, rw_ref, yb_ref, gf_ref, o_ref, y_ref, sem, *, final_norm):
    i = pl.program_id(0)
    tm = x_ref.shape[0]
    base = i * (TOP_K * tm)

    def body(t, carry):
        for k in range(TOP_K):
            src = dest_ref[base + TOP_K * t + k]
            pltpu.make_async_copy(yb_ref.at[pl.ds(src, 1), :], y_ref.at[k, pl.ds(t, 1), :], sem).start()
        return carry
    lax.fori_loop(0, tm, body, 0, unroll=8)
    for k in range(TOP_K):
        pltpu.make_async_copy(yb_ref.at[pl.ds(0, tm), :], y_ref.at[k], sem).wait()
    w = rw_ref[...]
    out = x_ref[...]
    for k in range(TOP_K):
        out = out + w[:, k:k + 1] * y_ref[k]
    if final_norm:
        out = _rms(out, gf_ref[...])
    o_ref[...] = out


def _combine(dest, x2, rw, yb, g_final, seq, final_norm):
    t, d = x2.shape
    tm = min(ROW_TILE, seq)
    return pl.pallas_call(
        functools.partial(_combine_kernel, final_norm=final_norm),
        grid_spec=pltpu.PrefetchScalarGridSpec(
            num_scalar_prefetch=1,
            grid=(t // tm,),
            in_specs=[pl.BlockSpec((tm, d), lambda i, dest: (i, 0)),
                      pl.BlockSpec((tm, LANES), lambda i, dest: (i, 0)),
                      pl.BlockSpec(memory_space=pl.ANY),
                      pl.BlockSpec((1, d), lambda i, dest: (0, 0))],
            out_specs=pl.BlockSpec((tm, d), lambda i, dest: (i, 0)),
            scratch_shapes=[pltpu.VMEM((TOP_K, tm, d), F32), pltpu.SemaphoreType.DMA]),
        out_shape=jax.ShapeDtypeStruct((t, d), F32),
        compiler_params=_params(("arbitrary",)),
        name="moe_combine",
    )(dest, x2, rw, yb, g_final)


def _route_plan(ri, counts_f, blk, n_blocks):
    cnt = counts_f[0, EXPERT_LANE0:EXPERT_LANE0 + N_EXPERTS].astype(I32)
    nblk = (cnt + blk - 1) // blk
    cum = jnp.cumsum(nblk)
    pad_start = (cum - nblk) * blk
    e = ri[:, 0:TOP_K]
    rank = ri[:, TOP_K:2 * TOP_K]
    onehot = e[:, :, None] == jnp.arange(N_EXPERTS, dtype=I32)[None, None, :]
    dest = rank + jnp.sum(jnp.where(onehot, pad_start[None, None, :], 0), axis=-1)
    blocks = jnp.arange(n_blocks, dtype=I32)
    block_expert = jnp.minimum(jnp.sum((cum[None, :] <= blocks[:, None]).astype(I32), axis=-1), N_EXPERTS - 1)
    n_active = cum[-1:].astype(I32)
    return dest.reshape(-1).astype(I32), block_expert, n_active


def kernel(x, mem, norm_mix, w_in, lambda_q1, lambda_k1, lambda_q2, lambda_k2, subln, conv_w, w_branch, w_o,
           norm_cross, norm_mem, w_cq, w_ckv, w_co, norm_ffn, w_router_group, w_router_expert, w_exp_gate,
           w_exp_up, w_exp_down, norm_final):
    batch, seq, d = x.shape
    depth = w_in.shape[0]
    n_mem = mem.shape[1]
    t = batch * seq
    dh = d // (2 * DA_HEADS)
    assert 2 * dh == LANES and d % LANES == 0 and w_in.shape[2] == 8 * d
    assert w_router_group.shape[2] == N_GROUPS and w_router_expert.shape[2] == N_EXPERTS
    blk = min(EXPERT_BLOCK, seq)
    n_blocks = (t * TOP_K) // blk + N_EXPERTS

    tabs = _rotary_tables(seq, dh)
    x2 = x.reshape(t, d)
    kv = _mem_kv(mem.reshape(batch * n_mem, d), norm_mem.reshape(1, d), w_ckv.astype(BF16))
    w_r = jnp.concatenate([w_router_group, w_router_expert,
                           jnp.zeros((depth, d, LANES - N_GROUPS - N_EXPERTS), F32)], axis=-1)
    xb = jnp.zeros((n_blocks * blk, d), F32)

    for l in range(depth):
        lambda_init = 0.8 - 0.6 * math.exp(-0.3 * l)
        proj = _inproj(x2, norm_mix[l].reshape(1, d), w_in[l].astype(BF16), tabs, seq, dh)
        y_attn = _attention(proj, lambda_q1[l].reshape(1, dh), lambda_k1[l].reshape(1, dh),
                            lambda_q2[l].reshape(1, dh), lambda_k2[l].reshape(1, dh),
                            subln[l].reshape(1, LANES), batch, seq, d, lambda_init)
        x2 = _mixer_out(x2, y_attn, proj, conv_w[l], w_branch[l, 0].astype(BF16), w_branch[l, 1].astype(BF16),
                        w_o[l].astype(BF16), seq)
        x2 = _cross_attention(x2, norm_cross[l].reshape(1, d), w_cq[l].astype(BF16), kv, l,
                              w_co[l].astype(BF16), seq, n_mem)
        g_ffn = norm_ffn[l].reshape(1, d)
        ri, rw, counts = _router(x2, g_ffn, w_r[l], seq)
        dest, block_expert, n_active = _route_plan(ri, counts, blk, n_blocks)
        xb = _dispatch(dest, x2, g_ffn, xb, seq)
        yb = _experts(block_expert, n_active, xb, w_exp_gate, w_exp_up, w_exp_down, l, blk)
        x2 = _combine(dest, x2, rw, yb, norm_final.reshape(1, d), seq, l == depth - 1)
    return x2.reshape(batch, seq, d)
```

```python
import functools
import math

import jax
import jax.numpy as jnp
from jax import lax
from jax.experimental import pallas as pl
from jax.experimental.pallas import tpu as pltpu

EPS = 1e-6
DA_HEADS = 8
MEM_HEADS = 4
N_GROUPS = 4
EXPERTS_PER_GROUP = 8
N_EXPERTS = N_GROUPS * EXPERTS_PER_GROUP
TOP_K = 2
ROPE_THETA = 500000.0
LANES = 128
F32_SUBLANES = 8
BF16_SUBLANES = 16
EXPERT_LANE0 = N_GROUPS
ROW_TILE = 512
Q_TILE = 2048
EXPERT_BLOCK = 256
PROJ_ROW_CHUNK = 1024
ATTN_ROW_CHUNK = 256
VMEM_LIMIT = 56 * 1024 * 1024

F32 = jnp.float32
BF16 = jnp.bfloat16
I32 = jnp.int32
NT_DIMS = (((1,), (1,)), ((), ()))


def _rms(x, g):
    return x * lax.rsqrt(jnp.mean(x * x, axis=-1, keepdims=True) + EPS) * g


def _params(sem, vmem=VMEM_LIMIT):
    return pltpu.CompilerParams(dimension_semantics=sem, vmem_limit_bytes=vmem)


def _inproj_kernel(x_ref, g_ref, w_ref, cos_ref, s1_ref, s2_ref, o_ref, h_ref, *, rc, half, q_scale):
    j = pl.program_id(1)
    tm, tn = o_ref.shape

    @pl.when(j == 0)
    def _():
        h_ref[...] = _rms(x_ref[...], g_ref[...]).astype(BF16)

    def run(rot, scale):
        def body(r, carry):
            r0 = pl.multiple_of(r * rc, rc)
            acc = jnp.dot(h_ref[pl.ds(r0, rc), :], w_ref[...], preferred_element_type=F32)
            if not rot:
                o_ref[pl.ds(r0, rc), :] = acc.astype(o_ref.dtype)
                return carry
            c = cos_ref[pl.ds(r0, rc), :]
            s1 = s1_ref[pl.ds(r0, rc), :]
            s2 = s2_ref[pl.ds(r0, rc), :]
            for cc in range(tn // LANES):
                a = acc[:, cc * LANES:(cc + 1) * LANES]
                a = a * c + pltpu.roll(a, half, 1) * s1 + pltpu.roll(a, LANES - half, 1) * s2
                if scale != 1.0:
                    a = a * scale
                o_ref[pl.ds(r0, rc), cc * LANES:(cc + 1) * LANES] = a.astype(o_ref.dtype)
            return carry
        lax.fori_loop(0, tm // rc, body, 0)

    @pl.when(j == 0)
    def _():
        run(True, q_scale)

    @pl.when(j == 1)
    def _():
        run(True, 1.0)

    @pl.when(j >= 2)
    def _():
        run(False, 1.0)


def _inproj(x2, g, w_bf, tabs, seq, dh):
    t, d = x2.shape
    n = w_bf.shape[1]
    tm, tn = seq, d
    rc = min(PROJ_ROW_CHUNK, tm)
    cos_t, s1_t, s2_t = tabs
    tab_spec = pl.BlockSpec((seq, LANES), lambda i, j: (0, 0))
    return pl.pallas_call(
        functools.partial(_inproj_kernel, rc=rc, half=dh // 8, q_scale=dh ** -0.5 * math.log2(math.e)),
        grid=(t // tm, n // tn),
        in_specs=[pl.BlockSpec((tm, d), lambda i, j: (i, 0)),
                  pl.BlockSpec((1, d), lambda i, j: (0, 0)),
                  pl.BlockSpec((d, tn), lambda i, j: (0, j)),
                  tab_spec, tab_spec, tab_spec],
        out_specs=pl.BlockSpec((tm, tn), lambda i, j: (i, j)),
        out_shape=jax.ShapeDtypeStruct((t, n), BF16),
        scratch_shapes=[pltpu.VMEM((tm, d), BF16)],
        compiler_params=_params(("parallel", "arbitrary")),
        name="inproj",
    )(x2, g, w_bf, cos_t, s1_t, s2_t)


def _rotary_tables(seq, dh):
    rot = dh // 4
    half = rot // 2
    inv = jnp.float32(ROPE_THETA) ** (-jnp.arange(0, rot, 2, dtype=F32) / rot)
    ang = jnp.arange(seq, dtype=F32)[:, None] * inv[None, :]
    cos, sin = jnp.cos(ang), jnp.sin(ang)
    lane = jnp.arange(LANES) % dh
    idx = lane % half
    c_t = jnp.where(lane < rot, cos[:, idx], 1.0)
    s1_t = jnp.where((lane >= half) & (lane < rot), sin[:, idx], 0.0)
    s2_t = jnp.where(lane < half, -sin[:, idx], 0.0)
    return c_t.astype(F32), s1_t.astype(F32), s2_t.astype(F32)


def _attn_kernel(q_ref, k_ref, v_ref, lq1_ref, lk1_ref, lq2_ref, lk2_ref, sub_ref, o_ref,
                 s_ref, p_ref, *, dh, lambda_init, rc):
    tq = q_ref.shape[0]
    n_chunks = tq // rc
    lane = lax.broadcasted_iota(I32, (1, LANES), 1)
    m1 = jnp.where(lane < dh, 1.0, 0.0).astype(BF16)
    m2 = jnp.where(lane >= dh, 1.0, 0.0).astype(BF16)
    lam = (jnp.exp(jnp.sum(lq1_ref[...] * lk1_ref[...], axis=-1, keepdims=True))
           - jnp.exp(jnp.sum(lq2_ref[...] * lk2_ref[...], axis=-1, keepdims=True)) + lambda_init)

    def scores(c):
        qc = q_ref[c * rc:(c + 1) * rc, :]
        qq = jnp.concatenate([qc * m1, qc * m2], axis=0)
        s_ref[c % 2] = lax.dot_general(qq, k_ref[...], NT_DIMS, preferred_element_type=F32)

    def softmax_pv(c):
        s = s_ref[c % 2]
        e = jnp.exp2(s - jnp.max(s, axis=-1, keepdims=True))
        l = jnp.sum(e, axis=-1, keepdims=True)
        w = e[:rc] - e[rc:] * (lam * l[:rc] / l[rc:])
        p_ref[c % 2] = w.astype(BF16)
        od = jnp.dot(p_ref[c % 2], v_ref[...], preferred_element_type=F32) / l[:rc]
        od = _rms(od, sub_ref[...]) * (1.0 - lambda_init)
        o_ref[c * rc:(c + 1) * rc, :] = od.astype(o_ref.dtype)

    scores(0)
    for c in range(n_chunks):
        if c + 1 < n_chunks:
            scores(c + 1)
        softmax_pv(c)


def _attention(proj, lq1, lk1, lq2, lk2, sub, batch, seq, d, lambda_init):
    t = proj.shape[0]
    nh = d // LANES
    dh = LANES // 2
    tq = min(Q_TILE, seq)
    nq = seq // tq
    rc = min(ATTN_ROW_CHUNK, tq)
    vec = pl.BlockSpec((1, dh), lambda b, h, qi: (0, 0))
    return pl.pallas_call(
        functools.partial(_attn_kernel, dh=dh, lambda_init=lambda_init, rc=rc),
        grid=(batch, nh, nq),
        scratch_shapes=[pltpu.VMEM((2, 2 * rc, seq), F32), pltpu.VMEM((2, rc, seq), BF16)],
        in_specs=[pl.BlockSpec((tq, LANES), lambda b, h, qi: (b * nq + qi, h)),
                  pl.BlockSpec((seq, LANES), lambda b, h, qi: (b, nh + h)),
                  pl.BlockSpec((seq, LANES), lambda b, h, qi: (b, 2 * nh + h)),
                  vec, vec, vec, vec,
                  pl.BlockSpec((1, LANES), lambda b, h, qi: (0, 0))],
        out_specs=pl.BlockSpec((tq, LANES), lambda b, h, qi: (b * nq + qi, h)),
        out_shape=jax.ShapeDtypeStruct((t, d), BF16),
        compiler_params=_params(("parallel", "parallel", "arbitrary")),
        name="diff_attn",
    )(proj, proj, proj, lq1, lk1, lq2, lk2, sub)


def _mixout_kernel(x_ref, ya_ref, cb_ref, cc_ref, cx_ref, g0_ref, g1_ref, pc_ref, px_ref, nc_ref, nx_ref,
                   cw_ref, wb0_ref, wb1_ref, wo_ref, o_ref, *, tiles_per_seq):
    i = pl.program_id(0)
    tm = x_ref.shape[0]
    u = cc_ref[...].astype(F32) * cx_ref[...].astype(F32)
    hr = pc_ref.shape[0]
    u_prev = pc_ref[hr - 1:hr, :].astype(F32) * px_ref[hr - 1:hr, :].astype(F32)
    u_next = nc_ref[0:1, :].astype(F32) * nx_ref[0:1, :].astype(F32)
    pos = i % tiles_per_seq
    u_prev = jnp.where(pos == 0, 0.0, u_prev)
    u_next = jnp.where(pos == tiles_per_seq - 1, 0.0, u_next)
    row = lax.broadcasted_iota(I32, u.shape, 0)
    u_up = jnp.where(row == 0, u_prev, pltpu.roll(u, 1, 0))
    u_dn = jnp.where(row == tm - 1, u_next, pltpu.roll(u, tm - 1, 0))
    conv = cw_ref[0:1, :] * u_up + cw_ref[1:2, :] * u + cw_ref[2:3, :] * u_dn
    y_conv = (cb_ref[...].astype(F32) * conv).astype(BF16)
    br0 = jnp.dot(ya_ref[...], wb0_ref[...], preferred_element_type=F32)
    br1 = jnp.dot(y_conv, wb1_ref[...], preferred_element_type=F32)
    z = jax.nn.sigmoid(g0_ref[...].astype(F32)) * br0 + jax.nn.sigmoid(g1_ref[...].astype(F32)) * br1
    o_ref[...] = x_ref[...] + jnp.dot(z.astype(BF16), wo_ref[...], preferred_element_type=F32)


def _mixer_out(x2, y_attn, proj, conv_w, wb0, wb1, wo, seq):
    t, d = x2.shape
    tm = min(ROW_TILE, seq)
    tps = seq // tm
    hr = BF16_SUBLANES
    nhb = t // hr
    row = lambda c: pl.BlockSpec((tm, d), lambda i: (i, c))
    prev = lambda c: pl.BlockSpec((hr, d), lambda i: (jnp.maximum(i * (tm // hr) - 1, 0), c))
    nxt = lambda c: pl.BlockSpec((hr, d), lambda i: (jnp.minimum((i + 1) * (tm // hr), nhb - 1), c))
    full = lambda r: pl.BlockSpec((r, d), lambda i: (0, 0))
    return pl.pallas_call(
        functools.partial(_mixout_kernel, tiles_per_seq=tps),
        grid=(t // tm,),
        in_specs=[row(0), row(0), row(3), row(4), row(5), row(6), row(7),
                  prev(4), prev(5), nxt(4), nxt(5),
                  full(conv_w.shape[0]), full(d), full(d), full(d)],
        out_specs=row(0),
        out_shape=jax.ShapeDtypeStruct((t, d), F32),
        compiler_params=_params(("parallel",)),
        name="mixer_out",
    )(x2, y_attn, proj, proj, proj, proj, proj, proj, proj, proj, proj, conv_w, wb0, wb1, wo)


def _memkv_kernel(m_ref, g_ref, w_ref, o_ref):
    h = _rms(m_ref[...], g_ref[...]).astype(BF16)
    o_ref[...] = jnp.dot(h, w_ref[...], preferred_element_type=F32).astype(o_ref.dtype)


def _mem_kv(mem2, g, w_bf):
    depth, d, n = w_bf.shape
    rows = mem2.shape[0]
    tn = min(n, 1024)
    return pl.pallas_call(
        _memkv_kernel,
        grid=(depth, n // tn),
        in_specs=[pl.BlockSpec((rows, d), lambda l, j: (0, 0)),
                  pl.BlockSpec((1, d), lambda l, j: (0, 0)),
                  pl.BlockSpec((None, d, tn), lambda l, j: (l, 0, j))],
        out_specs=pl.BlockSpec((None, rows, tn), lambda l, j: (l, 0, j)),
        out_shape=jax.ShapeDtypeStruct((depth, rows, n), BF16),
        compiler_params=_params(("parallel", "parallel")),
        name="mem_kv",
    )(mem2, g, w_bf)


def _cross_kernel(x_ref, g_ref, wq_ref, kv_ref, wo_ref, gf_ref, wrh_ref, wrl_ref,
                  o_ref, ri_ref, rw_ref, cnt_ref, carry_ref, *, heads):
    @pl.when(pl.program_id(0) == 0)
    def _():
        carry_ref[...] = jnp.zeros_like(carry_ref)

    x = x_ref[...]
    d = x.shape[1]
    hd = d // heads
    h = _rms(x, g_ref[...]).astype(BF16)
    q = (jnp.dot(h, wq_ref[...], preferred_element_type=F32) * (hd ** -0.5)).astype(BF16)
    outs = []
    for hh in range(heads):
        qh = q[:, hh * hd:(hh + 1) * hd]
        kh = kv_ref[:, hh * hd:(hh + 1) * hd]
        vh = kv_ref[:, d + hh * hd:d + (hh + 1) * hd]
        s = lax.dot_general(qh, kh, NT_DIMS, preferred_element_type=F32)
        p = jnp.exp(s - jnp.max(s, axis=-1, keepdims=True))
        l = jnp.sum(p, axis=-1, keepdims=True)
        outs.append((jnp.dot(p.astype(BF16), vh, preferred_element_type=F32) / l).astype(BF16))
    o = jnp.concatenate(outs, axis=1)
    x_new = x + jnp.dot(o, wo_ref[...], preferred_element_type=F32)
    o_ref[...] = x_new
    _route(_rms(x_new, gf_ref[...]), wrh_ref, wrl_ref, ri_ref, rw_ref, cnt_ref, carry_ref)


def _cross_attention(x2, g, wq, kv, layer, wo, g_ffn, wr_hi, wr_lo, seq, n_mem):
    t, d = x2.shape
    tm = min(ROW_TILE, seq)
    tps = seq // tm
    row = lambda w: pl.BlockSpec((tm, w), lambda i: (i, 0))
    const = lambda r, c: pl.BlockSpec((r, c), lambda i: (0, 0))
    return pl.pallas_call(
        functools.partial(_cross_kernel, heads=MEM_HEADS),
        grid=(t // tm,),
        in_specs=[row(d), const(1, d), const(d, d),
                  pl.BlockSpec((None, n_mem, 2 * d), lambda i: (layer, i // tps, 0)),
                  const(d, d), const(1, d), const(d, LANES), const(d, LANES)],
        out_specs=[row(d), row(LANES), row(LANES), const(F32_SUBLANES, LANES)],
        out_shape=[jax.ShapeDtypeStruct((t, d), F32),
                   jax.ShapeDtypeStruct((t, LANES), I32),
                   jax.ShapeDtypeStruct((t, LANES), F32),
                   jax.ShapeDtypeStruct((F32_SUBLANES, LANES), F32)],
        scratch_shapes=[pltpu.VMEM((1, LANES), F32)],
        compiler_params=_params(("arbitrary",)),
        name="cross_attn_router",
    )(x2, g, wq, kv, wo, g_ffn, wr_hi, wr_lo)


def _route(h, wrh_ref, wrl_ref, ri_ref, rw_ref, cnt_ref, carry_ref):
    tm = h.shape[0]
    h_hi = h.astype(BF16)
    h_lo = (h - h_hi.astype(F32)).astype(BF16)
    lg = (jnp.dot(h_hi, wrh_ref[...], preferred_element_type=F32)
          + jnp.dot(h_hi, wrl_ref[...], preferred_element_type=F32)
          + jnp.dot(h_lo, wrh_ref[...], preferred_element_type=F32))
    lane = lax.broadcasted_iota(I32, lg.shape, 1).astype(F32)
    neg = -jnp.inf

    def first_argmax(v):
        m = jnp.max(v, axis=-1, keepdims=True)
        return m, jnp.min(jnp.where(v == m, lane, float(LANES)), axis=-1, keepdims=True)

    gmask = lane < N_GROUPS
    gl = jnp.where(gmask, lg, neg)
    gmax, gidx = first_argmax(gl)
    g_w = 1.0 / jnp.sum(jnp.where(gmask, jnp.exp(gl - gmax), 0.0), axis=-1, keepdims=True)

    lo = EXPERT_LANE0 + gidx * EXPERTS_PER_GROUP
    emask = (lane >= lo) & (lane < lo + EXPERTS_PER_GROUP)
    el = jnp.where(emask, lg, neg)
    mx1, i1 = first_argmax(el)
    mx2, i2 = first_argmax(jnp.where(lane == i1, neg, el))
    esum = jnp.sum(jnp.where(emask, jnp.exp(el - mx1), 0.0), axis=-1, keepdims=True)
    p1 = 1.0 / esum
    p2 = jnp.exp(mx2 - mx1) / esum
    den = p1 + p2
    w1 = g_w * (p1 / den)
    w2 = g_w * (p2 / den)

    memb = jnp.where((lane == i1) | (lane == i2), 1.0, 0.0)
    r_i = lax.broadcasted_iota(I32, (tm, tm), 0)
    c_i = lax.broadcasted_iota(I32, (tm, tm), 1)
    ltri = jnp.where(c_i < r_i, 1.0, 0.0).astype(BF16)
    tot = jnp.dot(ltri, memb.astype(BF16), preferred_element_type=F32) + carry_ref[...]
    rank1 = jnp.sum(jnp.where(lane == i1, tot, 0.0), axis=-1, keepdims=True).astype(I32)
    rank2 = jnp.sum(jnp.where(lane == i2, tot, 0.0), axis=-1, keepdims=True).astype(I32)
    carry_ref[...] += jnp.sum(memb, axis=0, keepdims=True)

    e1 = (i1 - EXPERT_LANE0).astype(I32)
    e2 = (i2 - EXPERT_LANE0).astype(I32)
    ri_ref[...] = jnp.where(lane == 0, e1,
                            jnp.where(lane == 1, e2, jnp.where(lane == 2, rank1, jnp.where(lane == 3, rank2, 0))))
    rw_ref[...] = jnp.where(lane == 0, w1, jnp.where(lane == 1, w2, 0.0))
    cnt_ref[...] = jnp.broadcast_to(carry_ref[...], cnt_ref.shape)


def _dispatch_kernel(dest_ref, x_ref, g_ref, xb_in_ref, xb_ref, h_ref, sem):
    del xb_in_ref
    i = pl.program_id(0)
    tm = x_ref.shape[0]
    h_ref[...] = _rms(x_ref[...], g_ref[...])
    base = i * (TOP_K * tm)

    def body(t, carry):
        for k in range(TOP_K):
            dst = dest_ref[base + TOP_K * t + k]
            pltpu.make_async_copy(h_ref.at[pl.ds(t, 1), :], xb_ref.at[pl.ds(dst, 1), :], sem).start()
        return carry
    lax.fori_loop(0, tm, body, 0, unroll=8)
    for _ in range(TOP_K):
        pltpu.make_async_copy(h_ref, xb_ref.at[pl.ds(0, tm), :], sem).wait()


def _dispatch(dest, x2, g, xb, seq):
    t, d = x2.shape
    tm = min(ROW_TILE, seq)
    return pl.pallas_call(
        _dispatch_kernel,
        grid_spec=pltpu.PrefetchScalarGridSpec(
            num_scalar_prefetch=1,
            grid=(t // tm,),
            in_specs=[pl.BlockSpec((tm, d), lambda i, dest: (i, 0)),
                      pl.BlockSpec((1, d), lambda i, dest: (0, 0)),
                      pl.BlockSpec(memory_space=pl.ANY)],
            out_specs=pl.BlockSpec(memory_space=pl.ANY),
            scratch_shapes=[pltpu.VMEM((tm, d), F32), pltpu.SemaphoreType.DMA]),
        out_shape=jax.ShapeDtypeStruct(xb.shape, xb.dtype),
        input_output_aliases={3: 0},
        compiler_params=_params(("arbitrary",)),
        name="moe_dispatch",
    )(dest, x2, g, xb)


def _expert_kernel(be_ref, first_ref, slot_ref, nxt_ref, na_ref, xb_ref, wg_hbm, wu_hbm, wd_hbm, yb_ref,
                   wg_buf, wu_buf, wd_buf, wgb_ref, wub_ref, wdb_ref, sem, *, layer):
    i = pl.program_id(0)

    def fetch(e, slot):
        return (pltpu.make_async_copy(wg_hbm.at[layer, e], wg_buf.at[slot], sem.at[slot, 0]),
                pltpu.make_async_copy(wu_hbm.at[layer, e], wu_buf.at[slot], sem.at[slot, 1]),
                pltpu.make_async_copy(wd_hbm.at[layer, e], wd_buf.at[slot], sem.at[slot, 2]))

    @pl.when(i < na_ref[0])
    def _():
        @pl.when(first_ref[i] == 1)
        def _():
            e, slot = be_ref[i], slot_ref[i]

            @pl.when(i == 0)
            def _():
                for cp in fetch(e, slot):
                    cp.start()

            for cp in fetch(e, slot):
                cp.wait()
            wgb_ref[...] = wg_buf[slot].astype(BF16)
            wub_ref[...] = wu_buf[slot].astype(BF16)
            wdb_ref[...] = wd_buf[slot].astype(BF16)

            @pl.when(nxt_ref[i] >= 0)
            def _():
                for cp in fetch(nxt_ref[i], 1 - slot):
                    cp.start()

        xb = xb_ref[...].astype(BF16)
        gate = jnp.dot(xb, wgb_ref[...], preferred_element_type=F32)
        up = jnp.dot(xb, wub_ref[...], preferred_element_type=F32)
        act = (jax.nn.silu(gate) * up).astype(BF16)
        yb_ref[...] = jnp.dot(act, wdb_ref[...], preferred_element_type=F32)

    @pl.when(i >= na_ref[0])
    def _():
        yb_ref[...] = jnp.zeros_like(yb_ref)


def _experts(plan, xb, w_gate, w_up, w_down, layer, blk):
    rows, d = xb.shape
    de = w_gate.shape[-1]
    nb = rows // blk

    def row_map(i, be, first, slot, nxt, na):
        return (jnp.minimum(i, jnp.maximum(na[0] - 1, 0)), 0)

    hbm = pl.BlockSpec(memory_space=pl.ANY)
    return pl.pallas_call(
        functools.partial(_expert_kernel, layer=layer),
        grid_spec=pltpu.PrefetchScalarGridSpec(
            num_scalar_prefetch=5,
            grid=(nb,),
            in_specs=[pl.BlockSpec((blk, d), row_map), hbm, hbm, hbm],
            out_specs=pl.BlockSpec((blk, d), lambda i, *_: (i, 0)),
            scratch_shapes=[pltpu.VMEM((2, d, de), F32), pltpu.VMEM((2, d, de), F32), pltpu.VMEM((2, de, d), F32),
                            pltpu.VMEM((d, de), BF16), pltpu.VMEM((d, de), BF16), pltpu.VMEM((de, d), BF16),
                            pltpu.SemaphoreType.DMA((2, 3))]),
        out_shape=jax.ShapeDtypeStruct((rows, d), F32),
        compiler_params=_params(("arbitrary",)),
        name="moe_experts",
    )(*plan, xb, w_gate, w_up, w_down)


def _combine_kernel(dest_ref, x_ref, rw_ref, yb_ref, gf_ref, o_ref, y_ref, sem, *, final_norm):
    i = pl.program_id(0)
    tm = x_ref.shape[0]
    base = i * (TOP_K * tm)

    def body(t, carry):
        for k in range(TOP_K):
            src = dest_ref[base + TOP_K * t + k]
            pltpu.make_async_copy(yb_ref.at[pl.ds(src, 1), :], y_ref.at[k, pl.ds(t, 1), :], sem).start()
        return carry
    lax.fori_loop(0, tm, body, 0, unroll=8)
    for k in range(TOP_K):
        pltpu.make_async_copy(yb_ref.at[pl.ds(0, tm), :], y_ref.at[k], sem).wait()
    w = rw_ref[...]
    out = x_ref[...]
    for k in range(TOP_K):
        out = out + w[:, k:k + 1] * y_ref[k]
    if final_norm:
        out = _rms(out, gf_ref[...])
    o_ref[...] = out


def _combine(dest, x2, rw, yb, g_final, seq, final_norm):
    t, d = x2.shape
    tm = min(ROW_TILE, seq)
    return pl.pallas_call(
        functools.partial(_combine_kernel, final_norm=final_norm),
        grid_spec=pltpu.PrefetchScalarGridSpec(
            num_scalar_prefetch=1,
            grid=(t // tm,),
            in_specs=[pl.BlockSpec((tm, d), lambda i, dest: (i, 0)),
                      pl.BlockSpec((tm, LANES), lambda i, dest: (i, 0)),
                      pl.BlockSpec(memory_space=pl.ANY),
                      pl.BlockSpec((1, d), lambda i, dest: (0, 0))],
            out_specs=pl.BlockSpec((tm, d), lambda i, dest: (i, 0)),
            scratch_shapes=[pltpu.VMEM((TOP_K, tm, d), F32), pltpu.SemaphoreType.DMA]),
        out_shape=jax.ShapeDtypeStruct((t, d), F32),
        compiler_params=_params(("arbitrary",)),
        name="moe_combine",
    )(dest, x2, rw, yb, g_final)


def _route_plan(ri, counts_f, blk, n_blocks):
    cnt = counts_f[0, EXPERT_LANE0:EXPERT_LANE0 + N_EXPERTS].astype(I32)
    nblk = (cnt + blk - 1) // blk
    cum = jnp.cumsum(nblk)
    pad_start = (cum - nblk) * blk
    e = ri[:, 0:TOP_K]
    rank = ri[:, TOP_K:2 * TOP_K]
    onehot = e[:, :, None] == jnp.arange(N_EXPERTS, dtype=I32)[None, None, :]
    dest = rank + jnp.sum(jnp.where(onehot, pad_start[None, None, :], 0), axis=-1)
    blocks = jnp.arange(n_blocks, dtype=I32)
    block_expert = jnp.minimum(jnp.sum((cum[None, :] <= blocks[:, None]).astype(I32), axis=-1), N_EXPERTS - 1)
    n_active = cum[-1:].astype(I32)
    experts = jnp.arange(N_EXPERTS, dtype=I32)
    is_block_expert = block_expert[:, None] == experts[None, :]
    first = jnp.sum(jnp.where(is_block_expert, ((cum - nblk)[None, :] == blocks[:, None]).astype(I32), 0), axis=-1)
    used = (nblk > 0).astype(I32)
    slot_e = (jnp.cumsum(used) - used) % 2
    later = (experts[None, :] > experts[:, None]) & (nblk[None, :] > 0)
    next_e = jnp.min(jnp.where(later, experts[None, :], N_EXPERTS), axis=-1)
    next_e = jnp.where(next_e == N_EXPERTS, -1, next_e)
    slot = jnp.sum(jnp.where(is_block_expert, slot_e[None, :], 0), axis=-1)
    nxt = jnp.sum(jnp.where(is_block_expert, next_e[None, :], 0), axis=-1)
    plan = (block_expert, first.astype(I32), slot.astype(I32), nxt.astype(I32), n_active)
    return dest.reshape(-1).astype(I32), plan


def kernel(x, mem, norm_mix, w_in, lambda_q1, lambda_k1, lambda_q2, lambda_k2, subln, conv_w, w_branch, w_o,
           norm_cross, norm_mem, w_cq, w_ckv, w_co, norm_ffn, w_router_group, w_router_expert, w_exp_gate,
           w_exp_up, w_exp_down, norm_final):
    batch, seq, d = x.shape
    depth = w_in.shape[0]
    n_mem = mem.shape[1]
    t = batch * seq
    dh = d // (2 * DA_HEADS)
    assert 2 * dh == LANES and d % LANES == 0 and w_in.shape[2] == 8 * d
    assert w_router_group.shape[2] == N_GROUPS and w_router_expert.shape[2] == N_EXPERTS
    blk = min(EXPERT_BLOCK, seq)
    n_blocks = (t * TOP_K) // blk + N_EXPERTS

    tabs = _rotary_tables(seq, dh)
    x2 = x.reshape(t, d)
    kv = _mem_kv(mem.reshape(batch * n_mem, d), norm_mem.reshape(1, d), w_ckv.astype(BF16))
    w_r = jnp.concatenate([w_router_group, w_router_expert,
                           jnp.zeros((depth, d, LANES - N_GROUPS - N_EXPERTS), F32)], axis=-1)
    wr_hi = w_r.astype(BF16)
    wr_lo = (w_r - wr_hi.astype(F32)).astype(BF16)
    xb = jnp.zeros((n_blocks * blk, d), F32)

    for l in range(depth):
        lambda_init = 0.8 - 0.6 * math.exp(-0.3 * l)
        proj = _inproj(x2, norm_mix[l].reshape(1, d), w_in[l].astype(BF16), tabs, seq, dh)
        y_attn = _attention(proj, lambda_q1[l].reshape(1, dh), lambda_k1[l].reshape(1, dh),
                            lambda_q2[l].reshape(1, dh), lambda_k2[l].reshape(1, dh),
                            subln[l].reshape(1, LANES), batch, seq, d, lambda_init)
        x2 = _mixer_out(x2, y_attn, proj, conv_w[l], w_branch[l, 0].astype(BF16), w_branch[l, 1].astype(BF16),
                        w_o[l].astype(BF16), seq)
        g_ffn = norm_ffn[l].reshape(1, d)
        x2, ri, rw, counts = _cross_attention(x2, norm_cross[l].reshape(1, d), w_cq[l].astype(BF16), kv, l,
                                              w_co[l].astype(BF16), g_ffn, wr_hi[l], wr_lo[l], seq, n_mem)
        dest, plan = _route_plan(ri, counts, blk, n_blocks)
        xb = _dispatch(dest, x2, g_ffn, xb, seq)
        yb = _experts(plan, xb, w_exp_gate, w_exp_up, w_exp_down, l, blk)
        x2 = _combine(dest, x2, rw, yb, norm_final.reshape(1, d), seq, l == depth - 1)
    return x2.reshape(batch, seq, d)
```

```python
import functools
import math

import jax
import jax.numpy as jnp
from jax import lax
from jax.experimental import pallas as pl
from jax.experimental.pallas import tpu as pltpu

EPS = 1e-6
DA_HEADS = 8
MEM_HEADS = 4
N_GROUPS = 4
EXPERTS_PER_GROUP = 8
N_EXPERTS = N_GROUPS * EXPERTS_PER_GROUP
TOP_K = 2
ROPE_THETA = 500000.0
LANES = 128
F32_SUBLANES = 8
BF16_SUBLANES = 16
EXPERT_LANE0 = N_GROUPS
ROW_TILE = 512
Q_TILE = 2048
EXPERT_BLOCK = 256
PROJ_ROW_CHUNK = 1024
ATTN_SCORE_CHUNK = 128
ATTN_SOFTMAX_CHUNK = 128
VMEM_LIMIT = 56 * 1024 * 1024

F32 = jnp.float32
BF16 = jnp.bfloat16
I32 = jnp.int32
NT_DIMS = (((1,), (1,)), ((), ()))


def _rms(x, g):
    return x * lax.rsqrt(jnp.mean(x * x, axis=-1, keepdims=True) + EPS) * g


def _params(sem, vmem=VMEM_LIMIT):
    return pltpu.CompilerParams(dimension_semantics=sem, vmem_limit_bytes=vmem)


def _inproj_kernel(x_ref, g_ref, w_ref, cos_ref, s1_ref, s2_ref, o_ref, h_ref, *, rc, half, q_scale):
    j = pl.program_id(1)
    tm, tn = o_ref.shape

    @pl.when(j == 0)
    def _():
        h_ref[...] = _rms(x_ref[...], g_ref[...]).astype(BF16)

    def run(rot, scale):
        def body(r, carry):
            r0 = pl.multiple_of(r * rc, rc)
            acc = jnp.dot(h_ref[pl.ds(r0, rc), :], w_ref[...], preferred_element_type=F32)
            if not rot:
                o_ref[pl.ds(r0, rc), :] = acc.astype(o_ref.dtype)
                return carry
            c = cos_ref[pl.ds(r0, rc), :]
            s1 = s1_ref[pl.ds(r0, rc), :]
            s2 = s2_ref[pl.ds(r0, rc), :]
            for cc in range(tn // LANES):
                a = acc[:, cc * LANES:(cc + 1) * LANES]
                a = a * c + pltpu.roll(a, half, 1) * s1 + pltpu.roll(a, LANES - half, 1) * s2
                if scale != 1.0:
                    a = a * scale
                o_ref[pl.ds(r0, rc), cc * LANES:(cc + 1) * LANES] = a.astype(o_ref.dtype)
            return carry
        lax.fori_loop(0, tm // rc, body, 0)

    @pl.when(j == 0)
    def _():
        run(True, q_scale)

    @pl.when(j == 1)
    def _():
        run(True, 1.0)

    @pl.when(j >= 2)
    def _():
        run(False, 1.0)


def _inproj(x2, g, w_bf, tabs, seq, dh):
    t, d = x2.shape
    n = w_bf.shape[1]
    tm, tn = seq, d
    rc = min(PROJ_ROW_CHUNK, tm)
    cos_t, s1_t, s2_t = tabs
    tab_spec = pl.BlockSpec((seq, LANES), lambda i, j: (0, 0))
    return pl.pallas_call(
        functools.partial(_inproj_kernel, rc=rc, half=dh // 8, q_scale=dh ** -0.5 * math.log2(math.e)),
        grid=(t // tm, n // tn),
        in_specs=[pl.BlockSpec((tm, d), lambda i, j: (i, 0)),
                  pl.BlockSpec((1, d), lambda i, j: (0, 0)),
                  pl.BlockSpec((d, tn), lambda i, j: (0, j)),
                  tab_spec, tab_spec, tab_spec],
        out_specs=pl.BlockSpec((tm, tn), lambda i, j: (i, j)),
        out_shape=jax.ShapeDtypeStruct((t, n), BF16),
        scratch_shapes=[pltpu.VMEM((tm, d), BF16)],
        compiler_params=_params(("parallel", "arbitrary")),
        name="inproj",
    )(x2, g, w_bf, cos_t, s1_t, s2_t)


def _rotary_tables(seq, dh):
    rot = dh // 4
    half = rot // 2
    inv = jnp.float32(ROPE_THETA) ** (-jnp.arange(0, rot, 2, dtype=F32) / rot)
    ang = jnp.arange(seq, dtype=F32)[:, None] * inv[None, :]
    cos, sin = jnp.cos(ang), jnp.sin(ang)
    lane = jnp.arange(LANES) % dh
    idx = lane % half
    c_t = jnp.where(lane < rot, cos[:, idx], 1.0)
    s1_t = jnp.where((lane >= half) & (lane < rot), sin[:, idx], 0.0)
    s2_t = jnp.where(lane < half, -sin[:, idx], 0.0)
    return c_t.astype(F32), s1_t.astype(F32), s2_t.astype(F32)


def _attn_kernel(q_ref, k_ref, v_ref, lq1_ref, lk1_ref, lq2_ref, lk2_ref, sub_ref, o_ref,
                 s_ref, p_ref, *, dh, lambda_init, rc, rp):
    tq = q_ref.shape[0]
    n_chunks = tq // rc
    lane = lax.broadcasted_iota(I32, (1, LANES), 1)
    m1 = jnp.where(lane < dh, 1.0, 0.0).astype(BF16)
    m2 = jnp.where(lane >= dh, 1.0, 0.0).astype(BF16)
    lam = (jnp.exp(jnp.sum(lq1_ref[...] * lk1_ref[...], axis=-1, keepdims=True))
           - jnp.exp(jnp.sum(lq2_ref[...] * lk2_ref[...], axis=-1, keepdims=True)) + lambda_init)

    def scores(c):
        qc = q_ref[c * rc:(c + 1) * rc, :]
        qq = jnp.concatenate([qc * m1, qc * m2], axis=0)
        s_ref[c % 2] = lax.dot_general(qq, k_ref[...], NT_DIMS, preferred_element_type=F32)

    def softmax_pv(c, j):
        r0 = j * rp
        s1 = s_ref[c % 2, r0:r0 + rp, :]
        s2 = s_ref[c % 2, rc + r0:rc + r0 + rp, :]
        e1 = jnp.exp2(s1 - jnp.max(s1, axis=-1, keepdims=True))
        e2 = jnp.exp2(s2 - jnp.max(s2, axis=-1, keepdims=True))
        l1 = jnp.sum(e1, axis=-1, keepdims=True)
        l2 = jnp.sum(e2, axis=-1, keepdims=True)
        slot = (c * (rc // rp) + j) % 2
        p_ref[slot] = (e1 - e2 * (lam * l1 / l2)).astype(BF16)
        od = jnp.dot(p_ref[slot], v_ref[...], preferred_element_type=F32) / l1
        od = _rms(od, sub_ref[...]) * (1.0 - lambda_init)
        o_ref[c * rc + r0:c * rc + r0 + rp, :] = od.astype(o_ref.dtype)

    scores(0)
    for c in range(n_chunks):
        if c + 1 < n_chunks:
            scores(c + 1)
        for j in range(rc // rp):
            softmax_pv(c, j)


def _attention(proj, lq1, lk1, lq2, lk2, sub, batch, seq, d, lambda_init):
    t = proj.shape[0]
    nh = d // LANES
    dh = LANES // 2
    tq = min(Q_TILE, seq)
    nq = seq // tq
    rc = min(ATTN_SCORE_CHUNK, tq)
    rp = min(ATTN_SOFTMAX_CHUNK, rc)
    vec = pl.BlockSpec((1, dh), lambda b, h, qi: (0, 0))
    return pl.pallas_call(
        functools.partial(_attn_kernel, dh=dh, lambda_init=lambda_init, rc=rc, rp=rp),
        grid=(batch, nh, nq),
        scratch_shapes=[pltpu.VMEM((2, 2 * rc, seq), F32), pltpu.VMEM((2, rp, seq), BF16)],
        in_specs=[pl.BlockSpec((tq, LANES), lambda b, h, qi: (b * nq + qi, h)),
                  pl.BlockSpec((seq, LANES), lambda b, h, qi: (b, nh + h)),
                  pl.BlockSpec((seq, LANES), lambda b, h, qi: (b, 2 * nh + h)),
                  vec, vec, vec, vec,
                  pl.BlockSpec((1, LANES), lambda b, h, qi: (0, 0))],
        out_specs=pl.BlockSpec((tq, LANES), lambda b, h, qi: (b * nq + qi, h)),
        out_shape=jax.ShapeDtypeStruct((t, d), BF16),
        compiler_params=_params(("parallel", "parallel", "arbitrary")),
        name="diff_attn",
    )(proj, proj, proj, lq1, lk1, lq2, lk2, sub)


def _mixout_kernel(x_ref, ya_ref, cb_ref, cc_ref, cx_ref, g0_ref, g1_ref, pc_ref, px_ref, nc_ref, nx_ref,
                   cw_ref, wb0_ref, wb1_ref, wo_ref, o_ref, *, tiles_per_seq):
    i = pl.program_id(0)
    tm = x_ref.shape[0]
    u = cc_ref[...].astype(F32) * cx_ref[...].astype(F32)
    hr = pc_ref.shape[0]
    u_prev = pc_ref[hr - 1:hr, :].astype(F32) * px_ref[hr - 1:hr, :].astype(F32)
    u_next = nc_ref[0:1, :].astype(F32) * nx_ref[0:1, :].astype(F32)
    pos = i % tiles_per_seq
    u_prev = jnp.where(pos == 0, 0.0, u_prev)
    u_next = jnp.where(pos == tiles_per_seq - 1, 0.0, u_next)
    row = lax.broadcasted_iota(I32, u.shape, 0)
    u_up = jnp.where(row == 0, u_prev, pltpu.roll(u, 1, 0))
    u_dn = jnp.where(row == tm - 1, u_next, pltpu.roll(u, tm - 1, 0))
    conv = cw_ref[0:1, :] * u_up + cw_ref[1:2, :] * u + cw_ref[2:3, :] * u_dn
    y_conv = (cb_ref[...].astype(F32) * conv).astype(BF16)
    br0 = jnp.dot(ya_ref[...], wb0_ref[...], preferred_element_type=F32)
    br1 = jnp.dot(y_conv, wb1_ref[...], preferred_element_type=F32)
    z = jax.nn.sigmoid(g0_ref[...].astype(F32)) * br0 + jax.nn.sigmoid(g1_ref[...].astype(F32)) * br1
    o_ref[...] = x_ref[...] + jnp.dot(z.astype(BF16), wo_ref[...], preferred_element_type=F32)


def _mixer_out(x2, y_attn, proj, conv_w, wb0, wb1, wo, seq):
    t, d = x2.shape
    tm = min(ROW_TILE, seq)
    tps = seq // tm
    hr = BF16_SUBLANES
    nhb = t // hr
    row = lambda c: pl.BlockSpec((tm, d), lambda i: (i, c))
    prev = lambda c: pl.BlockSpec((hr, d), lambda i: (jnp.maximum(i * (tm // hr) - 1, 0), c))
    nxt = lambda c: pl.BlockSpec((hr, d), lambda i: (jnp.minimum((i + 1) * (tm // hr), nhb - 1), c))
    full = lambda r: pl.BlockSpec((r, d), lambda i: (0, 0))
    return pl.pallas_call(
        functools.partial(_mixout_kernel, tiles_per_seq=tps),
        grid=(t // tm,),
        in_specs=[row(0), row(0), row(3), row(4), row(5), row(6), row(7),
                  prev(4), prev(5), nxt(4), nxt(5),
                  full(conv_w.shape[0]), full(d), full(d), full(d)],
        out_specs=row(0),
        out_shape=jax.ShapeDtypeStruct((t, d), F32),
        compiler_params=_params(("parallel",)),
        name="mixer_out",
    )(x2, y_attn, proj, proj, proj, proj, proj, proj, proj, proj, proj, conv_w, wb0, wb1, wo)


def _memkv_kernel(m_ref, g_ref, w_ref, o_ref):
    h = _rms(m_ref[...], g_ref[...]).astype(BF16)
    o_ref[...] = jnp.dot(h, w_ref[...], preferred_element_type=F32).astype(o_ref.dtype)


def _mem_kv(mem2, g, w_bf):
    depth, d, n = w_bf.shape
    rows = mem2.shape[0]
    tn = min(n, 1024)
    return pl.pallas_call(
        _memkv_kernel,
        grid=(depth, n // tn),
        in_specs=[pl.BlockSpec((rows, d), lambda l, j: (0, 0)),
                  pl.BlockSpec((1, d), lambda l, j: (0, 0)),
                  pl.BlockSpec((None, d, tn), lambda l, j: (l, 0, j))],
        out_specs=pl.BlockSpec((None, rows, tn), lambda l, j: (l, 0, j)),
        out_shape=jax.ShapeDtypeStruct((depth, rows, n), BF16),
        compiler_params=_params(("parallel", "parallel")),
        name="mem_kv",
    )(mem2, g, w_bf)


def _cross_kernel(x_ref, g_ref, wq_ref, kv_ref, wo_ref, gf_ref, wrh_ref, wrl_ref,
                  o_ref, ri_ref, rw_ref, cnt_ref, carry_ref, *, heads):
    @pl.when(pl.program_id(0) == 0)
    def _():
        carry_ref[...] = jnp.zeros_like(carry_ref)

    x = x_ref[...]
    d = x.shape[1]
    hd = d // heads
    h = _rms(x, g_ref[...]).astype(BF16)
    q = (jnp.dot(h, wq_ref[...], preferred_element_type=F32) * (hd ** -0.5)).astype(BF16)
    outs = []
    for hh in range(heads):
        qh = q[:, hh * hd:(hh + 1) * hd]
        kh = kv_ref[:, hh * hd:(hh + 1) * hd]
        vh = kv_ref[:, d + hh * hd:d + (hh + 1) * hd]
        s = lax.dot_general(qh, kh, NT_DIMS, preferred_element_type=F32)
        p = jnp.exp(s - jnp.max(s, axis=-1, keepdims=True))
        l = jnp.sum(p, axis=-1, keepdims=True)
        outs.append((jnp.dot(p.astype(BF16), vh, preferred_element_type=F32) / l).astype(BF16))
    o = jnp.concatenate(outs, axis=1)
    x_new = x + jnp.dot(o, wo_ref[...], preferred_element_type=F32)
    o_ref[...] = x_new
    _route(_rms(x_new, gf_ref[...]), wrh_ref, wrl_ref, ri_ref, rw_ref, cnt_ref, carry_ref)


def _cross_attention(x2, g, wq, kv, layer, wo, g_ffn, wr_hi, wr_lo, seq, n_mem):
    t, d = x2.shape
    tm = min(ROW_TILE, seq)
    tps = seq // tm
    row = lambda w: pl.BlockSpec((tm, w), lambda i: (i, 0))
    const = lambda r, c: pl.BlockSpec((r, c), lambda i: (0, 0))
    return pl.pallas_call(
        functools.partial(_cross_kernel, heads=MEM_HEADS),
        grid=(t // tm,),
        in_specs=[row(d), const(1, d), const(d, d),
                  pl.BlockSpec((None, n_mem, 2 * d), lambda i: (layer, i // tps, 0)),
                  const(d, d), const(1, d), const(d, LANES), const(d, LANES)],
        out_specs=[row(d), row(LANES), row(LANES), const(F32_SUBLANES, LANES)],
        out_shape=[jax.ShapeDtypeStruct((t, d), F32),
                   jax.ShapeDtypeStruct((t, LANES), I32),
                   jax.ShapeDtypeStruct((t, LANES), F32),
                   jax.ShapeDtypeStruct((F32_SUBLANES, LANES), F32)],
        scratch_shapes=[pltpu.VMEM((1, LANES), F32)],
        compiler_params=_params(("arbitrary",)),
        name="cross_attn_router",
    )(x2, g, wq, kv, wo, g_ffn, wr_hi, wr_lo)


def _route(h, wrh_ref, wrl_ref, ri_ref, rw_ref, cnt_ref, carry_ref):
    tm = h.shape[0]
    h_hi = h.astype(BF16)
    h_lo = (h - h_hi.astype(F32)).astype(BF16)
    lg = (jnp.dot(h_hi, wrh_ref[...], preferred_element_type=F32)
          + jnp.dot(h_hi, wrl_ref[...], preferred_element_type=F32)
          + jnp.dot(h_lo, wrh_ref[...], preferred_element_type=F32))
    lane = lax.broadcasted_iota(I32, lg.shape, 1).astype(F32)
    neg = -jnp.inf

    def first_argmax(v):
        m = jnp.max(v, axis=-1, keepdims=True)
        return m, jnp.min(jnp.where(v == m, lane, float(LANES)), axis=-1, keepdims=True)

    gmask = lane < N_GROUPS
    gl = jnp.where(gmask, lg, neg)
    gmax, gidx = first_argmax(gl)
    g_w = 1.0 / jnp.sum(jnp.where(gmask, jnp.exp(gl - gmax), 0.0), axis=-1, keepdims=True)

    lo = EXPERT_LANE0 + gidx * EXPERTS_PER_GROUP
    emask = (lane >= lo) & (lane < lo + EXPERTS_PER_GROUP)
    el = jnp.where(emask, lg, neg)
    mx1, i1 = first_argmax(el)
    mx2, i2 = first_argmax(jnp.where(lane == i1, neg, el))
    esum = jnp.sum(jnp.where(emask, jnp.exp(el - mx1), 0.0), axis=-1, keepdims=True)
    p1 = 1.0 / esum
    p2 = jnp.exp(mx2 - mx1) / esum
    den = p1 + p2
    w1 = g_w * (p1 / den)
    w2 = g_w * (p2 / den)

    memb = jnp.where((lane == i1) | (lane == i2), 1.0, 0.0)
    r_i = lax.broadcasted_iota(I32, (tm, tm), 0)
    c_i = lax.broadcasted_iota(I32, (tm, tm), 1)
    ltri = jnp.where(c_i < r_i, 1.0, 0.0).astype(BF16)
    tot = jnp.dot(ltri, memb.astype(BF16), preferred_element_type=F32) + carry_ref[...]
    rank1 = jnp.sum(jnp.where(lane == i1, tot, 0.0), axis=-1, keepdims=True).astype(I32)
    rank2 = jnp.sum(jnp.where(lane == i2, tot, 0.0), axis=-1, keepdims=True).astype(I32)
    carry_ref[...] += jnp.sum(memb, axis=0, keepdims=True)

    e1 = (i1 - EXPERT_LANE0).astype(I32)
    e2 = (i2 - EXPERT_LANE0).astype(I32)
    ri_ref[...] = jnp.where(lane == 0, e1,
                            jnp.where(lane == 1, e2, jnp.where(lane == 2, rank1, jnp.where(lane == 3, rank2, 0))))
    rw_ref[...] = jnp.where(lane == 0, w1, jnp.where(lane == 1, w2, 0.0))
    cnt_ref[...] = jnp.broadcast_to(carry_ref[...], cnt_ref.shape)


def _dispatch_kernel(dest_ref, x_ref, g_ref, xb_in_ref, xb_ref, h_ref, sem):
    del xb_in_ref
    i = pl.program_id(0)
    tm = x_ref.shape[0]
    h_ref[...] = _rms(x_ref[...], g_ref[...])
    base = i * (TOP_K * tm)

    for t in range(tm):
        for k in range(TOP_K):
            dst = dest_ref[base + (TOP_K * t + k)]
            pltpu.make_async_copy(h_ref.at[pl.ds(t, 1), :], xb_ref.at[pl.ds(dst, 1), :], sem).start()
    for _ in range(TOP_K):
        pltpu.make_async_copy(h_ref, xb_ref.at[pl.ds(0, tm), :], sem).wait()


def _dispatch(dest, x2, g, xb, seq):
    t, d = x2.shape
    tm = min(ROW_TILE, seq)
    return pl.pallas_call(
        _dispatch_kernel,
        grid_spec=pltpu.PrefetchScalarGridSpec(
            num_scalar_prefetch=1,
            grid=(t // tm,),
            in_specs=[pl.BlockSpec((tm, d), lambda i, dest: (i, 0)),
                      pl.BlockSpec((1, d), lambda i, dest: (0, 0)),
                      pl.BlockSpec(memory_space=pl.ANY)],
            out_specs=pl.BlockSpec(memory_space=pl.ANY),
            scratch_shapes=[pltpu.VMEM((tm, d), F32), pltpu.SemaphoreType.DMA]),
        out_shape=jax.ShapeDtypeStruct(xb.shape, xb.dtype),
        input_output_aliases={3: 0},
        compiler_params=_params(("arbitrary",)),
        name="moe_dispatch",
    )(dest, x2, g, xb)


def _expert_kernel(be_ref, first_ref, slot_ref, nxt_ref, na_ref, xb_ref, wg_hbm, wu_hbm, wd_hbm, yb_ref,
                   wg_buf, wu_buf, wd_buf, wgb_ref, wub_ref, wdb_ref, sem, *, layer):
    i = pl.program_id(0)

    def fetch(e, slot):
        return (pltpu.make_async_copy(wg_hbm.at[layer, e], wg_buf.at[slot], sem.at[slot, 0]),
                pltpu.make_async_copy(wu_hbm.at[layer, e], wu_buf.at[slot], sem.at[slot, 1]),
                pltpu.make_async_copy(wd_hbm.at[layer, e], wd_buf.at[slot], sem.at[slot, 2]))

    @pl.when(i < na_ref[0])
    def _():
        @pl.when(first_ref[i] == 1)
        def _():
            e, slot = be_ref[i], slot_ref[i]

            @pl.when(i == 0)
            def _():
                for cp in fetch(e, slot):
                    cp.start()

            for cp in fetch(e, slot):
                cp.wait()
            wgb_ref[...] = wg_buf[slot].astype(BF16)
            wub_ref[...] = wu_buf[slot].astype(BF16)
            wdb_ref[...] = wd_buf[slot].astype(BF16)

            @pl.when(nxt_ref[i] >= 0)
            def _():
                for cp in fetch(nxt_ref[i], 1 - slot):
                    cp.start()

        xb = xb_ref[...].astype(BF16)
        gate = jnp.dot(xb, wgb_ref[...], preferred_element_type=F32)
        up = jnp.dot(xb, wub_ref[...], preferred_element_type=F32)
        act = (jax.nn.silu(gate) * up).astype(BF16)
        yb_ref[...] = jnp.dot(act, wdb_ref[...], preferred_element_type=F32)

    @pl.when(i >= na_ref[0])
    def _():
        yb_ref[...] = jnp.zeros_like(yb_ref)


def _experts(plan, xb, w_gate, w_up, w_down, layer, blk):
    rows, d = xb.shape
    de = w_gate.shape[-1]
    nb = rows // blk

    def row_map(i, be, first, slot, nxt, na):
        return (jnp.minimum(i, jnp.maximum(na[0] - 1, 0)), 0)

    hbm = pl.BlockSpec(memory_space=pl.ANY)
    return pl.pallas_call(
        functools.partial(_expert_kernel, layer=layer),
        grid_spec=pltpu.PrefetchScalarGridSpec(
            num_scalar_prefetch=5,
            grid=(nb,),
            in_specs=[pl.BlockSpec((blk, d), row_map), hbm, hbm, hbm],
            out_specs=pl.BlockSpec((blk, d), lambda i, *_: (i, 0)),
            scratch_shapes=[pltpu.VMEM((2, d, de), F32), pltpu.VMEM((2, d, de), F32), pltpu.VMEM((2, de, d), F32),
                            pltpu.VMEM((d, de), BF16), pltpu.VMEM((d, de), BF16), pltpu.VMEM((de, d), BF16),
                            pltpu.SemaphoreType.DMA((2, 3))]),
        out_shape=jax.ShapeDtypeStruct((rows, d), F32),
        compiler_params=_params(("arbitrary",)),
        name="moe_experts",
    )(*plan, xb, w_gate, w_up, w_down)


def _combine_kernel(dest_ref, x_ref, rw_ref, yb_ref, gf_ref, o_ref, y_ref, sem, *, final_norm):
    i = pl.program_id(0)
    tm = x_ref.shape[0]
    base = i * (TOP_K * tm)

    for t in range(tm):
        for k in range(TOP_K):
            src = dest_ref[base + (TOP_K * t + k)]
            pltpu.make_async_copy(yb_ref.at[pl.ds(src, 1), :], y_ref.at[k, pl.ds(t, 1), :], sem).start()
    for k in range(TOP_K):
        pltpu.make_async_copy(yb_ref.at[pl.ds(0, tm), :], y_ref.at[k], sem).wait()
    w = rw_ref[...]
    out = x_ref[...]
    for k in range(TOP_K):
        out = out + w[:, k:k + 1] * y_ref[k]
    if final_norm:
        out = _rms(out, gf_ref[...])
    o_ref[...] = out


def _combine(dest, x2, rw, yb, g_final, seq, final_norm):
    t, d = x2.shape
    tm = min(ROW_TILE, seq)
    return pl.pallas_call(
        functools.partial(_combine_kernel, final_norm=final_norm),
        grid_spec=pltpu.PrefetchScalarGridSpec(
            num_scalar_prefetch=1,
            grid=(t // tm,),
            in_specs=[pl.BlockSpec((tm, d), lambda i, dest: (i, 0)),
                      pl.BlockSpec((tm, LANES), lambda i, dest: (i, 0)),
                      pl.BlockSpec(memory_space=pl.ANY),
                      pl.BlockSpec((1, d), lambda i, dest: (0, 0))],
            out_specs=pl.BlockSpec((tm, d), lambda i, dest: (i, 0)),
            scratch_shapes=[pltpu.VMEM((TOP_K, tm, d), F32), pltpu.SemaphoreType.DMA]),
        out_shape=jax.ShapeDtypeStruct((t, d), F32),
        compiler_params=_params(("arbitrary",)),
        name="moe_combine",
    )(dest, x2, rw, yb, g_final)


def _route_plan(ri, counts_f, blk, n_blocks):
    cnt = counts_f[0, EXPERT_LANE0:EXPERT_LANE0 + N_EXPERTS].astype(I32)
    nblk = (cnt + blk - 1) // blk
    cum = jnp.cumsum(nblk)
    pad_start = (cum - nblk) * blk
    e = ri[:, 0:TOP_K]
    rank = ri[:, TOP_K:2 * TOP_K]
    onehot = e[:, :, None] == jnp.arange(N_EXPERTS, dtype=I32)[None, None, :]
    dest = rank + jnp.sum(jnp.where(onehot, pad_start[None, None, :], 0), axis=-1)
    blocks = jnp.arange(n_blocks, dtype=I32)
    block_expert = jnp.minimum(jnp.sum((cum[None, :] <= blocks[:, None]).astype(I32), axis=-1), N_EXPERTS - 1)
    n_active = cum[-1:].astype(I32)
    experts = jnp.arange(N_EXPERTS, dtype=I32)
    is_block_expert = block_expert[:, None] == experts[None, :]
    first = jnp.sum(jnp.where(is_block_expert, ((cum - nblk)[None, :] == blocks[:, None]).astype(I32), 0), axis=-1)
    used = (nblk > 0).astype(I32)
    slot_e = (jnp.cumsum(used) - used) % 2
    later = (experts[None, :] > experts[:, None]) & (nblk[None, :] > 0)
    next_e = jnp.min(jnp.where(later, experts[None, :], N_EXPERTS), axis=-1)
    next_e = jnp.where(next_e == N_EXPERTS, -1, next_e)
    slot = jnp.sum(jnp.where(is_block_expert, slot_e[None, :], 0), axis=-1)
    nxt = jnp.sum(jnp.where(is_block_expert, next_e[None, :], 0), axis=-1)
    plan = (block_expert, first.astype(I32), slot.astype(I32), nxt.astype(I32), n_active)
    return dest.reshape(-1).astype(I32), plan


def kernel(x, mem, norm_mix, w_in, lambda_q1, lambda_k1, lambda_q2, lambda_k2, subln, conv_w, w_branch, w_o,
           norm_cross, norm_mem, w_cq, w_ckv, w_co, norm_ffn, w_router_group, w_router_expert, w_exp_gate,
           w_exp_up, w_exp_down, norm_final):
    batch, seq, d = x.shape
    depth = w_in.shape[0]
    n_mem = mem.shape[1]
    t = batch * seq
    dh = d // (2 * DA_HEADS)
    assert 2 * dh == LANES and d % LANES == 0 and w_in.shape[2] == 8 * d
    assert w_router_group.shape[2] == N_GROUPS and w_router_expert.shape[2] == N_EXPERTS
    blk = min(EXPERT_BLOCK, seq)
    n_blocks = (t * TOP_K) // blk + N_EXPERTS

    tabs = _rotary_tables(seq, dh)
    x2 = x.reshape(t, d)
    kv = _mem_kv(mem.reshape(batch * n_mem, d), norm_mem.reshape(1, d), w_ckv.astype(BF16))
    w_r = jnp.concatenate([w_router_group, w_router_expert,
                           jnp.zeros((depth, d, LANES - N_GROUPS - N_EXPERTS), F32)], axis=-1)
    wr_hi = w_r.astype(BF16)
    wr_lo = (w_r - wr_hi.astype(F32)).astype(BF16)
    xb = jnp.zeros((n_blocks * blk, d), F32)

    for l in range(depth):
        lambda_init = 0.8 - 0.6 * math.exp(-0.3 * l)
        proj = _inproj(x2, norm_mix[l].reshape(1, d), w_in[l].astype(BF16), tabs, seq, dh)
        y_attn = _attention(proj, lambda_q1[l].reshape(1, dh), lambda_k1[l].reshape(1, dh),
                            lambda_q2[l].reshape(1, dh), lambda_k2[l].reshape(1, dh),
                            subln[l].reshape(1, LANES), batch, seq, d, lambda_init)
        x2 = _mixer_out(x2, y_attn, proj, conv_w[l], w_branch[l, 0].astype(BF16), w_branch[l, 1].astype(BF16),
                        w_o[l].astype(BF16), seq)
        g_ffn = norm_ffn[l].reshape(1, d)
        x2, ri, rw, counts = _cross_attention(x2, norm_cross[l].reshape(1, d), w_cq[l].astype(BF16), kv, l,
                                              w_co[l].astype(BF16), g_ffn, wr_hi[l], wr_lo[l], seq, n_mem)
        dest, plan = _route_plan(ri, counts, blk, n_blocks)
        xb = _dispatch(dest, x2, g_ffn, xb, seq)
        yb = _experts(plan, xb, w_exp_gate, w_exp_up, w_exp_down, l, blk)
        x2 = _combine(dest, x2, rw, yb, norm_final.reshape(1, d), seq, l == depth - 1)
    return x2.reshape(batch, seq, d)
```

```python
import functools
import math

import jax
import jax.numpy as jnp
from jax import lax
from jax.experimental import pallas as pl
from jax.experimental.pallas import tpu as pltpu

EPS = 1e-6
DA_HEADS = 8
MEM_HEADS = 4
N_GROUPS = 4
EXPERTS_PER_GROUP = 8
N_EXPERTS = N_GROUPS * EXPERTS_PER_GROUP
TOP_K = 2
ROPE_THETA = 500000.0
LANES = 128
F32_SUBLANES = 8
BF16_SUBLANES = 16
EXPERT_LANE0 = N_GROUPS
ROW_TILE = 512
ROUTER_TILE = 2048
Q_TILE = 2048
EXPERT_BLOCK = 256
PROJ_ROW_CHUNK = 1024
ATTN_SCORE_CHUNK = 128
ATTN_SOFTMAX_CHUNK = 128
VMEM_LIMIT = 56 * 1024 * 1024

F32 = jnp.float32
BF16 = jnp.bfloat16
I32 = jnp.int32
NT_DIMS = (((1,), (1,)), ((), ()))


def _rms(x, g):
    return x * lax.rsqrt(jnp.mean(x * x, axis=-1, keepdims=True) + EPS) * g


def _params(sem, vmem=VMEM_LIMIT):
    return pltpu.CompilerParams(dimension_semantics=sem, vmem_limit_bytes=vmem)


def _inproj_kernel(x_ref, g_ref, w_ref, cos_ref, s1_ref, s2_ref, o_ref, h_ref, *, rc, half, q_scale):
    j = pl.program_id(1)
    tm, tn = o_ref.shape

    @pl.when(j == 0)
    def _():
        h_ref[...] = _rms(x_ref[...], g_ref[...]).astype(BF16)

    def run(rot, scale):
        def body(r, carry):
            r0 = pl.multiple_of(r * rc, rc)
            acc = jnp.dot(h_ref[pl.ds(r0, rc), :], w_ref[...], preferred_element_type=F32)
            if not rot:
                o_ref[pl.ds(r0, rc), :] = acc.astype(o_ref.dtype)
                return carry
            c = cos_ref[pl.ds(r0, rc), :]
            s1 = s1_ref[pl.ds(r0, rc), :]
            s2 = s2_ref[pl.ds(r0, rc), :]
            for cc in range(tn // LANES):
                a = acc[:, cc * LANES:(cc + 1) * LANES]
                a = a * c + pltpu.roll(a, half, 1) * s1 + pltpu.roll(a, LANES - half, 1) * s2
                if scale != 1.0:
                    a = a * scale
                o_ref[pl.ds(r0, rc), cc * LANES:(cc + 1) * LANES] = a.astype(o_ref.dtype)
            return carry
        lax.fori_loop(0, tm // rc, body, 0)

    @pl.when(j == 0)
    def _():
        run(True, q_scale)

    @pl.when(j == 1)
    def _():
        run(True, 1.0)

    @pl.when(j >= 2)
    def _():
        run(False, 1.0)


def _inproj(x2, g, w_bf, tabs, seq, dh):
    t, d = x2.shape
    n = w_bf.shape[1]
    tm, tn = seq, d
    rc = min(PROJ_ROW_CHUNK, tm)
    cos_t, s1_t, s2_t = tabs
    tab_spec = pl.BlockSpec((seq, LANES), lambda i, j: (0, 0))
    return pl.pallas_call(
        functools.partial(_inproj_kernel, rc=rc, half=dh // 8, q_scale=dh ** -0.5 * math.log2(math.e)),
        grid=(t // tm, n // tn),
        in_specs=[pl.BlockSpec((tm, d), lambda i, j: (i, 0)),
                  pl.BlockSpec((1, d), lambda i, j: (0, 0)),
                  pl.BlockSpec((d, tn), lambda i, j: (0, j)),
                  tab_spec, tab_spec, tab_spec],
        out_specs=pl.BlockSpec((tm, tn), lambda i, j: (i, j)),
        out_shape=jax.ShapeDtypeStruct((t, n), BF16),
        scratch_shapes=[pltpu.VMEM((tm, d), BF16)],
        compiler_params=_params(("parallel", "arbitrary")),
        name="inproj",
    )(x2, g, w_bf, cos_t, s1_t, s2_t)


def _rotary_tables(seq, dh):
    rot = dh // 4
    half = rot // 2
    inv = jnp.float32(ROPE_THETA) ** (-jnp.arange(0, rot, 2, dtype=F32) / rot)
    ang = jnp.arange(seq, dtype=F32)[:, None] * inv[None, :]
    cos, sin = jnp.cos(ang), jnp.sin(ang)
    lane = jnp.arange(LANES) % dh
    idx = lane % half
    c_t = jnp.where(lane < rot, cos[:, idx], 1.0)
    s1_t = jnp.where((lane >= half) & (lane < rot), sin[:, idx], 0.0)
    s2_t = jnp.where(lane < half, -sin[:, idx], 0.0)
    return c_t.astype(F32), s1_t.astype(F32), s2_t.astype(F32)


def _attn_kernel(q_ref, k_ref, v_ref, lq1_ref, lk1_ref, lq2_ref, lk2_ref, sub_ref, o_ref,
                 s_ref, p_ref, *, dh, lambda_init, rc, rp):
    tq = q_ref.shape[0]
    n_chunks = tq // rc
    lane = lax.broadcasted_iota(I32, (1, LANES), 1)
    m1 = jnp.where(lane < dh, 1.0, 0.0).astype(BF16)
    m2 = jnp.where(lane >= dh, 1.0, 0.0).astype(BF16)
    lam = (jnp.exp(jnp.sum(lq1_ref[...] * lk1_ref[...], axis=-1, keepdims=True))
           - jnp.exp(jnp.sum(lq2_ref[...] * lk2_ref[...], axis=-1, keepdims=True)) + lambda_init)

    def scores(c):
        qc = q_ref[c * rc:(c + 1) * rc, :]
        qq = jnp.concatenate([qc * m1, qc * m2], axis=0)
        s_ref[c % 2] = lax.dot_general(qq, k_ref[...], NT_DIMS, preferred_element_type=F32)

    def softmax_pv(c, j):
        r0 = j * rp
        s1 = s_ref[c % 2, r0:r0 + rp, :]
        s2 = s_ref[c % 2, rc + r0:rc + r0 + rp, :]
        e1 = jnp.exp2(s1 - jnp.max(s1, axis=-1, keepdims=True))
        e2 = jnp.exp2(s2 - jnp.max(s2, axis=-1, keepdims=True))
        l1 = jnp.sum(e1, axis=-1, keepdims=True)
        l2 = jnp.sum(e2, axis=-1, keepdims=True)
        slot = (c * (rc // rp) + j) % 2
        p_ref[slot] = (e1 - e2 * (lam * l1 / l2)).astype(BF16)
        od = jnp.dot(p_ref[slot], v_ref[...], preferred_element_type=F32) / l1
        od = _rms(od, sub_ref[...]) * (1.0 - lambda_init)
        o_ref[c * rc + r0:c * rc + r0 + rp, :] = od.astype(o_ref.dtype)

    scores(0)
    for c in range(n_chunks):
        if c + 1 < n_chunks:
            scores(c + 1)
        for j in range(rc // rp):
            softmax_pv(c, j)


def _attention(proj, lq1, lk1, lq2, lk2, sub, batch, seq, d, lambda_init):
    t = proj.shape[0]
    nh = d // LANES
    dh = LANES // 2
    tq = min(Q_TILE, seq)
    nq = seq // tq
    rc = min(ATTN_SCORE_CHUNK, tq)
    rp = min(ATTN_SOFTMAX_CHUNK, rc)
    vec = pl.BlockSpec((1, dh), lambda b, h, qi: (0, 0))
    return pl.pallas_call(
        functools.partial(_attn_kernel, dh=dh, lambda_init=lambda_init, rc=rc, rp=rp),
        grid=(batch, nh, nq),
        scratch_shapes=[pltpu.VMEM((2, 2 * rc, seq), F32), pltpu.VMEM((2, rp, seq), BF16)],
        in_specs=[pl.BlockSpec((tq, LANES), lambda b, h, qi: (b * nq + qi, h)),
                  pl.BlockSpec((seq, LANES), lambda b, h, qi: (b, nh + h)),
                  pl.BlockSpec((seq, LANES), lambda b, h, qi: (b, 2 * nh + h)),
                  vec, vec, vec, vec,
                  pl.BlockSpec((1, LANES), lambda b, h, qi: (0, 0))],
        out_specs=pl.BlockSpec((tq, LANES), lambda b, h, qi: (b * nq + qi, h)),
        out_shape=jax.ShapeDtypeStruct((t, d), BF16),
        compiler_params=_params(("parallel", "parallel", "arbitrary")),
        name="diff_attn",
    )(proj, proj, proj, lq1, lk1, lq2, lk2, sub)


def _mixout_kernel(x_ref, ya_ref, cb_ref, cc_ref, cx_ref, g0_ref, g1_ref, pc_ref, px_ref, nc_ref, nx_ref,
                   cw_ref, wb0_ref, wb1_ref, wo_ref, o_ref, *, tiles_per_seq):
    i = pl.program_id(0)
    tm = x_ref.shape[0]
    u = cc_ref[...].astype(F32) * cx_ref[...].astype(F32)
    hr = pc_ref.shape[0]
    u_prev = pc_ref[hr - 1:hr, :].astype(F32) * px_ref[hr - 1:hr, :].astype(F32)
    u_next = nc_ref[0:1, :].astype(F32) * nx_ref[0:1, :].astype(F32)
    pos = i % tiles_per_seq
    u_prev = jnp.where(pos == 0, 0.0, u_prev)
    u_next = jnp.where(pos == tiles_per_seq - 1, 0.0, u_next)
    row = lax.broadcasted_iota(I32, u.shape, 0)
    u_up = jnp.where(row == 0, u_prev, pltpu.roll(u, 1, 0))
    u_dn = jnp.where(row == tm - 1, u_next, pltpu.roll(u, tm - 1, 0))
    conv = cw_ref[0:1, :] * u_up + cw_ref[1:2, :] * u + cw_ref[2:3, :] * u_dn
    y_conv = (cb_ref[...].astype(F32) * conv).astype(BF16)
    br0 = jnp.dot(ya_ref[...], wb0_ref[...], preferred_element_type=F32)
    br1 = jnp.dot(y_conv, wb1_ref[...], preferred_element_type=F32)
    z = jax.nn.sigmoid(g0_ref[...].astype(F32)) * br0 + jax.nn.sigmoid(g1_ref[...].astype(F32)) * br1
    o_ref[...] = x_ref[...] + jnp.dot(z.astype(BF16), wo_ref[...], preferred_element_type=F32)


def _mixer_out(x2, y_attn, proj, conv_w, wb0, wb1, wo, seq):
    t, d = x2.shape
    tm = min(ROW_TILE, seq)
    tps = seq // tm
    hr = BF16_SUBLANES
    nhb = t // hr
    row = lambda c: pl.BlockSpec((tm, d), lambda i: (i, c))
    prev = lambda c: pl.BlockSpec((hr, d), lambda i: (jnp.maximum(i * (tm // hr) - 1, 0), c))
    nxt = lambda c: pl.BlockSpec((hr, d), lambda i: (jnp.minimum((i + 1) * (tm // hr), nhb - 1), c))
    full = lambda r: pl.BlockSpec((r, d), lambda i: (0, 0))
    return pl.pallas_call(
        functools.partial(_mixout_kernel, tiles_per_seq=tps),
        grid=(t // tm,),
        in_specs=[row(0), row(0), row(3), row(4), row(5), row(6), row(7),
                  prev(4), prev(5), nxt(4), nxt(5),
                  full(conv_w.shape[0]), full(d), full(d), full(d)],
        out_specs=row(0),
        out_shape=jax.ShapeDtypeStruct((t, d), F32),
        compiler_params=_params(("parallel",)),
        name="mixer_out",
    )(x2, y_attn, proj, proj, proj, proj, proj, proj, proj, proj, proj, conv_w, wb0, wb1, wo)


def _memkv_kernel(m_ref, g_ref, w_ref, o_ref):
    h = _rms(m_ref[...], g_ref[...]).astype(BF16)
    o_ref[...] = jnp.dot(h, w_ref[...], preferred_element_type=F32).astype(o_ref.dtype)


def _mem_kv(mem2, g, w_bf):
    depth, d, n = w_bf.shape
    rows = mem2.shape[0]
    tn = min(n, 1024)
    return pl.pallas_call(
        _memkv_kernel,
        grid=(depth, n // tn),
        in_specs=[pl.BlockSpec((rows, d), lambda l, j: (0, 0)),
                  pl.BlockSpec((1, d), lambda l, j: (0, 0)),
                  pl.BlockSpec((None, d, tn), lambda l, j: (l, 0, j))],
        out_specs=pl.BlockSpec((None, rows, tn), lambda l, j: (l, 0, j)),
        out_shape=jax.ShapeDtypeStruct((depth, rows, n), BF16),
        compiler_params=_params(("parallel", "parallel")),
        name="mem_kv",
    )(mem2, g, w_bf)


def _cross_kernel(x_ref, g_ref, wq_ref, kv_ref, wo_ref, gf_ref, wr_ref, o_ref, lg_ref, *, heads):
    x = x_ref[...]
    d = x.shape[1]
    hd = d // heads
    h = _rms(x, g_ref[...]).astype(BF16)
    q = (jnp.dot(h, wq_ref[...], preferred_element_type=F32) * (hd ** -0.5)).astype(BF16)
    outs = []
    for hh in range(heads):
        qh = q[:, hh * hd:(hh + 1) * hd]
        kh = kv_ref[:, hh * hd:(hh + 1) * hd]
        vh = kv_ref[:, d + hh * hd:d + (hh + 1) * hd]
        s = lax.dot_general(qh, kh, NT_DIMS, preferred_element_type=F32)
        p = jnp.exp(s - jnp.max(s, axis=-1, keepdims=True))
        l = jnp.sum(p, axis=-1, keepdims=True)
        outs.append((jnp.dot(p.astype(BF16), vh, preferred_element_type=F32) / l).astype(BF16))
    o = jnp.concatenate(outs, axis=1)
    x_new = x + jnp.dot(o, wo_ref[...], preferred_element_type=F32)
    o_ref[...] = x_new
    lg_ref[...] = jnp.dot(_rms(x_new, gf_ref[...]).astype(BF16), wr_ref[...], preferred_element_type=F32)


def _cross_attention(x2, g, wq, kv, layer, wo, g_ffn, w_r, seq, n_mem):
    t, d = x2.shape
    tm = min(ROW_TILE, seq)
    tps = seq // tm
    row = lambda w: pl.BlockSpec((tm, w), lambda i: (i, 0))
    const = lambda r, c: pl.BlockSpec((r, c), lambda i: (0, 0))
    return pl.pallas_call(
        functools.partial(_cross_kernel, heads=MEM_HEADS),
        grid=(t // tm,),
        in_specs=[row(d), const(1, d), const(d, d),
                  pl.BlockSpec((None, n_mem, 2 * d), lambda i: (layer, i // tps, 0)),
                  const(d, d), const(1, d), const(d, LANES)],
        out_specs=[row(d), row(LANES)],
        out_shape=[jax.ShapeDtypeStruct((t, d), F32), jax.ShapeDtypeStruct((t, LANES), F32)],
        compiler_params=_params(("parallel",)),
        name="cross_attn",
    )(x2, g, wq, kv, wo, g_ffn, w_r)


def _router_kernel(lg_ref, ri_ref, rw_ref, cnt_ref, carry_ref, *, sub):
    @pl.when(pl.program_id(0) == 0)
    def _():
        carry_ref[...] = jnp.zeros_like(carry_ref)

    lg = lg_ref[...]
    tm = lg.shape[0]
    lane = lax.broadcasted_iota(I32, lg.shape, 1).astype(F32)
    neg = -jnp.inf

    def first_argmax(v):
        m = jnp.max(v, axis=-1, keepdims=True)
        return m, jnp.min(jnp.where(v == m, lane, float(LANES)), axis=-1, keepdims=True)

    gmask = lane < N_GROUPS
    gl = jnp.where(gmask, lg, neg)
    gmax, gidx = first_argmax(gl)
    g_w = 1.0 / jnp.sum(jnp.where(gmask, jnp.exp(gl - gmax), 0.0), axis=-1, keepdims=True)

    lo = EXPERT_LANE0 + gidx * EXPERTS_PER_GROUP
    emask = (lane >= lo) & (lane < lo + EXPERTS_PER_GROUP)
    el = jnp.where(emask, lg, neg)
    mx1, i1 = first_argmax(el)
    mx2, i2 = first_argmax(jnp.where(lane == i1, neg, el))
    esum = jnp.sum(jnp.where(emask, jnp.exp(el - mx1), 0.0), axis=-1, keepdims=True)
    p1 = 1.0 / esum
    p2 = jnp.exp(mx2 - mx1) / esum
    den = p1 + p2
    w1 = g_w * (p1 / den)
    w2 = g_w * (p2 / den)

    memb = jnp.where((lane == i1) | (lane == i2), 1.0, 0.0)
    r_i = lax.broadcasted_iota(I32, (sub, sub), 0)
    c_i = lax.broadcasted_iota(I32, (sub, sub), 1)
    ltri = jnp.where(c_i < r_i, 1.0, 0.0).astype(BF16)
    tots = []
    carry = carry_ref[...]
    for r0 in range(0, tm, sub):
        mb = memb[r0:r0 + sub]
        tots.append(jnp.dot(ltri, mb.astype(BF16), preferred_element_type=F32) + carry)
        carry = carry + jnp.sum(mb, axis=0, keepdims=True)
    carry_ref[...] = carry
    tot = jnp.concatenate(tots, axis=0)
    rank1 = jnp.sum(jnp.where(lane == i1, tot, 0.0), axis=-1, keepdims=True).astype(I32)
    rank2 = jnp.sum(jnp.where(lane == i2, tot, 0.0), axis=-1, keepdims=True).astype(I32)

    e1 = (i1 - EXPERT_LANE0).astype(I32)
    e2 = (i2 - EXPERT_LANE0).astype(I32)
    ri_ref[...] = jnp.where(lane == 0, e1,
                            jnp.where(lane == 1, e2, jnp.where(lane == 2, rank1, jnp.where(lane == 3, rank2, 0))))
    rw_ref[...] = jnp.where(lane == 0, w1, jnp.where(lane == 1, w2, 0.0))
    cnt_ref[...] = jnp.broadcast_to(carry, cnt_ref.shape)


def _router(lg, seq):
    t = lg.shape[0]
    tm = min(ROUTER_TILE, seq)
    row = pl.BlockSpec((tm, LANES), lambda i: (i, 0))
    return pl.pallas_call(
        functools.partial(_router_kernel, sub=min(ROW_TILE, tm)),
        grid=(t // tm,),
        in_specs=[row],
        out_specs=[row, row, pl.BlockSpec((F32_SUBLANES, LANES), lambda i: (0, 0))],
        out_shape=[jax.ShapeDtypeStruct((t, LANES), I32),
                   jax.ShapeDtypeStruct((t, LANES), F32),
                   jax.ShapeDtypeStruct((F32_SUBLANES, LANES), F32)],
        scratch_shapes=[pltpu.VMEM((1, LANES), F32)],
        compiler_params=_params(("arbitrary",)),
        name="moe_router",
    )(lg)


def _dispatch_kernel(dest_ref, x_ref, g_ref, xb_in_ref, xb_ref, h_ref, sem):
    del xb_in_ref
    i = pl.program_id(0)
    tm = x_ref.shape[0]
    h_ref[...] = _rms(x_ref[...], g_ref[...])
    base = i * (TOP_K * tm)

    for t in range(tm):
        for k in range(TOP_K):
            dst = dest_ref[base + (TOP_K * t + k)]
            pltpu.make_async_copy(h_ref.at[pl.ds(t, 1), :], xb_ref.at[pl.ds(dst, 1), :], sem).start()
    for _ in range(TOP_K):
        pltpu.make_async_copy(h_ref, xb_ref.at[pl.ds(0, tm), :], sem).wait()


def _dispatch(dest, x2, g, xb, seq):
    t, d = x2.shape
    tm = min(ROW_TILE, seq)
    return pl.pallas_call(
        _dispatch_kernel,
        grid_spec=pltpu.PrefetchScalarGridSpec(
            num_scalar_prefetch=1,
            grid=(t // tm,),
            in_specs=[pl.BlockSpec((tm, d), lambda i, dest: (i, 0)),
                      pl.BlockSpec((1, d), lambda i, dest: (0, 0)),
                      pl.BlockSpec(memory_space=pl.ANY)],
            out_specs=pl.BlockSpec(memory_space=pl.ANY),
            scratch_shapes=[pltpu.VMEM((tm, d), F32), pltpu.SemaphoreType.DMA]),
        out_shape=jax.ShapeDtypeStruct(xb.shape, xb.dtype),
        input_output_aliases={3: 0},
        compiler_params=_params(("arbitrary",)),
        name="moe_dispatch",
    )(dest, x2, g, xb)


def _expert_kernel(be_ref, first_ref, slot_ref, nxt_ref, na_ref, xb_ref, wg_hbm, wu_hbm, wd_hbm, yb_ref,
                   wg_buf, wu_buf, wd_buf, wgb_ref, wub_ref, wdb_ref, sem, *, layer):
    i = pl.program_id(0)

    def fetch(e, slot):
        return (pltpu.make_async_copy(wg_hbm.at[layer, e], wg_buf.at[slot], sem.at[slot, 0]),
                pltpu.make_async_copy(wu_hbm.at[layer, e], wu_buf.at[slot], sem.at[slot, 1]),
                pltpu.make_async_copy(wd_hbm.at[layer, e], wd_buf.at[slot], sem.at[slot, 2]))

    @pl.when(i < na_ref[0])
    def _():
        @pl.when(first_ref[i] == 1)
        def _():
            e, slot = be_ref[i], slot_ref[i]

            @pl.when(i == 0)
            def _():
                for cp in fetch(e, slot):
                    cp.start()

            for cp in fetch(e, slot):
                cp.wait()
            wgb_ref[...] = wg_buf[slot].astype(BF16)
            wub_ref[...] = wu_buf[slot].astype(BF16)
            wdb_ref[...] = wd_buf[slot].astype(BF16)

            @pl.when(nxt_ref[i] >= 0)
            def _():
                for cp in fetch(nxt_ref[i], 1 - slot):
                    cp.start()

        xb = xb_ref[...].astype(BF16)
        gate = jnp.dot(xb, wgb_ref[...], preferred_element_type=F32)
        up = jnp.dot(xb, wub_ref[...], preferred_element_type=F32)
        act = (jax.nn.silu(gate) * up).astype(BF16)
        yb_ref[...] = jnp.dot(act, wdb_ref[...], preferred_element_type=F32)

    @pl.when(i >= na_ref[0])
    def _():
        yb_ref[...] = jnp.zeros_like(yb_ref)


def _experts(plan, xb, w_gate, w_up, w_down, layer, blk):
    rows, d = xb.shape
    de = w_gate.shape[-1]
    nb = rows // blk

    def row_map(i, be, first, slot, nxt, na):
        return (jnp.minimum(i, jnp.maximum(na[0] - 1, 0)), 0)

    hbm = pl.BlockSpec(memory_space=pl.ANY)
    return pl.pallas_call(
        functools.partial(_expert_kernel, layer=layer),
        grid_spec=pltpu.PrefetchScalarGridSpec(
            num_scalar_prefetch=5,
            grid=(nb,),
            in_specs=[pl.BlockSpec((blk, d), row_map), hbm, hbm, hbm],
            out_specs=pl.BlockSpec((blk, d), lambda i, *_: (i, 0)),
            scratch_shapes=[pltpu.VMEM((2, d, de), F32), pltpu.VMEM((2, d, de), F32), pltpu.VMEM((2, de, d), F32),
                            pltpu.VMEM((d, de), BF16), pltpu.VMEM((d, de), BF16), pltpu.VMEM((de, d), BF16),
                            pltpu.SemaphoreType.DMA((2, 3))]),
        out_shape=jax.ShapeDtypeStruct((rows, d), F32),
        compiler_params=_params(("arbitrary",)),
        name="moe_experts",
    )(*plan, xb, w_gate, w_up, w_down)


def _combine_kernel(dest_ref, x_ref, rw_ref, yb_ref, gf_ref, o_ref, y_ref, sem, *, final_norm):
    i = pl.program_id(0)
    tm = x_ref.shape[0]
    base = i * (TOP_K * tm)

    for t in range(tm):
        for k in range(TOP_K):
            src = dest_ref[base + (TOP_K * t + k)]
            pltpu.make_async_copy(yb_ref.at[pl.ds(src, 1), :], y_ref.at[k, pl.ds(t, 1), :], sem).start()
    for k in range(TOP_K):
        pltpu.make_async_copy(yb_ref.at[pl.ds(0, tm), :], y_ref.at[k], sem).wait()
    w = rw_ref[...]
    out = x_ref[...]
    for k in range(TOP_K):
        out = out + w[:, k:k + 1] * y_ref[k]
    if final_norm:
        out = _rms(out, gf_ref[...])
    o_ref[...] = out


def _combine(dest, x2, rw, yb, g_final, seq, final_norm):
    t, d = x2.shape
    tm = min(ROW_TILE, seq)
    return pl.pallas_call(
        functools.partial(_combine_kernel, final_norm=final_norm),
        grid_spec=pltpu.PrefetchScalarGridSpec(
            num_scalar_prefetch=1,
            grid=(t // tm,),
            in_specs=[pl.BlockSpec((tm, d), lambda i, dest: (i, 0)),
                      pl.BlockSpec((tm, LANES), lambda i, dest: (i, 0)),
                      pl.BlockSpec(memory_space=pl.ANY),
                      pl.BlockSpec((1, d), lambda i, dest: (0, 0))],
            out_specs=pl.BlockSpec((tm, d), lambda i, dest: (i, 0)),
            scratch_shapes=[pltpu.VMEM((TOP_K, tm, d), F32), pltpu.SemaphoreType.DMA]),
        out_shape=jax.ShapeDtypeStruct((t, d), F32),
        compiler_params=_params(("arbitrary",)),
        name="moe_combine",
    )(dest, x2, rw, yb, g_final)


def _route_plan(ri, counts_f, blk, n_blocks):
    cnt = counts_f[0, EXPERT_LANE0:EXPERT_LANE0 + N_EXPERTS].astype(I32)
    nblk = (cnt + blk - 1) // blk
    cum = jnp.cumsum(nblk)
    pad_start = (cum - nblk) * blk
    e = ri[:, 0:TOP_K]
    rank = ri[:, TOP_K:2 * TOP_K]
    onehot = e[:, :, None] == jnp.arange(N_EXPERTS, dtype=I32)[None, None, :]
    dest = rank + jnp.sum(jnp.where(onehot, pad_start[None, None, :], 0), axis=-1)
    blocks = jnp.arange(n_blocks, dtype=I32)
    block_expert = jnp.minimum(jnp.sum((cum[None, :] <= blocks[:, None]).astype(I32), axis=-1), N_EXPERTS - 1)
    n_active = cum[-1:].astype(I32)
    experts = jnp.arange(N_EXPERTS, dtype=I32)
    is_block_expert = block_expert[:, None] == experts[None, :]
    first = jnp.sum(jnp.where(is_block_expert, ((cum - nblk)[None, :] == blocks[:, None]).astype(I32), 0), axis=-1)
    used = (nblk > 0).astype(I32)
    slot_e = (jnp.cumsum(used) - used) % 2
    later = (experts[None, :] > experts[:, None]) & (nblk[None, :] > 0)
    next_e = jnp.min(jnp.where(later, experts[None, :], N_EXPERTS), axis=-1)
    next_e = jnp.where(next_e == N_EXPERTS, -1, next_e)
    slot = jnp.sum(jnp.where(is_block_expert, slot_e[None, :], 0), axis=-1)
    nxt = jnp.sum(jnp.where(is_block_expert, next_e[None, :], 0), axis=-1)
    plan = (block_expert, first.astype(I32), slot.astype(I32), nxt.astype(I32), n_active)
    return dest.reshape(-1).astype(I32), plan


def kernel(x, mem, norm_mix, w_in, lambda_q1, lambda_k1, lambda_q2, lambda_k2, subln, conv_w, w_branch, w_o,
           norm_cross, norm_mem, w_cq, w_ckv, w_co, norm_ffn, w_router_group, w_router_expert, w_exp_gate,
           w_exp_up, w_exp_down, norm_final):
    batch, seq, d = x.shape
    depth = w_in.shape[0]
    n_mem = mem.shape[1]
    t = batch * seq
    dh = d // (2 * DA_HEADS)
    assert 2 * dh == LANES and d % LANES == 0 and w_in.shape[2] == 8 * d
    assert w_router_group.shape[2] == N_GROUPS and w_router_expert.shape[2] == N_EXPERTS
    blk = min(EXPERT_BLOCK, seq)
    n_blocks = (t * TOP_K) // blk + N_EXPERTS

    tabs = _rotary_tables(seq, dh)
    x2 = x.reshape(t, d)
    kv = _mem_kv(mem.reshape(batch * n_mem, d), norm_mem.reshape(1, d), w_ckv.astype(BF16))
    w_r = jnp.concatenate([w_router_group, w_router_expert,
                           jnp.zeros((depth, d, LANES - N_GROUPS - N_EXPERTS), F32)], axis=-1).astype(BF16)
    xb = jnp.zeros((n_blocks * blk, d), F32)

    for l in range(depth):
        lambda_init = 0.8 - 0.6 * math.exp(-0.3 * l)
        proj = _inproj(x2, norm_mix[l].reshape(1, d), w_in[l].astype(BF16), tabs, seq, dh)
        y_attn = _attention(proj, lambda_q1[l].reshape(1, dh), lambda_k1[l].reshape(1, dh),
                            lambda_q2[l].reshape(1, dh), lambda_k2[l].reshape(1, dh),
                            subln[l].reshape(1, LANES), batch, seq, d, lambda_init)
        x2 = _mixer_out(x2, y_attn, proj, conv_w[l], w_branch[l, 0].astype(BF16), w_branch[l, 1].astype(BF16),
                        w_o[l].astype(BF16), seq)
        g_ffn = norm_ffn[l].reshape(1, d)
        x2, logits = _cross_attention(x2, norm_cross[l].reshape(1, d), w_cq[l].astype(BF16), kv, l,
                                      w_co[l].astype(BF16), g_ffn, w_r[l], seq, n_mem)
        ri, rw, counts = _router(logits, seq)
        dest, plan = _route_plan(ri, counts, blk, n_blocks)
        xb = _dispatch(dest, x2, g_ffn, xb, seq)
        yb = _experts(plan, xb, w_exp_gate, w_exp_up, w_exp_down, l, blk)
        x2 = _combine(dest, x2, rw, yb, norm_final.reshape(1, d), seq, l == depth - 1)
    return x2.reshape(batch, seq, d)
```

```python
import functools
import math

import jax
import jax.numpy as jnp
from jax import lax
from jax.experimental import pallas as pl
from jax.experimental.pallas import tpu as pltpu
from jax.experimental.pallas import tpu_sc as plsc

EPS = 1e-6
DA_HEADS = 8
MEM_HEADS = 4
N_GROUPS = 4
EXPERTS_PER_GROUP = 8
N_EXPERTS = N_GROUPS * EXPERTS_PER_GROUP
TOP_K = 2
ROPE_THETA = 500000.0
LANES = 128
F32_SUBLANES = 8
BF16_SUBLANES = 16
EXPERT_LANE0 = N_GROUPS
ROW_TILE = 512
ROUTER_TILE = 2048
Q_TILE = 2048
EXPERT_BLOCK = 256
PROJ_ROW_CHUNK = 1024
ATTN_SCORE_CHUNK = 128
ATTN_SOFTMAX_CHUNK = 128
SC_CORES = 2
SC_WORKERS = 32
SC_CHUNK_ROWS = 32
VMEM_LIMIT = 56 * 1024 * 1024

F32 = jnp.float32
BF16 = jnp.bfloat16
I32 = jnp.int32
NT_DIMS = (((1,), (1,)), ((), ()))


def _rms(x, g):
    return x * lax.rsqrt(jnp.mean(x * x, axis=-1, keepdims=True) + EPS) * g


def _params(sem, vmem=VMEM_LIMIT):
    return pltpu.CompilerParams(dimension_semantics=sem, vmem_limit_bytes=vmem)


def _inproj_kernel(x_ref, g_ref, w_ref, cos_ref, s1_ref, s2_ref, o_ref, h_ref, *, rc, half, q_scale):
    j = pl.program_id(1)
    tm, tn = o_ref.shape

    @pl.when(j == 0)
    def _():
        h_ref[...] = _rms(x_ref[...], g_ref[...]).astype(BF16)

    def run(rot, scale):
        def body(r, carry):
            r0 = pl.multiple_of(r * rc, rc)
            acc = jnp.dot(h_ref[pl.ds(r0, rc), :], w_ref[...], preferred_element_type=F32)
            if not rot:
                o_ref[pl.ds(r0, rc), :] = acc.astype(o_ref.dtype)
                return carry
            c = cos_ref[pl.ds(r0, rc), :]
            s1 = s1_ref[pl.ds(r0, rc), :]
            s2 = s2_ref[pl.ds(r0, rc), :]
            for cc in range(tn // LANES):
                a = acc[:, cc * LANES:(cc + 1) * LANES]
                a = a * c + pltpu.roll(a, half, 1) * s1 + pltpu.roll(a, LANES - half, 1) * s2
                if scale != 1.0:
                    a = a * scale
                o_ref[pl.ds(r0, rc), cc * LANES:(cc + 1) * LANES] = a.astype(o_ref.dtype)
            return carry
        lax.fori_loop(0, tm // rc, body, 0)

    @pl.when(j == 0)
    def _():
        run(True, q_scale)

    @pl.when(j == 1)
    def _():
        run(True, 1.0)

    @pl.when(j >= 2)
    def _():
        run(False, 1.0)


def _inproj(x2, g, w_bf, tabs, seq, dh):
    t, d = x2.shape
    n = w_bf.shape[1]
    tm, tn = seq, d
    rc = min(PROJ_ROW_CHUNK, tm)
    cos_t, s1_t, s2_t = tabs
    tab_spec = pl.BlockSpec((seq, LANES), lambda i, j: (0, 0))
    return pl.pallas_call(
        functools.partial(_inproj_kernel, rc=rc, half=dh // 8, q_scale=dh ** -0.5 * math.log2(math.e)),
        grid=(t // tm, n // tn),
        in_specs=[pl.BlockSpec((tm, d), lambda i, j: (i, 0)),
                  pl.BlockSpec((1, d), lambda i, j: (0, 0)),
                  pl.BlockSpec((d, tn), lambda i, j: (0, j)),
                  tab_spec, tab_spec, tab_spec],
        out_specs=pl.BlockSpec((tm, tn), lambda i, j: (i, j)),
        out_shape=jax.ShapeDtypeStruct((t, n), BF16),
        scratch_shapes=[pltpu.VMEM((tm, d), BF16)],
        compiler_params=_params(("parallel", "arbitrary")),
        name="inproj",
    )(x2, g, w_bf, cos_t, s1_t, s2_t)


def _rotary_tables(seq, dh):
    rot = dh // 4
    half = rot // 2
    inv = jnp.float32(ROPE_THETA) ** (-jnp.arange(0, rot, 2, dtype=F32) / rot)
    ang = jnp.arange(seq, dtype=F32)[:, None] * inv[None, :]
    cos, sin = jnp.cos(ang), jnp.sin(ang)
    lane = jnp.arange(LANES) % dh
    idx = lane % half
    c_t = jnp.where(lane < rot, cos[:, idx], 1.0)
    s1_t = jnp.where((lane >= half) & (lane < rot), sin[:, idx], 0.0)
    s2_t = jnp.where(lane < half, -sin[:, idx], 0.0)
    return c_t.astype(F32), s1_t.astype(F32), s2_t.astype(F32)


def _attn_kernel(q_ref, k_ref, v_ref, lq1_ref, lk1_ref, lq2_ref, lk2_ref, sub_ref, o_ref,
                 s_ref, p_ref, *, dh, lambda_init, rc, rp):
    tq = q_ref.shape[0]
    n_chunks = tq // rc
    lane = lax.broadcasted_iota(I32, (1, LANES), 1)
    m1 = jnp.where(lane < dh, 1.0, 0.0).astype(BF16)
    m2 = jnp.where(lane >= dh, 1.0, 0.0).astype(BF16)
    lam = (jnp.exp(jnp.sum(lq1_ref[...] * lk1_ref[...], axis=-1, keepdims=True))
           - jnp.exp(jnp.sum(lq2_ref[...] * lk2_ref[...], axis=-1, keepdims=True)) + lambda_init)

    def scores(c):
        qc = q_ref[c * rc:(c + 1) * rc, :]
        qq = jnp.concatenate([qc * m1, qc * m2], axis=0)
        s_ref[c % 2] = lax.dot_general(qq, k_ref[...], NT_DIMS, preferred_element_type=F32)

    def softmax_pv(c, j):
        r0 = j * rp
        s1 = s_ref[c % 2, r0:r0 + rp, :]
        s2 = s_ref[c % 2, rc + r0:rc + r0 + rp, :]
        e1 = jnp.exp2(s1 - jnp.max(s1, axis=-1, keepdims=True))
        e2 = jnp.exp2(s2 - jnp.max(s2, axis=-1, keepdims=True))
        l1 = jnp.sum(e1, axis=-1, keepdims=True)
        l2 = jnp.sum(e2, axis=-1, keepdims=True)
        slot = (c * (rc // rp) + j) % 2
        p_ref[slot] = (e1 - e2 * (lam * l1 / l2)).astype(BF16)
        od = jnp.dot(p_ref[slot], v_ref[...], preferred_element_type=F32) / l1
        od = _rms(od, sub_ref[...]) * (1.0 - lambda_init)
        o_ref[c * rc + r0:c * rc + r0 + rp, :] = od.astype(o_ref.dtype)

    scores(0)
    for c in range(n_chunks):
        if c + 1 < n_chunks:
            scores(c + 1)
        for j in range(rc // rp):
            softmax_pv(c, j)


def _attention(proj, lq1, lk1, lq2, lk2, sub, batch, seq, d, lambda_init):
    t = proj.shape[0]
    nh = d // LANES
    dh = LANES // 2
    tq = min(Q_TILE, seq)
    nq = seq // tq
    rc = min(ATTN_SCORE_CHUNK, tq)
    rp = min(ATTN_SOFTMAX_CHUNK, rc)
    vec = pl.BlockSpec((1, dh), lambda b, h, qi: (0, 0))
    return pl.pallas_call(
        functools.partial(_attn_kernel, dh=dh, lambda_init=lambda_init, rc=rc, rp=rp),
        grid=(batch, nh, nq),
        scratch_shapes=[pltpu.VMEM((2, 2 * rc, seq), F32), pltpu.VMEM((2, rp, seq), BF16)],
        in_specs=[pl.BlockSpec((tq, LANES), lambda b, h, qi: (b * nq + qi, h)),
                  pl.BlockSpec((seq, LANES), lambda b, h, qi: (b, nh + h)),
                  pl.BlockSpec((seq, LANES), lambda b, h, qi: (b, 2 * nh + h)),
                  vec, vec, vec, vec,
                  pl.BlockSpec((1, LANES), lambda b, h, qi: (0, 0))],
        out_specs=pl.BlockSpec((tq, LANES), lambda b, h, qi: (b * nq + qi, h)),
        out_shape=jax.ShapeDtypeStruct((t, d), BF16),
        compiler_params=_params(("parallel", "parallel", "arbitrary")),
        name="diff_attn",
    )(proj, proj, proj, lq1, lk1, lq2, lk2, sub)


def _mixout_kernel(x_ref, ya_ref, cb_ref, cc_ref, cx_ref, g0_ref, g1_ref, pc_ref, px_ref, nc_ref, nx_ref,
                   cw_ref, wb0_ref, wb1_ref, wo_ref, o_ref, *, tiles_per_seq):
    i = pl.program_id(0)
    tm = x_ref.shape[0]
    u = cc_ref[...].astype(F32) * cx_ref[...].astype(F32)
    hr = pc_ref.shape[0]
    u_prev = pc_ref[hr - 1:hr, :].astype(F32) * px_ref[hr - 1:hr, :].astype(F32)
    u_next = nc_ref[0:1, :].astype(F32) * nx_ref[0:1, :].astype(F32)
    pos = i % tiles_per_seq
    u_prev = jnp.where(pos == 0, 0.0, u_prev)
    u_next = jnp.where(pos == tiles_per_seq - 1, 0.0, u_next)
    row = lax.broadcasted_iota(I32, u.shape, 0)
    u_up = jnp.where(row == 0, u_prev, pltpu.roll(u, 1, 0))
    u_dn = jnp.where(row == tm - 1, u_next, pltpu.roll(u, tm - 1, 0))
    conv = cw_ref[0:1, :] * u_up + cw_ref[1:2, :] * u + cw_ref[2:3, :] * u_dn
    y_conv = (cb_ref[...].astype(F32) * conv).astype(BF16)
    br0 = jnp.dot(ya_ref[...], wb0_ref[...], preferred_element_type=F32)
    br1 = jnp.dot(y_conv, wb1_ref[...], preferred_element_type=F32)
    z = jax.nn.sigmoid(g0_ref[...].astype(F32)) * br0 + jax.nn.sigmoid(g1_ref[...].astype(F32)) * br1
    o_ref[...] = x_ref[...] + jnp.dot(z.astype(BF16), wo_ref[...], preferred_element_type=F32)


def _mixer_out(x2, y_attn, proj, conv_w, wb0, wb1, wo, seq):
    t, d = x2.shape
    tm = min(ROW_TILE, seq)
    tps = seq // tm
    hr = BF16_SUBLANES
    nhb = t // hr
    row = lambda c: pl.BlockSpec((tm, d), lambda i: (i, c))
    prev = lambda c: pl.BlockSpec((hr, d), lambda i: (jnp.maximum(i * (tm // hr) - 1, 0), c))
    nxt = lambda c: pl.BlockSpec((hr, d), lambda i: (jnp.minimum((i + 1) * (tm // hr), nhb - 1), c))
    full = lambda r: pl.BlockSpec((r, d), lambda i: (0, 0))
    return pl.pallas_call(
        functools.partial(_mixout_kernel, tiles_per_seq=tps),
        grid=(t // tm,),
        in_specs=[row(0), row(0), row(3), row(4), row(5), row(6), row(7),
                  prev(4), prev(5), nxt(4), nxt(5),
                  full(conv_w.shape[0]), full(d), full(d), full(d)],
        out_specs=row(0),
        out_shape=jax.ShapeDtypeStruct((t, d), F32),
        compiler_params=_params(("parallel",)),
        name="mixer_out",
    )(x2, y_attn, proj, proj, proj, proj, proj, proj, proj, proj, proj, conv_w, wb0, wb1, wo)


def _memkv_kernel(m_ref, g_ref, w_ref, o_ref):
    h = _rms(m_ref[...], g_ref[...]).astype(BF16)
    o_ref[...] = jnp.dot(h, w_ref[...], preferred_element_type=F32).astype(o_ref.dtype)


def _mem_kv(mem2, g, w_bf):
    depth, d, n = w_bf.shape
    rows = mem2.shape[0]
    tn = min(n, 1024)
    return pl.pallas_call(
        _memkv_kernel,
        grid=(depth, n // tn),
        in_specs=[pl.BlockSpec((rows, d), lambda l, j: (0, 0)),
                  pl.BlockSpec((1, d), lambda l, j: (0, 0)),
                  pl.BlockSpec((None, d, tn), lambda l, j: (l, 0, j))],
        out_specs=pl.BlockSpec((None, rows, tn), lambda l, j: (l, 0, j)),
        out_shape=jax.ShapeDtypeStruct((depth, rows, n), BF16),
        compiler_params=_params(("parallel", "parallel")),
        name="mem_kv",
    )(mem2, g, w_bf)


def _cross_kernel(x_ref, g_ref, wq_ref, kv_ref, wo_ref, gf_ref, wr_ref, o_ref, lg_ref, *, heads):
    x = x_ref[...]
    d = x.shape[1]
    hd = d // heads
    h = _rms(x, g_ref[...]).astype(BF16)
    q = (jnp.dot(h, wq_ref[...], preferred_element_type=F32) * (hd ** -0.5)).astype(BF16)
    outs = []
    for hh in range(heads):
        qh = q[:, hh * hd:(hh + 1) * hd]
        kh = kv_ref[:, hh * hd:(hh + 1) * hd]
        vh = kv_ref[:, d + hh * hd:d + (hh + 1) * hd]
        s = lax.dot_general(qh, kh, NT_DIMS, preferred_element_type=F32)
        p = jnp.exp(s - jnp.max(s, axis=-1, keepdims=True))
        l = jnp.sum(p, axis=-1, keepdims=True)
        outs.append((jnp.dot(p.astype(BF16), vh, preferred_element_type=F32) / l).astype(BF16))
    o = jnp.concatenate(outs, axis=1)
    x_new = x + jnp.dot(o, wo_ref[...], preferred_element_type=F32)
    o_ref[...] = x_new
    lg_ref[...] = jnp.dot(_rms(x_new, gf_ref[...]).astype(BF16), wr_ref[...], preferred_element_type=F32)


def _cross_attention(x2, g, wq, kv, layer, wo, g_ffn, w_r, seq, n_mem):
    t, d = x2.shape
    tm = min(ROW_TILE, seq)
    tps = seq // tm
    row = lambda w: pl.BlockSpec((tm, w), lambda i: (i, 0))
    const = lambda r, c: pl.BlockSpec((r, c), lambda i: (0, 0))
    return pl.pallas_call(
        functools.partial(_cross_kernel, heads=MEM_HEADS),
        grid=(t // tm,),
        in_specs=[row(d), const(1, d), const(d, d),
                  pl.BlockSpec((None, n_mem, 2 * d), lambda i: (layer, i // tps, 0)),
                  const(d, d), const(1, d), const(d, LANES)],
        out_specs=[row(d), row(LANES)],
        out_shape=[jax.ShapeDtypeStruct((t, d), F32), jax.ShapeDtypeStruct((t, LANES), F32)],
        compiler_params=_params(("parallel",)),
        name="cross_attn",
    )(x2, g, wq, kv, wo, g_ffn, w_r)


def _router_kernel(lg_ref, ri_ref, rw_ref, cnt_ref, carry_ref, *, sub):
    @pl.when(pl.program_id(0) == 0)
    def _():
        carry_ref[...] = jnp.zeros_like(carry_ref)

    lg = lg_ref[...]
    tm = lg.shape[0]
    lane = lax.broadcasted_iota(I32, lg.shape, 1).astype(F32)
    neg = -jnp.inf

    def first_argmax(v):
        m = jnp.max(v, axis=-1, keepdims=True)
        return m, jnp.min(jnp.where(v == m, lane, float(LANES)), axis=-1, keepdims=True)

    gmask = lane < N_GROUPS
    gl = jnp.where(gmask, lg, neg)
    gmax, gidx = first_argmax(gl)
    g_w = 1.0 / jnp.sum(jnp.where(gmask, jnp.exp(gl - gmax), 0.0), axis=-1, keepdims=True)

    lo = EXPERT_LANE0 + gidx * EXPERTS_PER_GROUP
    emask = (lane >= lo) & (lane < lo + EXPERTS_PER_GROUP)
    el = jnp.where(emask, lg, neg)
    mx1, i1 = first_argmax(el)
    mx2, i2 = first_argmax(jnp.where(lane == i1, neg, el))
    esum = jnp.sum(jnp.where(emask, jnp.exp(el - mx1), 0.0), axis=-1, keepdims=True)
    p1 = 1.0 / esum
    p2 = jnp.exp(mx2 - mx1) / esum
    den = p1 + p2
    w1 = g_w * (p1 / den)
    w2 = g_w * (p2 / den)

    memb = jnp.where((lane == i1) | (lane == i2), 1.0, 0.0)
    r_i = lax.broadcasted_iota(I32, (sub, sub), 0)
    c_i = lax.broadcasted_iota(I32, (sub, sub), 1)
    ltri = jnp.where(c_i < r_i, 1.0, 0.0).astype(BF16)
    tots = []
    carry = carry_ref[...]
    for r0 in range(0, tm, sub):
        mb = memb[r0:r0 + sub]
        tots.append(jnp.dot(ltri, mb.astype(BF16), preferred_element_type=F32) + carry)
        carry = carry + jnp.sum(mb, axis=0, keepdims=True)
    carry_ref[...] = carry
    tot = jnp.concatenate(tots, axis=0)
    rank1 = jnp.sum(jnp.where(lane == i1, tot, 0.0), axis=-1, keepdims=True).astype(I32)
    rank2 = jnp.sum(jnp.where(lane == i2, tot, 0.0), axis=-1, keepdims=True).astype(I32)

    e1 = (i1 - EXPERT_LANE0).astype(I32)
    e2 = (i2 - EXPERT_LANE0).astype(I32)
    ri_ref[...] = jnp.where(lane == 0, e1,
                            jnp.where(lane == 1, e2, jnp.where(lane == 2, rank1, jnp.where(lane == 3, rank2, 0))))
    rw_ref[...] = jnp.where(lane == 0, w1, jnp.where(lane == 1, w2, 0.0))
    cnt_ref[...] = jnp.broadcast_to(carry, cnt_ref.shape)


def _router(lg, seq):
    t = lg.shape[0]
    tm = min(ROUTER_TILE, seq)
    row = pl.BlockSpec((tm, LANES), lambda i: (i, 0))
    return pl.pallas_call(
        functools.partial(_router_kernel, sub=min(ROW_TILE, tm)),
        grid=(t // tm,),
        in_specs=[row],
        out_specs=[row, row, pl.BlockSpec((F32_SUBLANES, LANES), lambda i: (0, 0))],
        out_shape=[jax.ShapeDtypeStruct((t, LANES), I32),
                   jax.ShapeDtypeStruct((t, LANES), F32),
                   jax.ShapeDtypeStruct((F32_SUBLANES, LANES), F32)],
        scratch_shapes=[pltpu.VMEM((1, LANES), F32)],
        compiler_params=_params(("arbitrary",)),
        name="moe_router",
    )(lg)


def _dispatch_kernel(dest_ref, x_ref, g_ref, xb_in_ref, xb_ref, h_ref, sem):
    del xb_in_ref
    i = pl.program_id(0)
    tm = x_ref.shape[0]
    h_ref[...] = _rms(x_ref[...], g_ref[...])
    base = i * (TOP_K * tm)

    for t in range(tm):
        for k in range(TOP_K):
            dst = dest_ref[base + (TOP_K * t + k)]
            pltpu.make_async_copy(h_ref.at[pl.ds(t, 1), :], xb_ref.at[pl.ds(dst, 1), :], sem).start()
    for _ in range(TOP_K):
        pltpu.make_async_copy(h_ref, xb_ref.at[pl.ds(0, tm), :], sem).wait()


def _dispatch(dest, x2, g, xb, seq):
    t, d = x2.shape
    tm = min(ROW_TILE, seq)
    return pl.pallas_call(
        _dispatch_kernel,
        grid_spec=pltpu.PrefetchScalarGridSpec(
            num_scalar_prefetch=1,
            grid=(t // tm,),
            in_specs=[pl.BlockSpec((tm, d), lambda i, dest: (i, 0)),
                      pl.BlockSpec((1, d), lambda i, dest: (0, 0)),
                      pl.BlockSpec(memory_space=pl.ANY)],
            out_specs=pl.BlockSpec(memory_space=pl.ANY),
            scratch_shapes=[pltpu.VMEM((tm, d), F32), pltpu.SemaphoreType.DMA]),
        out_shape=jax.ShapeDtypeStruct(xb.shape, xb.dtype),
        input_output_aliases={3: 0},
        compiler_params=_params(("arbitrary",)),
        name="moe_dispatch",
    )(dest, x2, g, xb)


def _expert_kernel(be_ref, first_ref, slot_ref, nxt_ref, na_ref, xb_ref, wg_hbm, wu_hbm, wd_hbm, yb_ref,
                   wg_buf, wu_buf, wd_buf, wgb_ref, wub_ref, wdb_ref, sem, *, layer):
    i = pl.program_id(0)

    def fetch(e, slot):
        return (pltpu.make_async_copy(wg_hbm.at[layer, e], wg_buf.at[slot], sem.at[slot, 0]),
                pltpu.make_async_copy(wu_hbm.at[layer, e], wu_buf.at[slot], sem.at[slot, 1]),
                pltpu.make_async_copy(wd_hbm.at[layer, e], wd_buf.at[slot], sem.at[slot, 2]))

    @pl.when(i < na_ref[0])
    def _():
        @pl.when(first_ref[i] == 1)
        def _():
            e, slot = be_ref[i], slot_ref[i]

            @pl.when(i == 0)
            def _():
                for cp in fetch(e, slot):
                    cp.start()

            for cp in fetch(e, slot):
                cp.wait()
            wgb_ref[...] = wg_buf[slot].astype(BF16)
            wub_ref[...] = wu_buf[slot].astype(BF16)
            wdb_ref[...] = wd_buf[slot].astype(BF16)

            @pl.when(nxt_ref[i] >= 0)
            def _():
                for cp in fetch(nxt_ref[i], 1 - slot):
                    cp.start()

        xb = xb_ref[...].astype(BF16)
        gate = jnp.dot(xb, wgb_ref[...], preferred_element_type=F32)
        up = jnp.dot(xb, wub_ref[...], preferred_element_type=F32)
        act = (jax.nn.silu(gate) * up).astype(BF16)
        yb_ref[...] = jnp.dot(act, wdb_ref[...], preferred_element_type=F32)

    @pl.when(i >= na_ref[0])
    def _():
        yb_ref[...] = jnp.zeros_like(yb_ref)


def _experts(plan, xb, w_gate, w_up, w_down, layer, blk):
    rows, d = xb.shape
    de = w_gate.shape[-1]
    nb = rows // blk

    def row_map(i, be, first, slot, nxt, na):
        return (jnp.minimum(i, jnp.maximum(na[0] - 1, 0)), 0)

    hbm = pl.BlockSpec(memory_space=pl.ANY)
    return pl.pallas_call(
        functools.partial(_expert_kernel, layer=layer),
        grid_spec=pltpu.PrefetchScalarGridSpec(
            num_scalar_prefetch=5,
            grid=(nb,),
            in_specs=[pl.BlockSpec((blk, d), row_map), hbm, hbm, hbm],
            out_specs=pl.BlockSpec((blk, d), lambda i, *_: (i, 0)),
            scratch_shapes=[pltpu.VMEM((2, d, de), F32), pltpu.VMEM((2, d, de), F32), pltpu.VMEM((2, de, d), F32),
                            pltpu.VMEM((d, de), BF16), pltpu.VMEM((d, de), BF16), pltpu.VMEM((de, d), BF16),
                            pltpu.SemaphoreType.DMA((2, 3))]),
        out_shape=jax.ShapeDtypeStruct((rows, d), F32),
        compiler_params=_params(("arbitrary",)),
        name="moe_experts",
    )(*plan, xb, w_gate, w_up, w_down)


def _combine_kernel(dest_ref, x_ref, rw_ref, yb_ref, gf_ref, o_ref, y_ref, sem, *, final_norm):
    i = pl.program_id(0)
    tm = x_ref.shape[0]
    base = i * (TOP_K * tm)

    for t in range(tm):
        for k in range(TOP_K):
            src = dest_ref[base + (TOP_K * t + k)]
            pltpu.make_async_copy(yb_ref.at[pl.ds(src, 1), :], y_ref.at[k, pl.ds(t, 1), :], sem).start()
    for k in range(TOP_K):
        pltpu.make_async_copy(yb_ref.at[pl.ds(0, tm), :], y_ref.at[k], sem).wait()
    w = rw_ref[...]
    out = x_ref[...]
    for k in range(TOP_K):
        out = out + w[:, k:k + 1] * y_ref[k]
    if final_norm:
        out = _rms(out, gf_ref[...])
    o_ref[...] = out


def _combine(dest, x2, rw, yb, g_final, seq, final_norm):
    t, d = x2.shape
    tm = min(ROW_TILE, seq)
    return pl.pallas_call(
        functools.partial(_combine_kernel, final_norm=final_norm),
        grid_spec=pltpu.PrefetchScalarGridSpec(
            num_scalar_prefetch=1,
            grid=(t // tm,),
            in_specs=[pl.BlockSpec((tm, d), lambda i, dest: (i, 0)),
                      pl.BlockSpec((tm, LANES), lambda i, dest: (i, 0)),
                      pl.BlockSpec(memory_space=pl.ANY),
                      pl.BlockSpec((1, d), lambda i, dest: (0, 0))],
            out_specs=pl.BlockSpec((tm, d), lambda i, dest: (i, 0)),
            scratch_shapes=[pltpu.VMEM((TOP_K, tm, d), F32), pltpu.SemaphoreType.DMA]),
        out_shape=jax.ShapeDtypeStruct((t, d), F32),
        compiler_params=_params(("arbitrary",)),
        name="moe_combine",
    )(dest, x2, rw, yb, g_final)


def _sc_gather_rows(table, idx):
    b = idx.shape[0]
    d = table.shape[1]
    per_w = b // SC_WORKERS
    ch = SC_CHUNK_ROWS
    mesh = plsc.VectorSubcoreMesh(core_axis_name="c", subcore_axis_name="s")

    def body(table_hbm, idx_hbm, out_hbm, idx_v, rows_v, sem):
        wid = lax.axis_index("s") * SC_CORES + lax.axis_index("c")
        base = wid * per_w
        pltpu.sync_copy(idx_hbm.at[pl.ds(base, per_w)], idx_v)

        @pl.loop(0, per_w // ch)
        def _(c):
            off = pl.multiple_of(c * ch, ch)
            pltpu.async_copy(table_hbm.at[idx_v.at[pl.ds(off, ch)]], rows_v, sem).wait()
            pltpu.sync_copy(rows_v, out_hbm.at[pl.ds(base + off, ch)])

    return pl.kernel(body, out_type=jax.ShapeDtypeStruct((b, d), table.dtype), mesh=mesh,
                     scratch_types=[pltpu.VMEM((per_w,), I32), pltpu.VMEM((ch, d), table.dtype),
                                    pltpu.SemaphoreType.DMA], name="sc_gather")(table, idx)


def _combine_dense_kernel(x_ref, rw_ref, y0_ref, y1_ref, gf_ref, o_ref, *, final_norm):
    w = rw_ref[...]
    out = x_ref[...] + w[:, 0:1] * y0_ref[...] + w[:, 1:2] * y1_ref[...]
    if final_norm:
        out = _rms(out, gf_ref[...])
    o_ref[...] = out


def _combine_dense(x2, rw, y2, g_final, seq, final_norm):
    t, d = x2.shape
    tm = min(ROW_TILE, seq)
    nt = t // tm
    row = lambda w: pl.BlockSpec((tm, w), lambda i: (i, 0))
    return pl.pallas_call(
        functools.partial(_combine_dense_kernel, final_norm=final_norm),
        grid=(nt,),
        in_specs=[row(d), row(LANES), row(d), pl.BlockSpec((tm, d), lambda i: (nt + i, 0)),
                  pl.BlockSpec((1, d), lambda i: (0, 0))],
        out_specs=row(d),
        out_shape=jax.ShapeDtypeStruct((t, d), F32),
        compiler_params=_params(("parallel",)),
        name="moe_combine_dense",
    )(x2, rw, y2, y2, g_final)


def _route_plan(ri, counts_f, blk, n_blocks):
    cnt = counts_f[0, EXPERT_LANE0:EXPERT_LANE0 + N_EXPERTS].astype(I32)
    nblk = (cnt + blk - 1) // blk
    cum = jnp.cumsum(nblk)
    pad_start = (cum - nblk) * blk
    e = ri[:, 0:TOP_K]
    rank = ri[:, TOP_K:2 * TOP_K]
    onehot = e[:, :, None] == jnp.arange(N_EXPERTS, dtype=I32)[None, None, :]
    dest = rank + jnp.sum(jnp.where(onehot, pad_start[None, None, :], 0), axis=-1)
    blocks = jnp.arange(n_blocks, dtype=I32)
    block_expert = jnp.minimum(jnp.sum((cum[None, :] <= blocks[:, None]).astype(I32), axis=-1), N_EXPERTS - 1)
    n_active = cum[-1:].astype(I32)
    experts = jnp.arange(N_EXPERTS, dtype=I32)
    is_block_expert = block_expert[:, None] == experts[None, :]
    first = jnp.sum(jnp.where(is_block_expert, ((cum - nblk)[None, :] == blocks[:, None]).astype(I32), 0), axis=-1)
    used = (nblk > 0).astype(I32)
    slot_e = (jnp.cumsum(used) - used) % 2
    later = (experts[None, :] > experts[:, None]) & (nblk[None, :] > 0)
    next_e = jnp.min(jnp.where(later, experts[None, :], N_EXPERTS), axis=-1)
    next_e = jnp.where(next_e == N_EXPERTS, -1, next_e)
    slot = jnp.sum(jnp.where(is_block_expert, slot_e[None, :], 0), axis=-1)
    nxt = jnp.sum(jnp.where(is_block_expert, next_e[None, :], 0), axis=-1)
    plan = (block_expert, first.astype(I32), slot.astype(I32), nxt.astype(I32), n_active)
    return dest.reshape(-1).astype(I32), plan


def kernel(x, mem, norm_mix, w_in, lambda_q1, lambda_k1, lambda_q2, lambda_k2, subln, conv_w, w_branch, w_o,
           norm_cross, norm_mem, w_cq, w_ckv, w_co, norm_ffn, w_router_group, w_router_expert, w_exp_gate,
           w_exp_up, w_exp_down, norm_final):
    batch, seq, d = x.shape
    depth = w_in.shape[0]
    n_mem = mem.shape[1]
    t = batch * seq
    dh = d // (2 * DA_HEADS)
    assert 2 * dh == LANES and d % LANES == 0 and w_in.shape[2] == 8 * d
    assert w_router_group.shape[2] == N_GROUPS and w_router_expert.shape[2] == N_EXPERTS
    blk = min(EXPERT_BLOCK, seq)
    n_blocks = (t * TOP_K) // blk + N_EXPERTS

    tabs = _rotary_tables(seq, dh)
    x2 = x.reshape(t, d)
    kv = _mem_kv(mem.reshape(batch * n_mem, d), norm_mem.reshape(1, d), w_ckv.astype(BF16))
    w_r = jnp.concatenate([w_router_group, w_router_expert,
                           jnp.zeros((depth, d, LANES - N_GROUPS - N_EXPERTS), F32)], axis=-1).astype(BF16)
    xb = jnp.zeros((n_blocks * blk, d), F32)

    for l in range(depth):
        lambda_init = 0.8 - 0.6 * math.exp(-0.3 * l)
        proj = _inproj(x2, norm_mix[l].reshape(1, d), w_in[l].astype(BF16), tabs, seq, dh)
        y_attn = _attention(proj, lambda_q1[l].reshape(1, dh), lambda_k1[l].reshape(1, dh),
                            lambda_q2[l].reshape(1, dh), lambda_k2[l].reshape(1, dh),
                            subln[l].reshape(1, LANES), batch, seq, d, lambda_init)
        x2 = _mixer_out(x2, y_attn, proj, conv_w[l], w_branch[l, 0].astype(BF16), w_branch[l, 1].astype(BF16),
                        w_o[l].astype(BF16), seq)
        g_ffn = norm_ffn[l].reshape(1, d)
        x2, logits = _cross_attention(x2, norm_cross[l].reshape(1, d), w_cq[l].astype(BF16), kv, l,
                                      w_co[l].astype(BF16), g_ffn, w_r[l], seq, n_mem)
        ri, rw, counts = _router(logits, seq)
        dest, plan = _route_plan(ri, counts, blk, n_blocks)
        xb = _dispatch(dest, x2, g_ffn, xb, seq)
        yb = _experts(plan, xb, w_exp_gate, w_exp_up, w_exp_down, l, blk)
        y2 = _sc_gather_rows(yb, dest.reshape(t, TOP_K).T.reshape(-1))
        x2 = _combine_dense(x2, rw, y2, norm_final.reshape(1, d), seq, l == depth - 1)
    return x2.reshape(batch, seq, d)
```

```python
import functools
import math

import jax
import jax.numpy as jnp
from jax import lax
from jax.experimental import pallas as pl
from jax.experimental.pallas import tpu as pltpu
from jax.experimental.pallas import tpu_sc as plsc

EPS = 1e-6
DA_HEADS = 8
MEM_HEADS = 4
N_GROUPS = 4
EXPERTS_PER_GROUP = 8
N_EXPERTS = N_GROUPS * EXPERTS_PER_GROUP
TOP_K = 2
ROPE_THETA = 500000.0
LANES = 128
F32_SUBLANES = 8
BF16_SUBLANES = 16
EXPERT_LANE0 = N_GROUPS
ROW_TILE = 512
ROUTER_TILE = 2048
Q_TILE = 2048
EXPERT_BLOCK = 256
PROJ_ROW_CHUNK = 1024
ATTN_SCORE_CHUNK = 128
ATTN_SOFTMAX_CHUNK = 128
TOKEN_STREAMS = 2
SC_CORES = 2
SC_WORKERS = 32
SC_CHUNK_ROWS = 32
VMEM_LIMIT = 56 * 1024 * 1024

F32 = jnp.float32
BF16 = jnp.bfloat16
I32 = jnp.int32
NT_DIMS = (((1,), (1,)), ((), ()))


def _rms(x, g):
    return x * lax.rsqrt(jnp.mean(x * x, axis=-1, keepdims=True) + EPS) * g


def _params(sem, vmem=VMEM_LIMIT):
    return pltpu.CompilerParams(dimension_semantics=sem, vmem_limit_bytes=vmem)


def _inproj_kernel(x_ref, g_ref, w_ref, cos_ref, s1_ref, s2_ref, o_ref, h_ref, *, rc, half, q_scale):
    j = pl.program_id(1)
    tm, tn = o_ref.shape

    @pl.when(j == 0)
    def _():
        h_ref[...] = _rms(x_ref[...], g_ref[...]).astype(BF16)

    def run(rot, scale):
        def body(r, carry):
            r0 = pl.multiple_of(r * rc, rc)
            acc = jnp.dot(h_ref[pl.ds(r0, rc), :], w_ref[...], preferred_element_type=F32)
            if not rot:
                o_ref[pl.ds(r0, rc), :] = acc.astype(o_ref.dtype)
                return carry
            c = cos_ref[pl.ds(r0, rc), :]
            s1 = s1_ref[pl.ds(r0, rc), :]
            s2 = s2_ref[pl.ds(r0, rc), :]
            for cc in range(tn // LANES):
                a = acc[:, cc * LANES:(cc + 1) * LANES]
                a = a * c + pltpu.roll(a, half, 1) * s1 + pltpu.roll(a, LANES - half, 1) * s2
                if scale != 1.0:
                    a = a * scale
                o_ref[pl.ds(r0, rc), cc * LANES:(cc + 1) * LANES] = a.astype(o_ref.dtype)
            return carry
        lax.fori_loop(0, tm // rc, body, 0)

    @pl.when(j == 0)
    def _():
        run(True, q_scale)

    @pl.when(j == 1)
    def _():
        run(True, 1.0)

    @pl.when(j >= 2)
    def _():
        run(False, 1.0)


def _inproj(x2, g, w_bf, tabs, seq, dh):
    t, d = x2.shape
    n = w_bf.shape[1]
    tm, tn = seq, d
    rc = min(PROJ_ROW_CHUNK, tm)
    cos_t, s1_t, s2_t = tabs
    tab_spec = pl.BlockSpec((seq, LANES), lambda i, j: (0, 0))
    return pl.pallas_call(
        functools.partial(_inproj_kernel, rc=rc, half=dh // 8, q_scale=dh ** -0.5 * math.log2(math.e)),
        grid=(t // tm, n // tn),
        in_specs=[pl.BlockSpec((tm, d), lambda i, j: (i, 0)),
                  pl.BlockSpec((1, d), lambda i, j: (0, 0)),
                  pl.BlockSpec((d, tn), lambda i, j: (0, j)),
                  tab_spec, tab_spec, tab_spec],
        out_specs=pl.BlockSpec((tm, tn), lambda i, j: (i, j)),
        out_shape=jax.ShapeDtypeStruct((t, n), BF16),
        scratch_shapes=[pltpu.VMEM((tm, d), BF16)],
        compiler_params=_params(("parallel", "arbitrary")),
        name="inproj",
    )(x2, g, w_bf, cos_t, s1_t, s2_t)


def _rotary_tables(seq, dh):
    rot = dh // 4
    half = rot // 2
    inv = jnp.float32(ROPE_THETA) ** (-jnp.arange(0, rot, 2, dtype=F32) / rot)
    ang = jnp.arange(seq, dtype=F32)[:, None] * inv[None, :]
    cos, sin = jnp.cos(ang), jnp.sin(ang)
    lane = jnp.arange(LANES) % dh
    idx = lane % half
    c_t = jnp.where(lane < rot, cos[:, idx], 1.0)
    s1_t = jnp.where((lane >= half) & (lane < rot), sin[:, idx], 0.0)
    s2_t = jnp.where(lane < half, -sin[:, idx], 0.0)
    return c_t.astype(F32), s1_t.astype(F32), s2_t.astype(F32)


def _attn_kernel(q_ref, k_ref, v_ref, lq1_ref, lk1_ref, lq2_ref, lk2_ref, sub_ref, o_ref,
                 s_ref, p_ref, *, dh, lambda_init, rc, rp):
    tq = q_ref.shape[0]
    n_chunks = tq // rc
    lane = lax.broadcasted_iota(I32, (1, LANES), 1)
    m1 = jnp.where(lane < dh, 1.0, 0.0).astype(BF16)
    m2 = jnp.where(lane >= dh, 1.0, 0.0).astype(BF16)
    lam = (jnp.exp(jnp.sum(lq1_ref[...] * lk1_ref[...], axis=-1, keepdims=True))
           - jnp.exp(jnp.sum(lq2_ref[...] * lk2_ref[...], axis=-1, keepdims=True)) + lambda_init)

    def scores(c):
        qc = q_ref[c * rc:(c + 1) * rc, :]
        qq = jnp.concatenate([qc * m1, qc * m2], axis=0)
        s_ref[c % 2] = lax.dot_general(qq, k_ref[...], NT_DIMS, preferred_element_type=F32)

    def softmax_pv(c, j):
        r0 = j * rp
        s1 = s_ref[c % 2, r0:r0 + rp, :]
        s2 = s_ref[c % 2, rc + r0:rc + r0 + rp, :]
        e1 = jnp.exp2(s1 - jnp.max(s1, axis=-1, keepdims=True))
        e2 = jnp.exp2(s2 - jnp.max(s2, axis=-1, keepdims=True))
        l1 = jnp.sum(e1, axis=-1, keepdims=True)
        l2 = jnp.sum(e2, axis=-1, keepdims=True)
        slot = (c * (rc // rp) + j) % 2
        p_ref[slot] = (e1 - e2 * (lam * l1 / l2)).astype(BF16)
        od = jnp.dot(p_ref[slot], v_ref[...], preferred_element_type=F32) / l1
        od = _rms(od, sub_ref[...]) * (1.0 - lambda_init)
        o_ref[c * rc + r0:c * rc + r0 + rp, :] = od.astype(o_ref.dtype)

    scores(0)
    for c in range(n_chunks):
        if c + 1 < n_chunks:
            scores(c + 1)
        for j in range(rc // rp):
            softmax_pv(c, j)


def _attention(proj, lq1, lk1, lq2, lk2, sub, batch, seq, d, lambda_init):
    t = proj.shape[0]
    nh = d // LANES
    dh = LANES // 2
    tq = min(Q_TILE, seq)
    nq = seq // tq
    rc = min(ATTN_SCORE_CHUNK, tq)
    rp = min(ATTN_SOFTMAX_CHUNK, rc)
    vec = pl.BlockSpec((1, dh), lambda b, h, qi: (0, 0))
    return pl.pallas_call(
        functools.partial(_attn_kernel, dh=dh, lambda_init=lambda_init, rc=rc, rp=rp),
        grid=(batch, nh, nq),
        scratch_shapes=[pltpu.VMEM((2, 2 * rc, seq), F32), pltpu.VMEM((2, rp, seq), BF16)],
        in_specs=[pl.BlockSpec((tq, LANES), lambda b, h, qi: (b * nq + qi, h)),
                  pl.BlockSpec((seq, LANES), lambda b, h, qi: (b, nh + h)),
                  pl.BlockSpec((seq, LANES), lambda b, h, qi: (b, 2 * nh + h)),
                  vec, vec, vec, vec,
                  pl.BlockSpec((1, LANES), lambda b, h, qi: (0, 0))],
        out_specs=pl.BlockSpec((tq, LANES), lambda b, h, qi: (b * nq + qi, h)),
        out_shape=jax.ShapeDtypeStruct((t, d), BF16),
        compiler_params=_params(("parallel", "parallel", "arbitrary")),
        name="diff_attn",
    )(proj, proj, proj, lq1, lk1, lq2, lk2, sub)


def _mixout_kernel(x_ref, ya_ref, cb_ref, cc_ref, cx_ref, g0_ref, g1_ref, pc_ref, px_ref, nc_ref, nx_ref,
                   cw_ref, wb0_ref, wb1_ref, wo_ref, o_ref, *, tiles_per_seq):
    i = pl.program_id(0)
    tm = x_ref.shape[0]
    u = cc_ref[...].astype(F32) * cx_ref[...].astype(F32)
    hr = pc_ref.shape[0]
    u_prev = pc_ref[hr - 1:hr, :].astype(F32) * px_ref[hr - 1:hr, :].astype(F32)
    u_next = nc_ref[0:1, :].astype(F32) * nx_ref[0:1, :].astype(F32)
    pos = i % tiles_per_seq
    u_prev = jnp.where(pos == 0, 0.0, u_prev)
    u_next = jnp.where(pos == tiles_per_seq - 1, 0.0, u_next)
    row = lax.broadcasted_iota(I32, u.shape, 0)
    u_up = jnp.where(row == 0, u_prev, pltpu.roll(u, 1, 0))
    u_dn = jnp.where(row == tm - 1, u_next, pltpu.roll(u, tm - 1, 0))
    conv = cw_ref[0:1, :] * u_up + cw_ref[1:2, :] * u + cw_ref[2:3, :] * u_dn
    y_conv = (cb_ref[...].astype(F32) * conv).astype(BF16)
    br0 = jnp.dot(ya_ref[...], wb0_ref[...], preferred_element_type=F32)
    br1 = jnp.dot(y_conv, wb1_ref[...], preferred_element_type=F32)
    z = jax.nn.sigmoid(g0_ref[...].astype(F32)) * br0 + jax.nn.sigmoid(g1_ref[...].astype(F32)) * br1
    o_ref[...] = x_ref[...] + jnp.dot(z.astype(BF16), wo_ref[...], preferred_element_type=F32)


def _mixer_out(x2, y_attn, proj, conv_w, wb0, wb1, wo, seq):
    t, d = x2.shape
    tm = min(ROW_TILE, seq)
    tps = seq // tm
    hr = BF16_SUBLANES
    nhb = t // hr
    row = lambda c: pl.BlockSpec((tm, d), lambda i: (i, c))
    prev = lambda c: pl.BlockSpec((hr, d), lambda i: (jnp.maximum(i * (tm // hr) - 1, 0), c))
    nxt = lambda c: pl.BlockSpec((hr, d), lambda i: (jnp.minimum((i + 1) * (tm // hr), nhb - 1), c))
    full = lambda r: pl.BlockSpec((r, d), lambda i: (0, 0))
    return pl.pallas_call(
        functools.partial(_mixout_kernel, tiles_per_seq=tps),
        grid=(t // tm,),
        in_specs=[row(0), row(0), row(3), row(4), row(5), row(6), row(7),
                  prev(4), prev(5), nxt(4), nxt(5),
                  full(conv_w.shape[0]), full(d), full(d), full(d)],
        out_specs=row(0),
        out_shape=jax.ShapeDtypeStruct((t, d), F32),
        compiler_params=_params(("parallel",)),
        name="mixer_out",
    )(x2, y_attn, proj, proj, proj, proj, proj, proj, proj, proj, proj, conv_w, wb0, wb1, wo)


def _memkv_kernel(m_ref, g_ref, w_ref, o_ref):
    h = _rms(m_ref[...], g_ref[...]).astype(BF16)
    o_ref[...] = jnp.dot(h, w_ref[...], preferred_element_type=F32).astype(o_ref.dtype)


def _mem_kv(mem2, g, w_bf):
    depth, d, n = w_bf.shape
    rows = mem2.shape[0]
    tn = min(n, 1024)
    return pl.pallas_call(
        _memkv_kernel,
        grid=(depth, n // tn),
        in_specs=[pl.BlockSpec((rows, d), lambda l, j: (0, 0)),
                  pl.BlockSpec((1, d), lambda l, j: (0, 0)),
                  pl.BlockSpec((None, d, tn), lambda l, j: (l, 0, j))],
        out_specs=pl.BlockSpec((None, rows, tn), lambda l, j: (l, 0, j)),
        out_shape=jax.ShapeDtypeStruct((depth, rows, n), BF16),
        compiler_params=_params(("parallel", "parallel")),
        name="mem_kv",
    )(mem2, g, w_bf)


def _cross_kernel(x_ref, g_ref, wq_ref, kv_ref, wo_ref, gf_ref, wr_ref, o_ref, lg_ref, *, heads):
    x = x_ref[...]
    d = x.shape[1]
    hd = d // heads
    h = _rms(x, g_ref[...]).astype(BF16)
    q = (jnp.dot(h, wq_ref[...], preferred_element_type=F32) * (hd ** -0.5)).astype(BF16)
    outs = []
    for hh in range(heads):
        qh = q[:, hh * hd:(hh + 1) * hd]
        kh = kv_ref[:, hh * hd:(hh + 1) * hd]
        vh = kv_ref[:, d + hh * hd:d + (hh + 1) * hd]
        s = lax.dot_general(qh, kh, NT_DIMS, preferred_element_type=F32)
        p = jnp.exp(s - jnp.max(s, axis=-1, keepdims=True))
        l = jnp.sum(p, axis=-1, keepdims=True)
        outs.append((jnp.dot(p.astype(BF16), vh, preferred_element_type=F32) / l).astype(BF16))
    o = jnp.concatenate(outs, axis=1)
    x_new = x + jnp.dot(o, wo_ref[...], preferred_element_type=F32)
    o_ref[...] = x_new
    lg_ref[...] = jnp.dot(_rms(x_new, gf_ref[...]).astype(BF16), wr_ref[...], preferred_element_type=F32)


def _cross_attention(x2, g, wq, kv, layer, batch0, wo, g_ffn, w_r, seq, n_mem):
    t, d = x2.shape
    tm = min(ROW_TILE, seq)
    tps = seq // tm
    row = lambda w: pl.BlockSpec((tm, w), lambda i: (i, 0))
    const = lambda r, c: pl.BlockSpec((r, c), lambda i: (0, 0))
    return pl.pallas_call(
        functools.partial(_cross_kernel, heads=MEM_HEADS),
        grid=(t // tm,),
        in_specs=[row(d), const(1, d), const(d, d),
                  pl.BlockSpec((None, n_mem, 2 * d), lambda i: (layer, batch0 + i // tps, 0)),
                  const(d, d), const(1, d), const(d, LANES)],
        out_specs=[row(d), row(LANES)],
        out_shape=[jax.ShapeDtypeStruct((t, d), F32), jax.ShapeDtypeStruct((t, LANES), F32)],
        compiler_params=_params(("parallel",)),
        name="cross_attn",
    )(x2, g, wq, kv, wo, g_ffn, w_r)


def _router_kernel(lg_ref, ri_ref, rw_ref, cnt_ref, carry_ref, *, sub):
    @pl.when(pl.program_id(0) == 0)
    def _():
        carry_ref[...] = jnp.zeros_like(carry_ref)

    lg = lg_ref[...]
    tm = lg.shape[0]
    lane = lax.broadcasted_iota(I32, lg.shape, 1).astype(F32)
    neg = -jnp.inf

    def first_argmax(v):
        m = jnp.max(v, axis=-1, keepdims=True)
        return m, jnp.min(jnp.where(v == m, lane, float(LANES)), axis=-1, keepdims=True)

    gmask = lane < N_GROUPS
    gl = jnp.where(gmask, lg, neg)
    gmax, gidx = first_argmax(gl)
    g_w = 1.0 / jnp.sum(jnp.where(gmask, jnp.exp(gl - gmax), 0.0), axis=-1, keepdims=True)

    lo = EXPERT_LANE0 + gidx * EXPERTS_PER_GROUP
    emask = (lane >= lo) & (lane < lo + EXPERTS_PER_GROUP)
    el = jnp.where(emask, lg, neg)
    mx1, i1 = first_argmax(el)
    mx2, i2 = first_argmax(jnp.where(lane == i1, neg, el))
    esum = jnp.sum(jnp.where(emask, jnp.exp(el - mx1), 0.0), axis=-1, keepdims=True)
    p1 = 1.0 / esum
    p2 = jnp.exp(mx2 - mx1) / esum
    den = p1 + p2
    w1 = g_w * (p1 / den)
    w2 = g_w * (p2 / den)

    memb = jnp.where((lane == i1) | (lane == i2), 1.0, 0.0)
    r_i = lax.broadcasted_iota(I32, (sub, sub), 0)
    c_i = lax.broadcasted_iota(I32, (sub, sub), 1)
    ltri = jnp.where(c_i < r_i, 1.0, 0.0).astype(BF16)
    tots = []
    carry = carry_ref[...]
    for r0 in range(0, tm, sub):
        mb = memb[r0:r0 + sub]
        tots.append(jnp.dot(ltri, mb.astype(BF16), preferred_element_type=F32) + carry)
        carry = carry + jnp.sum(mb, axis=0, keepdims=True)
    carry_ref[...] = carry
    tot = jnp.concatenate(tots, axis=0)
    rank1 = jnp.sum(jnp.where(lane == i1, tot, 0.0), axis=-1, keepdims=True).astype(I32)
    rank2 = jnp.sum(jnp.where(lane == i2, tot, 0.0), axis=-1, keepdims=True).astype(I32)

    e1 = (i1 - EXPERT_LANE0).astype(I32)
    e2 = (i2 - EXPERT_LANE0).astype(I32)
    ri_ref[...] = jnp.where(lane == 0, e1,
                            jnp.where(lane == 1, e2, jnp.where(lane == 2, rank1, jnp.where(lane == 3, rank2, 0))))
    rw_ref[...] = jnp.where(lane == 0, w1, jnp.where(lane == 1, w2, 0.0))
    cnt_ref[...] = jnp.broadcast_to(carry, cnt_ref.shape)


def _router(lg, seq):
    t = lg.shape[0]
    tm = min(ROUTER_TILE, seq)
    row = pl.BlockSpec((tm, LANES), lambda i: (i, 0))
    return pl.pallas_call(
        functools.partial(_router_kernel, sub=min(ROW_TILE, tm)),
        grid=(t // tm,),
        in_specs=[row],
        out_specs=[row, row, pl.BlockSpec((F32_SUBLANES, LANES), lambda i: (0, 0))],
        out_shape=[jax.ShapeDtypeStruct((t, LANES), I32),
                   jax.ShapeDtypeStruct((t, LANES), F32),
                   jax.ShapeDtypeStruct((F32_SUBLANES, LANES), F32)],
        scratch_shapes=[pltpu.VMEM((1, LANES), F32)],
        compiler_params=_params(("arbitrary",)),
        name="moe_router",
    )(lg)


def _dispatch_kernel(dest_ref, x_ref, g_ref, xb_in_ref, xb_ref, h_ref, sem):
    del xb_in_ref
    i = pl.program_id(0)
    tm = x_ref.shape[0]
    h_ref[...] = _rms(x_ref[...], g_ref[...])
    base = i * (TOP_K * tm)

    for t in range(tm):
        for k in range(TOP_K):
            dst = dest_ref[base + (TOP_K * t + k)]
            pltpu.make_async_copy(h_ref.at[pl.ds(t, 1), :], xb_ref.at[pl.ds(dst, 1), :], sem).start()
    for _ in range(TOP_K):
        pltpu.make_async_copy(h_ref, xb_ref.at[pl.ds(0, tm), :], sem).wait()


def _dispatch(dest, x2, g, xb, seq):
    t, d = x2.shape
    tm = min(ROW_TILE, seq)
    return pl.pallas_call(
        _dispatch_kernel,
        grid_spec=pltpu.PrefetchScalarGridSpec(
            num_scalar_prefetch=1,
            grid=(t // tm,),
            in_specs=[pl.BlockSpec((tm, d), lambda i, dest: (i, 0)),
                      pl.BlockSpec((1, d), lambda i, dest: (0, 0)),
                      pl.BlockSpec(memory_space=pl.ANY)],
            out_specs=pl.BlockSpec(memory_space=pl.ANY),
            scratch_shapes=[pltpu.VMEM((tm, d), F32), pltpu.SemaphoreType.DMA]),
        out_shape=jax.ShapeDtypeStruct(xb.shape, xb.dtype),
        input_output_aliases={3: 0},
        compiler_params=_params(("arbitrary",)),
        name="moe_dispatch",
    )(dest, x2, g, xb)


def _expert_kernel(be_ref, first_ref, slot_ref, nxt_ref, na_ref, xb_ref, wg_hbm, wu_hbm, wd_hbm, yb_ref,
                   wg_buf, wu_buf, wd_buf, wgb_ref, wub_ref, wdb_ref, sem, *, layer):
    i = pl.program_id(0)

    def fetch(e, slot):
        return (pltpu.make_async_copy(wg_hbm.at[layer, e], wg_buf.at[slot], sem.at[slot, 0]),
                pltpu.make_async_copy(wu_hbm.at[layer, e], wu_buf.at[slot], sem.at[slot, 1]),
                pltpu.make_async_copy(wd_hbm.at[layer, e], wd_buf.at[slot], sem.at[slot, 2]))

    @pl.when(i < na_ref[0])
    def _():
        @pl.when(first_ref[i] == 1)
        def _():
            e, slot = be_ref[i], slot_ref[i]

            @pl.when(i == 0)
            def _():
                for cp in fetch(e, slot):
                    cp.start()

            for cp in fetch(e, slot):
                cp.wait()
            wgb_ref[...] = wg_buf[slot].astype(BF16)
            wub_ref[...] = wu_buf[slot].astype(BF16)
            wdb_ref[...] = wd_buf[slot].astype(BF16)

            @pl.when(nxt_ref[i] >= 0)
            def _():
                for cp in fetch(nxt_ref[i], 1 - slot):
                    cp.start()

        xb = xb_ref[...].astype(BF16)
        gate = jnp.dot(xb, wgb_ref[...], preferred_element_type=F32)
        up = jnp.dot(xb, wub_ref[...], preferred_element_type=F32)
        act = (jax.nn.silu(gate) * up).astype(BF16)
        yb_ref[...] = jnp.dot(act, wdb_ref[...], preferred_element_type=F32)

    @pl.when(i >= na_ref[0])
    def _():
        yb_ref[...] = jnp.zeros_like(yb_ref)


def _experts(plan, xb, w_gate, w_up, w_down, layer, blk):
    rows, d = xb.shape
    de = w_gate.shape[-1]
    nb = rows // blk

    def row_map(i, be, first, slot, nxt, na):
        return (jnp.minimum(i, jnp.maximum(na[0] - 1, 0)), 0)

    hbm = pl.BlockSpec(memory_space=pl.ANY)
    return pl.pallas_call(
        functools.partial(_expert_kernel, layer=layer),
        grid_spec=pltpu.PrefetchScalarGridSpec(
            num_scalar_prefetch=5,
            grid=(nb,),
            in_specs=[pl.BlockSpec((blk, d), row_map), hbm, hbm, hbm],
            out_specs=pl.BlockSpec((blk, d), lambda i, *_: (i, 0)),
            scratch_shapes=[pltpu.VMEM((2, d, de), F32), pltpu.VMEM((2, d, de), F32), pltpu.VMEM((2, de, d), F32),
                            pltpu.VMEM((d, de), BF16), pltpu.VMEM((d, de), BF16), pltpu.VMEM((de, d), BF16),
                            pltpu.SemaphoreType.DMA((2, 3))]),
        out_shape=jax.ShapeDtypeStruct((rows, d), F32),
        compiler_params=_params(("arbitrary",)),
        name="moe_experts",
    )(*plan, xb, w_gate, w_up, w_down)


def _combine_kernel(dest_ref, x_ref, rw_ref, yb_ref, gf_ref, o_ref, y_ref, sem, *, final_norm):
    i = pl.program_id(0)
    tm = x_ref.shape[0]
    base = i * (TOP_K * tm)

    for t in range(tm):
        for k in range(TOP_K):
            src = dest_ref[base + (TOP_K * t + k)]
            pltpu.make_async_copy(yb_ref.at[pl.ds(src, 1), :], y_ref.at[k, pl.ds(t, 1), :], sem).start()
    for k in range(TOP_K):
        pltpu.make_async_copy(yb_ref.at[pl.ds(0, tm), :], y_ref.at[k], sem).wait()
    w = rw_ref[...]
    out = x_ref[...]
    for k in range(TOP_K):
        out = out + w[:, k:k + 1] * y_ref[k]
    if final_norm:
        out = _rms(out, gf_ref[...])
    o_ref[...] = out


def _combine(dest, x2, rw, yb, g_final, seq, final_norm):
    t, d = x2.shape
    tm = min(ROW_TILE, seq)
    return pl.pallas_call(
        functools.partial(_combine_kernel, final_norm=final_norm),
        grid_spec=pltpu.PrefetchScalarGridSpec(
            num_scalar_prefetch=1,
            grid=(t // tm,),
            in_specs=[pl.BlockSpec((tm, d), lambda i, dest: (i, 0)),
                      pl.BlockSpec((tm, LANES), lambda i, dest: (i, 0)),
                      pl.BlockSpec(memory_space=pl.ANY),
                      pl.BlockSpec((1, d), lambda i, dest: (0, 0))],
            out_specs=pl.BlockSpec((tm, d), lambda i, dest: (i, 0)),
            scratch_shapes=[pltpu.VMEM((TOP_K, tm, d), F32), pltpu.SemaphoreType.DMA]),
        out_shape=jax.ShapeDtypeStruct((t, d), F32),
        compiler_params=_params(("arbitrary",)),
        name="moe_combine",
    )(dest, x2, rw, yb, g_final)


def _sc_gather_rows(table, idx):
    b = idx.shape[0]
    d = table.shape[1]
    per_w = b // SC_WORKERS
    ch = SC_CHUNK_ROWS
    mesh = plsc.VectorSubcoreMesh(core_axis_name="c", subcore_axis_name="s")

    def body(table_hbm, idx_hbm, out_hbm, idx_v, rows_v, sem):
        wid = lax.axis_index("s") * SC_CORES + lax.axis_index("c")
        base = wid * per_w
        pltpu.sync_copy(idx_hbm.at[pl.ds(base, per_w)], idx_v)

        @pl.loop(0, per_w // ch)
        def _(c):
            off = pl.multiple_of(c * ch, ch)
            pltpu.async_copy(table_hbm.at[idx_v.at[pl.ds(off, ch)]], rows_v, sem).wait()
            pltpu.sync_copy(rows_v, out_hbm.at[pl.ds(base + off, ch)])

    return pl.kernel(body, out_type=jax.ShapeDtypeStruct((b, d), table.dtype), mesh=mesh,
                     scratch_types=[pltpu.VMEM((per_w,), I32), pltpu.VMEM((ch, d), table.dtype),
                                    pltpu.SemaphoreType.DMA], name="sc_gather")(table, idx)


def _combine_dense_kernel(x_ref, rw_ref, y0_ref, y1_ref, gf_ref, o_ref, *, final_norm):
    w = rw_ref[...]
    out = x_ref[...] + w[:, 0:1] * y0_ref[...] + w[:, 1:2] * y1_ref[...]
    if final_norm:
        out = _rms(out, gf_ref[...])
    o_ref[...] = out


def _combine_dense(x2, rw, y2, g_final, seq, final_norm):
    t, d = x2.shape
    tm = min(ROW_TILE, seq)
    nt = t // tm
    row = lambda w: pl.BlockSpec((tm, w), lambda i: (i, 0))
    return pl.pallas_call(
        functools.partial(_combine_dense_kernel, final_norm=final_norm),
        grid=(nt,),
        in_specs=[row(d), row(LANES), row(d), pl.BlockSpec((tm, d), lambda i: (nt + i, 0)),
                  pl.BlockSpec((1, d), lambda i: (0, 0))],
        out_specs=row(d),
        out_shape=jax.ShapeDtypeStruct((t, d), F32),
        compiler_params=_params(("parallel",)),
        name="moe_combine_dense",
    )(x2, rw, y2, y2, g_final)


def _route_plan(ri, counts_f, blk, n_blocks):
    cnt = counts_f[0, EXPERT_LANE0:EXPERT_LANE0 + N_EXPERTS].astype(I32)
    nblk = (cnt + blk - 1) // blk
    cum = jnp.cumsum(nblk)
    pad_start = (cum - nblk) * blk
    e = ri[:, 0:TOP_K]
    rank = ri[:, TOP_K:2 * TOP_K]
    onehot = e[:, :, None] == jnp.arange(N_EXPERTS, dtype=I32)[None, None, :]
    dest = rank + jnp.sum(jnp.where(onehot, pad_start[None, None, :], 0), axis=-1)
    blocks = jnp.arange(n_blocks, dtype=I32)
    block_expert = jnp.minimum(jnp.sum((cum[None, :] <= blocks[:, None]).astype(I32), axis=-1), N_EXPERTS - 1)
    n_active = cum[-1:].astype(I32)
    experts = jnp.arange(N_EXPERTS, dtype=I32)
    is_block_expert = block_expert[:, None] == experts[None, :]
    first = jnp.sum(jnp.where(is_block_expert, ((cum - nblk)[None, :] == blocks[:, None]).astype(I32), 0), axis=-1)
    used = (nblk > 0).astype(I32)
    slot_e = (jnp.cumsum(used) - used) % 2
    later = (experts[None, :] > experts[:, None]) & (nblk[None, :] > 0)
    next_e = jnp.min(jnp.where(later, experts[None, :], N_EXPERTS), axis=-1)
    next_e = jnp.where(next_e == N_EXPERTS, -1, next_e)
    slot = jnp.sum(jnp.where(is_block_expert, slot_e[None, :], 0), axis=-1)
    nxt = jnp.sum(jnp.where(is_block_expert, next_e[None, :], 0), axis=-1)
    plan = (block_expert, first.astype(I32), slot.astype(I32), nxt.astype(I32), n_active)
    return dest.reshape(-1).astype(I32), plan


def kernel(x, mem, norm_mix, w_in, lambda_q1, lambda_k1, lambda_q2, lambda_k2, subln, conv_w, w_branch, w_o,
           norm_cross, norm_mem, w_cq, w_ckv, w_co, norm_ffn, w_router_group, w_router_expert, w_exp_gate,
           w_exp_up, w_exp_down, norm_final):
    batch, seq, d = x.shape
    depth = w_in.shape[0]
    n_mem = mem.shape[1]
    t = batch * seq
    dh = d // (2 * DA_HEADS)
    assert 2 * dh == LANES and d % LANES == 0 and w_in.shape[2] == 8 * d
    assert w_router_group.shape[2] == N_GROUPS and w_router_expert.shape[2] == N_EXPERTS
    blk = min(EXPERT_BLOCK, seq)
    streams = TOKEN_STREAMS if batch % TOKEN_STREAMS == 0 else 1
    bs = batch // streams
    ts = bs * seq
    n_blocks = (ts * TOP_K) // blk + N_EXPERTS

    tabs = _rotary_tables(seq, dh)
    x2 = x.reshape(t, d)
    xs = [x2[s * ts:(s + 1) * ts] for s in range(streams)]
    kv = _mem_kv(mem.reshape(batch * n_mem, d), norm_mem.reshape(1, d), w_ckv.astype(BF16))
    w_r = jnp.concatenate([w_router_group, w_router_expert,
                           jnp.zeros((depth, d, LANES - N_GROUPS - N_EXPERTS), F32)], axis=-1).astype(BF16)
    xbs = [jnp.zeros((n_blocks * blk, d), F32) for _ in range(streams)]

    for l in range(depth):
        lambda_init = 0.8 - 0.6 * math.exp(-0.3 * l)
        w_in_l = w_in[l].astype(BF16)
        wb0, wb1, wo_l = w_branch[l, 0].astype(BF16), w_branch[l, 1].astype(BF16), w_o[l].astype(BF16)
        wq_l, wco_l = w_cq[l].astype(BF16), w_co[l].astype(BF16)
        g_ffn = norm_ffn[l].reshape(1, d)
        routed = []
        for s in range(streams):
            proj = _inproj(xs[s], norm_mix[l].reshape(1, d), w_in_l, tabs, seq, dh)
            y_attn = _attention(proj, lambda_q1[l].reshape(1, dh), lambda_k1[l].reshape(1, dh),
                                lambda_q2[l].reshape(1, dh), lambda_k2[l].reshape(1, dh),
                                subln[l].reshape(1, LANES), bs, seq, d, lambda_init)
            x_mid = _mixer_out(xs[s], y_attn, proj, conv_w[l], wb0, wb1, wo_l, seq)
            x_mid, logits = _cross_attention(x_mid, norm_cross[l].reshape(1, d), wq_l, kv, l, s * bs,
                                             wco_l, g_ffn, w_r[l], seq, n_mem)
            ri, rw, counts = _router(logits, seq)
            dest, plan = _route_plan(ri, counts, blk, n_blocks)
            xbs[s] = _dispatch(dest, x_mid, g_ffn, xbs[s], seq)
            routed.append((x_mid, rw, dest, plan))
        gathered = []
        for s in range(streams):
            x_mid, rw, dest, plan = routed[s]
            yb = _experts(plan, xbs[s], w_exp_gate, w_exp_up, w_exp_down, l, blk)
            gathered.append(_sc_gather_rows(yb, dest.reshape(ts, TOP_K).T.reshape(-1)))
        for s in range(streams):
            x_mid, rw, dest, plan = routed[s]
            xs[s] = _combine_dense(x_mid, rw, gathered[s], norm_final.reshape(1, d), seq, l == depth - 1)
    return jnp.concatenate(xs, axis=0).reshape(batch, seq, d)
```

```python
import functools
import math

import jax
import jax.numpy as jnp
from jax import lax
from jax.experimental import pallas as pl
from jax.experimental.pallas import tpu as pltpu
from jax.experimental.pallas import tpu_sc as plsc

EPS = 1e-6
DA_HEADS = 8
MEM_HEADS = 4
N_GROUPS = 4
EXPERTS_PER_GROUP = 8
N_EXPERTS = N_GROUPS * EXPERTS_PER_GROUP
TOP_K = 2
ROPE_THETA = 500000.0
LANES = 128
F32_SUBLANES = 8
BF16_SUBLANES = 16
EXPERT_LANE0 = N_GROUPS
ROW_TILE = 512
ROUTER_TILE = 2048
Q_TILE = 2048
EXPERT_BLOCK = 256
PROJ_ROW_CHUNK = 1024
ATTN_SCORE_CHUNK = 128
ATTN_SOFTMAX_CHUNK = 128
COMBINE_PARTS = 4
SC_CORES = 2
SC_WORKERS = 32
SC_CHUNK_ROWS = 32
VMEM_LIMIT = 56 * 1024 * 1024

F32 = jnp.float32
BF16 = jnp.bfloat16
I32 = jnp.int32
NT_DIMS = (((1,), (1,)), ((), ()))


def _rms(x, g):
    return x * lax.rsqrt(jnp.mean(x * x, axis=-1, keepdims=True) + EPS) * g


def _params(sem, vmem=VMEM_LIMIT):
    return pltpu.CompilerParams(dimension_semantics=sem, vmem_limit_bytes=vmem)


def _inproj_kernel(x_ref, g_ref, w_ref, cos_ref, s1_ref, s2_ref, o_ref, h_ref, *, rc, half, q_scale):
    j = pl.program_id(1)
    tm, tn = o_ref.shape

    @pl.when(j == 0)
    def _():
        h_ref[...] = _rms(x_ref[...], g_ref[...]).astype(BF16)

    def run(rot, scale):
        def body(r, carry):
            r0 = pl.multiple_of(r * rc, rc)
            acc = jnp.dot(h_ref[pl.ds(r0, rc), :], w_ref[...], preferred_element_type=F32)
            if not rot:
                o_ref[pl.ds(r0, rc), :] = acc.astype(o_ref.dtype)
                return carry
            c = cos_ref[pl.ds(r0, rc), :]
            s1 = s1_ref[pl.ds(r0, rc), :]
            s2 = s2_ref[pl.ds(r0, rc), :]
            for cc in range(tn // LANES):
                a = acc[:, cc * LANES:(cc + 1) * LANES]
                a = a * c + pltpu.roll(a, half, 1) * s1 + pltpu.roll(a, LANES - half, 1) * s2
                if scale != 1.0:
                    a = a * scale
                o_ref[pl.ds(r0, rc), cc * LANES:(cc + 1) * LANES] = a.astype(o_ref.dtype)
            return carry
        lax.fori_loop(0, tm // rc, body, 0)

    @pl.when(j == 0)
    def _():
        run(True, q_scale)

    @pl.when(j == 1)
    def _():
        run(True, 1.0)

    @pl.when(j >= 2)
    def _():
        run(False, 1.0)


def _inproj(x2, g, w_bf, tabs, seq, dh):
    t, d = x2.shape
    n = w_bf.shape[1]
    tm, tn = seq, d
    rc = min(PROJ_ROW_CHUNK, tm)
    cos_t, s1_t, s2_t = tabs
    tab_spec = pl.BlockSpec((seq, LANES), lambda i, j: (0, 0))
    return pl.pallas_call(
        functools.partial(_inproj_kernel, rc=rc, half=dh // 8, q_scale=dh ** -0.5 * math.log2(math.e)),
        grid=(t // tm, n // tn),
        in_specs=[pl.BlockSpec((tm, d), lambda i, j: (i, 0)),
                  pl.BlockSpec((1, d), lambda i, j: (0, 0)),
                  pl.BlockSpec((d, tn), lambda i, j: (0, j)),
                  tab_spec, tab_spec, tab_spec],
        out_specs=pl.BlockSpec((tm, tn), lambda i, j: (i, j)),
        out_shape=jax.ShapeDtypeStruct((t, n), BF16),
        scratch_shapes=[pltpu.VMEM((tm, d), BF16)],
        compiler_params=_params(("parallel", "arbitrary")),
        name="inproj",
    )(x2, g, w_bf, cos_t, s1_t, s2_t)


def _rotary_tables(seq, dh):
    rot = dh // 4
    half = rot // 2
    inv = jnp.float32(ROPE_THETA) ** (-jnp.arange(0, rot, 2, dtype=F32) / rot)
    ang = jnp.arange(seq, dtype=F32)[:, None] * inv[None, :]
    cos, sin = jnp.cos(ang), jnp.sin(ang)
    lane = jnp.arange(LANES) % dh
    idx = lane % half
    c_t = jnp.where(lane < rot, cos[:, idx], 1.0)
    s1_t = jnp.where((lane >= half) & (lane < rot), sin[:, idx], 0.0)
    s2_t = jnp.where(lane < half, -sin[:, idx], 0.0)
    return c_t.astype(F32), s1_t.astype(F32), s2_t.astype(F32)


def _attn_kernel(q_ref, k_ref, v_ref, lq1_ref, lk1_ref, lq2_ref, lk2_ref, sub_ref, o_ref,
                 s_ref, p_ref, *, dh, lambda_init, rc, rp):
    tq = q_ref.shape[0]
    n_chunks = tq // rc
    lane = lax.broadcasted_iota(I32, (1, LANES), 1)
    m1 = jnp.where(lane < dh, 1.0, 0.0).astype(BF16)
    m2 = jnp.where(lane >= dh, 1.0, 0.0).astype(BF16)
    lam = (jnp.exp(jnp.sum(lq1_ref[...] * lk1_ref[...], axis=-1, keepdims=True))
           - jnp.exp(jnp.sum(lq2_ref[...] * lk2_ref[...], axis=-1, keepdims=True)) + lambda_init)

    def scores(c):
        qc = q_ref[c * rc:(c + 1) * rc, :]
        qq = jnp.concatenate([qc * m1, qc * m2], axis=0)
        s_ref[c % 2] = lax.dot_general(qq, k_ref[...], NT_DIMS, preferred_element_type=F32)

    def softmax_pv(c, j):
        r0 = j * rp
        s1 = s_ref[c % 2, r0:r0 + rp, :]
        s2 = s_ref[c % 2, rc + r0:rc + r0 + rp, :]
        e1 = jnp.exp2(s1 - jnp.max(s1, axis=-1, keepdims=True))
        e2 = jnp.exp2(s2 - jnp.max(s2, axis=-1, keepdims=True))
        l1 = jnp.sum(e1, axis=-1, keepdims=True)
        l2 = jnp.sum(e2, axis=-1, keepdims=True)
        slot = (c * (rc // rp) + j) % 2
        p_ref[slot] = (e1 - e2 * (lam * l1 / l2)).astype(BF16)
        od = jnp.dot(p_ref[slot], v_ref[...], preferred_element_type=F32) / l1
        od = _rms(od, sub_ref[...]) * (1.0 - lambda_init)
        o_ref[c * rc + r0:c * rc + r0 + rp, :] = od.astype(o_ref.dtype)

    scores(0)
    for c in range(n_chunks):
        if c + 1 < n_chunks:
            scores(c + 1)
        for j in range(rc // rp):
            softmax_pv(c, j)


def _attention(proj, lq1, lk1, lq2, lk2, sub, batch, seq, d, lambda_init):
    t = proj.shape[0]
    nh = d // LANES
    dh = LANES // 2
    tq = min(Q_TILE, seq)
    nq = seq // tq
    rc = min(ATTN_SCORE_CHUNK, tq)
    rp = min(ATTN_SOFTMAX_CHUNK, rc)
    vec = pl.BlockSpec((1, dh), lambda b, h, qi: (0, 0))
    return pl.pallas_call(
        functools.partial(_attn_kernel, dh=dh, lambda_init=lambda_init, rc=rc, rp=rp),
        grid=(batch, nh, nq),
        scratch_shapes=[pltpu.VMEM((2, 2 * rc, seq), F32), pltpu.VMEM((2, rp, seq), BF16)],
        in_specs=[pl.BlockSpec((tq, LANES), lambda b, h, qi: (b * nq + qi, h)),
                  pl.BlockSpec((seq, LANES), lambda b, h, qi: (b, nh + h)),
                  pl.BlockSpec((seq, LANES), lambda b, h, qi: (b, 2 * nh + h)),
                  vec, vec, vec, vec,
                  pl.BlockSpec((1, LANES), lambda b, h, qi: (0, 0))],
        out_specs=pl.BlockSpec((tq, LANES), lambda b, h, qi: (b * nq + qi, h)),
        out_shape=jax.ShapeDtypeStruct((t, d), BF16),
        compiler_params=_params(("parallel", "parallel", "arbitrary")),
        name="diff_attn",
    )(proj, proj, proj, lq1, lk1, lq2, lk2, sub)


def _mixout_kernel(x_ref, ya_ref, cb_ref, cc_ref, cx_ref, g0_ref, g1_ref, pc_ref, px_ref, nc_ref, nx_ref,
                   cw_ref, wb0_ref, wb1_ref, wo_ref, o_ref, *, tiles_per_seq):
    i = pl.program_id(0)
    tm = x_ref.shape[0]
    u = cc_ref[...].astype(F32) * cx_ref[...].astype(F32)
    hr = pc_ref.shape[0]
    u_prev = pc_ref[hr - 1:hr, :].astype(F32) * px_ref[hr - 1:hr, :].astype(F32)
    u_next = nc_ref[0:1, :].astype(F32) * nx_ref[0:1, :].astype(F32)
    pos = i % tiles_per_seq
    u_prev = jnp.where(pos == 0, 0.0, u_prev)
    u_next = jnp.where(pos == tiles_per_seq - 1, 0.0, u_next)
    row = lax.broadcasted_iota(I32, u.shape, 0)
    u_up = jnp.where(row == 0, u_prev, pltpu.roll(u, 1, 0))
    u_dn = jnp.where(row == tm - 1, u_next, pltpu.roll(u, tm - 1, 0))
    conv = cw_ref[0:1, :] * u_up + cw_ref[1:2, :] * u + cw_ref[2:3, :] * u_dn
    y_conv = (cb_ref[...].astype(F32) * conv).astype(BF16)
    br0 = jnp.dot(ya_ref[...], wb0_ref[...], preferred_element_type=F32)
    br1 = jnp.dot(y_conv, wb1_ref[...], preferred_element_type=F32)
    z = jax.nn.sigmoid(g0_ref[...].astype(F32)) * br0 + jax.nn.sigmoid(g1_ref[...].astype(F32)) * br1
    o_ref[...] = x_ref[...] + jnp.dot(z.astype(BF16), wo_ref[...], preferred_element_type=F32)


def _mixer_out(x2, y_attn, proj, conv_w, wb0, wb1, wo, seq):
    t, d = x2.shape
    tm = min(ROW_TILE, seq)
    tps = seq // tm
    hr = BF16_SUBLANES
    nhb = t // hr
    row = lambda c: pl.BlockSpec((tm, d), lambda i: (i, c))
    prev = lambda c: pl.BlockSpec((hr, d), lambda i: (jnp.maximum(i * (tm // hr) - 1, 0), c))
    nxt = lambda c: pl.BlockSpec((hr, d), lambda i: (jnp.minimum((i + 1) * (tm // hr), nhb - 1), c))
    full = lambda r: pl.BlockSpec((r, d), lambda i: (0, 0))
    return pl.pallas_call(
        functools.partial(_mixout_kernel, tiles_per_seq=tps),
        grid=(t // tm,),
        in_specs=[row(0), row(0), row(3), row(4), row(5), row(6), row(7),
                  prev(4), prev(5), nxt(4), nxt(5),
                  full(conv_w.shape[0]), full(d), full(d), full(d)],
        out_specs=row(0),
        out_shape=jax.ShapeDtypeStruct((t, d), F32),
        compiler_params=_params(("parallel",)),
        name="mixer_out",
    )(x2, y_attn, proj, proj, proj, proj, proj, proj, proj, proj, proj, conv_w, wb0, wb1, wo)


def _memkv_kernel(m_ref, g_ref, w_ref, o_ref):
    h = _rms(m_ref[...], g_ref[...]).astype(BF16)
    o_ref[...] = jnp.dot(h, w_ref[...], preferred_element_type=F32).astype(o_ref.dtype)


def _mem_kv(mem2, g, w_bf):
    depth, d, n = w_bf.shape
    rows = mem2.shape[0]
    tn = min(n, 1024)
    return pl.pallas_call(
        _memkv_kernel,
        grid=(depth, n // tn),
        in_specs=[pl.BlockSpec((rows, d), lambda l, j: (0, 0)),
                  pl.BlockSpec((1, d), lambda l, j: (0, 0)),
                  pl.BlockSpec((None, d, tn), lambda l, j: (l, 0, j))],
        out_specs=pl.BlockSpec((None, rows, tn), lambda l, j: (l, 0, j)),
        out_shape=jax.ShapeDtypeStruct((depth, rows, n), BF16),
        compiler_params=_params(("parallel", "parallel")),
        name="mem_kv",
    )(mem2, g, w_bf)


def _cross_kernel(x_ref, g_ref, wq_ref, kv_ref, wo_ref, gf_ref, wr_ref, o_ref, hf_ref, lg_ref, *, heads):
    x = x_ref[...]
    d = x.shape[1]
    hd = d // heads
    h = _rms(x, g_ref[...]).astype(BF16)
    q = (jnp.dot(h, wq_ref[...], preferred_element_type=F32) * (hd ** -0.5)).astype(BF16)
    outs = []
    for hh in range(heads):
        qh = q[:, hh * hd:(hh + 1) * hd]
        kh = kv_ref[:, hh * hd:(hh + 1) * hd]
        vh = kv_ref[:, d + hh * hd:d + (hh + 1) * hd]
        s = lax.dot_general(qh, kh, NT_DIMS, preferred_element_type=F32)
        p = jnp.exp(s - jnp.max(s, axis=-1, keepdims=True))
        l = jnp.sum(p, axis=-1, keepdims=True)
        outs.append((jnp.dot(p.astype(BF16), vh, preferred_element_type=F32) / l).astype(BF16))
    o = jnp.concatenate(outs, axis=1)
    x_new = x + jnp.dot(o, wo_ref[...], preferred_element_type=F32)
    o_ref[...] = x_new
    h_ffn = _rms(x_new, gf_ref[...])
    hf_ref[...] = h_ffn
    lg_ref[...] = jnp.dot(h_ffn.astype(BF16), wr_ref[...], preferred_element_type=F32)


def _cross_attention(x2, g, wq, kv, layer, batch0, wo, g_ffn, w_r, seq, n_mem):
    t, d = x2.shape
    tm = min(ROW_TILE, seq)
    tps = seq // tm
    row = lambda w: pl.BlockSpec((tm, w), lambda i: (i, 0))
    const = lambda r, c: pl.BlockSpec((r, c), lambda i: (0, 0))
    return pl.pallas_call(
        functools.partial(_cross_kernel, heads=MEM_HEADS),
        grid=(t // tm,),
        in_specs=[row(d), const(1, d), const(d, d),
                  pl.BlockSpec((None, n_mem, 2 * d), lambda i: (layer, batch0 + i // tps, 0)),
                  const(d, d), const(1, d), const(d, LANES)],
        out_specs=[row(d), row(d), row(LANES)],
        out_shape=[jax.ShapeDtypeStruct((t, d), F32), jax.ShapeDtypeStruct((t, d), F32),
                   jax.ShapeDtypeStruct((t, LANES), F32)],
        compiler_params=_params(("parallel",)),
        name="cross_attn",
    )(x2, g, wq, kv, wo, g_ffn, w_r)


def _router_kernel(lg_ref, ri_ref, rw_ref, cnt_ref, carry_ref, *, sub):
    @pl.when(pl.program_id(0) == 0)
    def _():
        carry_ref[...] = jnp.zeros_like(carry_ref)

    lg = lg_ref[...]
    tm = lg.shape[0]
    lane = lax.broadcasted_iota(I32, lg.shape, 1).astype(F32)
    neg = -jnp.inf

    def first_argmax(v):
        m = jnp.max(v, axis=-1, keepdims=True)
        return m, jnp.min(jnp.where(v == m, lane, float(LANES)), axis=-1, keepdims=True)

    gmask = lane < N_GROUPS
    gl = jnp.where(gmask, lg, neg)
    gmax, gidx = first_argmax(gl)
    g_w = 1.0 / jnp.sum(jnp.where(gmask, jnp.exp(gl - gmax), 0.0), axis=-1, keepdims=True)

    lo = EXPERT_LANE0 + gidx * EXPERTS_PER_GROUP
    emask = (lane >= lo) & (lane < lo + EXPERTS_PER_GROUP)
    el = jnp.where(emask, lg, neg)
    mx1, i1 = first_argmax(el)
    mx2, i2 = first_argmax(jnp.where(lane == i1, neg, el))
    esum = jnp.sum(jnp.where(emask, jnp.exp(el - mx1), 0.0), axis=-1, keepdims=True)
    p1 = 1.0 / esum
    p2 = jnp.exp(mx2 - mx1) / esum
    den = p1 + p2
    w1 = g_w * (p1 / den)
    w2 = g_w * (p2 / den)

    memb = jnp.where((lane == i1) | (lane == i2), 1.0, 0.0)
    r_i = lax.broadcasted_iota(I32, (sub, sub), 0)
    c_i = lax.broadcasted_iota(I32, (sub, sub), 1)
    ltri = jnp.where(c_i < r_i, 1.0, 0.0).astype(BF16)
    tots = []
    carry = carry_ref[...]
    for r0 in range(0, tm, sub):
        mb = memb[r0:r0 + sub]
        tots.append(jnp.dot(ltri, mb.astype(BF16), preferred_element_type=F32) + carry)
        carry = carry + jnp.sum(mb, axis=0, keepdims=True)
    carry_ref[...] = carry
    tot = jnp.concatenate(tots, axis=0)
    rank1 = jnp.sum(jnp.where(lane == i1, tot, 0.0), axis=-1, keepdims=True).astype(I32)
    rank2 = jnp.sum(jnp.where(lane == i2, tot, 0.0), axis=-1, keepdims=True).astype(I32)

    e1 = (i1 - EXPERT_LANE0).astype(I32)
    e2 = (i2 - EXPERT_LANE0).astype(I32)
    ri_ref[...] = jnp.where(lane == 0, e1,
                            jnp.where(lane == 1, e2, jnp.where(lane == 2, rank1, jnp.where(lane == 3, rank2, 0))))
    rw_ref[...] = jnp.where(lane == 0, w1, jnp.where(lane == 1, w2, 0.0))
    cnt_ref[...] = jnp.broadcast_to(carry, cnt_ref.shape)


def _router(lg, seq):
    t = lg.shape[0]
    tm = min(ROUTER_TILE, seq)
    row = pl.BlockSpec((tm, LANES), lambda i: (i, 0))
    return pl.pallas_call(
        functools.partial(_router_kernel, sub=min(ROW_TILE, tm)),
        grid=(t // tm,),
        in_specs=[row],
        out_specs=[row, row, pl.BlockSpec((F32_SUBLANES, LANES), lambda i: (0, 0))],
        out_shape=[jax.ShapeDtypeStruct((t, LANES), I32),
                   jax.ShapeDtypeStruct((t, LANES), F32),
                   jax.ShapeDtypeStruct((F32_SUBLANES, LANES), F32)],
        scratch_shapes=[pltpu.VMEM((1, LANES), F32)],
        compiler_params=_params(("arbitrary",)),
        name="moe_router",
    )(lg)


def _dispatch_kernel(dest_ref, x_ref, g_ref, xb_in_ref, xb_ref, h_ref, sem):
    del xb_in_ref
    i = pl.program_id(0)
    tm = x_ref.shape[0]
    h_ref[...] = _rms(x_ref[...], g_ref[...])
    base = i * (TOP_K * tm)

    for t in range(tm):
        for k in range(TOP_K):
            dst = dest_ref[base + (TOP_K * t + k)]
            pltpu.make_async_copy(h_ref.at[pl.ds(t, 1), :], xb_ref.at[pl.ds(dst, 1), :], sem).start()
    for _ in range(TOP_K):
        pltpu.make_async_copy(h_ref, xb_ref.at[pl.ds(0, tm), :], sem).wait()


def _dispatch(dest, x2, g, xb, seq):
    t, d = x2.shape
    tm = min(ROW_TILE, seq)
    return pl.pallas_call(
        _dispatch_kernel,
        grid_spec=pltpu.PrefetchScalarGridSpec(
            num_scalar_prefetch=1,
            grid=(t // tm,),
            in_specs=[pl.BlockSpec((tm, d), lambda i, dest: (i, 0)),
                      pl.BlockSpec((1, d), lambda i, dest: (0, 0)),
                      pl.BlockSpec(memory_space=pl.ANY)],
            out_specs=pl.BlockSpec(memory_space=pl.ANY),
            scratch_shapes=[pltpu.VMEM((tm, d), F32), pltpu.SemaphoreType.DMA]),
        out_shape=jax.ShapeDtypeStruct(xb.shape, xb.dtype),
        input_output_aliases={3: 0},
        compiler_params=_params(("arbitrary",)),
        name="moe_dispatch",
    )(dest, x2, g, xb)


def _expert_kernel(be_ref, first_ref, slot_ref, nxt_ref, na_ref, xb_ref, wg_hbm, wu_hbm, wd_hbm, yb_ref,
                   wg_buf, wu_buf, wd_buf, wgb_ref, wub_ref, wdb_ref, sem, *, layer):
    i = pl.program_id(0)

    def fetch(e, slot):
        return (pltpu.make_async_copy(wg_hbm.at[layer, e], wg_buf.at[slot], sem.at[slot, 0]),
                pltpu.make_async_copy(wu_hbm.at[layer, e], wu_buf.at[slot], sem.at[slot, 1]),
                pltpu.make_async_copy(wd_hbm.at[layer, e], wd_buf.at[slot], sem.at[slot, 2]))

    @pl.when(i < na_ref[0])
    def _():
        @pl.when(first_ref[i] == 1)
        def _():
            e, slot = be_ref[i], slot_ref[i]

            @pl.when(i == 0)
            def _():
                for cp in fetch(e, slot):
                    cp.start()

            for cp in fetch(e, slot):
                cp.wait()
            wgb_ref[...] = wg_buf[slot].astype(BF16)
            wub_ref[...] = wu_buf[slot].astype(BF16)
            wdb_ref[...] = wd_buf[slot].astype(BF16)

            @pl.when(nxt_ref[i] >= 0)
            def _():
                for cp in fetch(nxt_ref[i], 1 - slot):
                    cp.start()

        xb = xb_ref[...].astype(BF16)
        gate = jnp.dot(xb, wgb_ref[...], preferred_element_type=F32)
        up = jnp.dot(xb, wub_ref[...], preferred_element_type=F32)
        act = (jax.nn.silu(gate) * up).astype(BF16)
        yb_ref[...] = jnp.dot(act, wdb_ref[...], preferred_element_type=F32)

    @pl.when(i >= na_ref[0])
    def _():
        yb_ref[...] = jnp.zeros_like(yb_ref)


def _experts(plan, xb, w_gate, w_up, w_down, layer, blk):
    rows, d = xb.shape
    de = w_gate.shape[-1]
    nb = rows // blk

    def row_map(i, be, first, slot, nxt, na):
        return (jnp.minimum(i, jnp.maximum(na[0] - 1, 0)), 0)

    hbm = pl.BlockSpec(memory_space=pl.ANY)
    return pl.pallas_call(
        functools.partial(_expert_kernel, layer=layer),
        grid_spec=pltpu.PrefetchScalarGridSpec(
            num_scalar_prefetch=5,
            grid=(nb,),
            in_specs=[pl.BlockSpec((blk, d), row_map), hbm, hbm, hbm],
            out_specs=pl.BlockSpec((blk, d), lambda i, *_: (i, 0)),
            scratch_shapes=[pltpu.VMEM((2, d, de), F32), pltpu.VMEM((2, d, de), F32), pltpu.VMEM((2, de, d), F32),
                            pltpu.VMEM((d, de), BF16), pltpu.VMEM((d, de), BF16), pltpu.VMEM((de, d), BF16),
                            pltpu.SemaphoreType.DMA((2, 3))]),
        out_shape=jax.ShapeDtypeStruct((rows, d), F32),
        compiler_params=_params(("arbitrary",)),
        name="moe_experts",
    )(*plan, xb, w_gate, w_up, w_down)


def _combine_kernel(dest_ref, x_ref, rw_ref, yb_ref, gf_ref, o_ref, y_ref, sem, *, final_norm):
    i = pl.program_id(0)
    tm = x_ref.shape[0]
    base = i * (TOP_K * tm)

    for t in range(tm):
        for k in range(TOP_K):
            src = dest_ref[base + (TOP_K * t + k)]
            pltpu.make_async_copy(yb_ref.at[pl.ds(src, 1), :], y_ref.at[k, pl.ds(t, 1), :], sem).start()
    for k in range(TOP_K):
        pltpu.make_async_copy(yb_ref.at[pl.ds(0, tm), :], y_ref.at[k], sem).wait()
    w = rw_ref[...]
    out = x_ref[...]
    for k in range(TOP_K):
        out = out + w[:, k:k + 1] * y_ref[k]
    if final_norm:
        out = _rms(out, gf_ref[...])
    o_ref[...] = out


def _combine(dest, x2, rw, yb, g_final, seq, final_norm):
    t, d = x2.shape
    tm = min(ROW_TILE, seq)
    return pl.pallas_call(
        functools.partial(_combine_kernel, final_norm=final_norm),
        grid_spec=pltpu.PrefetchScalarGridSpec(
            num_scalar_prefetch=1,
            grid=(t // tm,),
            in_specs=[pl.BlockSpec((tm, d), lambda i, dest: (i, 0)),
                      pl.BlockSpec((tm, LANES), lambda i, dest: (i, 0)),
                      pl.BlockSpec(memory_space=pl.ANY),
                      pl.BlockSpec((1, d), lambda i, dest: (0, 0))],
            out_specs=pl.BlockSpec((tm, d), lambda i, dest: (i, 0)),
            scratch_shapes=[pltpu.VMEM((TOP_K, tm, d), F32), pltpu.SemaphoreType.DMA]),
        out_shape=jax.ShapeDtypeStruct((t, d), F32),
        compiler_params=_params(("arbitrary",)),
        name="moe_combine",
    )(dest, x2, rw, yb, g_final)


def _sc_gather_rows(table, idx):
    b = idx.shape[0]
    d = table.shape[1]
    per_w = b // SC_WORKERS
    ch = SC_CHUNK_ROWS
    mesh = plsc.VectorSubcoreMesh(core_axis_name="c", subcore_axis_name="s")

    n_ch = per_w // ch
    assert per_w * SC_WORKERS == b and n_ch * ch == per_w and n_ch % 2 == 0

    def body(table_hbm, idx_hbm, out_hbm, idx_v, rows_v, gsem, wsem):
        wid = lax.axis_index("s") * SC_CORES + lax.axis_index("c")
        base = wid * per_w
        pltpu.sync_copy(idx_hbm.at[pl.ds(base, per_w)], idx_v)

        def write_back(c, buf):
            off = pl.multiple_of(c * ch, ch)
            return pltpu.make_async_copy(rows_v.at[buf], out_hbm.at[pl.ds(base + off, ch)], wsem.at[buf])

        @pl.loop(0, n_ch, step=2)
        def _(c0):
            for buf in range(2):
                c = c0 + buf

                @pl.when(c0 > 0)
                def _():
                    write_back(c - 2, buf).wait()

                off = pl.multiple_of(c * ch, ch)
                pltpu.async_copy(table_hbm.at[idx_v.at[pl.ds(off, ch)]], rows_v.at[buf], gsem).wait()
                write_back(c, buf).start()

        for buf in range(2):
            write_back(n_ch - 2 + buf, buf).wait()

    return pl.kernel(body, out_type=jax.ShapeDtypeStruct((b, d), table.dtype), mesh=mesh,
                     scratch_types=[pltpu.VMEM((per_w,), I32), pltpu.VMEM((2, ch, d), table.dtype),
                                    pltpu.SemaphoreType.DMA, pltpu.SemaphoreType.DMA((2,))],
                     name="sc_gather")(table, idx)


def _sc_scatter_rows(rows, idx, n_out):
    t, d = rows.shape
    _, top_k, n_ch, ch = idx.shape
    per_w = n_ch * ch
    assert per_w * SC_WORKERS == t and idx.shape[0] == SC_WORKERS
    mesh = plsc.VectorSubcoreMesh(core_axis_name="c", subcore_axis_name="s")

    def body(rows_hbm, idx_hbm, out_hbm, idx_v, rows_v, sem):
        wid = lax.axis_index("s") * SC_CORES + lax.axis_index("c")
        base = wid * per_w
        pltpu.sync_copy(idx_hbm.at[wid], idx_v)

        @pl.loop(0, n_ch)
        def _(c):
            off = pl.multiple_of(c * ch, ch)
            pltpu.sync_copy(rows_hbm.at[pl.ds(base + off, ch)], rows_v)
            for k in range(top_k):
                pltpu.async_copy(rows_v, out_hbm.at[idx_v.at[k, c]], sem).wait()

    return pl.kernel(body, out_type=jax.ShapeDtypeStruct((n_out, d), rows.dtype), mesh=mesh,
                     scratch_types=[pltpu.VMEM((top_k, n_ch, ch), I32), pltpu.VMEM((ch, d), rows.dtype),
                                    pltpu.SemaphoreType.DMA], name="sc_scatter")(rows, idx)


def _combine_dense_kernel(x_ref, rw_ref, y0_ref, y1_ref, gf_ref, o_ref, *, final_norm):
    w = rw_ref[...]
    out = x_ref[...] + w[:, 0:1] * y0_ref[...] + w[:, 1:2] * y1_ref[...]
    if final_norm:
        out = _rms(out, gf_ref[...])
    o_ref[...] = out


def _combine_dense(x2, rw, y2, g_final, seq, final_norm, part, n_parts):
    t, d = x2.shape
    tm = min(ROW_TILE, seq)
    nt = t // tm // n_parts
    row = lambda w: pl.BlockSpec((tm, w), lambda i: (part * nt + i, 0))
    return pl.pallas_call(
        functools.partial(_combine_dense_kernel, final_norm=final_norm),
        grid=(nt,),
        in_specs=[row(d), row(LANES), pl.BlockSpec((tm, d), lambda i: (i, 0)),
                  pl.BlockSpec((tm, d), lambda i: (nt + i, 0)), pl.BlockSpec((1, d), lambda i: (0, 0))],
        out_specs=row(d),
        out_shape=jax.ShapeDtypeStruct((t, d), F32),
        input_output_aliases={0: 0},
        compiler_params=_params(("parallel",)),
        name="moe_combine_dense",
    )(x2, rw, y2, y2, g_final)


def _route_plan(ri, counts_f, blk, n_blocks):
    cnt = counts_f[0, EXPERT_LANE0:EXPERT_LANE0 + N_EXPERTS].astype(I32)
    nblk = (cnt + blk - 1) // blk
    cum = jnp.cumsum(nblk)
    pad_start = (cum - nblk) * blk
    e = ri[:, 0:TOP_K]
    rank = ri[:, TOP_K:2 * TOP_K]
    onehot = e[:, :, None] == jnp.arange(N_EXPERTS, dtype=I32)[None, None, :]
    dest = rank + jnp.sum(jnp.where(onehot, pad_start[None, None, :], 0), axis=-1)
    blocks = jnp.arange(n_blocks, dtype=I32)
    block_expert = jnp.minimum(jnp.sum((cum[None, :] <= blocks[:, None]).astype(I32), axis=-1), N_EXPERTS - 1)
    n_active = cum[-1:].astype(I32)
    experts = jnp.arange(N_EXPERTS, dtype=I32)
    is_block_expert = block_expert[:, None] == experts[None, :]
    first = jnp.sum(jnp.where(is_block_expert, ((cum - nblk)[None, :] == blocks[:, None]).astype(I32), 0), axis=-1)
    used = (nblk > 0).astype(I32)
    slot_e = (jnp.cumsum(used) - used) % 2
    later = (experts[None, :] > experts[:, None]) & (nblk[None, :] > 0)
    next_e = jnp.min(jnp.where(later, experts[None, :], N_EXPERTS), axis=-1)
    next_e = jnp.where(next_e == N_EXPERTS, -1, next_e)
    slot = jnp.sum(jnp.where(is_block_expert, slot_e[None, :], 0), axis=-1)
    nxt = jnp.sum(jnp.where(is_block_expert, next_e[None, :], 0), axis=-1)
    plan = (block_expert, first.astype(I32), slot.astype(I32), nxt.astype(I32), n_active)
    return dest.reshape(-1).astype(I32), plan


def kernel(x, mem, norm_mix, w_in, lambda_q1, lambda_k1, lambda_q2, lambda_k2, subln, conv_w, w_branch, w_o,
           norm_cross, norm_mem, w_cq, w_ckv, w_co, norm_ffn, w_router_group, w_router_expert, w_exp_gate,
           w_exp_up, w_exp_down, norm_final):
    batch, seq, d = x.shape
    depth = w_in.shape[0]
    n_mem = mem.shape[1]
    t = batch * seq
    dh = d // (2 * DA_HEADS)
    assert 2 * dh == LANES and d % LANES == 0 and w_in.shape[2] == 8 * d
    assert w_router_group.shape[2] == N_GROUPS and w_router_expert.shape[2] == N_EXPERTS
    blk = min(EXPERT_BLOCK, seq)
    n_blocks = (t * TOP_K) // blk + N_EXPERTS
    parts = COMBINE_PARTS if t % (COMBINE_PARTS * SC_WORKERS * SC_CHUNK_ROWS) == 0 else 1
    tokens_per_worker = t // SC_WORKERS

    tabs = _rotary_tables(seq, dh)
    x2 = x.reshape(t, d)
    kv = _mem_kv(mem.reshape(batch * n_mem, d), norm_mem.reshape(1, d), w_ckv.astype(BF16))
    w_r = jnp.concatenate([w_router_group, w_router_expert,
                           jnp.zeros((depth, d, LANES - N_GROUPS - N_EXPERTS), F32)], axis=-1).astype(BF16)

    for l in range(depth):
        lambda_init = 0.8 - 0.6 * math.exp(-0.3 * l)
        proj = _inproj(x2, norm_mix[l].reshape(1, d), w_in[l].astype(BF16), tabs, seq, dh)
        y_attn = _attention(proj, lambda_q1[l].reshape(1, dh), lambda_k1[l].reshape(1, dh),
                            lambda_q2[l].reshape(1, dh), lambda_k2[l].reshape(1, dh),
                            subln[l].reshape(1, LANES), batch, seq, d, lambda_init)
        x2 = _mixer_out(x2, y_attn, proj, conv_w[l], w_branch[l, 0].astype(BF16), w_branch[l, 1].astype(BF16),
                        w_o[l].astype(BF16), seq)
        x2, h_ffn, logits = _cross_attention(x2, norm_cross[l].reshape(1, d), w_cq[l].astype(BF16), kv, l, 0,
                                             w_co[l].astype(BF16), norm_ffn[l].reshape(1, d), w_r[l], seq, n_mem)
        ri, rw, counts = _router(logits, seq)
        dest, plan = _route_plan(ri, counts, blk, n_blocks)
        dest2 = dest.reshape(t, TOP_K)
        scatter_idx = dest2.reshape(SC_WORKERS, tokens_per_worker // SC_CHUNK_ROWS, SC_CHUNK_ROWS, TOP_K)
        xb = _sc_scatter_rows(h_ffn, scatter_idx.transpose(0, 3, 1, 2), n_blocks * blk)
        yb = _experts(plan, xb, w_exp_gate, w_exp_up, w_exp_down, l, blk)
        gather_idx = dest2.reshape(parts, t // parts, TOP_K).transpose(0, 2, 1).reshape(parts, -1)
        gathered = [_sc_gather_rows(yb, gather_idx[p]) for p in range(parts)]
        for p in range(parts):
            x2 = _combine_dense(x2, rw, gathered[p], norm_final.reshape(1, d), seq, l == depth - 1, p, parts)
    return x2.reshape(batch, seq, d)
```

```python
import functools
import math

import jax
import jax.numpy as jnp
from jax import lax
from jax.experimental import pallas as pl
from jax.experimental.pallas import tpu as pltpu
from jax.experimental.pallas import tpu_sc as plsc

EPS = 1e-6
DA_HEADS = 8
MEM_HEADS = 4
N_GROUPS = 4
EXPERTS_PER_GROUP = 8
N_EXPERTS = N_GROUPS * EXPERTS_PER_GROUP
TOP_K = 2
ROPE_THETA = 500000.0
LANES = 128
F32_SUBLANES = 8
BF16_SUBLANES = 16
EXPERT_LANE0 = N_GROUPS
ROW_TILE = 512
ROUTER_TILE = 2048
Q_TILE = 2048
EXPERT_BLOCK = 256
PROJ_ROW_CHUNK = 1024
ATTN_SCORE_CHUNK = 128
ATTN_SOFTMAX_CHUNK = 128
COMBINE_PARTS = 4
SC_CORES = 2
SC_WORKERS = 32
SC_CHUNK_ROWS = 64
VMEM_LIMIT = 56 * 1024 * 1024

F32 = jnp.float32
BF16 = jnp.bfloat16
I32 = jnp.int32
NT_DIMS = (((1,), (1,)), ((), ()))


def _rms(x, g):
    return x * lax.rsqrt(jnp.mean(x * x, axis=-1, keepdims=True) + EPS) * g


def _pack_rows(x):
    half = x.shape[1] // 2
    lo = lax.bitcast_convert_type(x[:, :half].astype(jnp.bfloat16).astype(F32), jnp.uint32)
    hi = lax.bitcast_convert_type(x[:, half:].astype(jnp.bfloat16).astype(F32), jnp.uint32)
    return lax.bitcast_convert_type((hi & jnp.uint32(0xFFFF0000)) | (lo >> 16), I32)


def _unpack_rows(w):
    u = lax.bitcast_convert_type(w, jnp.uint32)
    lo = lax.bitcast_convert_type(u << 16, F32)
    hi = lax.bitcast_convert_type(u & jnp.uint32(0xFFFF0000), F32)
    return jnp.concatenate([lo, hi], axis=1)


def _params(sem, vmem=VMEM_LIMIT):
    return pltpu.CompilerParams(dimension_semantics=sem, vmem_limit_bytes=vmem)


def _inproj_kernel(x_ref, g_ref, w_ref, cos_ref, s1_ref, s2_ref, o_ref, h_ref, *, rc, half, q_scale):
    j = pl.program_id(1)
    tm, tn = o_ref.shape

    @pl.when(j == 0)
    def _():
        h_ref[...] = _rms(x_ref[...], g_ref[...]).astype(BF16)

    def run(rot, scale):
        def body(r, carry):
            r0 = pl.multiple_of(r * rc, rc)
            acc = jnp.dot(h_ref[pl.ds(r0, rc), :], w_ref[...], preferred_element_type=F32)
            if not rot:
                o_ref[pl.ds(r0, rc), :] = acc.astype(o_ref.dtype)
                return carry
            c = cos_ref[pl.ds(r0, rc), :]
            s1 = s1_ref[pl.ds(r0, rc), :]
            s2 = s2_ref[pl.ds(r0, rc), :]
            for cc in range(tn // LANES):
                a = acc[:, cc * LANES:(cc + 1) * LANES]
                a = a * c + pltpu.roll(a, half, 1) * s1 + pltpu.roll(a, LANES - half, 1) * s2
                if scale != 1.0:
                    a = a * scale
                o_ref[pl.ds(r0, rc), cc * LANES:(cc + 1) * LANES] = a.astype(o_ref.dtype)
            return carry
        lax.fori_loop(0, tm // rc, body, 0)

    @pl.when(j == 0)
    def _():
        run(True, q_scale)

    @pl.when(j == 1)
    def _():
        run(True, 1.0)

    @pl.when(j >= 2)
    def _():
        run(False, 1.0)


def _inproj(x2, g, w_bf, tabs, seq, dh):
    t, d = x2.shape
    n = w_bf.shape[1]
    tm, tn = seq, d
    rc = min(PROJ_ROW_CHUNK, tm)
    cos_t, s1_t, s2_t = tabs
    tab_spec = pl.BlockSpec((seq, LANES), lambda i, j: (0, 0))
    return pl.pallas_call(
        functools.partial(_inproj_kernel, rc=rc, half=dh // 8, q_scale=dh ** -0.5 * math.log2(math.e)),
        grid=(t // tm, n // tn),
        in_specs=[pl.BlockSpec((tm, d), lambda i, j: (i, 0)),
                  pl.BlockSpec((1, d), lambda i, j: (0, 0)),
                  pl.BlockSpec((d, tn), lambda i, j: (0, j)),
                  tab_spec, tab_spec, tab_spec],
        out_specs=pl.BlockSpec((tm, tn), lambda i, j: (i, j)),
        out_shape=jax.ShapeDtypeStruct((t, n), BF16),
        scratch_shapes=[pltpu.VMEM((tm, d), BF16)],
        compiler_params=_params(("parallel", "arbitrary")),
        name="inproj",
    )(x2, g, w_bf, cos_t, s1_t, s2_t)


def _rotary_tables(seq, dh):
    rot = dh // 4
    half = rot // 2
    inv = jnp.float32(ROPE_THETA) ** (-jnp.arange(0, rot, 2, dtype=F32) / rot)
    ang = jnp.arange(seq, dtype=F32)[:, None] * inv[None, :]
    cos, sin = jnp.cos(ang), jnp.sin(ang)
    lane = jnp.arange(LANES) % dh
    idx = lane % half
    c_t = jnp.where(lane < rot, cos[:, idx], 1.0)
    s1_t = jnp.where((lane >= half) & (lane < rot), sin[:, idx], 0.0)
    s2_t = jnp.where(lane < half, -sin[:, idx], 0.0)
    return c_t.astype(F32), s1_t.astype(F32), s2_t.astype(F32)


def _attn_kernel(q_ref, k_ref, v_ref, lq1_ref, lk1_ref, lq2_ref, lk2_ref, sub_ref, o_ref,
                 s_ref, p_ref, *, dh, lambda_init, rc, rp):
    tq = q_ref.shape[0]
    n_chunks = tq // rc
    lane = lax.broadcasted_iota(I32, (1, LANES), 1)
    m1 = jnp.where(lane < dh, 1.0, 0.0).astype(BF16)
    m2 = jnp.where(lane >= dh, 1.0, 0.0).astype(BF16)
    lam = (jnp.exp(jnp.sum(lq1_ref[...] * lk1_ref[...], axis=-1, keepdims=True))
           - jnp.exp(jnp.sum(lq2_ref[...] * lk2_ref[...], axis=-1, keepdims=True)) + lambda_init)

    def scores(c):
        qc = q_ref[c * rc:(c + 1) * rc, :]
        qq = jnp.concatenate([qc * m1, qc * m2], axis=0)
        s_ref[c % 2] = lax.dot_general(qq, k_ref[...], NT_DIMS, preferred_element_type=F32)

    def softmax_pv(c, j):
        r0 = j * rp
        s1 = s_ref[c % 2, r0:r0 + rp, :]
        s2 = s_ref[c % 2, rc + r0:rc + r0 + rp, :]
        e1 = jnp.exp2(s1 - jnp.max(s1, axis=-1, keepdims=True))
        e2 = jnp.exp2(s2 - jnp.max(s2, axis=-1, keepdims=True))
        l1 = jnp.sum(e1, axis=-1, keepdims=True)
        l2 = jnp.sum(e2, axis=-1, keepdims=True)
        slot = (c * (rc // rp) + j) % 2
        p_ref[slot] = (e1 - e2 * (lam * l1 / l2)).astype(BF16)
        od = jnp.dot(p_ref[slot], v_ref[...], preferred_element_type=F32) / l1
        od = _rms(od, sub_ref[...]) * (1.0 - lambda_init)
        o_ref[c * rc + r0:c * rc + r0 + rp, :] = od.astype(o_ref.dtype)

    scores(0)
    for c in range(n_chunks):
        if c + 1 < n_chunks:
            scores(c + 1)
        for j in range(rc // rp):
            softmax_pv(c, j)


def _attention(proj, lq1, lk1, lq2, lk2, sub, batch, seq, d, lambda_init):
    t = proj.shape[0]
    nh = d // LANES
    dh = LANES // 2
    tq = min(Q_TILE, seq)
    nq = seq // tq
    rc = min(ATTN_SCORE_CHUNK, tq)
    rp = min(ATTN_SOFTMAX_CHUNK, rc)
    vec = pl.BlockSpec((1, dh), lambda b, h, qi: (0, 0))
    return pl.pallas_call(
        functools.partial(_attn_kernel, dh=dh, lambda_init=lambda_init, rc=rc, rp=rp),
        grid=(batch, nh, nq),
        scratch_shapes=[pltpu.VMEM((2, 2 * rc, seq), F32), pltpu.VMEM((2, rp, seq), BF16)],
        in_specs=[pl.BlockSpec((tq, LANES), lambda b, h, qi: (b * nq + qi, h)),
                  pl.BlockSpec((seq, LANES), lambda b, h, qi: (b, nh + h)),
                  pl.BlockSpec((seq, LANES), lambda b, h, qi: (b, 2 * nh + h)),
                  vec, vec, vec, vec,
                  pl.BlockSpec((1, LANES), lambda b, h, qi: (0, 0))],
        out_specs=pl.BlockSpec((tq, LANES), lambda b, h, qi: (b * nq + qi, h)),
        out_shape=jax.ShapeDtypeStruct((t, d), BF16),
        compiler_params=_params(("parallel", "parallel", "arbitrary")),
        name="diff_attn",
    )(proj, proj, proj, lq1, lk1, lq2, lk2, sub)


def _mixout_kernel(x_ref, ya_ref, cb_ref, cc_ref, cx_ref, g0_ref, g1_ref, pc_ref, px_ref, nc_ref, nx_ref,
                   cw_ref, wb0_ref, wb1_ref, wo_ref, o_ref, *, tiles_per_seq):
    i = pl.program_id(0)
    tm = x_ref.shape[0]
    u = cc_ref[...].astype(F32) * cx_ref[...].astype(F32)
    hr = pc_ref.shape[0]
    u_prev = pc_ref[hr - 1:hr, :].astype(F32) * px_ref[hr - 1:hr, :].astype(F32)
    u_next = nc_ref[0:1, :].astype(F32) * nx_ref[0:1, :].astype(F32)
    pos = i % tiles_per_seq
    u_prev = jnp.where(pos == 0, 0.0, u_prev)
    u_next = jnp.where(pos == tiles_per_seq - 1, 0.0, u_next)
    row = lax.broadcasted_iota(I32, u.shape, 0)
    u_up = jnp.where(row == 0, u_prev, pltpu.roll(u, 1, 0))
    u_dn = jnp.where(row == tm - 1, u_next, pltpu.roll(u, tm - 1, 0))
    conv = cw_ref[0:1, :] * u_up + cw_ref[1:2, :] * u + cw_ref[2:3, :] * u_dn
    y_conv = (cb_ref[...].astype(F32) * conv).astype(BF16)
    br0 = jnp.dot(ya_ref[...], wb0_ref[...], preferred_element_type=F32)
    br1 = jnp.dot(y_conv, wb1_ref[...], preferred_element_type=F32)
    z = jax.nn.sigmoid(g0_ref[...].astype(F32)) * br0 + jax.nn.sigmoid(g1_ref[...].astype(F32)) * br1
    o_ref[...] = x_ref[...] + jnp.dot(z.astype(BF16), wo_ref[...], preferred_element_type=F32)


def _mixer_out(x2, y_attn, proj, conv_w, wb0, wb1, wo, seq):
    t, d = x2.shape
    tm = min(ROW_TILE, seq)
    tps = seq // tm
    hr = BF16_SUBLANES
    nhb = t // hr
    row = lambda c: pl.BlockSpec((tm, d), lambda i: (i, c))
    prev = lambda c: pl.BlockSpec((hr, d), lambda i: (jnp.maximum(i * (tm // hr) - 1, 0), c))
    nxt = lambda c: pl.BlockSpec((hr, d), lambda i: (jnp.minimum((i + 1) * (tm // hr), nhb - 1), c))
    full = lambda r: pl.BlockSpec((r, d), lambda i: (0, 0))
    return pl.pallas_call(
        functools.partial(_mixout_kernel, tiles_per_seq=tps),
        grid=(t // tm,),
        in_specs=[row(0), row(0), row(3), row(4), row(5), row(6), row(7),
                  prev(4), prev(5), nxt(4), nxt(5),
                  full(conv_w.shape[0]), full(d), full(d), full(d)],
        out_specs=row(0),
        out_shape=jax.ShapeDtypeStruct((t, d), F32),
        compiler_params=_params(("parallel",)),
        name="mixer_out",
    )(x2, y_attn, proj, proj, proj, proj, proj, proj, proj, proj, proj, conv_w, wb0, wb1, wo)


def _memkv_kernel(m_ref, g_ref, w_ref, o_ref):
    h = _rms(m_ref[...], g_ref[...]).astype(BF16)
    o_ref[...] = jnp.dot(h, w_ref[...], preferred_element_type=F32).astype(o_ref.dtype)


def _mem_kv(mem2, g, w_bf):
    depth, d, n = w_bf.shape
    rows = mem2.shape[0]
    tn = min(n, 1024)
    return pl.pallas_call(
        _memkv_kernel,
        grid=(depth, n // tn),
        in_specs=[pl.BlockSpec((rows, d), lambda l, j: (0, 0)),
                  pl.BlockSpec((1, d), lambda l, j: (0, 0)),
                  pl.BlockSpec((None, d, tn), lambda l, j: (l, 0, j))],
        out_specs=pl.BlockSpec((None, rows, tn), lambda l, j: (l, 0, j)),
        out_shape=jax.ShapeDtypeStruct((depth, rows, n), BF16),
        compiler_params=_params(("parallel", "parallel")),
        name="mem_kv",
    )(mem2, g, w_bf)


def _cross_kernel(x_ref, g_ref, wq_ref, kv_ref, wo_ref, gf_ref, wr_ref, o_ref, hf_ref, lg_ref, *, heads):
    x = x_ref[...]
    d = x.shape[1]
    hd = d // heads
    h = _rms(x, g_ref[...]).astype(BF16)
    q = (jnp.dot(h, wq_ref[...], preferred_element_type=F32) * (hd ** -0.5)).astype(BF16)
    outs = []
    for hh in range(heads):
        qh = q[:, hh * hd:(hh + 1) * hd]
        kh = kv_ref[:, hh * hd:(hh + 1) * hd]
        vh = kv_ref[:, d + hh * hd:d + (hh + 1) * hd]
        s = lax.dot_general(qh, kh, NT_DIMS, preferred_element_type=F32)
        p = jnp.exp(s - jnp.max(s, axis=-1, keepdims=True))
        l = jnp.sum(p, axis=-1, keepdims=True)
        outs.append((jnp.dot(p.astype(BF16), vh, preferred_element_type=F32) / l).astype(BF16))
    o = jnp.concatenate(outs, axis=1)
    x_new = x + jnp.dot(o, wo_ref[...], preferred_element_type=F32)
    o_ref[...] = x_new
    h_ffn = _rms(x_new, gf_ref[...])
    hf_ref[...] = _pack_rows(h_ffn)
    lg_ref[...] = jnp.dot(h_ffn.astype(BF16), wr_ref[...], preferred_element_type=F32)


def _cross_attention(x2, g, wq, kv, layer, batch0, wo, g_ffn, w_r, seq, n_mem):
    t, d = x2.shape
    tm = min(ROW_TILE, seq)
    tps = seq // tm
    row = lambda w: pl.BlockSpec((tm, w), lambda i: (i, 0))
    const = lambda r, c: pl.BlockSpec((r, c), lambda i: (0, 0))
    return pl.pallas_call(
        functools.partial(_cross_kernel, heads=MEM_HEADS),
        grid=(t // tm,),
        in_specs=[row(d), const(1, d), const(d, d),
                  pl.BlockSpec((None, n_mem, 2 * d), lambda i: (layer, batch0 + i // tps, 0)),
                  const(d, d), const(1, d), const(d, LANES)],
        out_specs=[row(d), row(d // 2), row(LANES)],
        out_shape=[jax.ShapeDtypeStruct((t, d), F32), jax.ShapeDtypeStruct((t, d // 2), I32),
                   jax.ShapeDtypeStruct((t, LANES), F32)],
        compiler_params=_params(("parallel",)),
        name="cross_attn",
    )(x2, g, wq, kv, wo, g_ffn, w_r)


def _router_kernel(lg_ref, ri_ref, rw_ref, cnt_ref, carry_ref, *, sub):
    @pl.when(pl.program_id(0) == 0)
    def _():
        carry_ref[...] = jnp.zeros_like(carry_ref)

    lg = lg_ref[...]
    tm = lg.shape[0]
    lane = lax.broadcasted_iota(I32, lg.shape, 1).astype(F32)
    neg = -jnp.inf

    def first_argmax(v):
        m = jnp.max(v, axis=-1, keepdims=True)
        return m, jnp.min(jnp.where(v == m, lane, float(LANES)), axis=-1, keepdims=True)

    gmask = lane < N_GROUPS
    gl = jnp.where(gmask, lg, neg)
    gmax, gidx = first_argmax(gl)
    g_w = 1.0 / jnp.sum(jnp.where(gmask, jnp.exp(gl - gmax), 0.0), axis=-1, keepdims=True)

    lo = EXPERT_LANE0 + gidx * EXPERTS_PER_GROUP
    emask = (lane >= lo) & (lane < lo + EXPERTS_PER_GROUP)
    el = jnp.where(emask, lg, neg)
    mx1, i1 = first_argmax(el)
    mx2, i2 = first_argmax(jnp.where(lane == i1, neg, el))
    esum = jnp.sum(jnp.where(emask, jnp.exp(el - mx1), 0.0), axis=-1, keepdims=True)
    p1 = 1.0 / esum
    p2 = jnp.exp(mx2 - mx1) / esum
    den = p1 + p2
    w1 = g_w * (p1 / den)
    w2 = g_w * (p2 / den)

    memb = jnp.where((lane == i1) | (lane == i2), 1.0, 0.0)
    r_i = lax.broadcasted_iota(I32, (sub, sub), 0)
    c_i = lax.broadcasted_iota(I32, (sub, sub), 1)
    ltri = jnp.where(c_i < r_i, 1.0, 0.0).astype(BF16)
    tots = []
    carry = carry_ref[...]
    for r0 in range(0, tm, sub):
        mb = memb[r0:r0 + sub]
        tots.append(jnp.dot(ltri, mb.astype(BF16), preferred_element_type=F32) + carry)
        carry = carry + jnp.sum(mb, axis=0, keepdims=True)
    carry_ref[...] = carry
    tot = jnp.concatenate(tots, axis=0)
    rank1 = jnp.sum(jnp.where(lane == i1, tot, 0.0), axis=-1, keepdims=True).astype(I32)
    rank2 = jnp.sum(jnp.where(lane == i2, tot, 0.0), axis=-1, keepdims=True).astype(I32)

    e1 = (i1 - EXPERT_LANE0).astype(I32)
    e2 = (i2 - EXPERT_LANE0).astype(I32)
    ri_ref[...] = jnp.where(lane == 0, e1,
                            jnp.where(lane == 1, e2, jnp.where(lane == 2, rank1, jnp.where(lane == 3, rank2, 0))))
    rw_ref[...] = jnp.where(lane == 0, w1, jnp.where(lane == 1, w2, 0.0))
    cnt_ref[...] = jnp.broadcast_to(carry, cnt_ref.shape)


def _router(lg, seq):
    t = lg.shape[0]
    tm = min(ROUTER_TILE, seq)
    row = pl.BlockSpec((tm, LANES), lambda i: (i, 0))
    return pl.pallas_call(
        functools.partial(_router_kernel, sub=min(ROW_TILE, tm)),
        grid=(t // tm,),
        in_specs=[row],
        out_specs=[row, row, pl.BlockSpec((F32_SUBLANES, LANES), lambda i: (0, 0))],
        out_shape=[jax.ShapeDtypeStruct((t, LANES), I32),
                   jax.ShapeDtypeStruct((t, LANES), F32),
                   jax.ShapeDtypeStruct((F32_SUBLANES, LANES), F32)],
        scratch_shapes=[pltpu.VMEM((1, LANES), F32)],
        compiler_params=_params(("arbitrary",)),
        name="moe_router",
    )(lg)


def _dispatch_kernel(dest_ref, x_ref, g_ref, xb_in_ref, xb_ref, h_ref, sem):
    del xb_in_ref
    i = pl.program_id(0)
    tm = x_ref.shape[0]
    h_ref[...] = _rms(x_ref[...], g_ref[...])
    base = i * (TOP_K * tm)

    for t in range(tm):
        for k in range(TOP_K):
            dst = dest_ref[base + (TOP_K * t + k)]
            pltpu.make_async_copy(h_ref.at[pl.ds(t, 1), :], xb_ref.at[pl.ds(dst, 1), :], sem).start()
    for _ in range(TOP_K):
        pltpu.make_async_copy(h_ref, xb_ref.at[pl.ds(0, tm), :], sem).wait()


def _dispatch(dest, x2, g, xb, seq):
    t, d = x2.shape
    tm = min(ROW_TILE, seq)
    return pl.pallas_call(
        _dispatch_kernel,
        grid_spec=pltpu.PrefetchScalarGridSpec(
            num_scalar_prefetch=1,
            grid=(t // tm,),
            in_specs=[pl.BlockSpec((tm, d), lambda i, dest: (i, 0)),
                      pl.BlockSpec((1, d), lambda i, dest: (0, 0)),
                      pl.BlockSpec(memory_space=pl.ANY)],
            out_specs=pl.BlockSpec(memory_space=pl.ANY),
            scratch_shapes=[pltpu.VMEM((tm, d), F32), pltpu.SemaphoreType.DMA]),
        out_shape=jax.ShapeDtypeStruct(xb.shape, xb.dtype),
        input_output_aliases={3: 0},
        compiler_params=_params(("arbitrary",)),
        name="moe_dispatch",
    )(dest, x2, g, xb)


def _expert_kernel(be_ref, first_ref, slot_ref, nxt_ref, na_ref, xb_ref, wg_hbm, wu_hbm, wd_hbm, yb_ref,
                   wg_buf, wu_buf, wd_buf, wgb_ref, wub_ref, wdb_ref, sem, *, layer):
    i = pl.program_id(0)

    def fetch(e, slot):
        return (pltpu.make_async_copy(wg_hbm.at[layer, e], wg_buf.at[slot], sem.at[slot, 0]),
                pltpu.make_async_copy(wu_hbm.at[layer, e], wu_buf.at[slot], sem.at[slot, 1]),
                pltpu.make_async_copy(wd_hbm.at[layer, e], wd_buf.at[slot], sem.at[slot, 2]))

    @pl.when(i < na_ref[0])
    def _():
        @pl.when(first_ref[i] == 1)
        def _():
            e, slot = be_ref[i], slot_ref[i]

            @pl.when(i == 0)
            def _():
                for cp in fetch(e, slot):
                    cp.start()

            for cp in fetch(e, slot):
                cp.wait()
            wgb_ref[...] = wg_buf[slot].astype(BF16)
            wub_ref[...] = wu_buf[slot].astype(BF16)
            wdb_ref[...] = wd_buf[slot].astype(BF16)

            @pl.when(nxt_ref[i] >= 0)
            def _():
                for cp in fetch(nxt_ref[i], 1 - slot):
                    cp.start()

        xb = _unpack_rows(xb_ref[...]).astype(BF16)
        gate = jnp.dot(xb, wgb_ref[...], preferred_element_type=F32)
        up = jnp.dot(xb, wub_ref[...], preferred_element_type=F32)
        act = (jax.nn.silu(gate) * up).astype(BF16)
        yb_ref[...] = _pack_rows(jnp.dot(act, wdb_ref[...], preferred_element_type=F32))

    @pl.when(i >= na_ref[0])
    def _():
        yb_ref[...] = jnp.zeros_like(yb_ref)


def _experts(plan, xb, w_gate, w_up, w_down, layer, blk):
    rows = xb.shape[0]
    d, de = w_gate.shape[-2:]
    nb = rows // blk

    def row_map(i, be, first, slot, nxt, na):
        return (jnp.minimum(i, jnp.maximum(na[0] - 1, 0)), 0)

    hbm = pl.BlockSpec(memory_space=pl.ANY)
    return pl.pallas_call(
        functools.partial(_expert_kernel, layer=layer),
        grid_spec=pltpu.PrefetchScalarGridSpec(
            num_scalar_prefetch=5,
            grid=(nb,),
            in_specs=[pl.BlockSpec((blk, d // 2), row_map), hbm, hbm, hbm],
            out_specs=pl.BlockSpec((blk, d // 2), lambda i, *_: (i, 0)),
            scratch_shapes=[pltpu.VMEM((2, d, de), F32), pltpu.VMEM((2, d, de), F32), pltpu.VMEM((2, de, d), F32),
                            pltpu.VMEM((d, de), BF16), pltpu.VMEM((d, de), BF16), pltpu.VMEM((de, d), BF16),
                            pltpu.SemaphoreType.DMA((2, 3))]),
        out_shape=jax.ShapeDtypeStruct((rows, d // 2), I32),
        compiler_params=_params(("arbitrary",)),
        name="moe_experts",
    )(*plan, xb, w_gate, w_up, w_down)


def _combine_kernel(dest_ref, x_ref, rw_ref, yb_ref, gf_ref, o_ref, y_ref, sem, *, final_norm):
    i = pl.program_id(0)
    tm = x_ref.shape[0]
    base = i * (TOP_K * tm)

    for t in range(tm):
        for k in range(TOP_K):
            src = dest_ref[base + (TOP_K * t + k)]
            pltpu.make_async_copy(yb_ref.at[pl.ds(src, 1), :], y_ref.at[k, pl.ds(t, 1), :], sem).start()
    for k in range(TOP_K):
        pltpu.make_async_copy(yb_ref.at[pl.ds(0, tm), :], y_ref.at[k], sem).wait()
    w = rw_ref[...]
    out = x_ref[...]
    for k in range(TOP_K):
        out = out + w[:, k:k + 1] * y_ref[k]
    if final_norm:
        out = _rms(out, gf_ref[...])
    o_ref[...] = out


def _combine(dest, x2, rw, yb, g_final, seq, final_norm):
    t, d = x2.shape
    tm = min(ROW_TILE, seq)
    return pl.pallas_call(
        functools.partial(_combine_kernel, final_norm=final_norm),
        grid_spec=pltpu.PrefetchScalarGridSpec(
            num_scalar_prefetch=1,
            grid=(t // tm,),
            in_specs=[pl.BlockSpec((tm, d), lambda i, dest: (i, 0)),
                      pl.BlockSpec((tm, LANES), lambda i, dest: (i, 0)),
                      pl.BlockSpec(memory_space=pl.ANY),
                      pl.BlockSpec((1, d), lambda i, dest: (0, 0))],
            out_specs=pl.BlockSpec((tm, d), lambda i, dest: (i, 0)),
            scratch_shapes=[pltpu.VMEM((TOP_K, tm, d), F32), pltpu.SemaphoreType.DMA]),
        out_shape=jax.ShapeDtypeStruct((t, d), F32),
        compiler_params=_params(("arbitrary",)),
        name="moe_combine",
    )(dest, x2, rw, yb, g_final)


def _sc_gather_rows(table, idx):
    b = idx.shape[0]
    d = table.shape[1]
    per_w = b // SC_WORKERS
    ch = SC_CHUNK_ROWS
    mesh = plsc.VectorSubcoreMesh(core_axis_name="c", subcore_axis_name="s")

    n_ch = per_w // ch
    assert per_w * SC_WORKERS == b and n_ch * ch == per_w and n_ch % 2 == 0

    def body(table_hbm, idx_hbm, out_hbm, idx_v, rows_v, gsem, wsem):
        wid = lax.axis_index("s") * SC_CORES + lax.axis_index("c")
        base = wid * per_w
        pltpu.sync_copy(idx_hbm.at[pl.ds(base, per_w)], idx_v)

        def write_back(c, buf):
            off = pl.multiple_of(c * ch, ch)
            return pltpu.make_async_copy(rows_v.at[buf], out_hbm.at[pl.ds(base + off, ch)], wsem.at[buf])

        @pl.loop(0, n_ch, step=2)
        def _(c0):
            for buf in range(2):
                c = c0 + buf

                @pl.when(c0 > 0)
                def _():
                    write_back(c - 2, buf).wait()

                off = pl.multiple_of(c * ch, ch)
                pltpu.async_copy(table_hbm.at[idx_v.at[pl.ds(off, ch)]], rows_v.at[buf], gsem).wait()
                write_back(c, buf).start()

        for buf in range(2):
            write_back(n_ch - 2 + buf, buf).wait()

    return pl.kernel(body, out_type=jax.ShapeDtypeStruct((b, d), table.dtype), mesh=mesh,
                     scratch_types=[pltpu.VMEM((per_w,), I32), pltpu.VMEM((2, ch, d), table.dtype),
                                    pltpu.SemaphoreType.DMA, pltpu.SemaphoreType.DMA((2,))],
                     name="sc_gather")(table, idx)


def _sc_scatter_rows(rows, idx, n_out):
    t, d = rows.shape
    _, top_k, n_ch, ch = idx.shape
    per_w = n_ch * ch
    assert per_w * SC_WORKERS == t and idx.shape[0] == SC_WORKERS
    mesh = plsc.VectorSubcoreMesh(core_axis_name="c", subcore_axis_name="s")

    def body(rows_hbm, idx_hbm, out_hbm, idx_v, rows_v, sem):
        wid = lax.axis_index("s") * SC_CORES + lax.axis_index("c")
        base = wid * per_w
        pltpu.sync_copy(idx_hbm.at[wid], idx_v)

        @pl.loop(0, n_ch)
        def _(c):
            off = pl.multiple_of(c * ch, ch)
            pltpu.sync_copy(rows_hbm.at[pl.ds(base + off, ch)], rows_v)
            for k in range(top_k):
                pltpu.async_copy(rows_v, out_hbm.at[idx_v.at[k, c]], sem).wait()

    return pl.kernel(body, out_type=jax.ShapeDtypeStruct((n_out, d), rows.dtype), mesh=mesh,
                     scratch_types=[pltpu.VMEM((top_k, n_ch, ch), I32), pltpu.VMEM((ch, d), rows.dtype),
                                    pltpu.SemaphoreType.DMA], name="sc_scatter")(rows, idx)


def _combine_dense_kernel(x_ref, rw_ref, y0_ref, y1_ref, gf_ref, o_ref, *, final_norm):
    w = rw_ref[...]
    out = x_ref[...] + w[:, 0:1] * _unpack_rows(y0_ref[...]) + w[:, 1:2] * _unpack_rows(y1_ref[...])
    if final_norm:
        out = _rms(out, gf_ref[...])
    o_ref[...] = out


def _combine_dense(x2, rw, y2, g_final, seq, final_norm, part, n_parts):
    t, d = x2.shape
    tm = min(ROW_TILE, seq)
    nt = t // tm // n_parts
    row = lambda w: pl.BlockSpec((tm, w), lambda i: (part * nt + i, 0))
    return pl.pallas_call(
        functools.partial(_combine_dense_kernel, final_norm=final_norm),
        grid=(nt,),
        in_specs=[row(d), row(LANES), pl.BlockSpec((tm, d // 2), lambda i: (i, 0)),
                  pl.BlockSpec((tm, d // 2), lambda i: (nt + i, 0)), pl.BlockSpec((1, d), lambda i: (0, 0))],
        out_specs=row(d),
        out_shape=jax.ShapeDtypeStruct((t, d), F32),
        input_output_aliases={0: 0},
        compiler_params=_params(("parallel",)),
        name="moe_combine_dense",
    )(x2, rw, y2, y2, g_final)


def _route_plan(ri, counts_f, blk, n_blocks):
    cnt = counts_f[0, EXPERT_LANE0:EXPERT_LANE0 + N_EXPERTS].astype(I32)
    nblk = (cnt + blk - 1) // blk
    cum = jnp.cumsum(nblk)
    pad_start = (cum - nblk) * blk
    e = ri[:, 0:TOP_K]
    rank = ri[:, TOP_K:2 * TOP_K]
    onehot = e[:, :, None] == jnp.arange(N_EXPERTS, dtype=I32)[None, None, :]
    dest = rank + jnp.sum(jnp.where(onehot, pad_start[None, None, :], 0), axis=-1)
    blocks = jnp.arange(n_blocks, dtype=I32)
    block_expert = jnp.minimum(jnp.sum((cum[None, :] <= blocks[:, None]).astype(I32), axis=-1), N_EXPERTS - 1)
    n_active = cum[-1:].astype(I32)
    experts = jnp.arange(N_EXPERTS, dtype=I32)
    is_block_expert = block_expert[:, None] == experts[None, :]
    first = jnp.sum(jnp.where(is_block_expert, ((cum - nblk)[None, :] == blocks[:, None]).astype(I32), 0), axis=-1)
    used = (nblk > 0).astype(I32)
    slot_e = (jnp.cumsum(used) - used) % 2
    later = (experts[None, :] > experts[:, None]) & (nblk[None, :] > 0)
    next_e = jnp.min(jnp.where(later, experts[None, :], N_EXPERTS), axis=-1)
    next_e = jnp.where(next_e == N_EXPERTS, -1, next_e)
    slot = jnp.sum(jnp.where(is_block_expert, slot_e[None, :], 0), axis=-1)
    nxt = jnp.sum(jnp.where(is_block_expert, next_e[None, :], 0), axis=-1)
    plan = (block_expert, first.astype(I32), slot.astype(I32), nxt.astype(I32), n_active)
    return dest.reshape(-1).astype(I32), plan


def kernel(x, mem, norm_mix, w_in, lambda_q1, lambda_k1, lambda_q2, lambda_k2, subln, conv_w, w_branch, w_o,
           norm_cross, norm_mem, w_cq, w_ckv, w_co, norm_ffn, w_router_group, w_router_expert, w_exp_gate,
           w_exp_up, w_exp_down, norm_final):
    batch, seq, d = x.shape
    depth = w_in.shape[0]
    n_mem = mem.shape[1]
    t = batch * seq
    dh = d // (2 * DA_HEADS)
    assert 2 * dh == LANES and d % LANES == 0 and w_in.shape[2] == 8 * d
    assert w_router_group.shape[2] == N_GROUPS and w_router_expert.shape[2] == N_EXPERTS
    blk = min(EXPERT_BLOCK, seq)
    n_blocks = (t * TOP_K) // blk + N_EXPERTS
    parts = COMBINE_PARTS if t % (COMBINE_PARTS * SC_WORKERS * SC_CHUNK_ROWS) == 0 else 1
    tokens_per_worker = t // SC_WORKERS

    tabs = _rotary_tables(seq, dh)
    x2 = x.reshape(t, d)
    kv = _mem_kv(mem.reshape(batch * n_mem, d), norm_mem.reshape(1, d), w_ckv.astype(BF16))
    w_r = jnp.concatenate([w_router_group, w_router_expert,
                           jnp.zeros((depth, d, LANES - N_GROUPS - N_EXPERTS), F32)], axis=-1).astype(BF16)

    for l in range(depth):
        lambda_init = 0.8 - 0.6 * math.exp(-0.3 * l)
        proj = _inproj(x2, norm_mix[l].reshape(1, d), w_in[l].astype(BF16), tabs, seq, dh)
        y_attn = _attention(proj, lambda_q1[l].reshape(1, dh), lambda_k1[l].reshape(1, dh),
                            lambda_q2[l].reshape(1, dh), lambda_k2[l].reshape(1, dh),
                            subln[l].reshape(1, LANES), batch, seq, d, lambda_init)
        x2 = _mixer_out(x2, y_attn, proj, conv_w[l], w_branch[l, 0].astype(BF16), w_branch[l, 1].astype(BF16),
                        w_o[l].astype(BF16), seq)
        x2, h_ffn, logits = _cross_attention(x2, norm_cross[l].reshape(1, d), w_cq[l].astype(BF16), kv, l, 0,
                                             w_co[l].astype(BF16), norm_ffn[l].reshape(1, d), w_r[l], seq, n_mem)
        ri, rw, counts = _router(logits, seq)
        dest, plan = _route_plan(ri, counts, blk, n_blocks)
        dest2 = dest.reshape(t, TOP_K)
        scatter_idx = dest2.reshape(SC_WORKERS, tokens_per_worker // SC_CHUNK_ROWS, SC_CHUNK_ROWS, TOP_K)
        xb = _sc_scatter_rows(h_ffn, scatter_idx.transpose(0, 3, 1, 2), n_blocks * blk)
        yb = _experts(plan, xb, w_exp_gate, w_exp_up, w_exp_down, l, blk)
        gather_idx = dest2.reshape(parts, t // parts, TOP_K).transpose(0, 2, 1).reshape(parts, -1)
        gathered = [_sc_gather_rows(yb, gather_idx[p]) for p in range(parts)]
        for p in range(parts):
            x2 = _combine_dense(x2, rw, gathered[p], norm_final.reshape(1, d), seq, l == depth - 1, p, parts)
    return x2.reshape(batch, seq, d)
```

```python
import functools
import math

import jax
import jax.numpy as jnp
from jax import lax
from jax.experimental import pallas as pl
from jax.experimental.pallas import tpu as pltpu
from jax.experimental.pallas import tpu_sc as plsc

EPS = 1e-6
DA_HEADS = 8
MEM_HEADS = 4
N_GROUPS = 4
EXPERTS_PER_GROUP = 8
N_EXPERTS = N_GROUPS * EXPERTS_PER_GROUP
TOP_K = 2
ROPE_THETA = 500000.0
LANES = 128
F32_SUBLANES = 8
BF16_SUBLANES = 16
EXPERT_LANE0 = N_GROUPS
ROW_TILE = 512
ROUTER_TILE = 2048
Q_TILE = 2048
EXPERT_BLOCK = 256
PROJ_ROW_CHUNK = 1024
PROJ_COMBINE_TILE = 1024
ATTN_SCORE_CHUNK = 128
ATTN_SOFTMAX_CHUNK = 128
COMBINE_PARTS = 4
SC_CORES = 2
SC_WORKERS = 32
SC_CHUNK_ROWS = 64
VMEM_LIMIT = 56 * 1024 * 1024

F32 = jnp.float32
BF16 = jnp.bfloat16
I32 = jnp.int32
NT_DIMS = (((1,), (1,)), ((), ()))


def _rms(x, g):
    return x * lax.rsqrt(jnp.mean(x * x, axis=-1, keepdims=True) + EPS) * g


def _pack_rows(x):
    half = x.shape[1] // 2
    lo = lax.bitcast_convert_type(x[:, :half].astype(jnp.bfloat16).astype(F32), jnp.uint32)
    hi = lax.bitcast_convert_type(x[:, half:].astype(jnp.bfloat16).astype(F32), jnp.uint32)
    return lax.bitcast_convert_type((hi & jnp.uint32(0xFFFF0000)) | (lo >> 16), I32)


def _unpack_rows(w):
    u = lax.bitcast_convert_type(w, jnp.uint32)
    lo = lax.bitcast_convert_type(u << 16, F32)
    hi = lax.bitcast_convert_type(u & jnp.uint32(0xFFFF0000), F32)
    return jnp.concatenate([lo, hi], axis=1)


def _params(sem, vmem=VMEM_LIMIT):
    return pltpu.CompilerParams(dimension_semantics=sem, vmem_limit_bytes=vmem)


def _inproj_kernel(x_ref, g_ref, w_ref, cos_ref, s1_ref, s2_ref, o_ref, h_ref, *, rc, half, q_scale):
    @pl.when(pl.program_id(1) == 0)
    def _():
        h_ref[...] = _rms(x_ref[...], g_ref[...]).astype(BF16)

    _proj_column_block(h_ref, w_ref, cos_ref, s1_ref, s2_ref, o_ref, rc, half, q_scale)


def _inproj_combine_kernel(x_ref, rw_ref, y0_ref, y1_ref, g_ref, w_ref, cos_ref, s1_ref, s2_ref, *rest,
                           rc, half, q_scale):
    xnew_ref, o_ref, h_ref = rest[-3:]

    @pl.when(pl.program_id(1) == 0)
    def _():
        w = rw_ref[...]
        x_new = x_ref[...] + w[:, 0:1] * _unpack_rows(y0_ref[...]) + w[:, 1:2] * _unpack_rows(y1_ref[...])
        xnew_ref[...] = x_new
        h_ref[...] = _rms(x_new, g_ref[...]).astype(BF16)

    _proj_column_block(h_ref, w_ref, cos_ref, s1_ref, s2_ref, o_ref, rc, half, q_scale)


def _proj_column_block(h_ref, w_ref, cos_ref, s1_ref, s2_ref, o_ref, rc, half, q_scale):
    j = pl.program_id(1)
    tm, tn = o_ref.shape

    def run(rot, scale):
        def body(r, carry):
            r0 = pl.multiple_of(r * rc, rc)
            acc = jnp.dot(h_ref[pl.ds(r0, rc), :], w_ref[...], preferred_element_type=F32)
            if not rot:
                o_ref[pl.ds(r0, rc), :] = acc.astype(o_ref.dtype)
                return carry
            c = cos_ref[pl.ds(r0, rc), :]
            s1 = s1_ref[pl.ds(r0, rc), :]
            s2 = s2_ref[pl.ds(r0, rc), :]
            for cc in range(tn // LANES):
                a = acc[:, cc * LANES:(cc + 1) * LANES]
                a = a * c + pltpu.roll(a, half, 1) * s1 + pltpu.roll(a, LANES - half, 1) * s2
                if scale != 1.0:
                    a = a * scale
                o_ref[pl.ds(r0, rc), cc * LANES:(cc + 1) * LANES] = a.astype(o_ref.dtype)
            return carry
        lax.fori_loop(0, tm // rc, body, 0)

    @pl.when(j == 0)
    def _():
        run(True, q_scale)

    @pl.when(j == 1)
    def _():
        run(True, 1.0)

    @pl.when(j >= 2)
    def _():
        run(False, 1.0)


def _inproj(x2, g, w_bf, tabs, seq, dh):
    t, d = x2.shape
    n = w_bf.shape[1]
    tm, tn = seq, d
    rc = min(PROJ_ROW_CHUNK, tm)
    cos_t, s1_t, s2_t = tabs
    tab_spec = pl.BlockSpec((seq, LANES), lambda i, j: (0, 0))
    return pl.pallas_call(
        functools.partial(_inproj_kernel, rc=rc, half=dh // 8, q_scale=dh ** -0.5 * math.log2(math.e)),
        grid=(t // tm, n // tn),
        in_specs=[pl.BlockSpec((tm, d), lambda i, j: (i, 0)),
                  pl.BlockSpec((1, d), lambda i, j: (0, 0)),
                  pl.BlockSpec((d, tn), lambda i, j: (0, j)),
                  tab_spec, tab_spec, tab_spec],
        out_specs=pl.BlockSpec((tm, tn), lambda i, j: (i, j)),
        out_shape=jax.ShapeDtypeStruct((t, n), BF16),
        scratch_shapes=[pltpu.VMEM((tm, d), BF16)],
        compiler_params=_params(("parallel", "arbitrary")),
        name="inproj",
    )(x2, g, w_bf, cos_t, s1_t, s2_t)


def _inproj_combine(x2, rw, y2, g, w_bf, tabs, proj_prev, seq, dh, part, n_parts):
    t, d = x2.shape
    n = w_bf.shape[1]
    tm, tn = min(PROJ_COMBINE_TILE, seq), d
    ntp = t // n_parts // tm
    row0 = part * ntp
    tiles_per_seq = seq // tm
    cos_t, s1_t, s2_t = tabs
    rows = lambda w: pl.BlockSpec((tm, w), lambda i, j: (row0 + i, 0))
    tab_spec = pl.BlockSpec((tm, LANES), lambda i, j: ((row0 + i) % tiles_per_seq, 0))
    in_specs = [rows(d), rows(LANES),
                pl.BlockSpec((tm, d // 2), lambda i, j: (i, 0)),
                pl.BlockSpec((tm, d // 2), lambda i, j: (ntp + i, 0)),
                pl.BlockSpec((1, d), lambda i, j: (0, 0)),
                pl.BlockSpec((d, tn), lambda i, j: (0, j)),
                tab_spec, tab_spec, tab_spec]
    args = [x2, rw, y2, y2, g, w_bf, cos_t, s1_t, s2_t]
    aliases = {0: 0}
    if proj_prev is not None:
        in_specs.append(pl.BlockSpec(memory_space=pl.ANY))
        args.append(proj_prev)
        aliases[len(args) - 1] = 1
    return pl.pallas_call(
        functools.partial(_inproj_combine_kernel, rc=min(PROJ_ROW_CHUNK, tm), half=dh // 8,
                          q_scale=dh ** -0.5 * math.log2(math.e)),
        grid=(ntp, n // tn),
        in_specs=in_specs,
        out_specs=[rows(d), pl.BlockSpec((tm, tn), lambda i, j: (row0 + i, j))],
        out_shape=[jax.ShapeDtypeStruct((t, d), F32), jax.ShapeDtypeStruct((t, n), BF16)],
        scratch_shapes=[pltpu.VMEM((tm, d), BF16)],
        input_output_aliases=aliases,
        compiler_params=_params(("parallel", "arbitrary")),
        name="inproj_combine",
    )(*args)


def _rotary_tables(seq, dh):
    rot = dh // 4
    half = rot // 2
    inv = jnp.float32(ROPE_THETA) ** (-jnp.arange(0, rot, 2, dtype=F32) / rot)
    ang = jnp.arange(seq, dtype=F32)[:, None] * inv[None, :]
    cos, sin = jnp.cos(ang), jnp.sin(ang)
    lane = jnp.arange(LANES) % dh
    idx = lane % half
    c_t = jnp.where(lane < rot, cos[:, idx], 1.0)
    s1_t = jnp.where((lane >= half) & (lane < rot), sin[:, idx], 0.0)
    s2_t = jnp.where(lane < half, -sin[:, idx], 0.0)
    return c_t.astype(F32), s1_t.astype(F32), s2_t.astype(F32)


def _attn_kernel(q_ref, k_ref, v_ref, lq1_ref, lk1_ref, lq2_ref, lk2_ref, sub_ref, o_ref,
                 s_ref, p_ref, *, dh, lambda_init, rc, rp):
    tq = q_ref.shape[0]
    n_chunks = tq // rc
    lane = lax.broadcasted_iota(I32, (1, LANES), 1)
    m1 = jnp.where(lane < dh, 1.0, 0.0).astype(BF16)
    m2 = jnp.where(lane >= dh, 1.0, 0.0).astype(BF16)
    lam = (jnp.exp(jnp.sum(lq1_ref[...] * lk1_ref[...], axis=-1, keepdims=True))
           - jnp.exp(jnp.sum(lq2_ref[...] * lk2_ref[...], axis=-1, keepdims=True)) + lambda_init)

    def scores(c):
        qc = q_ref[c * rc:(c + 1) * rc, :]
        qq = jnp.concatenate([qc * m1, qc * m2], axis=0)
        s_ref[c % 2] = lax.dot_general(qq, k_ref[...], NT_DIMS, preferred_element_type=F32)

    def softmax_pv(c, j):
        r0 = j * rp
        s1 = s_ref[c % 2, r0:r0 + rp, :]
        s2 = s_ref[c % 2, rc + r0:rc + r0 + rp, :]
        e1 = jnp.exp2(s1 - jnp.max(s1, axis=-1, keepdims=True))
        e2 = jnp.exp2(s2 - jnp.max(s2, axis=-1, keepdims=True))
        l1 = jnp.sum(e1, axis=-1, keepdims=True)
        l2 = jnp.sum(e2, axis=-1, keepdims=True)
        slot = (c * (rc // rp) + j) % 2
        p_ref[slot] = (e1 - e2 * (lam * l1 / l2)).astype(BF16)
        od = jnp.dot(p_ref[slot], v_ref[...], preferred_element_type=F32) / l1
        od = _rms(od, sub_ref[...]) * (1.0 - lambda_init)
        o_ref[c * rc + r0:c * rc + r0 + rp, :] = od.astype(o_ref.dtype)

    scores(0)
    for c in range(n_chunks):
        if c + 1 < n_chunks:
            scores(c + 1)
        for j in range(rc // rp):
            softmax_pv(c, j)


def _attention(proj, lq1, lk1, lq2, lk2, sub, batch, seq, d, lambda_init):
    t = proj.shape[0]
    nh = d // LANES
    dh = LANES // 2
    tq = min(Q_TILE, seq)
    nq = seq // tq
    rc = min(ATTN_SCORE_CHUNK, tq)
    rp = min(ATTN_SOFTMAX_CHUNK, rc)
    vec = pl.BlockSpec((1, dh), lambda b, h, qi: (0, 0))
    return pl.pallas_call(
        functools.partial(_attn_kernel, dh=dh, lambda_init=lambda_init, rc=rc, rp=rp),
        grid=(batch, nh, nq),
        scratch_shapes=[pltpu.VMEM((2, 2 * rc, seq), F32), pltpu.VMEM((2, rp, seq), BF16)],
        in_specs=[pl.BlockSpec((tq, LANES), lambda b, h, qi: (b * nq + qi, h)),
                  pl.BlockSpec((seq, LANES), lambda b, h, qi: (b, nh + h)),
                  pl.BlockSpec((seq, LANES), lambda b, h, qi: (b, 2 * nh + h)),
                  vec, vec, vec, vec,
                  pl.BlockSpec((1, LANES), lambda b, h, qi: (0, 0))],
        out_specs=pl.BlockSpec((tq, LANES), lambda b, h, qi: (b * nq + qi, h)),
        out_shape=jax.ShapeDtypeStruct((t, d), BF16),
        compiler_params=_params(("parallel", "parallel", "arbitrary")),
        name="diff_attn",
    )(proj, proj, proj, lq1, lk1, lq2, lk2, sub)


def _mixout_kernel(x_ref, ya_ref, cb_ref, cc_ref, cx_ref, g0_ref, g1_ref, pc_ref, px_ref, nc_ref, nx_ref,
                   cw_ref, wb0_ref, wb1_ref, wo_ref, o_ref, *, tiles_per_seq):
    i = pl.program_id(0)
    tm = x_ref.shape[0]
    u = cc_ref[...].astype(F32) * cx_ref[...].astype(F32)
    hr = pc_ref.shape[0]
    u_prev = pc_ref[hr - 1:hr, :].astype(F32) * px_ref[hr - 1:hr, :].astype(F32)
    u_next = nc_ref[0:1, :].astype(F32) * nx_ref[0:1, :].astype(F32)
    pos = i % tiles_per_seq
    u_prev = jnp.where(pos == 0, 0.0, u_prev)
    u_next = jnp.where(pos == tiles_per_seq - 1, 0.0, u_next)
    row = lax.broadcasted_iota(I32, u.shape, 0)
    u_up = jnp.where(row == 0, u_prev, pltpu.roll(u, 1, 0))
    u_dn = jnp.where(row == tm - 1, u_next, pltpu.roll(u, tm - 1, 0))
    conv = cw_ref[0:1, :] * u_up + cw_ref[1:2, :] * u + cw_ref[2:3, :] * u_dn
    y_conv = (cb_ref[...].astype(F32) * conv).astype(BF16)
    br0 = jnp.dot(ya_ref[...], wb0_ref[...], preferred_element_type=F32)
    br1 = jnp.dot(y_conv, wb1_ref[...], preferred_element_type=F32)
    z = jax.nn.sigmoid(g0_ref[...].astype(F32)) * br0 + jax.nn.sigmoid(g1_ref[...].astype(F32)) * br1
    o_ref[...] = x_ref[...] + jnp.dot(z.astype(BF16), wo_ref[...], preferred_element_type=F32)


def _mixer_out(x2, y_attn, proj, conv_w, wb0, wb1, wo, seq):
    t, d = x2.shape
    tm = min(ROW_TILE, seq)
    tps = seq // tm
    hr = BF16_SUBLANES
    nhb = t // hr
    row = lambda c: pl.BlockSpec((tm, d), lambda i: (i, c))
    prev = lambda c: pl.BlockSpec((hr, d), lambda i: (jnp.maximum(i * (tm // hr) - 1, 0), c))
    nxt = lambda c: pl.BlockSpec((hr, d), lambda i: (jnp.minimum((i + 1) * (tm // hr), nhb - 1), c))
    full = lambda r: pl.BlockSpec((r, d), lambda i: (0, 0))
    return pl.pallas_call(
        functools.partial(_mixout_kernel, tiles_per_seq=tps),
        grid=(t // tm,),
        in_specs=[row(0), row(0), row(3), row(4), row(5), row(6), row(7),
                  prev(4), prev(5), nxt(4), nxt(5),
                  full(conv_w.shape[0]), full(d), full(d), full(d)],
        out_specs=row(0),
        out_shape=jax.ShapeDtypeStruct((t, d), F32),
        compiler_params=_params(("parallel",)),
        name="mixer_out",
    )(x2, y_attn, proj, proj, proj, proj, proj, proj, proj, proj, proj, conv_w, wb0, wb1, wo)


def _memkv_kernel(m_ref, g_ref, w_ref, o_ref):
    h = _rms(m_ref[...], g_ref[...]).astype(BF16)
    o_ref[...] = jnp.dot(h, w_ref[...], preferred_element_type=F32).astype(o_ref.dtype)


def _mem_kv(mem2, g, w_bf):
    depth, d, n = w_bf.shape
    rows = mem2.shape[0]
    tn = min(n, 1024)
    return pl.pallas_call(
        _memkv_kernel,
        grid=(depth, n // tn),
        in_specs=[pl.BlockSpec((rows, d), lambda l, j: (0, 0)),
                  pl.BlockSpec((1, d), lambda l, j: (0, 0)),
                  pl.BlockSpec((None, d, tn), lambda l, j: (l, 0, j))],
        out_specs=pl.BlockSpec((None, rows, tn), lambda l, j: (l, 0, j)),
        out_shape=jax.ShapeDtypeStruct((depth, rows, n), BF16),
        compiler_params=_params(("parallel", "parallel")),
        name="mem_kv",
    )(mem2, g, w_bf)


def _cross_kernel(x_ref, g_ref, wq_ref, kv_ref, wo_ref, gf_ref, wr_ref, o_ref, hf_ref, lg_ref, *, heads):
    x = x_ref[...]
    d = x.shape[1]
    hd = d // heads
    h = _rms(x, g_ref[...]).astype(BF16)
    q = (jnp.dot(h, wq_ref[...], preferred_element_type=F32) * (hd ** -0.5)).astype(BF16)
    outs = []
    for hh in range(heads):
        qh = q[:, hh * hd:(hh + 1) * hd]
        kh = kv_ref[:, hh * hd:(hh + 1) * hd]
        vh = kv_ref[:, d + hh * hd:d + (hh + 1) * hd]
        s = lax.dot_general(qh, kh, NT_DIMS, preferred_element_type=F32)
        p = jnp.exp(s - jnp.max(s, axis=-1, keepdims=True))
        l = jnp.sum(p, axis=-1, keepdims=True)
        outs.append((jnp.dot(p.astype(BF16), vh, preferred_element_type=F32) / l).astype(BF16))
    o = jnp.concatenate(outs, axis=1)
    x_new = x + jnp.dot(o, wo_ref[...], preferred_element_type=F32)
    o_ref[...] = x_new
    h_ffn = _rms(x_new, gf_ref[...])
    hf_ref[...] = _pack_rows(h_ffn)
    lg_ref[...] = jnp.dot(h_ffn.astype(BF16), wr_ref[...], preferred_element_type=F32)


def _cross_attention(x2, g, wq, kv, layer, batch0, wo, g_ffn, w_r, seq, n_mem):
    t, d = x2.shape
    tm = min(ROW_TILE, seq)
    tps = seq // tm
    row = lambda w: pl.BlockSpec((tm, w), lambda i: (i, 0))
    const = lambda r, c: pl.BlockSpec((r, c), lambda i: (0, 0))
    return pl.pallas_call(
        functools.partial(_cross_kernel, heads=MEM_HEADS),
        grid=(t // tm,),
        in_specs=[row(d), const(1, d), const(d, d),
                  pl.BlockSpec((None, n_mem, 2 * d), lambda i: (layer, batch0 + i // tps, 0)),
                  const(d, d), const(1, d), const(d, LANES)],
        out_specs=[row(d), row(d // 2), row(LANES)],
        out_shape=[jax.ShapeDtypeStruct((t, d), F32), jax.ShapeDtypeStruct((t, d // 2), I32),
                   jax.ShapeDtypeStruct((t, LANES), F32)],
        compiler_params=_params(("parallel",)),
        name="cross_attn",
    )(x2, g, wq, kv, wo, g_ffn, w_r)


def _router_kernel(lg_ref, ri_ref, rw_ref, cnt_ref, carry_ref, *, sub):
    @pl.when(pl.program_id(0) == 0)
    def _():
        carry_ref[...] = jnp.zeros_like(carry_ref)

    lg = lg_ref[...]
    tm = lg.shape[0]
    lane = lax.broadcasted_iota(I32, lg.shape, 1).astype(F32)
    neg = -jnp.inf

    def first_argmax(v):
        m = jnp.max(v, axis=-1, keepdims=True)
        return m, jnp.min(jnp.where(v == m, lane, float(LANES)), axis=-1, keepdims=True)

    gmask = lane < N_GROUPS
    gl = jnp.where(gmask, lg, neg)
    gmax, gidx = first_argmax(gl)
    g_w = 1.0 / jnp.sum(jnp.where(gmask, jnp.exp(gl - gmax), 0.0), axis=-1, keepdims=True)

    lo = EXPERT_LANE0 + gidx * EXPERTS_PER_GROUP
    emask = (lane >= lo) & (lane < lo + EXPERTS_PER_GROUP)
    el = jnp.where(emask, lg, neg)
    mx1, i1 = first_argmax(el)
    mx2, i2 = first_argmax(jnp.where(lane == i1, neg, el))
    esum = jnp.sum(jnp.where(emask, jnp.exp(el - mx1), 0.0), axis=-1, keepdims=True)
    p1 = 1.0 / esum
    p2 = jnp.exp(mx2 - mx1) / esum
    den = p1 + p2
    w1 = g_w * (p1 / den)
    w2 = g_w * (p2 / den)

    memb = jnp.where((lane == i1) | (lane == i2), 1.0, 0.0)
    r_i = lax.broadcasted_iota(I32, (sub, sub), 0)
    c_i = lax.broadcasted_iota(I32, (sub, sub), 1)
    ltri = jnp.where(c_i < r_i, 1.0, 0.0).astype(BF16)
    tots = []
    carry = carry_ref[...]
    for r0 in range(0, tm, sub):
        mb = memb[r0:r0 + sub]
        tots.append(jnp.dot(ltri, mb.astype(BF16), preferred_element_type=F32) + carry)
        carry = carry + jnp.sum(mb, axis=0, keepdims=True)
    carry_ref[...] = carry
    tot = jnp.concatenate(tots, axis=0)
    rank1 = jnp.sum(jnp.where(lane == i1, tot, 0.0), axis=-1, keepdims=True).astype(I32)
    rank2 = jnp.sum(jnp.where(lane == i2, tot, 0.0), axis=-1, keepdims=True).astype(I32)

    e1 = (i1 - EXPERT_LANE0).astype(I32)
    e2 = (i2 - EXPERT_LANE0).astype(I32)
    ri = jnp.where(lane == 0, e1, jnp.where(lane == 1, e2, jnp.where(lane == 2, rank1, jnp.where(lane == 3, rank2, 0))))
    ri_ref[...] = ri.T[:F32_SUBLANES]
    rw_ref[...] = jnp.where(lane == 0, w1, jnp.where(lane == 1, w2, 0.0))
    cnt_ref[...] = jnp.broadcast_to(carry, cnt_ref.shape)


def _router(lg, seq):
    t = lg.shape[0]
    tm = min(ROUTER_TILE, seq)
    row = pl.BlockSpec((tm, LANES), lambda i: (i, 0))
    return pl.pallas_call(
        functools.partial(_router_kernel, sub=min(ROW_TILE, tm)),
        grid=(t // tm,),
        in_specs=[row],
        out_specs=[pl.BlockSpec((F32_SUBLANES, tm), lambda i: (0, i)), row,
                   pl.BlockSpec((F32_SUBLANES, LANES), lambda i: (0, 0))],
        out_shape=[jax.ShapeDtypeStruct((F32_SUBLANES, t), I32),
                   jax.ShapeDtypeStruct((t, LANES), F32),
                   jax.ShapeDtypeStruct((F32_SUBLANES, LANES), F32)],
        scratch_shapes=[pltpu.VMEM((1, LANES), F32)],
        compiler_params=_params(("arbitrary",)),
        name="moe_router",
    )(lg)


def _dispatch_kernel(dest_ref, x_ref, g_ref, xb_in_ref, xb_ref, h_ref, sem):
    del xb_in_ref
    i = pl.program_id(0)
    tm = x_ref.shape[0]
    h_ref[...] = _rms(x_ref[...], g_ref[...])
    base = i * (TOP_K * tm)

    for t in range(tm):
        for k in range(TOP_K):
            dst = dest_ref[base + (TOP_K * t + k)]
            pltpu.make_async_copy(h_ref.at[pl.ds(t, 1), :], xb_ref.at[pl.ds(dst, 1), :], sem).start()
    for _ in range(TOP_K):
        pltpu.make_async_copy(h_ref, xb_ref.at[pl.ds(0, tm), :], sem).wait()


def _dispatch(dest, x2, g, xb, seq):
    t, d = x2.shape
    tm = min(ROW_TILE, seq)
    return pl.pallas_call(
        _dispatch_kernel,
        grid_spec=pltpu.PrefetchScalarGridSpec(
            num_scalar_prefetch=1,
            grid=(t // tm,),
            in_specs=[pl.BlockSpec((tm, d), lambda i, dest: (i, 0)),
                      pl.BlockSpec((1, d), lambda i, dest: (0, 0)),
                      pl.BlockSpec(memory_space=pl.ANY)],
            out_specs=pl.BlockSpec(memory_space=pl.ANY),
            scratch_shapes=[pltpu.VMEM((tm, d), F32), pltpu.SemaphoreType.DMA]),
        out_shape=jax.ShapeDtypeStruct(xb.shape, xb.dtype),
        input_output_aliases={3: 0},
        compiler_params=_params(("arbitrary",)),
        name="moe_dispatch",
    )(dest, x2, g, xb)


def _expert_kernel(be_ref, first_ref, slot_ref, nxt_ref, na_ref, xb_ref, wg_hbm, wu_hbm, wd_hbm, yb_ref,
                   wg_buf, wu_buf, wd_buf, wgb_ref, wub_ref, wdb_ref, sem, *, layer):
    i = pl.program_id(0)

    def fetch(e, slot):
        return (pltpu.make_async_copy(wg_hbm.at[layer, e], wg_buf.at[slot], sem.at[slot, 0]),
                pltpu.make_async_copy(wu_hbm.at[layer, e], wu_buf.at[slot], sem.at[slot, 1]),
                pltpu.make_async_copy(wd_hbm.at[layer, e], wd_buf.at[slot], sem.at[slot, 2]))

    @pl.when(i < na_ref[0])
    def _():
        @pl.when(first_ref[i] == 1)
        def _():
            e, slot = be_ref[i], slot_ref[i]

            @pl.when(i == 0)
            def _():
                for cp in fetch(e, slot):
                    cp.start()

            for cp in fetch(e, slot):
                cp.wait()
            wgb_ref[...] = wg_buf[slot].astype(BF16)
            wub_ref[...] = wu_buf[slot].astype(BF16)
            wdb_ref[...] = wd_buf[slot].astype(BF16)

            @pl.when(nxt_ref[i] >= 0)
            def _():
                for cp in fetch(nxt_ref[i], 1 - slot):
                    cp.start()

        xb = _unpack_rows(xb_ref[...]).astype(BF16)
        gate = jnp.dot(xb, wgb_ref[...], preferred_element_type=F32)
        up = jnp.dot(xb, wub_ref[...], preferred_element_type=F32)
        act = (jax.nn.silu(gate) * up).astype(BF16)
        yb_ref[...] = _pack_rows(jnp.dot(act, wdb_ref[...], preferred_element_type=F32))

    @pl.when(i >= na_ref[0])
    def _():
        yb_ref[...] = jnp.zeros_like(yb_ref)


def _experts(plan, xb, w_gate, w_up, w_down, layer, blk):
    rows = xb.shape[0]
    d, de = w_gate.shape[-2:]
    nb = rows // blk

    def row_map(i, be, first, slot, nxt, na):
        return (jnp.minimum(i, jnp.maximum(na[0] - 1, 0)), 0)

    hbm = pl.BlockSpec(memory_space=pl.ANY)
    return pl.pallas_call(
        functools.partial(_expert_kernel, layer=layer),
        grid_spec=pltpu.PrefetchScalarGridSpec(
            num_scalar_prefetch=5,
            grid=(nb,),
            in_specs=[pl.BlockSpec((blk, d // 2), row_map), hbm, hbm, hbm],
            out_specs=pl.BlockSpec((blk, d // 2), lambda i, *_: (i, 0)),
            scratch_shapes=[pltpu.VMEM((2, d, de), F32), pltpu.VMEM((2, d, de), F32), pltpu.VMEM((2, de, d), F32),
                            pltpu.VMEM((d, de), BF16), pltpu.VMEM((d, de), BF16), pltpu.VMEM((de, d), BF16),
                            pltpu.SemaphoreType.DMA((2, 3))]),
        out_shape=jax.ShapeDtypeStruct((rows, d // 2), I32),
        compiler_params=_params(("arbitrary",)),
        name="moe_experts",
    )(*plan, xb, w_gate, w_up, w_down)


def _combine_kernel(dest_ref, x_ref, rw_ref, yb_ref, gf_ref, o_ref, y_ref, sem, *, final_norm):
    i = pl.program_id(0)
    tm = x_ref.shape[0]
    base = i * (TOP_K * tm)

    for t in range(tm):
        for k in range(TOP_K):
            src = dest_ref[base + (TOP_K * t + k)]
            pltpu.make_async_copy(yb_ref.at[pl.ds(src, 1), :], y_ref.at[k, pl.ds(t, 1), :], sem).start()
    for k in range(TOP_K):
        pltpu.make_async_copy(yb_ref.at[pl.ds(0, tm), :], y_ref.at[k], sem).wait()
    w = rw_ref[...]
    out = x_ref[...]
    for k in range(TOP_K):
        out = out + w[:, k:k + 1] * y_ref[k]
    if final_norm:
        out = _rms(out, gf_ref[...])
    o_ref[...] = out


def _combine(dest, x2, rw, yb, g_final, seq, final_norm):
    t, d = x2.shape
    tm = min(ROW_TILE, seq)
    return pl.pallas_call(
        functools.partial(_combine_kernel, final_norm=final_norm),
        grid_spec=pltpu.PrefetchScalarGridSpec(
            num_scalar_prefetch=1,
            grid=(t // tm,),
            in_specs=[pl.BlockSpec((tm, d), lambda i, dest: (i, 0)),
                      pl.BlockSpec((tm, LANES), lambda i, dest: (i, 0)),
                      pl.BlockSpec(memory_space=pl.ANY),
                      pl.BlockSpec((1, d), lambda i, dest: (0, 0))],
            out_specs=pl.BlockSpec((tm, d), lambda i, dest: (i, 0)),
            scratch_shapes=[pltpu.VMEM((TOP_K, tm, d), F32), pltpu.SemaphoreType.DMA]),
        out_shape=jax.ShapeDtypeStruct((t, d), F32),
        compiler_params=_params(("arbitrary",)),
        name="moe_combine",
    )(dest, x2, rw, yb, g_final)


def _sc_gather_rows(table, idx):
    b = idx.shape[0]
    d = table.shape[1]
    per_w = b // SC_WORKERS
    ch = SC_CHUNK_ROWS
    mesh = plsc.VectorSubcoreMesh(core_axis_name="c", subcore_axis_name="s")

    n_ch = per_w // ch
    assert per_w * SC_WORKERS == b and n_ch * ch == per_w and n_ch % 2 == 0

    def body(table_hbm, idx_hbm, out_hbm, idx_v, rows_v, gsem, wsem):
        wid = lax.axis_index("s") * SC_CORES + lax.axis_index("c")
        base = wid * per_w
        pltpu.sync_copy(idx_hbm.at[pl.ds(base, per_w)], idx_v)

        def write_back(c, buf):
            off = pl.multiple_of(c * ch, ch)
            return pltpu.make_async_copy(rows_v.at[buf], out_hbm.at[pl.ds(base + off, ch)], wsem.at[buf])

        @pl.loop(0, n_ch, step=2)
        def _(c0):
            for buf in range(2):
                c = c0 + buf

                @pl.when(c0 > 0)
                def _():
                    write_back(c - 2, buf).wait()

                off = pl.multiple_of(c * ch, ch)
                pltpu.async_copy(table_hbm.at[idx_v.at[pl.ds(off, ch)]], rows_v.at[buf], gsem).wait()
                write_back(c, buf).start()

        for buf in range(2):
            write_back(n_ch - 2 + buf, buf).wait()

    return pl.kernel(body, out_type=jax.ShapeDtypeStruct((b, d), table.dtype), mesh=mesh,
                     scratch_types=[pltpu.VMEM((per_w,), I32), pltpu.VMEM((2, ch, d), table.dtype),
                                    pltpu.SemaphoreType.DMA, pltpu.SemaphoreType.DMA((2,))],
                     name="sc_gather")(table, idx)


def _sc_scatter_rows(rows, idx, n_out):
    t, d = rows.shape
    top_k, _, n_ch, ch = idx.shape
    per_w = n_ch * ch
    assert per_w * SC_WORKERS == t and idx.shape[1] == SC_WORKERS
    mesh = plsc.VectorSubcoreMesh(core_axis_name="c", subcore_axis_name="s")

    def body(rows_hbm, idx_hbm, out_hbm, idx_v, rows_v, sem):
        wid = lax.axis_index("s") * SC_CORES + lax.axis_index("c")
        base = wid * per_w
        for k in range(top_k):
            pltpu.sync_copy(idx_hbm.at[k, wid], idx_v.at[k])

        @pl.loop(0, n_ch)
        def _(c):
            off = pl.multiple_of(c * ch, ch)
            pltpu.sync_copy(rows_hbm.at[pl.ds(base + off, ch)], rows_v)
            for k in range(top_k):
                pltpu.async_copy(rows_v, out_hbm.at[idx_v.at[k, c]], sem).wait()

    return pl.kernel(body, out_type=jax.ShapeDtypeStruct((n_out, d), rows.dtype), mesh=mesh,
                     scratch_types=[pltpu.VMEM((top_k, n_ch, ch), I32), pltpu.VMEM((ch, d), rows.dtype),
                                    pltpu.SemaphoreType.DMA], name="sc_scatter")(rows, idx)


def _combine_final_kernel(x_ref, rw_ref, y0_ref, y1_ref, gf_ref, o_ref):
    w = rw_ref[...]
    out = x_ref[...] + w[:, 0:1] * _unpack_rows(y0_ref[...]) + w[:, 1:2] * _unpack_rows(y1_ref[...])
    o_ref[...] = _rms(out, gf_ref[...])


def _combine_final(x2, rw, y2, g_final, seq, part, n_parts):
    t, d = x2.shape
    tm = min(ROW_TILE, seq)
    nt = t // tm // n_parts
    row = lambda w: pl.BlockSpec((tm, w), lambda i: (part * nt + i, 0))
    return pl.pallas_call(
        _combine_final_kernel,
        grid=(nt,),
        in_specs=[row(d), row(LANES), pl.BlockSpec((tm, d // 2), lambda i: (i, 0)),
                  pl.BlockSpec((tm, d // 2), lambda i: (nt + i, 0)), pl.BlockSpec((1, d), lambda i: (0, 0))],
        out_specs=row(d),
        out_shape=jax.ShapeDtypeStruct((t, d), F32),
        input_output_aliases={0: 0},
        compiler_params=_params(("parallel",)),
        name="moe_combine_final",
    )(x2, rw, y2, y2, g_final)


def _route_plan(ri, counts_f, blk, n_blocks):
    cnt = counts_f[0, EXPERT_LANE0:EXPERT_LANE0 + N_EXPERTS].astype(I32)
    nblk = (cnt + blk - 1) // blk
    cum = jnp.cumsum(nblk)
    pad_start = (cum - nblk) * blk
    e = ri[0:TOP_K]
    rank = ri[TOP_K:2 * TOP_K]
    onehot = e[:, :, None] == jnp.arange(N_EXPERTS, dtype=I32)[None, None, :]
    dest = rank + jnp.sum(jnp.where(onehot, pad_start[None, None, :], 0), axis=-1)
    blocks = jnp.arange(n_blocks, dtype=I32)
    block_expert = jnp.minimum(jnp.sum((cum[None, :] <= blocks[:, None]).astype(I32), axis=-1), N_EXPERTS - 1)
    n_active = cum[-1:].astype(I32)
    experts = jnp.arange(N_EXPERTS, dtype=I32)
    is_block_expert = block_expert[:, None] == experts[None, :]
    first = jnp.sum(jnp.where(is_block_expert, ((cum - nblk)[None, :] == blocks[:, None]).astype(I32), 0), axis=-1)
    used = (nblk > 0).astype(I32)
    slot_e = (jnp.cumsum(used) - used) % 2
    later = (experts[None, :] > experts[:, None]) & (nblk[None, :] > 0)
    next_e = jnp.min(jnp.where(later, experts[None, :], N_EXPERTS), axis=-1)
    next_e = jnp.where(next_e == N_EXPERTS, -1, next_e)
    slot = jnp.sum(jnp.where(is_block_expert, slot_e[None, :], 0), axis=-1)
    nxt = jnp.sum(jnp.where(is_block_expert, next_e[None, :], 0), axis=-1)
    plan = (block_expert, first.astype(I32), slot.astype(I32), nxt.astype(I32), n_active)
    return dest.astype(I32), plan


def kernel(x, mem, norm_mix, w_in, lambda_q1, lambda_k1, lambda_q2, lambda_k2, subln, conv_w, w_branch, w_o,
           norm_cross, norm_mem, w_cq, w_ckv, w_co, norm_ffn, w_router_group, w_router_expert, w_exp_gate,
           w_exp_up, w_exp_down, norm_final):
    batch, seq, d = x.shape
    depth = w_in.shape[0]
    n_mem = mem.shape[1]
    t = batch * seq
    dh = d // (2 * DA_HEADS)
    assert 2 * dh == LANES and d % LANES == 0 and w_in.shape[2] == 8 * d
    assert w_router_group.shape[2] == N_GROUPS and w_router_expert.shape[2] == N_EXPERTS
    blk = min(EXPERT_BLOCK, seq)
    n_blocks = (t * TOP_K) // blk + N_EXPERTS
    parts = COMBINE_PARTS if t % (COMBINE_PARTS * SC_WORKERS * SC_CHUNK_ROWS) == 0 else 1
    tokens_per_worker = t // SC_WORKERS

    tabs = _rotary_tables(seq, dh)
    x2 = x.reshape(t, d)
    kv = _mem_kv(mem.reshape(batch * n_mem, d), norm_mem.reshape(1, d), w_ckv.astype(BF16))
    w_r = jnp.concatenate([w_router_group, w_router_expert,
                           jnp.zeros((depth, d, LANES - N_GROUPS - N_EXPERTS), F32)], axis=-1).astype(BF16)

    moe_out = None
    for l in range(depth):
        lambda_init = 0.8 - 0.6 * math.exp(-0.3 * l)
        g_mix, w_in_l = norm_mix[l].reshape(1, d), w_in[l].astype(BF16)
        if moe_out is None:
            proj = _inproj(x2, g_mix, w_in_l, tabs, seq, dh)
        else:
            proj = None
            for p in range(parts):
                x2, proj = _inproj_combine(x2, moe_out[0], moe_out[1][p], g_mix, w_in_l, tabs, proj, seq, dh,
                                           p, parts)
        y_attn = _attention(proj, lambda_q1[l].reshape(1, dh), lambda_k1[l].reshape(1, dh),
                            lambda_q2[l].reshape(1, dh), lambda_k2[l].reshape(1, dh),
                            subln[l].reshape(1, LANES), batch, seq, d, lambda_init)
        x2 = _mixer_out(x2, y_attn, proj, conv_w[l], w_branch[l, 0].astype(BF16), w_branch[l, 1].astype(BF16),
                        w_o[l].astype(BF16), seq)
        x2, h_ffn, logits = _cross_attention(x2, norm_cross[l].reshape(1, d), w_cq[l].astype(BF16), kv, l, 0,
                                             w_co[l].astype(BF16), norm_ffn[l].reshape(1, d), w_r[l], seq, n_mem)
        ri, rw, counts = _router(logits, seq)
        dest, plan = _route_plan(ri, counts, blk, n_blocks)
        scatter_idx = dest.reshape(TOP_K, SC_WORKERS, tokens_per_worker // SC_CHUNK_ROWS, SC_CHUNK_ROWS)
        xb = _sc_scatter_rows(h_ffn, scatter_idx, n_blocks * blk)
        yb = _experts(plan, xb, w_exp_gate, w_exp_up, w_exp_down, l, blk)
        tp = t // parts
        moe_out = (rw, [_sc_gather_rows(yb, dest[:, p * tp:(p + 1) * tp].reshape(-1)) for p in range(parts)])
    for p in range(parts):
        x2 = _combine_final(x2, moe_out[0], moe_out[1][p], norm_final.reshape(1, d), seq, p, parts)
    return x2.reshape(batch, seq, d)
```

```python
import functools
import math

import jax
import jax.numpy as jnp
from jax import lax
from jax.experimental import pallas as pl
from jax.experimental.pallas import tpu as pltpu
from jax.experimental.pallas import tpu_sc as plsc

EPS = 1e-6
DA_HEADS = 8
MEM_HEADS = 4
N_GROUPS = 4
EXPERTS_PER_GROUP = 8
N_EXPERTS = N_GROUPS * EXPERTS_PER_GROUP
TOP_K = 2
ROPE_THETA = 500000.0
LANES = 128
F32_SUBLANES = 8
BF16_SUBLANES = 16
EXPERT_LANE0 = N_GROUPS
ROW_TILE = 512
ROUTER_TILE = 2048
Q_TILE = 2048
EXPERT_BLOCK = 256
PROJ_ROW_CHUNK = 1024
PROJ_COMBINE_TILE = 1024
ATTN_SCORE_CHUNK = 128
ATTN_SOFTMAX_CHUNK = 128
COMBINE_PARTS = 4
SC_CORES = 2
SC_WORKERS = 32
SC_CHUNK_ROWS = 64
VMEM_LIMIT = 56 * 1024 * 1024

F32 = jnp.float32
BF16 = jnp.bfloat16
I32 = jnp.int32
NT_DIMS = (((1,), (1,)), ((), ()))


def _rms(x, g):
    return x * lax.rsqrt(jnp.mean(x * x, axis=-1, keepdims=True) + EPS) * g


def _pack_rows(x):
    half = x.shape[1] // 2
    lo = lax.bitcast_convert_type(x[:, :half].astype(jnp.bfloat16).astype(F32), jnp.uint32)
    hi = lax.bitcast_convert_type(x[:, half:].astype(jnp.bfloat16).astype(F32), jnp.uint32)
    return lax.bitcast_convert_type((hi & jnp.uint32(0xFFFF0000)) | (lo >> 16), I32)


def _unpack_rows(w):
    u = lax.bitcast_convert_type(w, jnp.uint32)
    lo = lax.bitcast_convert_type(u << 16, F32)
    hi = lax.bitcast_convert_type(u & jnp.uint32(0xFFFF0000), F32)
    return jnp.concatenate([lo, hi], axis=1)


def _params(sem, vmem=VMEM_LIMIT):
    return pltpu.CompilerParams(dimension_semantics=sem, vmem_limit_bytes=vmem)


def _inproj_kernel(x_ref, g_ref, w_ref, cos_ref, s1_ref, s2_ref, o_ref, h_ref, *, rc, half, q_scale):
    @pl.when(pl.program_id(1) == 0)
    def _():
        h_ref[...] = _rms(x_ref[...], g_ref[...]).astype(BF16)

    _proj_column_block(h_ref, w_ref, cos_ref, s1_ref, s2_ref, o_ref, rc, half, q_scale)


def _inproj_combine_kernel(x_ref, rw_ref, y0_ref, y1_ref, g_ref, w_ref, cos_ref, s1_ref, s2_ref, *rest,
                           rc, half, q_scale):
    xnew_ref, o_ref, h_ref = rest[-3:]

    @pl.when(pl.program_id(1) == 0)
    def _():
        w = rw_ref[...]
        x_new = x_ref[...] + w[:, 0:1] * _unpack_rows(y0_ref[...]) + w[:, 1:2] * _unpack_rows(y1_ref[...])
        xnew_ref[...] = x_new
        h_ref[...] = _rms(x_new, g_ref[...]).astype(BF16)

    _proj_column_block(h_ref, w_ref, cos_ref, s1_ref, s2_ref, o_ref, rc, half, q_scale)


def _proj_column_block(h_ref, w_ref, cos_ref, s1_ref, s2_ref, o_ref, rc, half, q_scale):
    j = pl.program_id(1)
    tm, tn = o_ref.shape

    def run(rot, scale):
        def body(r, carry):
            r0 = pl.multiple_of(r * rc, rc)
            acc = jnp.dot(h_ref[pl.ds(r0, rc), :], w_ref[...], preferred_element_type=F32)
            if not rot:
                o_ref[pl.ds(r0, rc), :] = acc.astype(o_ref.dtype)
                return carry
            c = cos_ref[pl.ds(r0, rc), :]
            s1 = s1_ref[pl.ds(r0, rc), :]
            s2 = s2_ref[pl.ds(r0, rc), :]
            for cc in range(tn // LANES):
                a = acc[:, cc * LANES:(cc + 1) * LANES]
                a = a * c + pltpu.roll(a, half, 1) * s1 + pltpu.roll(a, LANES - half, 1) * s2
                if scale != 1.0:
                    a = a * scale
                o_ref[pl.ds(r0, rc), cc * LANES:(cc + 1) * LANES] = a.astype(o_ref.dtype)
            return carry
        lax.fori_loop(0, tm // rc, body, 0)

    @pl.when(j == 0)
    def _():
        run(True, q_scale)

    @pl.when(j == 1)
    def _():
        run(True, 1.0)

    @pl.when(j >= 2)
    def _():
        run(False, 1.0)


def _inproj(x2, g, w_bf, tabs, seq, dh):
    t, d = x2.shape
    n = w_bf.shape[1]
    tm, tn = seq, d
    rc = min(PROJ_ROW_CHUNK, tm)
    cos_t, s1_t, s2_t = tabs
    tab_spec = pl.BlockSpec((seq, LANES), lambda i, j: (0, 0))
    return pl.pallas_call(
        functools.partial(_inproj_kernel, rc=rc, half=dh // 8, q_scale=dh ** -0.5 * math.log2(math.e)),
        grid=(t // tm, n // tn),
        in_specs=[pl.BlockSpec((tm, d), lambda i, j: (i, 0)),
                  pl.BlockSpec((1, d), lambda i, j: (0, 0)),
                  pl.BlockSpec((d, tn), lambda i, j: (0, j)),
                  tab_spec, tab_spec, tab_spec],
        out_specs=pl.BlockSpec((tm, tn), lambda i, j: (i, j)),
        out_shape=jax.ShapeDtypeStruct((t, n), BF16),
        scratch_shapes=[pltpu.VMEM((tm, d), BF16)],
        compiler_params=_params(("parallel", "arbitrary")),
        name="inproj",
    )(x2, g, w_bf, cos_t, s1_t, s2_t)


def _inproj_combine(x2, rw, y2, g, w_bf, tabs, proj_prev, seq, dh, part, n_parts):
    t, d = x2.shape
    n = w_bf.shape[1]
    tm, tn = min(PROJ_COMBINE_TILE, seq), d
    ntp = t // n_parts // tm
    row0 = part * ntp
    tiles_per_seq = seq // tm
    cos_t, s1_t, s2_t = tabs
    rows = lambda w: pl.BlockSpec((tm, w), lambda i, j: (row0 + i, 0))
    tab_spec = pl.BlockSpec((tm, LANES), lambda i, j: ((row0 + i) % tiles_per_seq, 0))
    in_specs = [rows(d), rows(LANES),
                pl.BlockSpec((tm, d // 2), lambda i, j: (i, 0)),
                pl.BlockSpec((tm, d // 2), lambda i, j: (ntp + i, 0)),
                pl.BlockSpec((1, d), lambda i, j: (0, 0)),
                pl.BlockSpec((d, tn), lambda i, j: (0, j)),
                tab_spec, tab_spec, tab_spec]
    args = [x2, rw, y2, y2, g, w_bf, cos_t, s1_t, s2_t]
    aliases = {0: 0}
    if proj_prev is not None:
        in_specs.append(pl.BlockSpec(memory_space=pl.ANY))
        args.append(proj_prev)
        aliases[len(args) - 1] = 1
    return pl.pallas_call(
        functools.partial(_inproj_combine_kernel, rc=min(PROJ_ROW_CHUNK, tm), half=dh // 8,
                          q_scale=dh ** -0.5 * math.log2(math.e)),
        grid=(ntp, n // tn),
        in_specs=in_specs,
        out_specs=[rows(d), pl.BlockSpec((tm, tn), lambda i, j: (row0 + i, j))],
        out_shape=[jax.ShapeDtypeStruct((t, d), F32), jax.ShapeDtypeStruct((t, n), BF16)],
        scratch_shapes=[pltpu.VMEM((tm, d), BF16)],
        input_output_aliases=aliases,
        compiler_params=_params(("parallel", "arbitrary")),
        name="inproj_combine",
    )(*args)


def _rotary_tables(seq, dh):
    rot = dh // 4
    half = rot // 2
    inv = jnp.float32(ROPE_THETA) ** (-jnp.arange(0, rot, 2, dtype=F32) / rot)
    ang = jnp.arange(seq, dtype=F32)[:, None] * inv[None, :]
    cos, sin = jnp.cos(ang), jnp.sin(ang)
    lane = jnp.arange(LANES) % dh
    idx = lane % half
    c_t = jnp.where(lane < rot, cos[:, idx], 1.0)
    s1_t = jnp.where((lane >= half) & (lane < rot), sin[:, idx], 0.0)
    s2_t = jnp.where(lane < half, -sin[:, idx], 0.0)
    return c_t.astype(F32), s1_t.astype(F32), s2_t.astype(F32)


def _attn_kernel(q_ref, k_ref, v_ref, lq1_ref, lk1_ref, lq2_ref, lk2_ref, sub_ref, o_ref,
                 s_ref, p_ref, *, dh, lambda_init, rc, rp):
    tq = q_ref.shape[0]
    n_chunks = tq // rc
    lane = lax.broadcasted_iota(I32, (1, LANES), 1)
    m1 = jnp.where(lane < dh, 1.0, 0.0).astype(BF16)
    m2 = jnp.where(lane >= dh, 1.0, 0.0).astype(BF16)
    lam = (jnp.exp(jnp.sum(lq1_ref[...] * lk1_ref[...], axis=-1, keepdims=True))
           - jnp.exp(jnp.sum(lq2_ref[...] * lk2_ref[...], axis=-1, keepdims=True)) + lambda_init)

    def scores(c):
        qc = q_ref[c * rc:(c + 1) * rc, :]
        qq = jnp.concatenate([qc * m1, qc * m2], axis=0)
        s_ref[c % 2] = lax.dot_general(qq, k_ref[...], NT_DIMS, preferred_element_type=F32)

    def softmax_pv(c, j):
        r0 = j * rp
        s1 = s_ref[c % 2, r0:r0 + rp, :]
        s2 = s_ref[c % 2, rc + r0:rc + r0 + rp, :]
        e1 = jnp.exp2(s1 - jnp.max(s1, axis=-1, keepdims=True))
        e2 = jnp.exp2(s2 - jnp.max(s2, axis=-1, keepdims=True))
        l1 = jnp.sum(e1, axis=-1, keepdims=True)
        l2 = jnp.sum(e2, axis=-1, keepdims=True)
        slot = (c * (rc // rp) + j) % 2
        p_ref[slot] = (e1 - e2 * (lam * l1 / l2)).astype(BF16)
        od = jnp.dot(p_ref[slot], v_ref[...], preferred_element_type=F32) / l1
        od = _rms(od, sub_ref[...]) * (1.0 - lambda_init)
        o_ref[c * rc + r0:c * rc + r0 + rp, :] = od.astype(o_ref.dtype)

    scores(0)
    for c in range(n_chunks):
        if c + 1 < n_chunks:
            scores(c + 1)
        for j in range(rc // rp):
            softmax_pv(c, j)


def _attention(proj, lq1, lk1, lq2, lk2, sub, batch, seq, d, lambda_init):
    t = proj.shape[0]
    nh = d // LANES
    dh = LANES // 2
    tq = min(Q_TILE, seq)
    nq = seq // tq
    rc = min(ATTN_SCORE_CHUNK, tq)
    rp = min(ATTN_SOFTMAX_CHUNK, rc)
    vec = pl.BlockSpec((1, dh), lambda b, h, qi: (0, 0))
    return pl.pallas_call(
        functools.partial(_attn_kernel, dh=dh, lambda_init=lambda_init, rc=rc, rp=rp),
        grid=(batch, nh, nq),
        scratch_shapes=[pltpu.VMEM((2, 2 * rc, seq), F32), pltpu.VMEM((2, rp, seq), BF16)],
        in_specs=[pl.BlockSpec((tq, LANES), lambda b, h, qi: (b * nq + qi, h)),
                  pl.BlockSpec((seq, LANES), lambda b, h, qi: (b, nh + h)),
                  pl.BlockSpec((seq, LANES), lambda b, h, qi: (b, 2 * nh + h)),
                  vec, vec, vec, vec,
                  pl.BlockSpec((1, LANES), lambda b, h, qi: (0, 0))],
        out_specs=pl.BlockSpec((tq, LANES), lambda b, h, qi: (b * nq + qi, h)),
        out_shape=jax.ShapeDtypeStruct((t, d), BF16),
        compiler_params=_params(("parallel", "parallel", "arbitrary")),
        name="diff_attn",
    )(proj, proj, proj, lq1, lk1, lq2, lk2, sub)


def _mixout_kernel(x_ref, ya_ref, cb_ref, cc_ref, cx_ref, g0_ref, g1_ref, pc_ref, px_ref, nc_ref, nx_ref,
                   cw_ref, wb0_ref, wb1_ref, wo_ref, o_ref, *, tiles_per_seq):
    i = pl.program_id(0)
    tm = x_ref.shape[0]
    u = cc_ref[...].astype(F32) * cx_ref[...].astype(F32)
    hr = pc_ref.shape[0]
    u_prev = pc_ref[hr - 1:hr, :].astype(F32) * px_ref[hr - 1:hr, :].astype(F32)
    u_next = nc_ref[0:1, :].astype(F32) * nx_ref[0:1, :].astype(F32)
    pos = i % tiles_per_seq
    u_prev = jnp.where(pos == 0, 0.0, u_prev)
    u_next = jnp.where(pos == tiles_per_seq - 1, 0.0, u_next)
    row = lax.broadcasted_iota(I32, u.shape, 0)
    u_up = jnp.where(row == 0, u_prev, pltpu.roll(u, 1, 0))
    u_dn = jnp.where(row == tm - 1, u_next, pltpu.roll(u, tm - 1, 0))
    conv = cw_ref[0:1, :] * u_up + cw_ref[1:2, :] * u + cw_ref[2:3, :] * u_dn
    y_conv = (cb_ref[...].astype(F32) * conv).astype(BF16)
    br0 = jnp.dot(ya_ref[...], wb0_ref[...], preferred_element_type=F32)
    br1 = jnp.dot(y_conv, wb1_ref[...], preferred_element_type=F32)
    z = jax.nn.sigmoid(g0_ref[...].astype(F32)) * br0 + jax.nn.sigmoid(g1_ref[...].astype(F32)) * br1
    o_ref[...] = x_ref[...] + jnp.dot(z.astype(BF16), wo_ref[...], preferred_element_type=F32)


def _mixer_out(x2, y_attn, proj, conv_w, wb0, wb1, wo, seq):
    t, d = x2.shape
    tm = min(ROW_TILE, seq)
    tps = seq // tm
    hr = BF16_SUBLANES
    nhb = t // hr
    row = lambda c: pl.BlockSpec((tm, d), lambda i: (i, c))
    prev = lambda c: pl.BlockSpec((hr, d), lambda i: (jnp.maximum(i * (tm // hr) - 1, 0), c))
    nxt = lambda c: pl.BlockSpec((hr, d), lambda i: (jnp.minimum((i + 1) * (tm // hr), nhb - 1), c))
    full = lambda r: pl.BlockSpec((r, d), lambda i: (0, 0))
    return pl.pallas_call(
        functools.partial(_mixout_kernel, tiles_per_seq=tps),
        grid=(t // tm,),
        in_specs=[row(0), row(0), row(3), row(4), row(5), row(6), row(7),
                  prev(4), prev(5), nxt(4), nxt(5),
                  full(conv_w.shape[0]), full(d), full(d), full(d)],
        out_specs=row(0),
        out_shape=jax.ShapeDtypeStruct((t, d), F32),
        compiler_params=_params(("parallel",)),
        name="mixer_out",
    )(x2, y_attn, proj, proj, proj, proj, proj, proj, proj, proj, proj, conv_w, wb0, wb1, wo)


def _memkv_kernel(m_ref, g_ref, w_ref, o_ref):
    h = _rms(m_ref[...], g_ref[...]).astype(BF16)
    o_ref[...] = jnp.dot(h, w_ref[...], preferred_element_type=F32).astype(o_ref.dtype)


def _mem_kv(mem2, g, w_bf):
    depth, d, n = w_bf.shape
    rows = mem2.shape[0]
    tn = min(n, 1024)
    return pl.pallas_call(
        _memkv_kernel,
        grid=(depth, n // tn),
        in_specs=[pl.BlockSpec((rows, d), lambda l, j: (0, 0)),
                  pl.BlockSpec((1, d), lambda l, j: (0, 0)),
                  pl.BlockSpec((None, d, tn), lambda l, j: (l, 0, j))],
        out_specs=pl.BlockSpec((None, rows, tn), lambda l, j: (l, 0, j)),
        out_shape=jax.ShapeDtypeStruct((depth, rows, n), BF16),
        compiler_params=_params(("parallel", "parallel")),
        name="mem_kv",
    )(mem2, g, w_bf)


def _cross_kernel(x_ref, g_ref, wq_ref, kv_ref, wo_ref, gf_ref, wr_ref, o_ref, hf_ref, lg_ref, *, heads):
    x = x_ref[...]
    d = x.shape[1]
    hd = d // heads
    h = _rms(x, g_ref[...]).astype(BF16)
    q = (jnp.dot(h, wq_ref[...], preferred_element_type=F32) * (hd ** -0.5)).astype(BF16)
    outs = []
    for hh in range(heads):
        qh = q[:, hh * hd:(hh + 1) * hd]
        kh = kv_ref[:, hh * hd:(hh + 1) * hd]
        vh = kv_ref[:, d + hh * hd:d + (hh + 1) * hd]
        s = lax.dot_general(qh, kh, NT_DIMS, preferred_element_type=F32)
        p = jnp.exp(s - jnp.max(s, axis=-1, keepdims=True))
        l = jnp.sum(p, axis=-1, keepdims=True)
        outs.append((jnp.dot(p.astype(BF16), vh, preferred_element_type=F32) / l).astype(BF16))
    o = jnp.concatenate(outs, axis=1)
    x_new = x + jnp.dot(o, wo_ref[...], preferred_element_type=F32)
    o_ref[...] = x_new
    h_ffn = _rms(x_new, gf_ref[...])
    hf_ref[...] = _pack_rows(h_ffn)
    lg_ref[...] = jnp.dot(h_ffn.astype(BF16), wr_ref[...], preferred_element_type=F32)


def _cross_attention(x2, g, wq, kv, layer, batch0, wo, g_ffn, w_r, seq, n_mem):
    t, d = x2.shape
    tm = min(ROW_TILE, seq)
    tps = seq // tm
    row = lambda w: pl.BlockSpec((tm, w), lambda i: (i, 0))
    const = lambda r, c: pl.BlockSpec((r, c), lambda i: (0, 0))
    return pl.pallas_call(
        functools.partial(_cross_kernel, heads=MEM_HEADS),
        grid=(t // tm,),
        in_specs=[row(d), const(1, d), const(d, d),
                  pl.BlockSpec((None, n_mem, 2 * d), lambda i: (layer, batch0 + i // tps, 0)),
                  const(d, d), const(1, d), const(d, LANES)],
        out_specs=[row(d), row(d // 2), row(LANES)],
        out_shape=[jax.ShapeDtypeStruct((t, d), F32), jax.ShapeDtypeStruct((t, d // 2), I32),
                   jax.ShapeDtypeStruct((t, LANES), F32)],
        compiler_params=_params(("parallel",)),
        name="cross_attn",
    )(x2, g, wq, kv, wo, g_ffn, w_r)


def _router_kernel(lg_ref, ri_ref, rw_ref, cnt_ref, carry_ref, *, sub):
    @pl.when(pl.program_id(0) == 0)
    def _():
        carry_ref[...] = jnp.zeros_like(carry_ref)

    lg = lg_ref[...]
    tm = lg.shape[0]
    lane = lax.broadcasted_iota(I32, lg.shape, 1).astype(F32)
    neg = -jnp.inf

    def first_argmax(v):
        m = jnp.max(v, axis=-1, keepdims=True)
        return m, jnp.min(jnp.where(v == m, lane, float(LANES)), axis=-1, keepdims=True)

    gmask = lane < N_GROUPS
    gl = jnp.where(gmask, lg, neg)
    gmax, gidx = first_argmax(gl)
    g_w = 1.0 / jnp.sum(jnp.where(gmask, jnp.exp(gl - gmax), 0.0), axis=-1, keepdims=True)

    lo = EXPERT_LANE0 + gidx * EXPERTS_PER_GROUP
    emask = (lane >= lo) & (lane < lo + EXPERTS_PER_GROUP)
    el = jnp.where(emask, lg, neg)
    mx1, i1 = first_argmax(el)
    mx2, i2 = first_argmax(jnp.where(lane == i1, neg, el))
    esum = jnp.sum(jnp.where(emask, jnp.exp(el - mx1), 0.0), axis=-1, keepdims=True)
    p1 = 1.0 / esum
    p2 = jnp.exp(mx2 - mx1) / esum
    den = p1 + p2
    w1 = g_w * (p1 / den)
    w2 = g_w * (p2 / den)

    memb = jnp.where((lane == i1) | (lane == i2), 1.0, 0.0)
    r_i = lax.broadcasted_iota(I32, (sub, sub), 0)
    c_i = lax.broadcasted_iota(I32, (sub, sub), 1)
    ltri = jnp.where(c_i < r_i, 1.0, 0.0).astype(BF16)
    tots = []
    carry = carry_ref[...]
    for r0 in range(0, tm, sub):
        mb = memb[r0:r0 + sub]
        tots.append(jnp.dot(ltri, mb.astype(BF16), preferred_element_type=F32) + carry)
        carry = carry + jnp.sum(mb, axis=0, keepdims=True)
    carry_ref[...] = carry
    tot = jnp.concatenate(tots, axis=0)
    rank1 = jnp.sum(jnp.where(lane == i1, tot, 0.0), axis=-1, keepdims=True).astype(I32)
    rank2 = jnp.sum(jnp.where(lane == i2, tot, 0.0), axis=-1, keepdims=True).astype(I32)

    e1 = (i1 - EXPERT_LANE0).astype(I32)
    e2 = (i2 - EXPERT_LANE0).astype(I32)
    ri = jnp.where(lane == 0, e1, jnp.where(lane == 1, e2, jnp.where(lane == 2, rank1, jnp.where(lane == 3, rank2, 0))))
    ri_ref[...] = ri.T[:F32_SUBLANES]
    rw_ref[...] = jnp.where(lane == 0, w1, jnp.where(lane == 1, w2, 0.0))
    cnt_ref[...] = jnp.broadcast_to(carry, cnt_ref.shape)


def _router(lg, seq):
    t = lg.shape[0]
    tm = min(ROUTER_TILE, seq)
    row = pl.BlockSpec((tm, LANES), lambda i: (i, 0))
    return pl.pallas_call(
        functools.partial(_router_kernel, sub=min(ROW_TILE, tm)),
        grid=(t // tm,),
        in_specs=[row],
        out_specs=[pl.BlockSpec((F32_SUBLANES, tm), lambda i: (0, i)), row,
                   pl.BlockSpec((F32_SUBLANES, LANES), lambda i: (0, 0))],
        out_shape=[jax.ShapeDtypeStruct((F32_SUBLANES, t), I32),
                   jax.ShapeDtypeStruct((t, LANES), F32),
                   jax.ShapeDtypeStruct((F32_SUBLANES, LANES), F32)],
        scratch_shapes=[pltpu.VMEM((1, LANES), F32)],
        compiler_params=_params(("arbitrary",)),
        name="moe_router",
    )(lg)


def _expert_kernel(be_ref, first_ref, slot_ref, nxt_ref, na_ref, xb_ref, wg_hbm, wu_hbm, wd_hbm, yb_ref,
                   wg_buf, wu_buf, wd_buf, wgb_ref, wub_ref, wdb_ref, sem, *, layer):
    i = pl.program_id(0)

    def fetch(e, slot):
        return (pltpu.make_async_copy(wg_hbm.at[layer, e], wg_buf.at[slot], sem.at[slot, 0]),
                pltpu.make_async_copy(wu_hbm.at[layer, e], wu_buf.at[slot], sem.at[slot, 1]),
                pltpu.make_async_copy(wd_hbm.at[layer, e], wd_buf.at[slot], sem.at[slot, 2]))

    @pl.when(i < na_ref[0])
    def _():
        @pl.when(first_ref[i] == 1)
        def _():
            e, slot = be_ref[i], slot_ref[i]

            @pl.when(i == 0)
            def _():
                for cp in fetch(e, slot):
                    cp.start()

            for cp in fetch(e, slot):
                cp.wait()
            wgb_ref[...] = wg_buf[slot].astype(BF16)
            wub_ref[...] = wu_buf[slot].astype(BF16)
            wdb_ref[...] = wd_buf[slot].astype(BF16)

            @pl.when(nxt_ref[i] >= 0)
            def _():
                for cp in fetch(nxt_ref[i], 1 - slot):
                    cp.start()

        xb = _unpack_rows(xb_ref[...]).astype(BF16)
        gate = jnp.dot(xb, wgb_ref[...], preferred_element_type=F32)
        up = jnp.dot(xb, wub_ref[...], preferred_element_type=F32)
        act = (jax.nn.silu(gate) * up).astype(BF16)
        yb_ref[...] = _pack_rows(jnp.dot(act, wdb_ref[...], preferred_element_type=F32))

    @pl.when(i >= na_ref[0])
    def _():
        yb_ref[...] = jnp.zeros_like(yb_ref)


def _experts(plan, xb, w_gate, w_up, w_down, layer, blk):
    rows = xb.shape[0]
    d, de = w_gate.shape[-2:]
    nb = rows // blk

    def row_map(i, be, first, slot, nxt, na):
        return (jnp.minimum(i, jnp.maximum(na[0] - 1, 0)), 0)

    hbm = pl.BlockSpec(memory_space=pl.ANY)
    return pl.pallas_call(
        functools.partial(_expert_kernel, layer=layer),
        grid_spec=pltpu.PrefetchScalarGridSpec(
            num_scalar_prefetch=5,
            grid=(nb,),
            in_specs=[pl.BlockSpec((blk, d // 2), row_map), hbm, hbm, hbm],
            out_specs=pl.BlockSpec((blk, d // 2), lambda i, *_: (i, 0)),
            scratch_shapes=[pltpu.VMEM((2, d, de), F32), pltpu.VMEM((2, d, de), F32), pltpu.VMEM((2, de, d), F32),
                            pltpu.VMEM((d, de), BF16), pltpu.VMEM((d, de), BF16), pltpu.VMEM((de, d), BF16),
                            pltpu.SemaphoreType.DMA((2, 3))]),
        out_shape=jax.ShapeDtypeStruct((rows, d // 2), I32),
        compiler_params=_params(("arbitrary",)),
        name="moe_experts",
    )(*plan, xb, w_gate, w_up, w_down)


def _sc_gather_rows(table, idx):
    b = idx.shape[0]
    d = table.shape[1]
    per_w = b // SC_WORKERS
    ch = SC_CHUNK_ROWS
    mesh = plsc.VectorSubcoreMesh(core_axis_name="c", subcore_axis_name="s")

    n_ch = per_w // ch
    assert per_w * SC_WORKERS == b and n_ch * ch == per_w and n_ch % 2 == 0

    def body(table_hbm, idx_hbm, out_hbm, idx_v, rows_v, gsem, wsem):
        wid = lax.axis_index("s") * SC_CORES + lax.axis_index("c")
        base = wid * per_w
        pltpu.sync_copy(idx_hbm.at[pl.ds(base, per_w)], idx_v)

        def write_back(c, buf):
            off = pl.multiple_of(c * ch, ch)
            return pltpu.make_async_copy(rows_v.at[buf], out_hbm.at[pl.ds(base + off, ch)], wsem.at[buf])

        @pl.loop(0, n_ch, step=2)
        def _(c0):
            for buf in range(2):
                c = c0 + buf

                @pl.when(c0 > 0)
                def _():
                    write_back(c - 2, buf).wait()

                off = pl.multiple_of(c * ch, ch)
                pltpu.async_copy(table_hbm.at[idx_v.at[pl.ds(off, ch)]], rows_v.at[buf], gsem).wait()
                write_back(c, buf).start()

        for buf in range(2):
            write_back(n_ch - 2 + buf, buf).wait()

    return pl.kernel(body, out_type=jax.ShapeDtypeStruct((b, d), table.dtype), mesh=mesh,
                     scratch_types=[pltpu.VMEM((per_w,), I32), pltpu.VMEM((2, ch, d), table.dtype),
                                    pltpu.SemaphoreType.DMA, pltpu.SemaphoreType.DMA((2,))],
                     name="sc_gather")(table, idx)


def _sc_scatter_rows(rows, idx, n_out):
    t, d = rows.shape
    top_k, _, n_ch, ch = idx.shape
    per_w = n_ch * ch
    assert per_w * SC_WORKERS == t and idx.shape[1] == SC_WORKERS
    mesh = plsc.VectorSubcoreMesh(core_axis_name="c", subcore_axis_name="s")

    assert n_ch % 2 == 0

    def body(rows_hbm, idx_hbm, out_hbm, idx_v, rows_v, lsem, ssem):
        wid = lax.axis_index("s") * SC_CORES + lax.axis_index("c")
        base = wid * per_w
        for k in range(top_k):
            pltpu.sync_copy(idx_hbm.at[k, wid], idx_v.at[k])

        def load(c, buf):
            off = pl.multiple_of(c * ch, ch)
            return pltpu.make_async_copy(rows_hbm.at[pl.ds(base + off, ch)], rows_v.at[buf], lsem.at[buf])

        load(0, 0).start()

        @pl.loop(0, n_ch, step=2)
        def _(c0):
            for buf in range(2):
                c = c0 + buf
                load(c, buf).wait()

                @pl.when(c + 1 < n_ch)
                def _():
                    load(c + 1, 1 - buf).start()

                scatters = [pltpu.async_copy(rows_v.at[buf], out_hbm.at[idx_v.at[k, c]], ssem)
                            for k in range(top_k)]
                for cp in scatters:
                    cp.wait()

    return pl.kernel(body, out_type=jax.ShapeDtypeStruct((n_out, d), rows.dtype), mesh=mesh,
                     scratch_types=[pltpu.VMEM((top_k, n_ch, ch), I32), pltpu.VMEM((2, ch, d), rows.dtype),
                                    pltpu.SemaphoreType.DMA((2,)), pltpu.SemaphoreType.DMA],
                     name="sc_scatter")(rows, idx)


def _combine_final_kernel(x_ref, rw_ref, y0_ref, y1_ref, gf_ref, o_ref):
    w = rw_ref[...]
    out = x_ref[...] + w[:, 0:1] * _unpack_rows(y0_ref[...]) + w[:, 1:2] * _unpack_rows(y1_ref[...])
    o_ref[...] = _rms(out, gf_ref[...])


def _combine_final(x2, rw, y2, g_final, seq, part, n_parts):
    t, d = x2.shape
    tm = min(ROW_TILE, seq)
    nt = t // tm // n_parts
    row = lambda w: pl.BlockSpec((tm, w), lambda i: (part * nt + i, 0))
    return pl.pallas_call(
        _combine_final_kernel,
        grid=(nt,),
        in_specs=[row(d), row(LANES), pl.BlockSpec((tm, d // 2), lambda i: (i, 0)),
                  pl.BlockSpec((tm, d // 2), lambda i: (nt + i, 0)), pl.BlockSpec((1, d), lambda i: (0, 0))],
        out_specs=row(d),
        out_shape=jax.ShapeDtypeStruct((t, d), F32),
        input_output_aliases={0: 0},
        compiler_params=_params(("parallel",)),
        name="moe_combine_final",
    )(x2, rw, y2, y2, g_final)


def _route_plan(ri, counts_f, blk, n_blocks):
    cnt = counts_f[0, EXPERT_LANE0:EXPERT_LANE0 + N_EXPERTS].astype(I32)
    nblk = (cnt + blk - 1) // blk
    cum = jnp.cumsum(nblk)
    pad_start = (cum - nblk) * blk
    e = ri[0:TOP_K]
    rank = ri[TOP_K:2 * TOP_K]
    onehot = e[:, :, None] == jnp.arange(N_EXPERTS, dtype=I32)[None, None, :]
    dest = rank + jnp.sum(jnp.where(onehot, pad_start[None, None, :], 0), axis=-1)
    blocks = jnp.arange(n_blocks, dtype=I32)
    block_expert = jnp.minimum(jnp.sum((cum[None, :] <= blocks[:, None]).astype(I32), axis=-1), N_EXPERTS - 1)
    n_active = cum[-1:].astype(I32)
    experts = jnp.arange(N_EXPERTS, dtype=I32)
    is_block_expert = block_expert[:, None] == experts[None, :]
    first = jnp.sum(jnp.where(is_block_expert, ((cum - nblk)[None, :] == blocks[:, None]).astype(I32), 0), axis=-1)
    used = (nblk > 0).astype(I32)
    slot_e = (jnp.cumsum(used) - used) % 2
    later = (experts[None, :] > experts[:, None]) & (nblk[None, :] > 0)
    next_e = jnp.min(jnp.where(later, experts[None, :], N_EXPERTS), axis=-1)
    next_e = jnp.where(next_e == N_EXPERTS, -1, next_e)
    slot = jnp.sum(jnp.where(is_block_expert, slot_e[None, :], 0), axis=-1)
    nxt = jnp.sum(jnp.where(is_block_expert, next_e[None, :], 0), axis=-1)
    plan = (block_expert, first.astype(I32), slot.astype(I32), nxt.astype(I32), n_active)
    return dest.astype(I32), plan


def kernel(x, mem, norm_mix, w_in, lambda_q1, lambda_k1, lambda_q2, lambda_k2, subln, conv_w, w_branch, w_o,
           norm_cross, norm_mem, w_cq, w_ckv, w_co, norm_ffn, w_router_group, w_router_expert, w_exp_gate,
           w_exp_up, w_exp_down, norm_final):
    batch, seq, d = x.shape
    depth = w_in.shape[0]
    n_mem = mem.shape[1]
    t = batch * seq
    dh = d // (2 * DA_HEADS)
    assert 2 * dh == LANES and d % LANES == 0 and w_in.shape[2] == 8 * d
    assert w_router_group.shape[2] == N_GROUPS and w_router_expert.shape[2] == N_EXPERTS
    blk = min(EXPERT_BLOCK, seq)
    n_blocks = (t * TOP_K) // blk + N_EXPERTS
    parts = COMBINE_PARTS if t % (COMBINE_PARTS * SC_WORKERS * SC_CHUNK_ROWS) == 0 else 1
    tokens_per_worker = t // SC_WORKERS

    tabs = _rotary_tables(seq, dh)
    x2 = x.reshape(t, d)
    kv = _mem_kv(mem.reshape(batch * n_mem, d), norm_mem.reshape(1, d), w_ckv.astype(BF16))
    w_r = jnp.concatenate([w_router_group, w_router_expert,
                           jnp.zeros((depth, d, LANES - N_GROUPS - N_EXPERTS), F32)], axis=-1).astype(BF16)

    moe_out = None
    for l in range(depth):
        lambda_init = 0.8 - 0.6 * math.exp(-0.3 * l)
        g_mix, w_in_l = norm_mix[l].reshape(1, d), w_in[l].astype(BF16)
        if moe_out is None:
            proj = _inproj(x2, g_mix, w_in_l, tabs, seq, dh)
        else:
            proj = None
            for p in range(parts):
                x2, proj = _inproj_combine(x2, moe_out[0], moe_out[1][p], g_mix, w_in_l, tabs, proj, seq, dh,
                                           p, parts)
        y_attn = _attention(proj, lambda_q1[l].reshape(1, dh), lambda_k1[l].reshape(1, dh),
                            lambda_q2[l].reshape(1, dh), lambda_k2[l].reshape(1, dh),
                            subln[l].reshape(1, LANES), batch, seq, d, lambda_init)
        x2 = _mixer_out(x2, y_attn, proj, conv_w[l], w_branch[l, 0].astype(BF16), w_branch[l, 1].astype(BF16),
                        w_o[l].astype(BF16), seq)
        x2, h_ffn, logits = _cross_attention(x2, norm_cross[l].reshape(1, d), w_cq[l].astype(BF16), kv, l, 0,
                                             w_co[l].astype(BF16), norm_ffn[l].reshape(1, d), w_r[l], seq, n_mem)
        ri, rw, counts = _router(logits, seq)
        dest, plan = _route_plan(ri, counts, blk, n_blocks)
        scatter_idx = dest.reshape(TOP_K, SC_WORKERS, tokens_per_worker // SC_CHUNK_ROWS, SC_CHUNK_ROWS)
        xb = _sc_scatter_rows(h_ffn, scatter_idx, n_blocks * blk)
        yb = _experts(plan, xb, w_exp_gate, w_exp_up, w_exp_down, l, blk)
        tp = t // parts
        moe_out = (rw, [_sc_gather_rows(yb, dest[:, p * tp:(p + 1) * tp].reshape(-1)) for p in range(parts)])
    for p in range(parts):
        x2 = _combine_final(x2, moe_out[0], moe_out[1][p], norm_final.reshape(1, d), seq, p, parts)
    return x2.reshape(batch, seq, d)
```

```python
import functools
import math

import jax
import jax.numpy as jnp
from jax import lax
from jax.experimental import pallas as pl
from jax.experimental.pallas import tpu as pltpu
from jax.experimental.pallas import tpu_sc as plsc

EPS = 1e-6
DA_HEADS = 8
MEM_HEADS = 4
N_GROUPS = 4
EXPERTS_PER_GROUP = 8
N_EXPERTS = N_GROUPS * EXPERTS_PER_GROUP
TOP_K = 2
ROPE_THETA = 500000.0
LANES = 128
F32_SUBLANES = 8
BF16_SUBLANES = 16
EXPERT_LANE0 = N_GROUPS
ROW_TILE = 512
ROUTER_TILE = 2048
Q_TILE = 2048
EXPERT_BLOCK = 256
PROJ_ROW_CHUNK = 1024
PROJ_COMBINE_TILE = 1024
ATTN_SCORE_CHUNK = 128
ATTN_SOFTMAX_CHUNK = 128
COMBINE_PARTS = 4
SC_CORES = 2
SC_WORKERS = 32
SC_CHUNK_ROWS = 64
VMEM_LIMIT = 56 * 1024 * 1024

F32 = jnp.float32
BF16 = jnp.bfloat16
I32 = jnp.int32
NT_DIMS = (((1,), (1,)), ((), ()))


def _rms(x, g):
    return x * lax.rsqrt(jnp.mean(x * x, axis=-1, keepdims=True) + EPS) * g


def _pack_rows(x):
    half = x.shape[1] // 2
    lo = lax.bitcast_convert_type(x[:, :half].astype(jnp.bfloat16).astype(F32), jnp.uint32)
    hi = lax.bitcast_convert_type(x[:, half:].astype(jnp.bfloat16).astype(F32), jnp.uint32)
    return lax.bitcast_convert_type((hi & jnp.uint32(0xFFFF0000)) | (lo >> 16), I32)


def _unpack_rows(w):
    u = lax.bitcast_convert_type(w, jnp.uint32)
    lo = lax.bitcast_convert_type(u << 16, F32)
    hi = lax.bitcast_convert_type(u & jnp.uint32(0xFFFF0000), F32)
    return jnp.concatenate([lo, hi], axis=1)


def _params(sem, vmem=VMEM_LIMIT):
    return pltpu.CompilerParams(dimension_semantics=sem, vmem_limit_bytes=vmem)


def _inproj_kernel(x_ref, g_ref, w_ref, cos_ref, s1_ref, s2_ref, o_ref, h_ref, *, rc, half, q_scale):
    @pl.when(pl.program_id(1) == 0)
    def _():
        h_ref[...] = _rms(x_ref[...], g_ref[...]).astype(BF16)

    _proj_column_block(h_ref, w_ref, cos_ref, s1_ref, s2_ref, o_ref, rc, half, q_scale)


def _inproj_combine_kernel(x_ref, rw_ref, y0_ref, y1_ref, g_ref, w_ref, cos_ref, s1_ref, s2_ref, *rest,
                           rc, half, q_scale):
    xnew_ref, o_ref, h_ref = rest[-3:]

    @pl.when(pl.program_id(1) == 0)
    def _():
        w = rw_ref[...]
        x_new = x_ref[...] + w[:, 0:1] * _unpack_rows(y0_ref[...]) + w[:, 1:2] * _unpack_rows(y1_ref[...])
        xnew_ref[...] = x_new
        h_ref[...] = _rms(x_new, g_ref[...]).astype(BF16)

    _proj_column_block(h_ref, w_ref, cos_ref, s1_ref, s2_ref, o_ref, rc, half, q_scale)


def _proj_column_block(h_ref, w_ref, cos_ref, s1_ref, s2_ref, o_ref, rc, half, q_scale):
    j = pl.program_id(1)
    tm, tn = o_ref.shape

    def run(rot, scale):
        def body(r, carry):
            r0 = pl.multiple_of(r * rc, rc)
            acc = jnp.dot(h_ref[pl.ds(r0, rc), :], w_ref[...], preferred_element_type=F32)
            if not rot:
                o_ref[pl.ds(r0, rc), :] = acc.astype(o_ref.dtype)
                return carry
            c = cos_ref[pl.ds(r0, rc), :]
            s1 = s1_ref[pl.ds(r0, rc), :]
            s2 = s2_ref[pl.ds(r0, rc), :]
            for cc in range(tn // LANES):
                a = acc[:, cc * LANES:(cc + 1) * LANES]
                a = a * c + pltpu.roll(a, half, 1) * s1 + pltpu.roll(a, LANES - half, 1) * s2
                if scale != 1.0:
                    a = a * scale
                o_ref[pl.ds(r0, rc), cc * LANES:(cc + 1) * LANES] = a.astype(o_ref.dtype)
            return carry
        lax.fori_loop(0, tm // rc, body, 0)

    @pl.when(j == 0)
    def _():
        run(True, q_scale)

    @pl.when(j == 1)
    def _():
        run(True, 1.0)

    @pl.when(j >= 2)
    def _():
        run(False, 1.0)


def _inproj(x2, g, w_bf, layer, tabs, seq, dh):
    t, d = x2.shape
    n = w_bf.shape[2]
    tm, tn = seq, d
    rc = min(PROJ_ROW_CHUNK, tm)
    cos_t, s1_t, s2_t = tabs
    tab_spec = pl.BlockSpec((seq, LANES), lambda i, j: (0, 0))
    return pl.pallas_call(
        functools.partial(_inproj_kernel, rc=rc, half=dh // 8, q_scale=dh ** -0.5 * math.log2(math.e)),
        grid=(t // tm, n // tn),
        in_specs=[pl.BlockSpec((tm, d), lambda i, j: (i, 0)),
                  pl.BlockSpec((1, d), lambda i, j: (0, 0)),
                  pl.BlockSpec((None, d, tn), lambda i, j: (layer, 0, j)),
                  tab_spec, tab_spec, tab_spec],
        out_specs=pl.BlockSpec((tm, tn), lambda i, j: (i, j)),
        out_shape=jax.ShapeDtypeStruct((t, n), BF16),
        scratch_shapes=[pltpu.VMEM((tm, d), BF16)],
        compiler_params=_params(("parallel", "arbitrary")),
        name="inproj",
    )(x2, g, w_bf, cos_t, s1_t, s2_t)


def _inproj_combine(x2, rw, y2, g, w_bf, layer, tabs, proj_prev, seq, dh, part, n_parts):
    t, d = x2.shape
    n = w_bf.shape[2]
    tm, tn = min(PROJ_COMBINE_TILE, seq), d
    ntp = t // n_parts // tm
    row0 = part * ntp
    tiles_per_seq = seq // tm
    cos_t, s1_t, s2_t = tabs
    rows = lambda w: pl.BlockSpec((tm, w), lambda i, j: (row0 + i, 0))
    tab_spec = pl.BlockSpec((tm, LANES), lambda i, j: ((row0 + i) % tiles_per_seq, 0))
    in_specs = [rows(d), rows(LANES),
                pl.BlockSpec((tm, d // 2), lambda i, j: (i, 0)),
                pl.BlockSpec((tm, d // 2), lambda i, j: (ntp + i, 0)),
                pl.BlockSpec((1, d), lambda i, j: (0, 0)),
                pl.BlockSpec((None, d, tn), lambda i, j: (layer, 0, j)),
                tab_spec, tab_spec, tab_spec]
    args = [x2, rw, y2, y2, g, w_bf, cos_t, s1_t, s2_t]
    aliases = {0: 0}
    if proj_prev is not None:
        in_specs.append(pl.BlockSpec(memory_space=pl.ANY))
        args.append(proj_prev)
        aliases[len(args) - 1] = 1
    return pl.pallas_call(
        functools.partial(_inproj_combine_kernel, rc=min(PROJ_ROW_CHUNK, tm), half=dh // 8,
                          q_scale=dh ** -0.5 * math.log2(math.e)),
        grid=(ntp, n // tn),
        in_specs=in_specs,
        out_specs=[rows(d), pl.BlockSpec((tm, tn), lambda i, j: (row0 + i, j))],
        out_shape=[jax.ShapeDtypeStruct((t, d), F32), jax.ShapeDtypeStruct((t, n), BF16)],
        scratch_shapes=[pltpu.VMEM((tm, d), BF16)],
        input_output_aliases=aliases,
        compiler_params=_params(("parallel", "arbitrary")),
        name="inproj_combine",
    )(*args)


def _rotary_tables(seq, dh):
    rot = dh // 4
    half = rot // 2
    inv = jnp.float32(ROPE_THETA) ** (-jnp.arange(0, rot, 2, dtype=F32) / rot)
    ang = jnp.arange(seq, dtype=F32)[:, None] * inv[None, :]
    cos, sin = jnp.cos(ang), jnp.sin(ang)
    lane = jnp.arange(LANES) % dh
    idx = lane % half
    c_t = jnp.where(lane < rot, cos[:, idx], 1.0)
    s1_t = jnp.where((lane >= half) & (lane < rot), sin[:, idx], 0.0)
    s2_t = jnp.where(lane < half, -sin[:, idx], 0.0)
    return c_t.astype(F32), s1_t.astype(F32), s2_t.astype(F32)


def _attn_kernel(q_ref, k_ref, v_ref, lq1_ref, lk1_ref, lq2_ref, lk2_ref, sub_ref, o_ref,
                 s_ref, p_ref, *, dh, lambda_init, rc, rp):
    tq = q_ref.shape[0]
    n_chunks = tq // rc
    lane = lax.broadcasted_iota(I32, (1, LANES), 1)
    m1 = jnp.where(lane < dh, 1.0, 0.0).astype(BF16)
    m2 = jnp.where(lane >= dh, 1.0, 0.0).astype(BF16)
    lam = (jnp.exp(jnp.sum(lq1_ref[...] * lk1_ref[...], axis=-1, keepdims=True))
           - jnp.exp(jnp.sum(lq2_ref[...] * lk2_ref[...], axis=-1, keepdims=True)) + lambda_init)

    def scores(c):
        qc = q_ref[c * rc:(c + 1) * rc, :]
        qq = jnp.concatenate([qc * m1, qc * m2], axis=0)
        s_ref[c % 2] = lax.dot_general(qq, k_ref[...], NT_DIMS, preferred_element_type=F32)

    def softmax_pv(c, j):
        r0 = j * rp
        s1 = s_ref[c % 2, r0:r0 + rp, :]
        s2 = s_ref[c % 2, rc + r0:rc + r0 + rp, :]
        e1 = jnp.exp2(s1 - jnp.max(s1, axis=-1, keepdims=True))
        e2 = jnp.exp2(s2 - jnp.max(s2, axis=-1, keepdims=True))
        l1 = jnp.sum(e1, axis=-1, keepdims=True)
        l2 = jnp.sum(e2, axis=-1, keepdims=True)
        slot = (c * (rc // rp) + j) % 2
        p_ref[slot] = (e1 - e2 * (lam * l1 / l2)).astype(BF16)
        od = jnp.dot(p_ref[slot], v_ref[...], preferred_element_type=F32) / l1
        od = _rms(od, sub_ref[...]) * (1.0 - lambda_init)
        o_ref[c * rc + r0:c * rc + r0 + rp, :] = od.astype(o_ref.dtype)

    scores(0)
    for c in range(n_chunks):
        if c + 1 < n_chunks:
            scores(c + 1)
        for j in range(rc // rp):
            softmax_pv(c, j)


def _attention(proj, lq1, lk1, lq2, lk2, sub, batch, seq, d, lambda_init):
    t = proj.shape[0]
    nh = d // LANES
    dh = LANES // 2
    tq = min(Q_TILE, seq)
    nq = seq // tq
    rc = min(ATTN_SCORE_CHUNK, tq)
    rp = min(ATTN_SOFTMAX_CHUNK, rc)
    vec = pl.BlockSpec((1, dh), lambda b, h, qi: (0, 0))
    return pl.pallas_call(
        functools.partial(_attn_kernel, dh=dh, lambda_init=lambda_init, rc=rc, rp=rp),
        grid=(batch, nh, nq),
        scratch_shapes=[pltpu.VMEM((2, 2 * rc, seq), F32), pltpu.VMEM((2, rp, seq), BF16)],
        in_specs=[pl.BlockSpec((tq, LANES), lambda b, h, qi: (b * nq + qi, h)),
                  pl.BlockSpec((seq, LANES), lambda b, h, qi: (b, nh + h)),
                  pl.BlockSpec((seq, LANES), lambda b, h, qi: (b, 2 * nh + h)),
                  vec, vec, vec, vec,
                  pl.BlockSpec((1, LANES), lambda b, h, qi: (0, 0))],
        out_specs=pl.BlockSpec((tq, LANES), lambda b, h, qi: (b * nq + qi, h)),
        out_shape=jax.ShapeDtypeStruct((t, d), BF16),
        compiler_params=_params(("parallel", "parallel", "arbitrary")),
        name="diff_attn",
    )(proj, proj, proj, lq1, lk1, lq2, lk2, sub)


def _mixout_kernel(x_ref, ya_ref, cb_ref, cc_ref, cx_ref, g0_ref, g1_ref, pc_ref, px_ref, nc_ref, nx_ref,
                   cw_ref, wb0_ref, wb1_ref, wo_ref, o_ref, *, tiles_per_seq):
    i = pl.program_id(0)
    tm = x_ref.shape[0]
    u = cc_ref[...].astype(F32) * cx_ref[...].astype(F32)
    hr = pc_ref.shape[0]
    u_prev = pc_ref[hr - 1:hr, :].astype(F32) * px_ref[hr - 1:hr, :].astype(F32)
    u_next = nc_ref[0:1, :].astype(F32) * nx_ref[0:1, :].astype(F32)
    pos = i % tiles_per_seq
    u_prev = jnp.where(pos == 0, 0.0, u_prev)
    u_next = jnp.where(pos == tiles_per_seq - 1, 0.0, u_next)
    row = lax.broadcasted_iota(I32, u.shape, 0)
    u_up = jnp.where(row == 0, u_prev, pltpu.roll(u, 1, 0))
    u_dn = jnp.where(row == tm - 1, u_next, pltpu.roll(u, tm - 1, 0))
    conv = cw_ref[0:1, :] * u_up + cw_ref[1:2, :] * u + cw_ref[2:3, :] * u_dn
    y_conv = (cb_ref[...].astype(F32) * conv).astype(BF16)
    br0 = jnp.dot(ya_ref[...], wb0_ref[...], preferred_element_type=F32)
    br1 = jnp.dot(y_conv, wb1_ref[...], preferred_element_type=F32)
    z = jax.nn.sigmoid(g0_ref[...].astype(F32)) * br0 + jax.nn.sigmoid(g1_ref[...].astype(F32)) * br1
    o_ref[...] = x_ref[...] + jnp.dot(z.astype(BF16), wo_ref[...], preferred_element_type=F32)


def _mixer_out(x2, y_attn, proj, conv_w, w_branch, w_o, layer, seq):
    t, d = x2.shape
    tm = min(ROW_TILE, seq)
    tps = seq // tm
    hr = BF16_SUBLANES
    nhb = t // hr
    row = lambda c: pl.BlockSpec((tm, d), lambda i: (i, c))
    prev = lambda c: pl.BlockSpec((hr, d), lambda i: (jnp.maximum(i * (tm // hr) - 1, 0), c))
    nxt = lambda c: pl.BlockSpec((hr, d), lambda i: (jnp.minimum((i + 1) * (tm // hr), nhb - 1), c))
    branch = lambda b: pl.BlockSpec((None, None, d, d), lambda i: (layer, b, 0, 0))
    return pl.pallas_call(
        functools.partial(_mixout_kernel, tiles_per_seq=tps),
        grid=(t // tm,),
        in_specs=[row(0), row(0), row(3), row(4), row(5), row(6), row(7),
                  prev(4), prev(5), nxt(4), nxt(5),
                  pl.BlockSpec((None,) + conv_w.shape[1:], lambda i: (layer, 0, 0)),
                  branch(0), branch(1), pl.BlockSpec((None, d, d), lambda i: (layer, 0, 0))],
        out_specs=row(0),
        out_shape=jax.ShapeDtypeStruct((t, d), F32),
        compiler_params=_params(("parallel",)),
        name="mixer_out",
    )(x2, y_attn, proj, proj, proj, proj, proj, proj, proj, proj, proj, conv_w, w_branch, w_branch, w_o)


def _memkv_kernel(m_ref, g_ref, w_ref, o_ref):
    h = _rms(m_ref[...], g_ref[...]).astype(BF16)
    o_ref[...] = jnp.dot(h, w_ref[...], preferred_element_type=F32).astype(o_ref.dtype)


def _mem_kv(mem2, g, w_bf):
    depth, d, n = w_bf.shape
    rows = mem2.shape[0]
    tn = min(n, 1024)
    return pl.pallas_call(
        _memkv_kernel,
        grid=(depth, n // tn),
        in_specs=[pl.BlockSpec((rows, d), lambda l, j: (0, 0)),
                  pl.BlockSpec((1, d), lambda l, j: (0, 0)),
                  pl.BlockSpec((None, d, tn), lambda l, j: (l, 0, j))],
        out_specs=pl.BlockSpec((None, rows, tn), lambda l, j: (l, 0, j)),
        out_shape=jax.ShapeDtypeStruct((depth, rows, n), BF16),
        compiler_params=_params(("parallel", "parallel")),
        name="mem_kv",
    )(mem2, g, w_bf)


def _cross_kernel(x_ref, g_ref, wq_ref, kv_ref, wo_ref, gf_ref, wr_ref, o_ref, hf_ref, lg_ref, *, heads):
    x = x_ref[...]
    d = x.shape[1]
    hd = d // heads
    h = _rms(x, g_ref[...]).astype(BF16)
    q = (jnp.dot(h, wq_ref[...], preferred_element_type=F32) * (hd ** -0.5)).astype(BF16)
    outs = []
    for hh in range(heads):
        qh = q[:, hh * hd:(hh + 1) * hd]
        kh = kv_ref[:, hh * hd:(hh + 1) * hd]
        vh = kv_ref[:, d + hh * hd:d + (hh + 1) * hd]
        s = lax.dot_general(qh, kh, NT_DIMS, preferred_element_type=F32)
        p = jnp.exp(s - jnp.max(s, axis=-1, keepdims=True))
        l = jnp.sum(p, axis=-1, keepdims=True)
        outs.append((jnp.dot(p.astype(BF16), vh, preferred_element_type=F32) / l).astype(BF16))
    o = jnp.concatenate(outs, axis=1)
    x_new = x + jnp.dot(o, wo_ref[...], preferred_element_type=F32)
    o_ref[...] = x_new
    h_ffn = _rms(x_new, gf_ref[...])
    hf_ref[...] = _pack_rows(h_ffn)
    lg_ref[...] = jnp.dot(h_ffn.astype(BF16), wr_ref[...], preferred_element_type=F32)


def _cross_attention(x2, g, wq, kv, layer, wo, g_ffn, w_r, seq, n_mem):
    t, d = x2.shape
    tm = min(ROW_TILE, seq)
    tps = seq // tm
    row = lambda w: pl.BlockSpec((tm, w), lambda i: (i, 0))
    const = lambda r, c: pl.BlockSpec((r, c), lambda i: (0, 0))
    of_layer = lambda r, c: pl.BlockSpec((None, r, c), lambda i: (layer, 0, 0))
    return pl.pallas_call(
        functools.partial(_cross_kernel, heads=MEM_HEADS),
        grid=(t // tm,),
        in_specs=[row(d), const(1, d), of_layer(d, d),
                  pl.BlockSpec((None, n_mem, 2 * d), lambda i: (layer, i // tps, 0)),
                  of_layer(d, d), const(1, d), of_layer(d, LANES)],
        out_specs=[row(d), row(d // 2), row(LANES)],
        out_shape=[jax.ShapeDtypeStruct((t, d), F32), jax.ShapeDtypeStruct((t, d // 2), I32),
                   jax.ShapeDtypeStruct((t, LANES), F32)],
        compiler_params=_params(("parallel",)),
        name="cross_attn",
    )(x2, g, wq, kv, wo, g_ffn, w_r)


def _router_kernel(lg_ref, ri_ref, rw_ref, cnt_ref, carry_ref, *, sub):
    @pl.when(pl.program_id(0) == 0)
    def _():
        carry_ref[...] = jnp.zeros_like(carry_ref)

    lg = lg_ref[...]
    tm = lg.shape[0]
    lane = lax.broadcasted_iota(I32, lg.shape, 1).astype(F32)
    neg = -jnp.inf

    def first_argmax(v):
        m = jnp.max(v, axis=-1, keepdims=True)
        return m, jnp.min(jnp.where(v == m, lane, float(LANES)), axis=-1, keepdims=True)

    gmask = lane < N_GROUPS
    gl = jnp.where(gmask, lg, neg)
    gmax, gidx = first_argmax(gl)
    g_w = 1.0 / jnp.sum(jnp.where(gmask, jnp.exp(gl - gmax), 0.0), axis=-1, keepdims=True)

    lo = EXPERT_LANE0 + gidx * EXPERTS_PER_GROUP
    emask = (lane >= lo) & (lane < lo + EXPERTS_PER_GROUP)
    el = jnp.where(emask, lg, neg)
    mx1, i1 = first_argmax(el)
    mx2, i2 = first_argmax(jnp.where(lane == i1, neg, el))
    esum = jnp.sum(jnp.where(emask, jnp.exp(el - mx1), 0.0), axis=-1, keepdims=True)
    p1 = 1.0 / esum
    p2 = jnp.exp(mx2 - mx1) / esum
    den = p1 + p2
    w1 = g_w * (p1 / den)
    w2 = g_w * (p2 / den)

    memb = jnp.where((lane == i1) | (lane == i2), 1.0, 0.0)
    r_i = lax.broadcasted_iota(I32, (sub, sub), 0)
    c_i = lax.broadcasted_iota(I32, (sub, sub), 1)
    ltri = jnp.where(c_i < r_i, 1.0, 0.0).astype(BF16)
    tots = []
    carry = carry_ref[...]
    for r0 in range(0, tm, sub):
        mb = memb[r0:r0 + sub]
        tots.append(jnp.dot(ltri, mb.astype(BF16), preferred_element_type=F32) + carry)
        carry = carry + jnp.sum(mb, axis=0, keepdims=True)
    carry_ref[...] = carry
    tot = jnp.concatenate(tots, axis=0)
    rank1 = jnp.sum(jnp.where(lane == i1, tot, 0.0), axis=-1, keepdims=True).astype(I32)
    rank2 = jnp.sum(jnp.where(lane == i2, tot, 0.0), axis=-1, keepdims=True).astype(I32)

    e1 = (i1 - EXPERT_LANE0).astype(I32)
    e2 = (i2 - EXPERT_LANE0).astype(I32)
    ri = jnp.where(lane == 0, e1, jnp.where(lane == 1, e2, jnp.where(lane == 2, rank1, jnp.where(lane == 3, rank2, 0))))
    ri_ref[...] = ri.T[:F32_SUBLANES]
    rw_ref[...] = jnp.where(lane == 0, w1, jnp.where(lane == 1, w2, 0.0))
    cnt_ref[...] = jnp.broadcast_to(carry, cnt_ref.shape)


def _router(lg, seq):
    t = lg.shape[0]
    tm = min(ROUTER_TILE, seq)
    row = pl.BlockSpec((tm, LANES), lambda i: (i, 0))
    return pl.pallas_call(
        functools.partial(_router_kernel, sub=min(ROW_TILE, tm)),
        grid=(t // tm,),
        in_specs=[row],
        out_specs=[pl.BlockSpec((F32_SUBLANES, tm), lambda i: (0, i)), row,
                   pl.BlockSpec((F32_SUBLANES, LANES), lambda i: (0, 0))],
        out_shape=[jax.ShapeDtypeStruct((F32_SUBLANES, t), I32),
                   jax.ShapeDtypeStruct((t, LANES), F32),
                   jax.ShapeDtypeStruct((F32_SUBLANES, LANES), F32)],
        scratch_shapes=[pltpu.VMEM((1, LANES), F32)],
        compiler_params=_params(("arbitrary",)),
        name="moe_router",
    )(lg)


def _expert_kernel(be_ref, first_ref, slot_ref, nxt_ref, na_ref, xb_ref, wg_hbm, wu_hbm, wd_hbm, yb_ref,
                   wg_buf, wu_buf, wd_buf, wgb_ref, wub_ref, wdb_ref, sem, *, layer):
    i = pl.program_id(0)

    def fetch(e, slot):
        return (pltpu.make_async_copy(wg_hbm.at[layer, e], wg_buf.at[slot], sem.at[slot, 0]),
                pltpu.make_async_copy(wu_hbm.at[layer, e], wu_buf.at[slot], sem.at[slot, 1]),
                pltpu.make_async_copy(wd_hbm.at[layer, e], wd_buf.at[slot], sem.at[slot, 2]))

    @pl.when(i < na_ref[0])
    def _():
        @pl.when(first_ref[i] == 1)
        def _():
            e, slot = be_ref[i], slot_ref[i]

            @pl.when(i == 0)
            def _():
                for cp in fetch(e, slot):
                    cp.start()

            for cp in fetch(e, slot):
                cp.wait()
            wgb_ref[...] = wg_buf[slot].astype(BF16)
            wub_ref[...] = wu_buf[slot].astype(BF16)
            wdb_ref[...] = wd_buf[slot].astype(BF16)

            @pl.when(nxt_ref[i] >= 0)
            def _():
                for cp in fetch(nxt_ref[i], 1 - slot):
                    cp.start()

        xb = _unpack_rows(xb_ref[...]).astype(BF16)
        gate = jnp.dot(xb, wgb_ref[...], preferred_element_type=F32)
        up = jnp.dot(xb, wub_ref[...], preferred_element_type=F32)
        act = (jax.nn.silu(gate) * up).astype(BF16)
        yb_ref[...] = _pack_rows(jnp.dot(act, wdb_ref[...], preferred_element_type=F32))

    @pl.when(i >= na_ref[0])
    def _():
        yb_ref[...] = jnp.zeros_like(yb_ref)


def _experts(plan, xb, w_gate, w_up, w_down, layer, blk):
    rows = xb.shape[0]
    d, de = w_gate.shape[-2:]
    nb = rows // blk

    def row_map(i, be, first, slot, nxt, na):
        return (jnp.minimum(i, jnp.maximum(na[0] - 1, 0)), 0)

    hbm = pl.BlockSpec(memory_space=pl.ANY)
    return pl.pallas_call(
        functools.partial(_expert_kernel, layer=layer),
        grid_spec=pltpu.PrefetchScalarGridSpec(
            num_scalar_prefetch=5,
            grid=(nb,),
            in_specs=[pl.BlockSpec((blk, d // 2), row_map), hbm, hbm, hbm],
            out_specs=pl.BlockSpec((blk, d // 2), lambda i, *_: (i, 0)),
            scratch_shapes=[pltpu.VMEM((2, d, de), F32), pltpu.VMEM((2, d, de), F32), pltpu.VMEM((2, de, d), F32),
                            pltpu.VMEM((d, de), BF16), pltpu.VMEM((d, de), BF16), pltpu.VMEM((de, d), BF16),
                            pltpu.SemaphoreType.DMA((2, 3))]),
        out_shape=jax.ShapeDtypeStruct((rows, d // 2), I32),
        compiler_params=_params(("arbitrary",)),
        name="moe_experts",
    )(*plan, xb, w_gate, w_up, w_down)


def _sc_gather_rows(table, idx):
    b = idx.shape[0]
    d = table.shape[1]
    per_w = b // SC_WORKERS
    ch = SC_CHUNK_ROWS
    mesh = plsc.VectorSubcoreMesh(core_axis_name="c", subcore_axis_name="s")

    n_ch = per_w // ch
    assert per_w * SC_WORKERS == b and n_ch * ch == per_w and n_ch % 2 == 0

    def body(table_hbm, idx_hbm, out_hbm, idx_v, rows_v, gsem, wsem):
        wid = lax.axis_index("s") * SC_CORES + lax.axis_index("c")
        base = wid * per_w
        pltpu.sync_copy(idx_hbm.at[pl.ds(base, per_w)], idx_v)

        def write_back(c, buf):
            off = pl.multiple_of(c * ch, ch)
            return pltpu.make_async_copy(rows_v.at[buf], out_hbm.at[pl.ds(base + off, ch)], wsem.at[buf])

        @pl.loop(0, n_ch, step=2)
        def _(c0):
            for buf in range(2):
                c = c0 + buf

                @pl.when(c0 > 0)
                def _():
                    write_back(c - 2, buf).wait()

                off = pl.multiple_of(c * ch, ch)
                pltpu.async_copy(table_hbm.at[idx_v.at[pl.ds(off, ch)]], rows_v.at[buf], gsem).wait()
                write_back(c, buf).start()

        for buf in range(2):
            write_back(n_ch - 2 + buf, buf).wait()

    return pl.kernel(body, out_type=jax.ShapeDtypeStruct((b, d), table.dtype), mesh=mesh,
                     scratch_types=[pltpu.VMEM((per_w,), I32), pltpu.VMEM((2, ch, d), table.dtype),
                                    pltpu.SemaphoreType.DMA, pltpu.SemaphoreType.DMA((2,))],
                     name="sc_gather")(table, idx)


def _sc_scatter_rows(rows, idx, n_out):
    t, d = rows.shape
    top_k, _, n_ch, ch = idx.shape
    per_w = n_ch * ch
    assert per_w * SC_WORKERS == t and idx.shape[1] == SC_WORKERS
    mesh = plsc.VectorSubcoreMesh(core_axis_name="c", subcore_axis_name="s")

    assert n_ch % 2 == 0

    def body(rows_hbm, idx_hbm, out_hbm, idx_v, rows_v, lsem, ssem):
        wid = lax.axis_index("s") * SC_CORES + lax.axis_index("c")
        base = wid * per_w
        for k in range(top_k):
            pltpu.sync_copy(idx_hbm.at[k, wid], idx_v.at[k])

        def load(c, buf):
            off = pl.multiple_of(c * ch, ch)
            return pltpu.make_async_copy(rows_hbm.at[pl.ds(base + off, ch)], rows_v.at[buf], lsem.at[buf])

        load(0, 0).start()

        @pl.loop(0, n_ch, step=2)
        def _(c0):
            for buf in range(2):
                c = c0 + buf
                load(c, buf).wait()

                @pl.when(c + 1 < n_ch)
                def _():
                    load(c + 1, 1 - buf).start()

                scatters = [pltpu.async_copy(rows_v.at[buf], out_hbm.at[idx_v.at[k, c]], ssem)
                            for k in range(top_k)]
                for cp in scatters:
                    cp.wait()

    return pl.kernel(body, out_type=jax.ShapeDtypeStruct((n_out, d), rows.dtype), mesh=mesh,
                     scratch_types=[pltpu.VMEM((top_k, n_ch, ch), I32), pltpu.VMEM((2, ch, d), rows.dtype),
                                    pltpu.SemaphoreType.DMA((2,)), pltpu.SemaphoreType.DMA],
                     name="sc_scatter")(rows, idx)


def _combine_final_kernel(x_ref, rw_ref, y0_ref, y1_ref, gf_ref, o_ref):
    w = rw_ref[...]
    out = x_ref[...] + w[:, 0:1] * _unpack_rows(y0_ref[...]) + w[:, 1:2] * _unpack_rows(y1_ref[...])
    o_ref[...] = _rms(out, gf_ref[...])


def _combine_final(x2, rw, y2, g_final, seq, part, n_parts):
    t, d = x2.shape
    tm = min(ROW_TILE, seq)
    nt = t // tm // n_parts
    row = lambda w: pl.BlockSpec((tm, w), lambda i: (part * nt + i, 0))
    return pl.pallas_call(
        _combine_final_kernel,
        grid=(nt,),
        in_specs=[row(d), row(LANES), pl.BlockSpec((tm, d // 2), lambda i: (i, 0)),
                  pl.BlockSpec((tm, d // 2), lambda i: (nt + i, 0)), pl.BlockSpec((1, d), lambda i: (0, 0))],
        out_specs=row(d),
        out_shape=jax.ShapeDtypeStruct((t, d), F32),
        input_output_aliases={0: 0},
        compiler_params=_params(("parallel",)),
        name="moe_combine_final",
    )(x2, rw, y2, y2, g_final)


def _route_plan(ri, counts_f, blk, n_blocks):
    cnt = counts_f[0, EXPERT_LANE0:EXPERT_LANE0 + N_EXPERTS].astype(I32)
    nblk = (cnt + blk - 1) // blk
    cum = jnp.cumsum(nblk)
    pad_start = (cum - nblk) * blk
    e = ri[0:TOP_K]
    rank = ri[TOP_K:2 * TOP_K]
    onehot = e[:, :, None] == jnp.arange(N_EXPERTS, dtype=I32)[None, None, :]
    dest = rank + jnp.sum(jnp.where(onehot, pad_start[None, None, :], 0), axis=-1)
    blocks = jnp.arange(n_blocks, dtype=I32)
    block_expert = jnp.minimum(jnp.sum((cum[None, :] <= blocks[:, None]).astype(I32), axis=-1), N_EXPERTS - 1)
    n_active = cum[-1:].astype(I32)
    experts = jnp.arange(N_EXPERTS, dtype=I32)
    is_block_expert = block_expert[:, None] == experts[None, :]
    first = jnp.sum(jnp.where(is_block_expert, ((cum - nblk)[None, :] == blocks[:, None]).astype(I32), 0), axis=-1)
    used = (nblk > 0).astype(I32)
    slot_e = (jnp.cumsum(used) - used) % 2
    later = (experts[None, :] > experts[:, None]) & (nblk[None, :] > 0)
    next_e = jnp.min(jnp.where(later, experts[None, :], N_EXPERTS), axis=-1)
    next_e = jnp.where(next_e == N_EXPERTS, -1, next_e)
    slot = jnp.sum(jnp.where(is_block_expert, slot_e[None, :], 0), axis=-1)
    nxt = jnp.sum(jnp.where(is_block_expert, next_e[None, :], 0), axis=-1)
    plan = (block_expert, first.astype(I32), slot.astype(I32), nxt.astype(I32), n_active)
    return dest.astype(I32), plan


def kernel(x, mem, norm_mix, w_in, lambda_q1, lambda_k1, lambda_q2, lambda_k2, subln, conv_w, w_branch, w_o,
           norm_cross, norm_mem, w_cq, w_ckv, w_co, norm_ffn, w_router_group, w_router_expert, w_exp_gate,
           w_exp_up, w_exp_down, norm_final):
    batch, seq, d = x.shape
    depth = w_in.shape[0]
    n_mem = mem.shape[1]
    t = batch * seq
    dh = d // (2 * DA_HEADS)
    assert 2 * dh == LANES and d % LANES == 0 and w_in.shape[2] == 8 * d
    assert w_router_group.shape[2] == N_GROUPS and w_router_expert.shape[2] == N_EXPERTS
    blk = min(EXPERT_BLOCK, seq)
    n_blocks = (t * TOP_K) // blk + N_EXPERTS
    parts = COMBINE_PARTS if t % (COMBINE_PARTS * SC_WORKERS * SC_CHUNK_ROWS) == 0 else 1
    tokens_per_worker = t // SC_WORKERS

    tabs = _rotary_tables(seq, dh)
    x2 = x.reshape(t, d)
    kv = _mem_kv(mem.reshape(batch * n_mem, d), norm_mem.reshape(1, d), w_ckv.astype(BF16))
    w_r = jnp.concatenate([w_router_group, w_router_expert,
                           jnp.zeros((depth, d, LANES - N_GROUPS - N_EXPERTS), F32)], axis=-1).astype(BF16)

    w_in_bf, w_branch_bf, w_o_bf = w_in.astype(BF16), w_branch.astype(BF16), w_o.astype(BF16)
    w_cq_bf, w_co_bf = w_cq.astype(BF16), w_co.astype(BF16)

    moe_out = None
    for l in range(depth):
        lambda_init = 0.8 - 0.6 * math.exp(-0.3 * l)
        g_mix = norm_mix[l].reshape(1, d)
        if moe_out is None:
            proj = _inproj(x2, g_mix, w_in_bf, l, tabs, seq, dh)
        else:
            proj = None
            for p in range(parts):
                x2, proj = _inproj_combine(x2, moe_out[0], moe_out[1][p], g_mix, w_in_bf, l, tabs, proj, seq, dh,
                                           p, parts)
        y_attn = _attention(proj, lambda_q1[l].reshape(1, dh), lambda_k1[l].reshape(1, dh),
                            lambda_q2[l].reshape(1, dh), lambda_k2[l].reshape(1, dh),
                            subln[l].reshape(1, LANES), batch, seq, d, lambda_init)
        x2 = _mixer_out(x2, y_attn, proj, conv_w, w_branch_bf, w_o_bf, l, seq)
        x2, h_ffn, logits = _cross_attention(x2, norm_cross[l].reshape(1, d), w_cq_bf, kv, l, w_co_bf,
                                             norm_ffn[l].reshape(1, d), w_r, seq, n_mem)
        ri, rw, counts = _router(logits, seq)
        dest, plan = _route_plan(ri, counts, blk, n_blocks)
        scatter_idx = dest.reshape(TOP_K, SC_WORKERS, tokens_per_worker // SC_CHUNK_ROWS, SC_CHUNK_ROWS)
        xb = _sc_scatter_rows(h_ffn, scatter_idx, n_blocks * blk)
        yb = _experts(plan, xb, w_exp_gate, w_exp_up, w_exp_down, l, blk)
        tp = t // parts
        moe_out = (rw, [_sc_gather_rows(yb, dest[:, p * tp:(p + 1) * tp].reshape(-1)) for p in range(parts)])
    for p in range(parts):
        x2 = _combine_final(x2, moe_out[0], moe_out[1][p], norm_final.reshape(1, d), seq, p, parts)
    return x2.reshape(batch, seq, d)
```

```python
import functools
import math

import jax
import jax.numpy as jnp
from jax import lax
from jax.experimental import pallas as pl
from jax.experimental.pallas import tpu as pltpu
from jax.experimental.pallas import tpu_sc as plsc

EPS = 1e-6
DA_HEADS = 8
MEM_HEADS = 4
N_GROUPS = 4
EXPERTS_PER_GROUP = 8
N_EXPERTS = N_GROUPS * EXPERTS_PER_GROUP
TOP_K = 2
ROPE_THETA = 500000.0
LANES = 128
F32_SUBLANES = 8
BF16_SUBLANES = 16
EXPERT_LANE0 = N_GROUPS
ROW_TILE = 512
ROUTER_TILE = 2048
Q_TILE = 2048
EXPERT_BLOCK = 256
PROJ_ROW_CHUNK = 1024
PROJ_COMBINE_TILE = 1024
ATTN_SCORE_CHUNK = 128
ATTN_SOFTMAX_CHUNK = 128
COMBINE_PARTS = 4
SC_CORES = 2
SC_WORKERS = 32
SC_CHUNK_ROWS = 64
VMEM_LIMIT = 56 * 1024 * 1024

F32 = jnp.float32
BF16 = jnp.bfloat16
I32 = jnp.int32
NT_DIMS = (((1,), (1,)), ((), ()))


def _rms(x, g):
    return x * lax.rsqrt(jnp.mean(x * x, axis=-1, keepdims=True) + EPS) * g


def _pack_rows(x):
    half = x.shape[1] // 2
    lo = lax.bitcast_convert_type(x[:, :half].astype(jnp.bfloat16).astype(F32), jnp.uint32)
    hi = lax.bitcast_convert_type(x[:, half:].astype(jnp.bfloat16).astype(F32), jnp.uint32)
    return lax.bitcast_convert_type((hi & jnp.uint32(0xFFFF0000)) | (lo >> 16), I32)


def _unpack_rows(w):
    u = lax.bitcast_convert_type(w, jnp.uint32)
    lo = lax.bitcast_convert_type(u << 16, F32)
    hi = lax.bitcast_convert_type(u & jnp.uint32(0xFFFF0000), F32)
    return jnp.concatenate([lo, hi], axis=1)


def _params(sem, vmem=VMEM_LIMIT):
    return pltpu.CompilerParams(dimension_semantics=sem, vmem_limit_bytes=vmem)


def _inproj_kernel(x_ref, g_ref, w_ref, cos_ref, s1_ref, s2_ref, o_ref, h_ref, *, rc, half, q_scale):
    @pl.when(pl.program_id(1) == 0)
    def _():
        h_ref[...] = _rms(x_ref[...], g_ref[...]).astype(BF16)

    _proj_column_block(h_ref, w_ref, cos_ref, s1_ref, s2_ref, o_ref, rc, half, q_scale)


def _inproj_combine_kernel(x_ref, rw_ref, y0_ref, y1_ref, g_ref, w_ref, cos_ref, s1_ref, s2_ref, *rest,
                           rc, half, q_scale):
    xnew_ref, o_ref, h_ref = rest[-3:]

    @pl.when(pl.program_id(1) == 0)
    def _():
        w = rw_ref[...]
        x_new = x_ref[...] + w[:, 0:1] * _unpack_rows(y0_ref[...]) + w[:, 1:2] * _unpack_rows(y1_ref[...])
        xnew_ref[...] = x_new
        h_ref[...] = _rms(x_new, g_ref[...]).astype(BF16)

    _proj_column_block(h_ref, w_ref, cos_ref, s1_ref, s2_ref, o_ref, rc, half, q_scale)


def _proj_column_block(h_ref, w_ref, cos_ref, s1_ref, s2_ref, o_ref, rc, half, q_scale):
    j = pl.program_id(1)
    tm, tn = o_ref.shape

    def run(rot, scale):
        def body(r, carry):
            r0 = pl.multiple_of(r * rc, rc)
            acc = jnp.dot(h_ref[pl.ds(r0, rc), :], w_ref[...], preferred_element_type=F32)
            if not rot:
                o_ref[pl.ds(r0, rc), :] = acc.astype(o_ref.dtype)
                return carry
            c = cos_ref[pl.ds(r0, rc), :]
            s1 = s1_ref[pl.ds(r0, rc), :]
            s2 = s2_ref[pl.ds(r0, rc), :]
            for cc in range(tn // LANES):
                a = acc[:, cc * LANES:(cc + 1) * LANES]
                a = a * c + pltpu.roll(a, half, 1) * s1 + pltpu.roll(a, LANES - half, 1) * s2
                if scale != 1.0:
                    a = a * scale
                o_ref[pl.ds(r0, rc), cc * LANES:(cc + 1) * LANES] = a.astype(o_ref.dtype)
            return carry
        lax.fori_loop(0, tm // rc, body, 0)

    @pl.when(j == 0)
    def _():
        run(True, q_scale)

    @pl.when(j == 1)
    def _():
        run(True, 1.0)

    @pl.when(j >= 2)
    def _():
        run(False, 1.0)


def _inproj(x2, g, w_bf, layer, tabs, seq, dh):
    t, d = x2.shape
    n = w_bf.shape[2]
    tm, tn = seq, d
    rc = min(PROJ_ROW_CHUNK, tm)
    cos_t, s1_t, s2_t = tabs
    tab_spec = pl.BlockSpec((seq, LANES), lambda i, j: (0, 0))
    return pl.pallas_call(
        functools.partial(_inproj_kernel, rc=rc, half=dh // 8, q_scale=dh ** -0.5 * math.log2(math.e)),
        grid=(t // tm, n // tn),
        in_specs=[pl.BlockSpec((tm, d), lambda i, j: (i, 0)),
                  pl.BlockSpec((1, d), lambda i, j: (0, 0)),
                  pl.BlockSpec((None, d, tn), lambda i, j: (layer, 0, j)),
                  tab_spec, tab_spec, tab_spec],
        out_specs=pl.BlockSpec((tm, tn), lambda i, j: (i, j)),
        out_shape=jax.ShapeDtypeStruct((t, n), BF16),
        scratch_shapes=[pltpu.VMEM((tm, d), BF16)],
        compiler_params=_params(("parallel", "arbitrary")),
        name="inproj",
    )(x2, g, w_bf, cos_t, s1_t, s2_t)


def _inproj_combine(x2, rw, y2, g, w_bf, layer, tabs, proj_prev, seq, dh, part, n_parts):
    t, d = x2.shape
    n = w_bf.shape[2]
    tm, tn = min(PROJ_COMBINE_TILE, seq), d
    ntp = t // n_parts // tm
    row0 = part * ntp
    tiles_per_seq = seq // tm
    cos_t, s1_t, s2_t = tabs
    rows = lambda w: pl.BlockSpec((tm, w), lambda i, j: (row0 + i, 0))
    tab_spec = pl.BlockSpec((tm, LANES), lambda i, j: ((row0 + i) % tiles_per_seq, 0))
    in_specs = [rows(d), rows(LANES),
                pl.BlockSpec((tm, d // 2), lambda i, j: (i, 0)),
                pl.BlockSpec((tm, d // 2), lambda i, j: (ntp + i, 0)),
                pl.BlockSpec((1, d), lambda i, j: (0, 0)),
                pl.BlockSpec((None, d, tn), lambda i, j: (layer, 0, j)),
                tab_spec, tab_spec, tab_spec]
    args = [x2, rw, y2, y2, g, w_bf, cos_t, s1_t, s2_t]
    aliases = {0: 0}
    if proj_prev is not None:
        in_specs.append(pl.BlockSpec(memory_space=pl.ANY))
        args.append(proj_prev)
        aliases[len(args) - 1] = 1
    return pl.pallas_call(
        functools.partial(_inproj_combine_kernel, rc=min(PROJ_ROW_CHUNK, tm), half=dh // 8,
                          q_scale=dh ** -0.5 * math.log2(math.e)),
        grid=(ntp, n // tn),
        in_specs=in_specs,
        out_specs=[rows(d), pl.BlockSpec((tm, tn), lambda i, j: (row0 + i, j))],
        out_shape=[jax.ShapeDtypeStruct((t, d), F32), jax.ShapeDtypeStruct((t, n), BF16)],
        scratch_shapes=[pltpu.VMEM((tm, d), BF16)],
        input_output_aliases=aliases,
        compiler_params=_params(("parallel", "arbitrary")),
        name="inproj_combine",
    )(*args)


def _rotary_tables(seq, dh):
    rot = dh // 4
    half = rot // 2
    inv = jnp.float32(ROPE_THETA) ** (-jnp.arange(0, rot, 2, dtype=F32) / rot)
    ang = jnp.arange(seq, dtype=F32)[:, None] * inv[None, :]
    cos, sin = jnp.cos(ang), jnp.sin(ang)
    lane = jnp.arange(LANES) % dh
    idx = lane % half
    c_t = jnp.where(lane < rot, cos[:, idx], 1.0)
    s1_t = jnp.where((lane >= half) & (lane < rot), sin[:, idx], 0.0)
    s2_t = jnp.where(lane < half, -sin[:, idx], 0.0)
    return c_t.astype(F32), s1_t.astype(F32), s2_t.astype(F32)


def _attn_kernel(q_ref, k_ref, v_ref, lq1_ref, lk1_ref, lq2_ref, lk2_ref, sub_ref, o_ref,
                 s_ref, p_ref, kt_ref, *, dh, lambda_init, rc, rp):
    tq = q_ref.shape[0]
    n_chunks = tq // rc
    lane = lax.broadcasted_iota(I32, (1, LANES), 1)
    m1 = jnp.where(lane < dh, 1.0, 0.0).astype(BF16)
    m2 = jnp.where(lane >= dh, 1.0, 0.0).astype(BF16)
    lam = (jnp.exp(jnp.sum(lq1_ref[...] * lk1_ref[...], axis=-1, keepdims=True))
           - jnp.exp(jnp.sum(lq2_ref[...] * lk2_ref[...], axis=-1, keepdims=True)) + lambda_init)

    kt_ref[...] = k_ref[...].T

    def scores(c):
        qc = q_ref[c * rc:(c + 1) * rc, :]
        qq = jnp.concatenate([qc * m1, qc * m2], axis=0)
        s_ref[c % 2] = jnp.dot(qq, kt_ref[...], preferred_element_type=F32)

    def softmax_pv(c, j):
        r0 = j * rp
        s1 = s_ref[c % 2, r0:r0 + rp, :]
        s2 = s_ref[c % 2, rc + r0:rc + r0 + rp, :]
        e1 = jnp.exp2(s1 - jnp.max(s1, axis=-1, keepdims=True))
        e2 = jnp.exp2(s2 - jnp.max(s2, axis=-1, keepdims=True))
        l1 = jnp.sum(e1, axis=-1, keepdims=True)
        l2 = jnp.sum(e2, axis=-1, keepdims=True)
        slot = (c * (rc // rp) + j) % 2
        p_ref[slot] = (e1 - e2 * (lam * l1 / l2)).astype(BF16)
        od = jnp.dot(p_ref[slot], v_ref[...], preferred_element_type=F32) / l1
        od = _rms(od, sub_ref[...]) * (1.0 - lambda_init)
        o_ref[c * rc + r0:c * rc + r0 + rp, :] = od.astype(o_ref.dtype)

    scores(0)
    for c in range(n_chunks):
        if c + 1 < n_chunks:
            scores(c + 1)
        for j in range(rc // rp):
            softmax_pv(c, j)


def _attention(proj, lq1, lk1, lq2, lk2, sub, batch, seq, d, lambda_init):
    t = proj.shape[0]
    nh = d // LANES
    dh = LANES // 2
    tq = min(Q_TILE, seq)
    nq = seq // tq
    rc = min(ATTN_SCORE_CHUNK, tq)
    rp = min(ATTN_SOFTMAX_CHUNK, rc)
    vec = pl.BlockSpec((1, dh), lambda b, h, qi: (0, 0))
    return pl.pallas_call(
        functools.partial(_attn_kernel, dh=dh, lambda_init=lambda_init, rc=rc, rp=rp),
        grid=(batch, nh, nq),
        scratch_shapes=[pltpu.VMEM((2, 2 * rc, seq), F32), pltpu.VMEM((2, rp, seq), BF16),
                        pltpu.VMEM((LANES, seq), BF16)],
        in_specs=[pl.BlockSpec((tq, LANES), lambda b, h, qi: (b * nq + qi, h)),
                  pl.BlockSpec((seq, LANES), lambda b, h, qi: (b, nh + h)),
                  pl.BlockSpec((seq, LANES), lambda b, h, qi: (b, 2 * nh + h)),
                  vec, vec, vec, vec,
                  pl.BlockSpec((1, LANES), lambda b, h, qi: (0, 0))],
        out_specs=pl.BlockSpec((tq, LANES), lambda b, h, qi: (b * nq + qi, h)),
        out_shape=jax.ShapeDtypeStruct((t, d), BF16),
        compiler_params=_params(("parallel", "parallel", "arbitrary")),
        name="diff_attn",
    )(proj, proj, proj, lq1, lk1, lq2, lk2, sub)


def _mixout_kernel(x_ref, ya_ref, cb_ref, cc_ref, cx_ref, g0_ref, g1_ref, pc_ref, px_ref, nc_ref, nx_ref,
                   cw_ref, wb0_ref, wb1_ref, wo_ref, o_ref, *, tiles_per_seq):
    i = pl.program_id(0)
    tm = x_ref.shape[0]
    u = cc_ref[...].astype(F32) * cx_ref[...].astype(F32)
    hr = pc_ref.shape[0]
    u_prev = pc_ref[hr - 1:hr, :].astype(F32) * px_ref[hr - 1:hr, :].astype(F32)
    u_next = nc_ref[0:1, :].astype(F32) * nx_ref[0:1, :].astype(F32)
    pos = i % tiles_per_seq
    u_prev = jnp.where(pos == 0, 0.0, u_prev)
    u_next = jnp.where(pos == tiles_per_seq - 1, 0.0, u_next)
    row = lax.broadcasted_iota(I32, u.shape, 0)
    u_up = jnp.where(row == 0, u_prev, pltpu.roll(u, 1, 0))
    u_dn = jnp.where(row == tm - 1, u_next, pltpu.roll(u, tm - 1, 0))
    conv = cw_ref[0:1, :] * u_up + cw_ref[1:2, :] * u + cw_ref[2:3, :] * u_dn
    y_conv = (cb_ref[...].astype(F32) * conv).astype(BF16)
    br0 = jnp.dot(ya_ref[...], wb0_ref[...], preferred_element_type=F32)
    br1 = jnp.dot(y_conv, wb1_ref[...], preferred_element_type=F32)
    z = jax.nn.sigmoid(g0_ref[...].astype(F32)) * br0 + jax.nn.sigmoid(g1_ref[...].astype(F32)) * br1
    o_ref[...] = x_ref[...] + jnp.dot(z.astype(BF16), wo_ref[...], preferred_element_type=F32)


def _mixer_out(x2, y_attn, proj, conv_w, w_branch, w_o, layer, seq):
    t, d = x2.shape
    tm = min(ROW_TILE, seq)
    tps = seq // tm
    hr = BF16_SUBLANES
    nhb = t // hr
    row = lambda c: pl.BlockSpec((tm, d), lambda i: (i, c))
    prev = lambda c: pl.BlockSpec((hr, d), lambda i: (jnp.maximum(i * (tm // hr) - 1, 0), c))
    nxt = lambda c: pl.BlockSpec((hr, d), lambda i: (jnp.minimum((i + 1) * (tm // hr), nhb - 1), c))
    branch = lambda b: pl.BlockSpec((None, None, d, d), lambda i: (layer, b, 0, 0))
    return pl.pallas_call(
        functools.partial(_mixout_kernel, tiles_per_seq=tps),
        grid=(t // tm,),
        in_specs=[row(0), row(0), row(3), row(4), row(5), row(6), row(7),
                  prev(4), prev(5), nxt(4), nxt(5),
                  pl.BlockSpec((None,) + conv_w.shape[1:], lambda i: (layer, 0, 0)),
                  branch(0), branch(1), pl.BlockSpec((None, d, d), lambda i: (layer, 0, 0))],
        out_specs=row(0),
        out_shape=jax.ShapeDtypeStruct((t, d), F32),
        compiler_params=_params(("parallel",)),
        name="mixer_out",
    )(x2, y_attn, proj, proj, proj, proj, proj, proj, proj, proj, proj, conv_w, w_branch, w_branch, w_o)


def _memkv_kernel(m_ref, g_ref, w_ref, o_ref):
    h = _rms(m_ref[...], g_ref[...]).astype(BF16)
    o_ref[...] = jnp.dot(h, w_ref[...], preferred_element_type=F32).astype(o_ref.dtype)


def _mem_kv(mem2, g, w_bf):
    depth, d, n = w_bf.shape
    rows = mem2.shape[0]
    tn = min(n, 1024)
    return pl.pallas_call(
        _memkv_kernel,
        grid=(depth, n // tn),
        in_specs=[pl.BlockSpec((rows, d), lambda l, j: (0, 0)),
                  pl.BlockSpec((1, d), lambda l, j: (0, 0)),
                  pl.BlockSpec((None, d, tn), lambda l, j: (l, 0, j))],
        out_specs=pl.BlockSpec((None, rows, tn), lambda l, j: (l, 0, j)),
        out_shape=jax.ShapeDtypeStruct((depth, rows, n), BF16),
        compiler_params=_params(("parallel", "parallel")),
        name="mem_kv",
    )(mem2, g, w_bf)


def _cross_kernel(x_ref, g_ref, wq_ref, kv_ref, wo_ref, gf_ref, wr_ref, o_ref, hf_ref, lg_ref, *, heads):
    x = x_ref[...]
    d = x.shape[1]
    hd = d // heads
    h = _rms(x, g_ref[...]).astype(BF16)
    q = (jnp.dot(h, wq_ref[...], preferred_element_type=F32) * (hd ** -0.5)).astype(BF16)
    outs = []
    for hh in range(heads):
        qh = q[:, hh * hd:(hh + 1) * hd]
        kh = kv_ref[:, hh * hd:(hh + 1) * hd]
        vh = kv_ref[:, d + hh * hd:d + (hh + 1) * hd]
        s = lax.dot_general(qh, kh, NT_DIMS, preferred_element_type=F32)
        p = jnp.exp(s - jnp.max(s, axis=-1, keepdims=True))
        l = jnp.sum(p, axis=-1, keepdims=True)
        outs.append((jnp.dot(p.astype(BF16), vh, preferred_element_type=F32) / l).astype(BF16))
    o = jnp.concatenate(outs, axis=1)
    x_new = x + jnp.dot(o, wo_ref[...], preferred_element_type=F32)
    o_ref[...] = x_new
    h_ffn = _rms(x_new, gf_ref[...])
    hf_ref[...] = _pack_rows(h_ffn)
    lg_ref[...] = jnp.dot(h_ffn.astype(BF16), wr_ref[...], preferred_element_type=F32)


def _cross_attention(x2, g, wq, kv, layer, wo, g_ffn, w_r, seq, n_mem):
    t, d = x2.shape
    tm = min(ROW_TILE, seq)
    tps = seq // tm
    row = lambda w: pl.BlockSpec((tm, w), lambda i: (i, 0))
    const = lambda r, c: pl.BlockSpec((r, c), lambda i: (0, 0))
    of_layer = lambda r, c: pl.BlockSpec((None, r, c), lambda i: (layer, 0, 0))
    return pl.pallas_call(
        functools.partial(_cross_kernel, heads=MEM_HEADS),
        grid=(t // tm,),
        in_specs=[row(d), const(1, d), of_layer(d, d),
                  pl.BlockSpec((None, n_mem, 2 * d), lambda i: (layer, i // tps, 0)),
                  of_layer(d, d), const(1, d), of_layer(d, LANES)],
        out_specs=[row(d), row(d // 2), row(LANES)],
        out_shape=[jax.ShapeDtypeStruct((t, d), F32), jax.ShapeDtypeStruct((t, d // 2), I32),
                   jax.ShapeDtypeStruct((t, LANES), F32)],
        compiler_params=_params(("parallel",)),
        name="cross_attn",
    )(x2, g, wq, kv, wo, g_ffn, w_r)


def _router_kernel(lg_ref, ri_ref, rw_ref, cnt_ref, carry_ref, *, sub):
    @pl.when(pl.program_id(0) == 0)
    def _():
        carry_ref[...] = jnp.zeros_like(carry_ref)

    lg = lg_ref[...]
    tm = lg.shape[0]
    lane = lax.broadcasted_iota(I32, lg.shape, 1).astype(F32)
    neg = -jnp.inf

    def first_argmax(v):
        m = jnp.max(v, axis=-1, keepdims=True)
        return m, jnp.min(jnp.where(v == m, lane, float(LANES)), axis=-1, keepdims=True)

    gmask = lane < N_GROUPS
    gl = jnp.where(gmask, lg, neg)
    gmax, gidx = first_argmax(gl)
    g_w = 1.0 / jnp.sum(jnp.where(gmask, jnp.exp(gl - gmax), 0.0), axis=-1, keepdims=True)

    lo = EXPERT_LANE0 + gidx * EXPERTS_PER_GROUP
    emask = (lane >= lo) & (lane < lo + EXPERTS_PER_GROUP)
    el = jnp.where(emask, lg, neg)
    mx1, i1 = first_argmax(el)
    mx2, i2 = first_argmax(jnp.where(lane == i1, neg, el))
    esum = jnp.sum(jnp.where(emask, jnp.exp(el - mx1), 0.0), axis=-1, keepdims=True)
    p1 = 1.0 / esum
    p2 = jnp.exp(mx2 - mx1) / esum
    den = p1 + p2
    w1 = g_w * (p1 / den)
    w2 = g_w * (p2 / den)

    memb = jnp.where((lane == i1) | (lane == i2), 1.0, 0.0)
    r_i = lax.broadcasted_iota(I32, (sub, sub), 0)
    c_i = lax.broadcasted_iota(I32, (sub, sub), 1)
    ltri = jnp.where(c_i < r_i, 1.0, 0.0).astype(BF16)
    tots = []
    carry = carry_ref[...]
    for r0 in range(0, tm, sub):
        mb = memb[r0:r0 + sub]
        tots.append(jnp.dot(ltri, mb.astype(BF16), preferred_element_type=F32) + carry)
        carry = carry + jnp.sum(mb, axis=0, keepdims=True)
    carry_ref[...] = carry
    tot = jnp.concatenate(tots, axis=0)
    rank1 = jnp.sum(jnp.where(lane == i1, tot, 0.0), axis=-1, keepdims=True).astype(I32)
    rank2 = jnp.sum(jnp.where(lane == i2, tot, 0.0), axis=-1, keepdims=True).astype(I32)

    e1 = (i1 - EXPERT_LANE0).astype(I32)
    e2 = (i2 - EXPERT_LANE0).astype(I32)
    ri = jnp.where(lane == 0, e1, jnp.where(lane == 1, e2, jnp.where(lane == 2, rank1, jnp.where(lane == 3, rank2, 0))))
    ri_ref[...] = ri.T[:F32_SUBLANES]
    rw_ref[...] = jnp.where(lane == 0, w1, jnp.where(lane == 1, w2, 0.0))
    cnt_ref[...] = jnp.broadcast_to(carry, cnt_ref.shape)


def _router(lg, seq):
    t = lg.shape[0]
    tm = min(ROUTER_TILE, seq)
    row = pl.BlockSpec((tm, LANES), lambda i: (i, 0))
    return pl.pallas_call(
        functools.partial(_router_kernel, sub=min(ROW_TILE, tm)),
        grid=(t // tm,),
        in_specs=[row],
        out_specs=[pl.BlockSpec((F32_SUBLANES, tm), lambda i: (0, i)), row,
                   pl.BlockSpec((F32_SUBLANES, LANES), lambda i: (0, 0))],
        out_shape=[jax.ShapeDtypeStruct((F32_SUBLANES, t), I32),
                   jax.ShapeDtypeStruct((t, LANES), F32),
                   jax.ShapeDtypeStruct((F32_SUBLANES, LANES), F32)],
        scratch_shapes=[pltpu.VMEM((1, LANES), F32)],
        compiler_params=_params(("arbitrary",)),
        name="moe_router",
    )(lg)


def _expert_kernel(be_ref, first_ref, slot_ref, nxt_ref, na_ref, xb_ref, wg_hbm, wu_hbm, wd_hbm, yb_ref,
                   wg_buf, wu_buf, wd_buf, wgb_ref, wub_ref, wdb_ref, sem, *, layer):
    i = pl.program_id(0)

    def fetch(e, slot):
        return (pltpu.make_async_copy(wg_hbm.at[layer, e], wg_buf.at[slot], sem.at[slot, 0]),
                pltpu.make_async_copy(wu_hbm.at[layer, e], wu_buf.at[slot], sem.at[slot, 1]),
                pltpu.make_async_copy(wd_hbm.at[layer, e], wd_buf.at[slot], sem.at[slot, 2]))

    @pl.when(i < na_ref[0])
    def _():
        @pl.when(first_ref[i] == 1)
        def _():
            e, slot = be_ref[i], slot_ref[i]

            @pl.when(i == 0)
            def _():
                for cp in fetch(e, slot):
                    cp.start()

            for cp in fetch(e, slot):
                cp.wait()
            wgb_ref[...] = wg_buf[slot].astype(BF16)
            wub_ref[...] = wu_buf[slot].astype(BF16)
            wdb_ref[...] = wd_buf[slot].astype(BF16)

            @pl.when(nxt_ref[i] >= 0)
            def _():
                for cp in fetch(nxt_ref[i], 1 - slot):
                    cp.start()

        xb = _unpack_rows(xb_ref[...]).astype(BF16)
        gate = jnp.dot(xb, wgb_ref[...], preferred_element_type=F32)
        up = jnp.dot(xb, wub_ref[...], preferred_element_type=F32)
        act = (jax.nn.silu(gate) * up).astype(BF16)
        yb_ref[...] = _pack_rows(jnp.dot(act, wdb_ref[...], preferred_element_type=F32))

    @pl.when(i >= na_ref[0])
    def _():
        yb_ref[...] = jnp.zeros_like(yb_ref)


def _experts(plan, xb, w_gate, w_up, w_down, layer, blk):
    rows = xb.shape[0]
    d, de = w_gate.shape[-2:]
    nb = rows // blk

    def row_map(i, be, first, slot, nxt, na):
        return (jnp.minimum(i, jnp.maximum(na[0] - 1, 0)), 0)

    hbm = pl.BlockSpec(memory_space=pl.ANY)
    return pl.pallas_call(
        functools.partial(_expert_kernel, layer=layer),
        grid_spec=pltpu.PrefetchScalarGridSpec(
            num_scalar_prefetch=5,
            grid=(nb,),
            in_specs=[pl.BlockSpec((blk, d // 2), row_map), hbm, hbm, hbm],
            out_specs=pl.BlockSpec((blk, d // 2), lambda i, *_: (i, 0)),
            scratch_shapes=[pltpu.VMEM((2, d, de), F32), pltpu.VMEM((2, d, de), F32), pltpu.VMEM((2, de, d), F32),
                            pltpu.VMEM((d, de), BF16), pltpu.VMEM((d, de), BF16), pltpu.VMEM((de, d), BF16),
                            pltpu.SemaphoreType.DMA((2, 3))]),
        out_shape=jax.ShapeDtypeStruct((rows, d // 2), I32),
        compiler_params=_params(("arbitrary",)),
        name="moe_experts",
    )(*plan, xb, w_gate, w_up, w_down)


def _sc_gather_rows(table, idx):
    b = idx.shape[0]
    d = table.shape[1]
    per_w = b // SC_WORKERS
    ch = SC_CHUNK_ROWS
    mesh = plsc.VectorSubcoreMesh(core_axis_name="c", subcore_axis_name="s")

    n_ch = per_w // ch
    assert per_w * SC_WORKERS == b and n_ch * ch == per_w and n_ch % 2 == 0

    def body(table_hbm, idx_hbm, out_hbm, idx_v, rows_v, gsem, wsem):
        wid = lax.axis_index("s") * SC_CORES + lax.axis_index("c")
        base = wid * per_w
        pltpu.sync_copy(idx_hbm.at[pl.ds(base, per_w)], idx_v)

        def write_back(c, buf):
            off = pl.multiple_of(c * ch, ch)
            return pltpu.make_async_copy(rows_v.at[buf], out_hbm.at[pl.ds(base + off, ch)], wsem.at[buf])

        @pl.loop(0, n_ch, step=2)
        def _(c0):
            for buf in range(2):
                c = c0 + buf

                @pl.when(c0 > 0)
                def _():
                    write_back(c - 2, buf).wait()

                off = pl.multiple_of(c * ch, ch)
                pltpu.async_copy(table_hbm.at[idx_v.at[pl.ds(off, ch)]], rows_v.at[buf], gsem).wait()
                write_back(c, buf).start()

        for buf in range(2):
            write_back(n_ch - 2 + buf, buf).wait()

    return pl.kernel(body, out_type=jax.ShapeDtypeStruct((b, d), table.dtype), mesh=mesh,
                     scratch_types=[pltpu.VMEM((per_w,), I32), pltpu.VMEM((2, ch, d), table.dtype),
                                    pltpu.SemaphoreType.DMA, pltpu.SemaphoreType.DMA((2,))],
                     name="sc_gather")(table, idx)


def _sc_scatter_rows(rows, idx, n_out):
    t, d = rows.shape
    top_k, _, n_ch, ch = idx.shape
    per_w = n_ch * ch
    assert per_w * SC_WORKERS == t and idx.shape[1] == SC_WORKERS
    mesh = plsc.VectorSubcoreMesh(core_axis_name="c", subcore_axis_name="s")

    assert n_ch % 2 == 0

    def body(rows_hbm, idx_hbm, out_hbm, idx_v, rows_v, lsem, ssem):
        wid = lax.axis_index("s") * SC_CORES + lax.axis_index("c")
        base = wid * per_w
        for k in range(top_k):
            pltpu.sync_copy(idx_hbm.at[k, wid], idx_v.at[k])

        def load(c, buf):
            off = pl.multiple_of(c * ch, ch)
            return pltpu.make_async_copy(rows_hbm.at[pl.ds(base + off, ch)], rows_v.at[buf], lsem.at[buf])

        load(0, 0).start()

        @pl.loop(0, n_ch, step=2)
        def _(c0):
            for buf in range(2):
                c = c0 + buf
                load(c, buf).wait()

                @pl.when(c + 1 < n_ch)
                def _():
                    load(c + 1, 1 - buf).start()

                scatters = [pltpu.async_copy(rows_v.at[buf], out_hbm.at[idx_v.at[k, c]], ssem)
                            for k in range(top_k)]
                for cp in scatters:
                    cp.wait()

    return pl.kernel(body, out_type=jax.ShapeDtypeStruct((n_out, d), rows.dtype), mesh=mesh,
                     scratch_types=[pltpu.VMEM((top_k, n_ch, ch), I32), pltpu.VMEM((2, ch, d), rows.dtype),
                                    pltpu.SemaphoreType.DMA((2,)), pltpu.SemaphoreType.DMA],
                     name="sc_scatter")(rows, idx)


def _combine_final_kernel(x_ref, rw_ref, y0_ref, y1_ref, gf_ref, o_ref):
    w = rw_ref[...]
    out = x_ref[...] + w[:, 0:1] * _unpack_rows(y0_ref[...]) + w[:, 1:2] * _unpack_rows(y1_ref[...])
    o_ref[...] = _rms(out, gf_ref[...])


def _combine_final(x2, rw, y2, g_final, seq, part, n_parts):
    t, d = x2.shape
    tm = min(ROW_TILE, seq)
    nt = t // tm // n_parts
    row = lambda w: pl.BlockSpec((tm, w), lambda i: (part * nt + i, 0))
    return pl.pallas_call(
        _combine_final_kernel,
        grid=(nt,),
        in_specs=[row(d), row(LANES), pl.BlockSpec((tm, d // 2), lambda i: (i, 0)),
                  pl.BlockSpec((tm, d // 2), lambda i: (nt + i, 0)), pl.BlockSpec((1, d), lambda i: (0, 0))],
        out_specs=row(d),
        out_shape=jax.ShapeDtypeStruct((t, d), F32),
        input_output_aliases={0: 0},
        compiler_params=_params(("parallel",)),
        name="moe_combine_final",
    )(x2, rw, y2, y2, g_final)


def _route_plan(ri, counts_f, blk, n_blocks):
    cnt = counts_f[0, EXPERT_LANE0:EXPERT_LANE0 + N_EXPERTS].astype(I32)
    nblk = (cnt + blk - 1) // blk
    cum = jnp.cumsum(nblk)
    pad_start = (cum - nblk) * blk
    e = ri[0:TOP_K]
    rank = ri[TOP_K:2 * TOP_K]
    onehot = e[:, :, None] == jnp.arange(N_EXPERTS, dtype=I32)[None, None, :]
    dest = rank + jnp.sum(jnp.where(onehot, pad_start[None, None, :], 0), axis=-1)
    blocks = jnp.arange(n_blocks, dtype=I32)
    block_expert = jnp.minimum(jnp.sum((cum[None, :] <= blocks[:, None]).astype(I32), axis=-1), N_EXPERTS - 1)
    n_active = cum[-1:].astype(I32)
    experts = jnp.arange(N_EXPERTS, dtype=I32)
    is_block_expert = block_expert[:, None] == experts[None, :]
    first = jnp.sum(jnp.where(is_block_expert, ((cum - nblk)[None, :] == blocks[:, None]).astype(I32), 0), axis=-1)
    used = (nblk > 0).astype(I32)
    slot_e = (jnp.cumsum(used) - used) % 2
    later = (experts[None, :] > experts[:, None]) & (nblk[None, :] > 0)
    next_e = jnp.min(jnp.where(later, experts[None, :], N_EXPERTS), axis=-1)
    next_e = jnp.where(next_e == N_EXPERTS, -1, next_e)
    slot = jnp.sum(jnp.where(is_block_expert, slot_e[None, :], 0), axis=-1)
    nxt = jnp.sum(jnp.where(is_block_expert, next_e[None, :], 0), axis=-1)
    plan = (block_expert, first.astype(I32), slot.astype(I32), nxt.astype(I32), n_active)
    return dest.astype(I32), plan


def kernel(x, mem, norm_mix, w_in, lambda_q1, lambda_k1, lambda_q2, lambda_k2, subln, conv_w, w_branch, w_o,
           norm_cross, norm_mem, w_cq, w_ckv, w_co, norm_ffn, w_router_group, w_router_expert, w_exp_gate,
           w_exp_up, w_exp_down, norm_final):
    batch, seq, d = x.shape
    depth = w_in.shape[0]
    n_mem = mem.shape[1]
    t = batch * seq
    dh = d // (2 * DA_HEADS)
    assert 2 * dh == LANES and d % LANES == 0 and w_in.shape[2] == 8 * d
    assert w_router_group.shape[2] == N_GROUPS and w_router_expert.shape[2] == N_EXPERTS
    blk = min(EXPERT_BLOCK, seq)
    n_blocks = (t * TOP_K) // blk + N_EXPERTS
    parts = COMBINE_PARTS if t % (COMBINE_PARTS * SC_WORKERS * SC_CHUNK_ROWS) == 0 else 1
    tokens_per_worker = t // SC_WORKERS

    tabs = _rotary_tables(seq, dh)
    x2 = x.reshape(t, d)
    kv = _mem_kv(mem.reshape(batch * n_mem, d), norm_mem.reshape(1, d), w_ckv.astype(BF16))
    w_r = jnp.concatenate([w_router_group, w_router_expert,
                           jnp.zeros((depth, d, LANES - N_GROUPS - N_EXPERTS), F32)], axis=-1).astype(BF16)

    w_in_bf, w_branch_bf, w_o_bf = w_in.astype(BF16), w_branch.astype(BF16), w_o.astype(BF16)
    w_cq_bf, w_co_bf = w_cq.astype(BF16), w_co.astype(BF16)

    moe_out = None
    for l in range(depth):
        lambda_init = 0.8 - 0.6 * math.exp(-0.3 * l)
        g_mix = norm_mix[l].reshape(1, d)
        if moe_out is None:
            proj = _inproj(x2, g_mix, w_in_bf, l, tabs, seq, dh)
        else:
            proj = None
            for p in range(parts):
                x2, proj = _inproj_combine(x2, moe_out[0], moe_out[1][p], g_mix, w_in_bf, l, tabs, proj, seq, dh,
                                           p, parts)
        y_attn = _attention(proj, lambda_q1[l].reshape(1, dh), lambda_k1[l].reshape(1, dh),
                            lambda_q2[l].reshape(1, dh), lambda_k2[l].reshape(1, dh),
                            subln[l].reshape(1, LANES), batch, seq, d, lambda_init)
        x2 = _mixer_out(x2, y_attn, proj, conv_w, w_branch_bf, w_o_bf, l, seq)
        x2, h_ffn, logits = _cross_attention(x2, norm_cross[l].reshape(1, d), w_cq_bf, kv, l, w_co_bf,
                                             norm_ffn[l].reshape(1, d), w_r, seq, n_mem)
        ri, rw, counts = _router(logits, seq)
        dest, plan = _route_plan(ri, counts, blk, n_blocks)
        scatter_idx = dest.reshape(TOP_K, SC_WORKERS, tokens_per_worker // SC_CHUNK_ROWS, SC_CHUNK_ROWS)
        xb = _sc_scatter_rows(h_ffn, scatter_idx, n_blocks * blk)
        yb = _experts(plan, xb, w_exp_gate, w_exp_up, w_exp_down, l, blk)
        tp = t // parts
        moe_out = (rw, [_sc_gather_rows(yb, dest[:, p * tp:(p + 1) * tp].reshape(-1)) for p in range(parts)])
    for p in range(parts):
        x2 = _combine_final(x2, moe_out[0], moe_out[1][p], norm_final.reshape(1, d), seq, p, parts)
    return x2.reshape(batch, seq, d)
```

```python
import functools
import math

import jax
import jax.numpy as jnp
from jax import lax
from jax.experimental import pallas as pl
from jax.experimental.pallas import tpu as pltpu
from jax.experimental.pallas import tpu_sc as plsc

EPS = 1e-6
DA_HEADS = 8
MEM_HEADS = 4
N_GROUPS = 4
EXPERTS_PER_GROUP = 8
N_EXPERTS = N_GROUPS * EXPERTS_PER_GROUP
TOP_K = 2
ROPE_THETA = 500000.0
LANES = 128
F32_SUBLANES = 8
BF16_SUBLANES = 16
EXPERT_LANE0 = N_GROUPS
ROW_TILE = 512
ROUTER_TILE = 2048
Q_TILE = 2048
EXPERT_BLOCK = 256
PROJ_ROW_CHUNK = 1024
PROJ_COMBINE_TILE = 2048
ATTN_SCORE_CHUNK = 128
ATTN_SOFTMAX_CHUNK = 128
COMBINE_PARTS = 4
SC_CORES = 2
SC_WORKERS = 32
SC_CHUNK_ROWS = 64
VMEM_LIMIT = 56 * 1024 * 1024

F32 = jnp.float32
BF16 = jnp.bfloat16
I32 = jnp.int32
NT_DIMS = (((1,), (1,)), ((), ()))


def _rms(x, g):
    return x * lax.rsqrt(jnp.mean(x * x, axis=-1, keepdims=True) + EPS) * g


def _pack_rows(x):
    half = x.shape[1] // 2
    lo = lax.bitcast_convert_type(x[:, :half].astype(jnp.bfloat16).astype(F32), jnp.uint32)
    hi = lax.bitcast_convert_type(x[:, half:].astype(jnp.bfloat16).astype(F32), jnp.uint32)
    return lax.bitcast_convert_type((hi & jnp.uint32(0xFFFF0000)) | (lo >> 16), I32)


def _unpack_rows(w):
    u = lax.bitcast_convert_type(w, jnp.uint32)
    lo = lax.bitcast_convert_type(u << 16, F32)
    hi = lax.bitcast_convert_type(u & jnp.uint32(0xFFFF0000), F32)
    return jnp.concatenate([lo, hi], axis=1)


def _params(sem, vmem=VMEM_LIMIT):
    return pltpu.CompilerParams(dimension_semantics=sem, vmem_limit_bytes=vmem)


def _inproj_kernel(x_ref, g_ref, w_ref, cos_ref, s1_ref, s2_ref, o_ref, h_ref, *, rc, half, q_scale):
    @pl.when(pl.program_id(1) == 0)
    def _():
        h_ref[...] = _rms(x_ref[...], g_ref[...]).astype(BF16)

    _proj_column_block(h_ref, w_ref, cos_ref, s1_ref, s2_ref, o_ref, rc, half, q_scale)


def _inproj_combine_kernel(x_ref, rw_ref, y0_ref, y1_ref, g_ref, w_ref, cos_ref, s1_ref, s2_ref, *rest,
                           rc, half, q_scale):
    xnew_ref, o_ref, h_ref = rest[-3:]

    @pl.when(pl.program_id(1) == 0)
    def _():
        w = rw_ref[...]
        x_new = x_ref[...] + w[:, 0:1] * _unpack_rows(y0_ref[...]) + w[:, 1:2] * _unpack_rows(y1_ref[...])
        xnew_ref[...] = x_new
        h_ref[...] = _rms(x_new, g_ref[...]).astype(BF16)

    _proj_column_block(h_ref, w_ref, cos_ref, s1_ref, s2_ref, o_ref, rc, half, q_scale)


def _proj_column_block(h_ref, w_ref, cos_ref, s1_ref, s2_ref, o_ref, rc, half, q_scale):
    j = pl.program_id(1)
    tm, tn = o_ref.shape

    def run(rot, scale):
        def body(r, carry):
            r0 = pl.multiple_of(r * rc, rc)
            acc = jnp.dot(h_ref[pl.ds(r0, rc), :], w_ref[...], preferred_element_type=F32)
            if not rot:
                o_ref[pl.ds(r0, rc), :] = acc.astype(o_ref.dtype)
                return carry
            c = cos_ref[pl.ds(r0, rc), :]
            s1 = s1_ref[pl.ds(r0, rc), :]
            s2 = s2_ref[pl.ds(r0, rc), :]
            for cc in range(tn // LANES):
                a = acc[:, cc * LANES:(cc + 1) * LANES]
                a = a * c + pltpu.roll(a, half, 1) * s1 + pltpu.roll(a, LANES - half, 1) * s2
                if scale != 1.0:
                    a = a * scale
                o_ref[pl.ds(r0, rc), cc * LANES:(cc + 1) * LANES] = a.astype(o_ref.dtype)
            return carry
        lax.fori_loop(0, tm // rc, body, 0)

    @pl.when(j == 0)
    def _():
        run(True, q_scale)

    @pl.when(j == 1)
    def _():
        run(True, 1.0)

    @pl.when(j >= 2)
    def _():
        run(False, 1.0)


def _inproj(x2, g, w_bf, layer, tabs, seq, dh):
    t, d = x2.shape
    n = w_bf.shape[2]
    tm, tn = seq, d
    rc = min(PROJ_ROW_CHUNK, tm)
    cos_t, s1_t, s2_t = tabs
    tab_spec = pl.BlockSpec((seq, LANES), lambda i, j: (0, 0))
    return pl.pallas_call(
        functools.partial(_inproj_kernel, rc=rc, half=dh // 8, q_scale=dh ** -0.5 * math.log2(math.e)),
        grid=(t // tm, n // tn),
        in_specs=[pl.BlockSpec((tm, d), lambda i, j: (i, 0)),
                  pl.BlockSpec((1, d), lambda i, j: (0, 0)),
                  pl.BlockSpec((None, d, tn), lambda i, j: (layer, 0, j)),
                  tab_spec, tab_spec, tab_spec],
        out_specs=pl.BlockSpec((tm, tn), lambda i, j: (i, j)),
        out_shape=jax.ShapeDtypeStruct((t, n), BF16),
        scratch_shapes=[pltpu.VMEM((tm, d), BF16)],
        compiler_params=_params(("parallel", "arbitrary")),
        name="inproj",
    )(x2, g, w_bf, cos_t, s1_t, s2_t)


def _inproj_combine(x2, rw, y2, g, w_bf, layer, tabs, proj_prev, seq, dh, part, n_parts):
    t, d = x2.shape
    n = w_bf.shape[2]
    tm, tn = min(PROJ_COMBINE_TILE, seq), d
    ntp = t // n_parts // tm
    row0 = part * ntp
    tiles_per_seq = seq // tm
    cos_t, s1_t, s2_t = tabs
    once = pl.Buffered(1)
    rows = lambda w: pl.BlockSpec((tm, w), lambda i, j: (row0 + i, 0), pipeline_mode=once)
    tab_spec = pl.BlockSpec((tm, LANES), lambda i, j: ((row0 + i) % tiles_per_seq, 0), pipeline_mode=once)
    in_specs = [rows(d), rows(LANES),
                pl.BlockSpec((tm, d // 2), lambda i, j: (i, 0), pipeline_mode=once),
                pl.BlockSpec((tm, d // 2), lambda i, j: (ntp + i, 0), pipeline_mode=once),
                pl.BlockSpec((1, d), lambda i, j: (0, 0)),
                pl.BlockSpec((None, d, tn), lambda i, j: (layer, 0, j)),
                tab_spec, tab_spec, tab_spec]
    args = [x2, rw, y2, y2, g, w_bf, cos_t, s1_t, s2_t]
    aliases = {0: 0}
    if proj_prev is not None:
        in_specs.append(pl.BlockSpec(memory_space=pl.ANY))
        args.append(proj_prev)
        aliases[len(args) - 1] = 1
    return pl.pallas_call(
        functools.partial(_inproj_combine_kernel, rc=min(PROJ_ROW_CHUNK, tm), half=dh // 8,
                          q_scale=dh ** -0.5 * math.log2(math.e)),
        grid=(ntp, n // tn),
        in_specs=in_specs,
        out_specs=[rows(d), pl.BlockSpec((tm, tn), lambda i, j: (row0 + i, j))],
        out_shape=[jax.ShapeDtypeStruct((t, d), F32), jax.ShapeDtypeStruct((t, n), BF16)],
        scratch_shapes=[pltpu.VMEM((tm, d), BF16)],
        input_output_aliases=aliases,
        compiler_params=_params(("parallel", "arbitrary")),
        name="inproj_combine",
    )(*args)


def _rotary_tables(seq, dh):
    rot = dh // 4
    half = rot // 2
    inv = jnp.float32(ROPE_THETA) ** (-jnp.arange(0, rot, 2, dtype=F32) / rot)
    ang = jnp.arange(seq, dtype=F32)[:, None] * inv[None, :]
    cos, sin = jnp.cos(ang), jnp.sin(ang)
    lane = jnp.arange(LANES) % dh
    idx = lane % half
    c_t = jnp.where(lane < rot, cos[:, idx], 1.0)
    s1_t = jnp.where((lane >= half) & (lane < rot), sin[:, idx], 0.0)
    s2_t = jnp.where(lane < half, -sin[:, idx], 0.0)
    return c_t.astype(F32), s1_t.astype(F32), s2_t.astype(F32)


def _attn_kernel(q_ref, k_ref, v_ref, lq1_ref, lk1_ref, lq2_ref, lk2_ref, sub_ref, o_ref,
                 s_ref, p_ref, kt_ref, *, dh, lambda_init, rc, rp):
    tq = q_ref.shape[0]
    n_chunks = tq // rc
    lane = lax.broadcasted_iota(I32, (1, LANES), 1)
    m1 = jnp.where(lane < dh, 1.0, 0.0).astype(BF16)
    m2 = jnp.where(lane >= dh, 1.0, 0.0).astype(BF16)
    lam = (jnp.exp(jnp.sum(lq1_ref[...] * lk1_ref[...], axis=-1, keepdims=True))
           - jnp.exp(jnp.sum(lq2_ref[...] * lk2_ref[...], axis=-1, keepdims=True)) + lambda_init)

    kt_ref[...] = k_ref[...].T

    def scores(c):
        qc = q_ref[c * rc:(c + 1) * rc, :]
        qq = jnp.concatenate([qc * m1, qc * m2], axis=0)
        s_ref[c % 2] = jnp.dot(qq, kt_ref[...], preferred_element_type=F32)

    def softmax_pv(c, j):
        r0 = j * rp
        s1 = s_ref[c % 2, r0:r0 + rp, :]
        s2 = s_ref[c % 2, rc + r0:rc + r0 + rp, :]
        e1 = jnp.exp2(s1 - jnp.max(s1, axis=-1, keepdims=True))
        e2 = jnp.exp2(s2 - jnp.max(s2, axis=-1, keepdims=True))
        l1 = jnp.sum(e1, axis=-1, keepdims=True)
        l2 = jnp.sum(e2, axis=-1, keepdims=True)
        slot = (c * (rc // rp) + j) % 2
        p_ref[slot] = (e1 - e2 * (lam * l1 / l2)).astype(BF16)
        od = jnp.dot(p_ref[slot], v_ref[...], preferred_element_type=F32) / l1
        od = _rms(od, sub_ref[...]) * (1.0 - lambda_init)
        o_ref[c * rc + r0:c * rc + r0 + rp, :] = od.astype(o_ref.dtype)

    scores(0)
    for c in range(n_chunks):
        if c + 1 < n_chunks:
            scores(c + 1)
        for j in range(rc // rp):
            softmax_pv(c, j)


def _attention(proj, lq1, lk1, lq2, lk2, sub, batch, seq, d, lambda_init):
    t = proj.shape[0]
    nh = d // LANES
    dh = LANES // 2
    tq = min(Q_TILE, seq)
    nq = seq // tq
    rc = min(ATTN_SCORE_CHUNK, tq)
    rp = min(ATTN_SOFTMAX_CHUNK, rc)
    vec = pl.BlockSpec((1, dh), lambda b, h, qi: (0, 0))
    return pl.pallas_call(
        functools.partial(_attn_kernel, dh=dh, lambda_init=lambda_init, rc=rc, rp=rp),
        grid=(batch, nh, nq),
        scratch_shapes=[pltpu.VMEM((2, 2 * rc, seq), F32), pltpu.VMEM((2, rp, seq), BF16),
                        pltpu.VMEM((LANES, seq), BF16)],
        in_specs=[pl.BlockSpec((tq, LANES), lambda b, h, qi: (b * nq + qi, h)),
                  pl.BlockSpec((seq, LANES), lambda b, h, qi: (b, nh + h)),
                  pl.BlockSpec((seq, LANES), lambda b, h, qi: (b, 2 * nh + h)),
                  vec, vec, vec, vec,
                  pl.BlockSpec((1, LANES), lambda b, h, qi: (0, 0))],
        out_specs=pl.BlockSpec((tq, LANES), lambda b, h, qi: (b * nq + qi, h)),
        out_shape=jax.ShapeDtypeStruct((t, d), BF16),
        compiler_params=_params(("parallel", "parallel", "arbitrary")),
        name="diff_attn",
    )(proj, proj, proj, lq1, lk1, lq2, lk2, sub)


def _mixout_kernel(x_ref, ya_ref, cb_ref, cc_ref, cx_ref, g0_ref, g1_ref, pc_ref, px_ref, nc_ref, nx_ref,
                   cw_ref, wb0_ref, wb1_ref, wo_ref, o_ref, *, tiles_per_seq):
    i = pl.program_id(0)
    tm = x_ref.shape[0]
    u = cc_ref[...].astype(F32) * cx_ref[...].astype(F32)
    hr = pc_ref.shape[0]
    u_prev = pc_ref[hr - 1:hr, :].astype(F32) * px_ref[hr - 1:hr, :].astype(F32)
    u_next = nc_ref[0:1, :].astype(F32) * nx_ref[0:1, :].astype(F32)
    pos = i % tiles_per_seq
    u_prev = jnp.where(pos == 0, 0.0, u_prev)
    u_next = jnp.where(pos == tiles_per_seq - 1, 0.0, u_next)
    row = lax.broadcasted_iota(I32, u.shape, 0)
    u_up = jnp.where(row == 0, u_prev, pltpu.roll(u, 1, 0))
    u_dn = jnp.where(row == tm - 1, u_next, pltpu.roll(u, tm - 1, 0))
    conv = cw_ref[0:1, :] * u_up + cw_ref[1:2, :] * u + cw_ref[2:3, :] * u_dn
    y_conv = (cb_ref[...].astype(F32) * conv).astype(BF16)
    br0 = jnp.dot(ya_ref[...], wb0_ref[...], preferred_element_type=F32)
    br1 = jnp.dot(y_conv, wb1_ref[...], preferred_element_type=F32)
    z = jax.nn.sigmoid(g0_ref[...].astype(F32)) * br0 + jax.nn.sigmoid(g1_ref[...].astype(F32)) * br1
    o_ref[...] = x_ref[...] + jnp.dot(z.astype(BF16), wo_ref[...], preferred_element_type=F32)


def _mixer_out(x2, y_attn, proj, conv_w, w_branch, w_o, layer, seq):
    t, d = x2.shape
    tm = min(ROW_TILE, seq)
    tps = seq // tm
    hr = BF16_SUBLANES
    nhb = t // hr
    row = lambda c: pl.BlockSpec((tm, d), lambda i: (i, c))
    prev = lambda c: pl.BlockSpec((hr, d), lambda i: (jnp.maximum(i * (tm // hr) - 1, 0), c))
    nxt = lambda c: pl.BlockSpec((hr, d), lambda i: (jnp.minimum((i + 1) * (tm // hr), nhb - 1), c))
    branch = lambda b: pl.BlockSpec((None, None, d, d), lambda i: (layer, b, 0, 0))
    return pl.pallas_call(
        functools.partial(_mixout_kernel, tiles_per_seq=tps),
        grid=(t // tm,),
        in_specs=[row(0), row(0), row(3), row(4), row(5), row(6), row(7),
                  prev(4), prev(5), nxt(4), nxt(5),
                  pl.BlockSpec((None,) + conv_w.shape[1:], lambda i: (layer, 0, 0)),
                  branch(0), branch(1), pl.BlockSpec((None, d, d), lambda i: (layer, 0, 0))],
        out_specs=row(0),
        out_shape=jax.ShapeDtypeStruct((t, d), F32),
        compiler_params=_params(("parallel",)),
        name="mixer_out",
    )(x2, y_attn, proj, proj, proj, proj, proj, proj, proj, proj, proj, conv_w, w_branch, w_branch, w_o)


def _memkv_kernel(m_ref, g_ref, w_ref, o_ref):
    h = _rms(m_ref[...], g_ref[...]).astype(BF16)
    o_ref[...] = jnp.dot(h, w_ref[...], preferred_element_type=F32).astype(o_ref.dtype)


def _mem_kv(mem2, g, w_bf):
    depth, d, n = w_bf.shape
    rows = mem2.shape[0]
    tn = min(n, 1024)
    return pl.pallas_call(
        _memkv_kernel,
        grid=(depth, n // tn),
        in_specs=[pl.BlockSpec((rows, d), lambda l, j: (0, 0)),
                  pl.BlockSpec((1, d), lambda l, j: (0, 0)),
                  pl.BlockSpec((None, d, tn), lambda l, j: (l, 0, j))],
        out_specs=pl.BlockSpec((None, rows, tn), lambda l, j: (l, 0, j)),
        out_shape=jax.ShapeDtypeStruct((depth, rows, n), BF16),
        compiler_params=_params(("parallel", "parallel")),
        name="mem_kv",
    )(mem2, g, w_bf)


def _cross_kernel(x_ref, g_ref, wq_ref, kv_ref, wo_ref, gf_ref, wr_ref, o_ref, hf_ref, lg_ref, *, heads):
    x = x_ref[...]
    d = x.shape[1]
    hd = d // heads
    h = _rms(x, g_ref[...]).astype(BF16)
    q = (jnp.dot(h, wq_ref[...], preferred_element_type=F32) * (hd ** -0.5)).astype(BF16)
    outs = []
    for hh in range(heads):
        qh = q[:, hh * hd:(hh + 1) * hd]
        kh = kv_ref[:, hh * hd:(hh + 1) * hd]
        vh = kv_ref[:, d + hh * hd:d + (hh + 1) * hd]
        s = lax.dot_general(qh, kh, NT_DIMS, preferred_element_type=F32)
        p = jnp.exp(s - jnp.max(s, axis=-1, keepdims=True))
        l = jnp.sum(p, axis=-1, keepdims=True)
        outs.append((jnp.dot(p.astype(BF16), vh, preferred_element_type=F32) / l).astype(BF16))
    o = jnp.concatenate(outs, axis=1)
    x_new = x + jnp.dot(o, wo_ref[...], preferred_element_type=F32)
    o_ref[...] = x_new
    h_ffn = _rms(x_new, gf_ref[...])
    hf_ref[...] = _pack_rows(h_ffn)
    lg_ref[...] = jnp.dot(h_ffn.astype(BF16), wr_ref[...], preferred_element_type=F32)


def _cross_attention(x2, g, wq, kv, layer, wo, g_ffn, w_r, seq, n_mem):
    t, d = x2.shape
    tm = min(ROW_TILE, seq)
    tps = seq // tm
    row = lambda w: pl.BlockSpec((tm, w), lambda i: (i, 0))
    const = lambda r, c: pl.BlockSpec((r, c), lambda i: (0, 0))
    of_layer = lambda r, c: pl.BlockSpec((None, r, c), lambda i: (layer, 0, 0))
    return pl.pallas_call(
        functools.partial(_cross_kernel, heads=MEM_HEADS),
        grid=(t // tm,),
        in_specs=[row(d), const(1, d), of_layer(d, d),
                  pl.BlockSpec((None, n_mem, 2 * d), lambda i: (layer, i // tps, 0)),
                  of_layer(d, d), const(1, d), of_layer(d, LANES)],
        out_specs=[row(d), row(d // 2), row(LANES)],
        out_shape=[jax.ShapeDtypeStruct((t, d), F32), jax.ShapeDtypeStruct((t, d // 2), I32),
                   jax.ShapeDtypeStruct((t, LANES), F32)],
        compiler_params=_params(("parallel",)),
        name="cross_attn",
    )(x2, g, wq, kv, wo, g_ffn, w_r)


def _router_kernel(lg_ref, ri_ref, rw_ref, cnt_ref, carry_ref, *, sub):
    @pl.when(pl.program_id(0) == 0)
    def _():
        carry_ref[...] = jnp.zeros_like(carry_ref)

    lg = lg_ref[...]
    tm = lg.shape[0]
    lane = lax.broadcasted_iota(I32, lg.shape, 1).astype(F32)
    neg = -jnp.inf

    def first_argmax(v):
        m = jnp.max(v, axis=-1, keepdims=True)
        return m, jnp.min(jnp.where(v == m, lane, float(LANES)), axis=-1, keepdims=True)

    gmask = lane < N_GROUPS
    gl = jnp.where(gmask, lg, neg)
    gmax, gidx = first_argmax(gl)
    g_w = 1.0 / jnp.sum(jnp.where(gmask, jnp.exp(gl - gmax), 0.0), axis=-1, keepdims=True)

    lo = EXPERT_LANE0 + gidx * EXPERTS_PER_GROUP
    emask = (lane >= lo) & (lane < lo + EXPERTS_PER_GROUP)
    el = jnp.where(emask, lg, neg)
    mx1, i1 = first_argmax(el)
    mx2, i2 = first_argmax(jnp.where(lane == i1, neg, el))
    esum = jnp.sum(jnp.where(emask, jnp.exp(el - mx1), 0.0), axis=-1, keepdims=True)
    p1 = 1.0 / esum
    p2 = jnp.exp(mx2 - mx1) / esum
    den = p1 + p2
    w1 = g_w * (p1 / den)
    w2 = g_w * (p2 / den)

    memb = jnp.where((lane == i1) | (lane == i2), 1.0, 0.0)
    r_i = lax.broadcasted_iota(I32, (sub, sub), 0)
    c_i = lax.broadcasted_iota(I32, (sub, sub), 1)
    ltri = jnp.where(c_i < r_i, 1.0, 0.0).astype(BF16)
    tots = []
    carry = carry_ref[...]
    for r0 in range(0, tm, sub):
        mb = memb[r0:r0 + sub]
        tots.append(jnp.dot(ltri, mb.astype(BF16), preferred_element_type=F32) + carry)
        carry = carry + jnp.sum(mb, axis=0, keepdims=True)
    carry_ref[...] = carry
    tot = jnp.concatenate(tots, axis=0)
    rank1 = jnp.sum(jnp.where(lane == i1, tot, 0.0), axis=-1, keepdims=True).astype(I32)
    rank2 = jnp.sum(jnp.where(lane == i2, tot, 0.0), axis=-1, keepdims=True).astype(I32)

    e1 = (i1 - EXPERT_LANE0).astype(I32)
    e2 = (i2 - EXPERT_LANE0).astype(I32)
    ri = jnp.where(lane == 0, e1, jnp.where(lane == 1, e2, jnp.where(lane == 2, rank1, jnp.where(lane == 3, rank2, 0))))
    ri_ref[...] = ri.T[:F32_SUBLANES]
    rw_ref[...] = jnp.where(lane == 0, w1, jnp.where(lane == 1, w2, 0.0))
    cnt_ref[...] = jnp.broadcast_to(carry, cnt_ref.shape)


def _router(lg, seq):
    t = lg.shape[0]
    tm = min(ROUTER_TILE, seq)
    row = pl.BlockSpec((tm, LANES), lambda i: (i, 0))
    return pl.pallas_call(
        functools.partial(_router_kernel, sub=min(ROW_TILE, tm)),
        grid=(t // tm,),
        in_specs=[row],
        out_specs=[pl.BlockSpec((F32_SUBLANES, tm), lambda i: (0, i)), row,
                   pl.BlockSpec((F32_SUBLANES, LANES), lambda i: (0, 0))],
        out_shape=[jax.ShapeDtypeStruct((F32_SUBLANES, t), I32),
                   jax.ShapeDtypeStruct((t, LANES), F32),
                   jax.ShapeDtypeStruct((F32_SUBLANES, LANES), F32)],
        scratch_shapes=[pltpu.VMEM((1, LANES), F32)],
        compiler_params=_params(("arbitrary",)),
        name="moe_router",
    )(lg)


def _expert_kernel(be_ref, first_ref, slot_ref, nxt_ref, na_ref, xb_ref, wg_hbm, wu_hbm, wd_hbm, yb_ref,
                   wg_buf, wu_buf, wd_buf, wgb_ref, wub_ref, wdb_ref, sem, *, layer):
    i = pl.program_id(0)

    def fetch(e, slot):
        return (pltpu.make_async_copy(wg_hbm.at[layer, e], wg_buf.at[slot], sem.at[slot, 0]),
                pltpu.make_async_copy(wu_hbm.at[layer, e], wu_buf.at[slot], sem.at[slot, 1]),
                pltpu.make_async_copy(wd_hbm.at[layer, e], wd_buf.at[slot], sem.at[slot, 2]))

    @pl.when(i < na_ref[0])
    def _():
        @pl.when(first_ref[i] == 1)
        def _():
            e, slot = be_ref[i], slot_ref[i]

            @pl.when(i == 0)
            def _():
                for cp in fetch(e, slot):
                    cp.start()

            for cp in fetch(e, slot):
                cp.wait()
            wgb_ref[...] = wg_buf[slot].astype(BF16)
            wub_ref[...] = wu_buf[slot].astype(BF16)
            wdb_ref[...] = wd_buf[slot].astype(BF16)

            @pl.when(nxt_ref[i] >= 0)
            def _():
                for cp in fetch(nxt_ref[i], 1 - slot):
                    cp.start()

        xb = _unpack_rows(xb_ref[...]).astype(BF16)
        gate = jnp.dot(xb, wgb_ref[...], preferred_element_type=F32)
        up = jnp.dot(xb, wub_ref[...], preferred_element_type=F32)
        act = (jax.nn.silu(gate) * up).astype(BF16)
        yb_ref[...] = _pack_rows(jnp.dot(act, wdb_ref[...], preferred_element_type=F32))

    @pl.when(i >= na_ref[0])
    def _():
        yb_ref[...] = jnp.zeros_like(yb_ref)


def _experts(plan, xb, w_gate, w_up, w_down, layer, blk):
    rows = xb.shape[0]
    d, de = w_gate.shape[-2:]
    nb = rows // blk

    def row_map(i, be, first, slot, nxt, na):
        return (jnp.minimum(i, jnp.maximum(na[0] - 1, 0)), 0)

    hbm = pl.BlockSpec(memory_space=pl.ANY)
    return pl.pallas_call(
        functools.partial(_expert_kernel, layer=layer),
        grid_spec=pltpu.PrefetchScalarGridSpec(
            num_scalar_prefetch=5,
            grid=(nb,),
            in_specs=[pl.BlockSpec((blk, d // 2), row_map), hbm, hbm, hbm],
            out_specs=pl.BlockSpec((blk, d // 2), lambda i, *_: (i, 0)),
            scratch_shapes=[pltpu.VMEM((2, d, de), F32), pltpu.VMEM((2, d, de), F32), pltpu.VMEM((2, de, d), F32),
                            pltpu.VMEM((d, de), BF16), pltpu.VMEM((d, de), BF16), pltpu.VMEM((de, d), BF16),
                            pltpu.SemaphoreType.DMA((2, 3))]),
        out_shape=jax.ShapeDtypeStruct((rows, d // 2), I32),
        compiler_params=_params(("arbitrary",)),
        name="moe_experts",
    )(*plan, xb, w_gate, w_up, w_down)


def _sc_gather_rows(table, idx):
    b = idx.shape[0]
    d = table.shape[1]
    per_w = b // SC_WORKERS
    ch = SC_CHUNK_ROWS
    mesh = plsc.VectorSubcoreMesh(core_axis_name="c", subcore_axis_name="s")

    n_ch = per_w // ch
    assert per_w * SC_WORKERS == b and n_ch * ch == per_w and n_ch % 2 == 0

    def body(table_hbm, idx_hbm, out_hbm, idx_v, rows_v, gsem, wsem):
        wid = lax.axis_index("s") * SC_CORES + lax.axis_index("c")
        base = wid * per_w
        pltpu.sync_copy(idx_hbm.at[pl.ds(base, per_w)], idx_v)

        def write_back(c, buf):
            off = pl.multiple_of(c * ch, ch)
            return pltpu.make_async_copy(rows_v.at[buf], out_hbm.at[pl.ds(base + off, ch)], wsem.at[buf])

        @pl.loop(0, n_ch, step=2)
        def _(c0):
            for buf in range(2):
                c = c0 + buf

                @pl.when(c0 > 0)
                def _():
                    write_back(c - 2, buf).wait()

                off = pl.multiple_of(c * ch, ch)
                pltpu.async_copy(table_hbm.at[idx_v.at[pl.ds(off, ch)]], rows_v.at[buf], gsem).wait()
                write_back(c, buf).start()

        for buf in range(2):
            write_back(n_ch - 2 + buf, buf).wait()

    return pl.kernel(body, out_type=jax.ShapeDtypeStruct((b, d), table.dtype), mesh=mesh,
                     scratch_types=[pltpu.VMEM((per_w,), I32), pltpu.VMEM((2, ch, d), table.dtype),
                                    pltpu.SemaphoreType.DMA, pltpu.SemaphoreType.DMA((2,))],
                     name="sc_gather")(table, idx)


def _sc_scatter_rows(rows, idx, n_out):
    t, d = rows.shape
    top_k, _, n_ch, ch = idx.shape
    per_w = n_ch * ch
    assert per_w * SC_WORKERS == t and idx.shape[1] == SC_WORKERS
    mesh = plsc.VectorSubcoreMesh(core_axis_name="c", subcore_axis_name="s")

    assert n_ch % 2 == 0

    def body(rows_hbm, idx_hbm, out_hbm, idx_v, rows_v, lsem, ssem):
        wid = lax.axis_index("s") * SC_CORES + lax.axis_index("c")
        base = wid * per_w
        for k in range(top_k):
            pltpu.sync_copy(idx_hbm.at[k, wid], idx_v.at[k])

        def load(c, buf):
            off = pl.multiple_of(c * ch, ch)
            return pltpu.make_async_copy(rows_hbm.at[pl.ds(base + off, ch)], rows_v.at[buf], lsem.at[buf])

        load(0, 0).start()

        @pl.loop(0, n_ch, step=2)
        def _(c0):
            for buf in range(2):
                c = c0 + buf
                load(c, buf).wait()

                @pl.when(c + 1 < n_ch)
                def _():
                    load(c + 1, 1 - buf).start()

                scatters = [pltpu.async_copy(rows_v.at[buf], out_hbm.at[idx_v.at[k, c]], ssem)
                            for k in range(top_k)]
                for cp in scatters:
                    cp.wait()

    return pl.kernel(body, out_type=jax.ShapeDtypeStruct((n_out, d), rows.dtype), mesh=mesh,
                     scratch_types=[pltpu.VMEM((top_k, n_ch, ch), I32), pltpu.VMEM((2, ch, d), rows.dtype),
                                    pltpu.SemaphoreType.DMA((2,)), pltpu.SemaphoreType.DMA],
                     name="sc_scatter")(rows, idx)


def _combine_final_kernel(x_ref, rw_ref, y0_ref, y1_ref, gf_ref, o_ref):
    w = rw_ref[...]
    out = x_ref[...] + w[:, 0:1] * _unpack_rows(y0_ref[...]) + w[:, 1:2] * _unpack_rows(y1_ref[...])
    o_ref[...] = _rms(out, gf_ref[...])


def _combine_final(x2, rw, y2, g_final, seq, part, n_parts):
    t, d = x2.shape
    tm = min(ROW_TILE, seq)
    nt = t // tm // n_parts
    row = lambda w: pl.BlockSpec((tm, w), lambda i: (part * nt + i, 0))
    return pl.pallas_call(
        _combine_final_kernel,
        grid=(nt,),
        in_specs=[row(d), row(LANES), pl.BlockSpec((tm, d // 2), lambda i: (i, 0)),
                  pl.BlockSpec((tm, d // 2), lambda i: (nt + i, 0)), pl.BlockSpec((1, d), lambda i: (0, 0))],
        out_specs=row(d),
        out_shape=jax.ShapeDtypeStruct((t, d), F32),
        input_output_aliases={0: 0},
        compiler_params=_params(("parallel",)),
        name="moe_combine_final",
    )(x2, rw, y2, y2, g_final)


def _route_plan(ri, counts_f, blk, n_blocks):
    cnt = counts_f[0, EXPERT_LANE0:EXPERT_LANE0 + N_EXPERTS].astype(I32)
    nblk = (cnt + blk - 1) // blk
    cum = jnp.cumsum(nblk)
    pad_start = (cum - nblk) * blk
    e = ri[0:TOP_K]
    rank = ri[TOP_K:2 * TOP_K]
    onehot = e[:, :, None] == jnp.arange(N_EXPERTS, dtype=I32)[None, None, :]
    dest = rank + jnp.sum(jnp.where(onehot, pad_start[None, None, :], 0), axis=-1)
    blocks = jnp.arange(n_blocks, dtype=I32)
    block_expert = jnp.minimum(jnp.sum((cum[None, :] <= blocks[:, None]).astype(I32), axis=-1), N_EXPERTS - 1)
    n_active = cum[-1:].astype(I32)
    experts = jnp.arange(N_EXPERTS, dtype=I32)
    is_block_expert = block_expert[:, None] == experts[None, :]
    first = jnp.sum(jnp.where(is_block_expert, ((cum - nblk)[None, :] == blocks[:, None]).astype(I32), 0), axis=-1)
    used = (nblk > 0).astype(I32)
    slot_e = (jnp.cumsum(used) - used) % 2
    later = (experts[None, :] > experts[:, None]) & (nblk[None, :] > 0)
    next_e = jnp.min(jnp.where(later, experts[None, :], N_EXPERTS), axis=-1)
    next_e = jnp.where(next_e == N_EXPERTS, -1, next_e)
    slot = jnp.sum(jnp.where(is_block_expert, slot_e[None, :], 0), axis=-1)
    nxt = jnp.sum(jnp.where(is_block_expert, next_e[None, :], 0), axis=-1)
    plan = (block_expert, first.astype(I32), slot.astype(I32), nxt.astype(I32), n_active)
    return dest.astype(I32), plan


def kernel(x, mem, norm_mix, w_in, lambda_q1, lambda_k1, lambda_q2, lambda_k2, subln, conv_w, w_branch, w_o,
           norm_cross, norm_mem, w_cq, w_ckv, w_co, norm_ffn, w_router_group, w_router_expert, w_exp_gate,
           w_exp_up, w_exp_down, norm_final):
    batch, seq, d = x.shape
    depth = w_in.shape[0]
    n_mem = mem.shape[1]
    t = batch * seq
    dh = d // (2 * DA_HEADS)
    assert 2 * dh == LANES and d % LANES == 0 and w_in.shape[2] == 8 * d
    assert w_router_group.shape[2] == N_GROUPS and w_router_expert.shape[2] == N_EXPERTS
    blk = min(EXPERT_BLOCK, seq)
    n_blocks = (t * TOP_K) // blk + N_EXPERTS
    parts = COMBINE_PARTS if t % (COMBINE_PARTS * SC_WORKERS * SC_CHUNK_ROWS) == 0 else 1
    tokens_per_worker = t // SC_WORKERS

    tabs = _rotary_tables(seq, dh)
    x2 = x.reshape(t, d)
    kv = _mem_kv(mem.reshape(batch * n_mem, d), norm_mem.reshape(1, d), w_ckv.astype(BF16))
    w_r = jnp.concatenate([w_router_group, w_router_expert,
                           jnp.zeros((depth, d, LANES - N_GROUPS - N_EXPERTS), F32)], axis=-1).astype(BF16)

    w_in_bf, w_branch_bf, w_o_bf = w_in.astype(BF16), w_branch.astype(BF16), w_o.astype(BF16)
    w_cq_bf, w_co_bf = w_cq.astype(BF16), w_co.astype(BF16)

    moe_out = None
    for l in range(depth):
        lambda_init = 0.8 - 0.6 * math.exp(-0.3 * l)
        g_mix = norm_mix[l].reshape(1, d)
        if moe_out is None:
            proj = _inproj(x2, g_mix, w_in_bf, l, tabs, seq, dh)
        else:
            proj = None
            for p in range(parts):
                x2, proj = _inproj_combine(x2, moe_out[0], moe_out[1][p], g_mix, w_in_bf, l, tabs, proj, seq, dh,
                                           p, parts)
        y_attn = _attention(proj, lambda_q1[l].reshape(1, dh), lambda_k1[l].reshape(1, dh),
                            lambda_q2[l].reshape(1, dh), lambda_k2[l].reshape(1, dh),
                            subln[l].reshape(1, LANES), batch, seq, d, lambda_init)
        x2 = _mixer_out(x2, y_attn, proj, conv_w, w_branch_bf, w_o_bf, l, seq)
        x2, h_ffn, logits = _cross_attention(x2, norm_cross[l].reshape(1, d), w_cq_bf, kv, l, w_co_bf,
                                             norm_ffn[l].reshape(1, d), w_r, seq, n_mem)
        ri, rw, counts = _router(logits, seq)
        dest, plan = _route_plan(ri, counts, blk, n_blocks)
        scatter_idx = dest.reshape(TOP_K, SC_WORKERS, tokens_per_worker // SC_CHUNK_ROWS, SC_CHUNK_ROWS)
        xb = _sc_scatter_rows(h_ffn, scatter_idx, n_blocks * blk)
        yb = _experts(plan, xb, w_exp_gate, w_exp_up, w_exp_down, l, blk)
        tp = t // parts
        moe_out = (rw, [_sc_gather_rows(yb, dest[:, p * tp:(p + 1) * tp].reshape(-1)) for p in range(parts)])
    for p in range(parts):
        x2 = _combine_final(x2, moe_out[0], moe_out[1][p], norm_final.reshape(1, d), seq, p, parts)
    return x2.reshape(batch, seq, d)
```

```python
import functools
import math

import jax
import jax.numpy as jnp
from jax import lax
from jax.experimental import pallas as pl
from jax.experimental.pallas import tpu as pltpu
from jax.experimental.pallas import tpu_sc as plsc

EPS = 1e-6
DA_HEADS = 8
MEM_HEADS = 4
N_GROUPS = 4
EXPERTS_PER_GROUP = 8
N_EXPERTS = N_GROUPS * EXPERTS_PER_GROUP
TOP_K = 2
ROPE_THETA = 500000.0
LANES = 128
F32_SUBLANES = 8
BF16_SUBLANES = 16
EXPERT_LANE0 = N_GROUPS
ROW_TILE = 512
ROUTER_TILE = 2048
Q_TILE = 2048
EXPERT_BLOCK = 256
PROJ_ROW_CHUNK = 1024
PROJ_COMBINE_TILE = 2048
ATTN_SCORE_CHUNK = 128
ATTN_SOFTMAX_CHUNK = 128
COMBINE_PARTS = 2
SC_CORES = 2
SC_WORKERS = 32
SC_CHUNK_ROWS = 64
VMEM_LIMIT = 56 * 1024 * 1024

F32 = jnp.float32
BF16 = jnp.bfloat16
I32 = jnp.int32
NT_DIMS = (((1,), (1,)), ((), ()))


def _rms(x, g):
    return x * lax.rsqrt(jnp.mean(x * x, axis=-1, keepdims=True) + EPS) * g


def _pack_rows(x):
    half = x.shape[1] // 2
    lo = lax.bitcast_convert_type(x[:, :half].astype(jnp.bfloat16).astype(F32), jnp.uint32)
    hi = lax.bitcast_convert_type(x[:, half:].astype(jnp.bfloat16).astype(F32), jnp.uint32)
    return lax.bitcast_convert_type((hi & jnp.uint32(0xFFFF0000)) | (lo >> 16), I32)


def _unpack_rows(w):
    u = lax.bitcast_convert_type(w, jnp.uint32)
    lo = lax.bitcast_convert_type(u << 16, F32)
    hi = lax.bitcast_convert_type(u & jnp.uint32(0xFFFF0000), F32)
    return jnp.concatenate([lo, hi], axis=1)


def _params(sem, vmem=VMEM_LIMIT):
    return pltpu.CompilerParams(dimension_semantics=sem, vmem_limit_bytes=vmem)


def _inproj_kernel(x_ref, g_ref, w_ref, cos_ref, s1_ref, s2_ref, o_ref, h_ref, *, rc, half, q_scale):
    @pl.when(pl.program_id(1) == 0)
    def _():
        h_ref[...] = _rms(x_ref[...], g_ref[...]).astype(BF16)

    _proj_column_block(h_ref, w_ref, cos_ref, s1_ref, s2_ref, o_ref, rc, half, q_scale)


def _inproj_combine_kernel(x_ref, rw_ref, y0_ref, y1_ref, g_ref, w_ref, cos_ref, s1_ref, s2_ref, *rest,
                           rc, half, q_scale):
    xnew_ref, o_ref, h_ref = rest[-3:]

    @pl.when(pl.program_id(1) == 0)
    def _():
        w = rw_ref[...]
        x_new = x_ref[...] + w[:, 0:1] * _unpack_rows(y0_ref[...]) + w[:, 1:2] * _unpack_rows(y1_ref[...])
        xnew_ref[...] = x_new
        h_ref[...] = _rms(x_new, g_ref[...]).astype(BF16)

    _proj_column_block(h_ref, w_ref, cos_ref, s1_ref, s2_ref, o_ref, rc, half, q_scale)


def _proj_column_block(h_ref, w_ref, cos_ref, s1_ref, s2_ref, o_ref, rc, half, q_scale):
    j = pl.program_id(1)
    tm, tn = o_ref.shape

    def run(rot, scale):
        def body(r, carry):
            r0 = pl.multiple_of(r * rc, rc)
            acc = jnp.dot(h_ref[pl.ds(r0, rc), :], w_ref[...], preferred_element_type=F32)
            if not rot:
                o_ref[pl.ds(r0, rc), :] = acc.astype(o_ref.dtype)
                return carry
            c = cos_ref[pl.ds(r0, rc), :]
            s1 = s1_ref[pl.ds(r0, rc), :]
            s2 = s2_ref[pl.ds(r0, rc), :]
            for cc in range(tn // LANES):
                a = acc[:, cc * LANES:(cc + 1) * LANES]
                a = a * c + pltpu.roll(a, half, 1) * s1 + pltpu.roll(a, LANES - half, 1) * s2
                if scale != 1.0:
                    a = a * scale
                o_ref[pl.ds(r0, rc), cc * LANES:(cc + 1) * LANES] = a.astype(o_ref.dtype)
            return carry
        lax.fori_loop(0, tm // rc, body, 0)

    @pl.when(j == 0)
    def _():
        run(True, q_scale)

    @pl.when(j == 1)
    def _():
        run(True, 1.0)

    @pl.when(j >= 2)
    def _():
        run(False, 1.0)


def _inproj(x2, g, w_bf, layer, tabs, seq, dh):
    t, d = x2.shape
    n = w_bf.shape[2]
    tm, tn = seq, d
    rc = min(PROJ_ROW_CHUNK, tm)
    cos_t, s1_t, s2_t = tabs
    tab_spec = pl.BlockSpec((seq, LANES), lambda i, j: (0, 0))
    return pl.pallas_call(
        functools.partial(_inproj_kernel, rc=rc, half=dh // 8, q_scale=dh ** -0.5 * math.log2(math.e)),
        grid=(t // tm, n // tn),
        in_specs=[pl.BlockSpec((tm, d), lambda i, j: (i, 0)),
                  pl.BlockSpec((1, d), lambda i, j: (0, 0)),
                  pl.BlockSpec((None, d, tn), lambda i, j: (layer, 0, j)),
                  tab_spec, tab_spec, tab_spec],
        out_specs=pl.BlockSpec((tm, tn), lambda i, j: (i, j)),
        out_shape=jax.ShapeDtypeStruct((t, n), BF16),
        scratch_shapes=[pltpu.VMEM((tm, d), BF16)],
        compiler_params=_params(("parallel", "arbitrary")),
        name="inproj",
    )(x2, g, w_bf, cos_t, s1_t, s2_t)


def _inproj_combine(x2, rw, y2, g, w_bf, layer, tabs, proj_prev, seq, dh, part, n_parts):
    t, d = x2.shape
    n = w_bf.shape[2]
    tm, tn = min(PROJ_COMBINE_TILE, seq), d
    ntp = t // n_parts // tm
    row0 = part * ntp
    tiles_per_seq = seq // tm
    cos_t, s1_t, s2_t = tabs
    once = pl.Buffered(1)
    rows = lambda w: pl.BlockSpec((tm, w), lambda i, j: (row0 + i, 0), pipeline_mode=once)
    tab_spec = pl.BlockSpec((tm, LANES), lambda i, j: ((row0 + i) % tiles_per_seq, 0), pipeline_mode=once)
    in_specs = [rows(d), rows(LANES),
                pl.BlockSpec((tm, d // 2), lambda i, j: (i, 0), pipeline_mode=once),
                pl.BlockSpec((tm, d // 2), lambda i, j: (ntp + i, 0), pipeline_mode=once),
                pl.BlockSpec((1, d), lambda i, j: (0, 0)),
                pl.BlockSpec((None, d, tn), lambda i, j: (layer, 0, j)),
                tab_spec, tab_spec, tab_spec]
    args = [x2, rw, y2, y2, g, w_bf, cos_t, s1_t, s2_t]
    aliases = {0: 0}
    if proj_prev is not None:
        in_specs.append(pl.BlockSpec(memory_space=pl.ANY))
        args.append(proj_prev)
        aliases[len(args) - 1] = 1
    return pl.pallas_call(
        functools.partial(_inproj_combine_kernel, rc=min(PROJ_ROW_CHUNK, tm), half=dh // 8,
                          q_scale=dh ** -0.5 * math.log2(math.e)),
        grid=(ntp, n // tn),
        in_specs=in_specs,
        out_specs=[rows(d), pl.BlockSpec((tm, tn), lambda i, j: (row0 + i, j))],
        out_shape=[jax.ShapeDtypeStruct((t, d), F32), jax.ShapeDtypeStruct((t, n), BF16)],
        scratch_shapes=[pltpu.VMEM((tm, d), BF16)],
        input_output_aliases=aliases,
        compiler_params=_params(("parallel", "arbitrary")),
        name="inproj_combine",
    )(*args)


def _rotary_tables(seq, dh):
    rot = dh // 4
    half = rot // 2
    inv = jnp.float32(ROPE_THETA) ** (-jnp.arange(0, rot, 2, dtype=F32) / rot)
    ang = jnp.arange(seq, dtype=F32)[:, None] * inv[None, :]
    cos, sin = jnp.cos(ang), jnp.sin(ang)
    lane = jnp.arange(LANES) % dh
    idx = lane % half
    c_t = jnp.where(lane < rot, cos[:, idx], 1.0)
    s1_t = jnp.where((lane >= half) & (lane < rot), sin[:, idx], 0.0)
    s2_t = jnp.where(lane < half, -sin[:, idx], 0.0)
    return c_t.astype(F32), s1_t.astype(F32), s2_t.astype(F32)


def _attn_kernel(q_ref, k_ref, v_ref, lq1_ref, lk1_ref, lq2_ref, lk2_ref, sub_ref, o_ref,
                 s_ref, p_ref, kt_ref, *, dh, lambda_init, rc, rp):
    tq = q_ref.shape[0]
    n_chunks = tq // rc
    lane = lax.broadcasted_iota(I32, (1, LANES), 1)
    m1 = jnp.where(lane < dh, 1.0, 0.0).astype(BF16)
    m2 = jnp.where(lane >= dh, 1.0, 0.0).astype(BF16)
    lam = (jnp.exp(jnp.sum(lq1_ref[...] * lk1_ref[...], axis=-1, keepdims=True))
           - jnp.exp(jnp.sum(lq2_ref[...] * lk2_ref[...], axis=-1, keepdims=True)) + lambda_init)

    kt_ref[...] = k_ref[...].T

    def scores(c):
        qc = q_ref[c * rc:(c + 1) * rc, :]
        qq = jnp.concatenate([qc * m1, qc * m2], axis=0)
        s_ref[c % 2] = jnp.dot(qq, kt_ref[...], preferred_element_type=F32)

    def softmax_pv(c, j):
        r0 = j * rp
        s1 = s_ref[c % 2, r0:r0 + rp, :]
        s2 = s_ref[c % 2, rc + r0:rc + r0 + rp, :]
        e1 = jnp.exp2(s1 - jnp.max(s1, axis=-1, keepdims=True))
        e2 = jnp.exp2(s2 - jnp.max(s2, axis=-1, keepdims=True))
        l1 = jnp.sum(e1, axis=-1, keepdims=True)
        l2 = jnp.sum(e2, axis=-1, keepdims=True)
        slot = (c * (rc // rp) + j) % 2
        p_ref[slot] = (e1 - e2 * (lam * l1 / l2)).astype(BF16)
        od = jnp.dot(p_ref[slot], v_ref[...], preferred_element_type=F32) / l1
        od = _rms(od, sub_ref[...]) * (1.0 - lambda_init)
        o_ref[c * rc + r0:c * rc + r0 + rp, :] = od.astype(o_ref.dtype)

    scores(0)
    for c in range(n_chunks):
        if c + 1 < n_chunks:
            scores(c + 1)
        for j in range(rc // rp):
            softmax_pv(c, j)


def _attention(proj, lq1, lk1, lq2, lk2, sub, batch, seq, d, lambda_init):
    t = proj.shape[0]
    nh = d // LANES
    dh = LANES // 2
    tq = min(Q_TILE, seq)
    nq = seq // tq
    rc = min(ATTN_SCORE_CHUNK, tq)
    rp = min(ATTN_SOFTMAX_CHUNK, rc)
    vec = pl.BlockSpec((1, dh), lambda b, h, qi: (0, 0))
    return pl.pallas_call(
        functools.partial(_attn_kernel, dh=dh, lambda_init=lambda_init, rc=rc, rp=rp),
        grid=(batch, nh, nq),
        scratch_shapes=[pltpu.VMEM((2, 2 * rc, seq), F32), pltpu.VMEM((2, rp, seq), BF16),
                        pltpu.VMEM((LANES, seq), BF16)],
        in_specs=[pl.BlockSpec((tq, LANES), lambda b, h, qi: (b * nq + qi, h)),
                  pl.BlockSpec((seq, LANES), lambda b, h, qi: (b, nh + h)),
                  pl.BlockSpec((seq, LANES), lambda b, h, qi: (b, 2 * nh + h)),
                  vec, vec, vec, vec,
                  pl.BlockSpec((1, LANES), lambda b, h, qi: (0, 0))],
        out_specs=pl.BlockSpec((tq, LANES), lambda b, h, qi: (b * nq + qi, h)),
        out_shape=jax.ShapeDtypeStruct((t, d), BF16),
        compiler_params=_params(("parallel", "parallel", "arbitrary")),
        name="diff_attn",
    )(proj, proj, proj, lq1, lk1, lq2, lk2, sub)


def _mixout_kernel(x_ref, ya_ref, cb_ref, cc_ref, cx_ref, g0_ref, g1_ref, pc_ref, px_ref, nc_ref, nx_ref,
                   cw_ref, wb0_ref, wb1_ref, wo_ref, o_ref, *, tiles_per_seq):
    i = pl.program_id(0)
    tm = x_ref.shape[0]
    u = cc_ref[...].astype(F32) * cx_ref[...].astype(F32)
    hr = pc_ref.shape[0]
    u_prev = pc_ref[hr - 1:hr, :].astype(F32) * px_ref[hr - 1:hr, :].astype(F32)
    u_next = nc_ref[0:1, :].astype(F32) * nx_ref[0:1, :].astype(F32)
    pos = i % tiles_per_seq
    u_prev = jnp.where(pos == 0, 0.0, u_prev)
    u_next = jnp.where(pos == tiles_per_seq - 1, 0.0, u_next)
    row = lax.broadcasted_iota(I32, u.shape, 0)
    u_up = jnp.where(row == 0, u_prev, pltpu.roll(u, 1, 0))
    u_dn = jnp.where(row == tm - 1, u_next, pltpu.roll(u, tm - 1, 0))
    conv = cw_ref[0:1, :] * u_up + cw_ref[1:2, :] * u + cw_ref[2:3, :] * u_dn
    y_conv = (cb_ref[...].astype(F32) * conv).astype(BF16)
    br0 = jnp.dot(ya_ref[...], wb0_ref[...], preferred_element_type=F32)
    br1 = jnp.dot(y_conv, wb1_ref[...], preferred_element_type=F32)
    z = jax.nn.sigmoid(g0_ref[...].astype(F32)) * br0 + jax.nn.sigmoid(g1_ref[...].astype(F32)) * br1
    o_ref[...] = x_ref[...] + jnp.dot(z.astype(BF16), wo_ref[...], preferred_element_type=F32)


def _mixer_out(x2, y_attn, proj, conv_w, w_branch, w_o, layer, seq):
    t, d = x2.shape
    tm = min(ROW_TILE, seq)
    tps = seq // tm
    hr = BF16_SUBLANES
    nhb = t // hr
    row = lambda c: pl.BlockSpec((tm, d), lambda i: (i, c))
    prev = lambda c: pl.BlockSpec((hr, d), lambda i: (jnp.maximum(i * (tm // hr) - 1, 0), c))
    nxt = lambda c: pl.BlockSpec((hr, d), lambda i: (jnp.minimum((i + 1) * (tm // hr), nhb - 1), c))
    branch = lambda b: pl.BlockSpec((None, None, d, d), lambda i: (layer, b, 0, 0))
    return pl.pallas_call(
        functools.partial(_mixout_kernel, tiles_per_seq=tps),
        grid=(t // tm,),
        in_specs=[row(0), row(0), row(3), row(4), row(5), row(6), row(7),
                  prev(4), prev(5), nxt(4), nxt(5),
                  pl.BlockSpec((None,) + conv_w.shape[1:], lambda i: (layer, 0, 0)),
                  branch(0), branch(1), pl.BlockSpec((None, d, d), lambda i: (layer, 0, 0))],
        out_specs=row(0),
        out_shape=jax.ShapeDtypeStruct((t, d), F32),
        compiler_params=_params(("parallel",)),
        name="mixer_out",
    )(x2, y_attn, proj, proj, proj, proj, proj, proj, proj, proj, proj, conv_w, w_branch, w_branch, w_o)


def _memkv_kernel(m_ref, g_ref, w_ref, o_ref):
    h = _rms(m_ref[...], g_ref[...]).astype(BF16)
    o_ref[...] = jnp.dot(h, w_ref[...], preferred_element_type=F32).astype(o_ref.dtype)


def _mem_kv(mem2, g, w_bf):
    depth, d, n = w_bf.shape
    rows = mem2.shape[0]
    tn = min(n, 1024)
    return pl.pallas_call(
        _memkv_kernel,
        grid=(depth, n // tn),
        in_specs=[pl.BlockSpec((rows, d), lambda l, j: (0, 0)),
                  pl.BlockSpec((1, d), lambda l, j: (0, 0)),
                  pl.BlockSpec((None, d, tn), lambda l, j: (l, 0, j))],
        out_specs=pl.BlockSpec((None, rows, tn), lambda l, j: (l, 0, j)),
        out_shape=jax.ShapeDtypeStruct((depth, rows, n), BF16),
        compiler_params=_params(("parallel", "parallel")),
        name="mem_kv",
    )(mem2, g, w_bf)


def _cross_kernel(x_ref, g_ref, wq_ref, kv_ref, wo_ref, gf_ref, wr_ref, o_ref, hf_ref, lg_ref, *, heads):
    x = x_ref[...]
    d = x.shape[1]
    hd = d // heads
    h = _rms(x, g_ref[...]).astype(BF16)
    q = (jnp.dot(h, wq_ref[...], preferred_element_type=F32) * (hd ** -0.5)).astype(BF16)
    outs = []
    for hh in range(heads):
        qh = q[:, hh * hd:(hh + 1) * hd]
        kh = kv_ref[:, hh * hd:(hh + 1) * hd]
        vh = kv_ref[:, d + hh * hd:d + (hh + 1) * hd]
        s = lax.dot_general(qh, kh, NT_DIMS, preferred_element_type=F32)
        p = jnp.exp(s - jnp.max(s, axis=-1, keepdims=True))
        l = jnp.sum(p, axis=-1, keepdims=True)
        outs.append((jnp.dot(p.astype(BF16), vh, preferred_element_type=F32) / l).astype(BF16))
    o = jnp.concatenate(outs, axis=1)
    x_new = x + jnp.dot(o, wo_ref[...], preferred_element_type=F32)
    o_ref[...] = x_new
    h_ffn = _rms(x_new, gf_ref[...])
    hf_ref[...] = _pack_rows(h_ffn)
    lg_ref[...] = jnp.dot(h_ffn.astype(BF16), wr_ref[...], preferred_element_type=F32)


def _cross_attention(x2, g, wq, kv, layer, wo, g_ffn, w_r, seq, n_mem):
    t, d = x2.shape
    tm = min(ROW_TILE, seq)
    tps = seq // tm
    row = lambda w: pl.BlockSpec((tm, w), lambda i: (i, 0))
    const = lambda r, c: pl.BlockSpec((r, c), lambda i: (0, 0))
    of_layer = lambda r, c: pl.BlockSpec((None, r, c), lambda i: (layer, 0, 0))
    return pl.pallas_call(
        functools.partial(_cross_kernel, heads=MEM_HEADS),
        grid=(t // tm,),
        in_specs=[row(d), const(1, d), of_layer(d, d),
                  pl.BlockSpec((None, n_mem, 2 * d), lambda i: (layer, i // tps, 0)),
                  of_layer(d, d), const(1, d), of_layer(d, LANES)],
        out_specs=[row(d), row(d // 2), row(LANES)],
        out_shape=[jax.ShapeDtypeStruct((t, d), F32), jax.ShapeDtypeStruct((t, d // 2), I32),
                   jax.ShapeDtypeStruct((t, LANES), F32)],
        compiler_params=_params(("parallel",)),
        name="cross_attn",
    )(x2, g, wq, kv, wo, g_ffn, w_r)


def _router_kernel(lg_ref, ri_ref, rw_ref, cnt_ref, carry_ref, *, sub):
    @pl.when(pl.program_id(0) == 0)
    def _():
        carry_ref[...] = jnp.zeros_like(carry_ref)

    lg = lg_ref[...]
    tm = lg.shape[0]
    lane = lax.broadcasted_iota(I32, lg.shape, 1).astype(F32)
    neg = -jnp.inf

    def first_argmax(v):
        m = jnp.max(v, axis=-1, keepdims=True)
        return m, jnp.min(jnp.where(v == m, lane, float(LANES)), axis=-1, keepdims=True)

    gmask = lane < N_GROUPS
    gl = jnp.where(gmask, lg, neg)
    gmax, gidx = first_argmax(gl)
    g_w = 1.0 / jnp.sum(jnp.where(gmask, jnp.exp(gl - gmax), 0.0), axis=-1, keepdims=True)

    lo = EXPERT_LANE0 + gidx * EXPERTS_PER_GROUP
    emask = (lane >= lo) & (lane < lo + EXPERTS_PER_GROUP)
    el = jnp.where(emask, lg, neg)
    mx1, i1 = first_argmax(el)
    mx2, i2 = first_argmax(jnp.where(lane == i1, neg, el))
    esum = jnp.sum(jnp.where(emask, jnp.exp(el - mx1), 0.0), axis=-1, keepdims=True)
    p1 = 1.0 / esum
    p2 = jnp.exp(mx2 - mx1) / esum
    den = p1 + p2
    w1 = g_w * (p1 / den)
    w2 = g_w * (p2 / den)

    memb = jnp.where((lane == i1) | (lane == i2), 1.0, 0.0)
    r_i = lax.broadcasted_iota(I32, (sub, sub), 0)
    c_i = lax.broadcasted_iota(I32, (sub, sub), 1)
    ltri = jnp.where(c_i < r_i, 1.0, 0.0).astype(BF16)
    tots = []
    carry = carry_ref[...]
    for r0 in range(0, tm, sub):
        mb = memb[r0:r0 + sub]
        tots.append(jnp.dot(ltri, mb.astype(BF16), preferred_element_type=F32) + carry)
        carry = carry + jnp.sum(mb, axis=0, keepdims=True)
    carry_ref[...] = carry
    tot = jnp.concatenate(tots, axis=0)
    rank1 = jnp.sum(jnp.where(lane == i1, tot, 0.0), axis=-1, keepdims=True).astype(I32)
    rank2 = jnp.sum(jnp.where(lane == i2, tot, 0.0), axis=-1, keepdims=True).astype(I32)

    e1 = (i1 - EXPERT_LANE0).astype(I32)
    e2 = (i2 - EXPERT_LANE0).astype(I32)
    ri = jnp.where(lane == 0, e1, jnp.where(lane == 1, e2, jnp.where(lane == 2, rank1, jnp.where(lane == 3, rank2, 0))))
    ri_ref[...] = ri.T[:F32_SUBLANES]
    rw_ref[...] = jnp.where(lane == 0, w1, jnp.where(lane == 1, w2, 0.0))
    cnt_ref[...] = jnp.broadcast_to(carry, cnt_ref.shape)


def _router(lg, seq):
    t = lg.shape[0]
    tm = min(ROUTER_TILE, seq)
    row = pl.BlockSpec((tm, LANES), lambda i: (i, 0))
    return pl.pallas_call(
        functools.partial(_router_kernel, sub=min(ROW_TILE, tm)),
        grid=(t // tm,),
        in_specs=[row],
        out_specs=[pl.BlockSpec((F32_SUBLANES, tm), lambda i: (0, i)), row,
                   pl.BlockSpec((F32_SUBLANES, LANES), lambda i: (0, 0))],
        out_shape=[jax.ShapeDtypeStruct((F32_SUBLANES, t), I32),
                   jax.ShapeDtypeStruct((t, LANES), F32),
                   jax.ShapeDtypeStruct((F32_SUBLANES, LANES), F32)],
        scratch_shapes=[pltpu.VMEM((1, LANES), F32)],
        compiler_params=_params(("arbitrary",)),
        name="moe_router",
    )(lg)


def _expert_kernel(be_ref, first_ref, slot_ref, nxt_ref, na_ref, xb_ref, wg_hbm, wu_hbm, wd_hbm, yb_ref,
                   wg_buf, wu_buf, wd_buf, wgb_ref, wub_ref, wdb_ref, sem, *, layer):
    i = pl.program_id(0)

    def fetch(e, slot):
        return (pltpu.make_async_copy(wg_hbm.at[layer, e], wg_buf.at[slot], sem.at[slot, 0]),
                pltpu.make_async_copy(wu_hbm.at[layer, e], wu_buf.at[slot], sem.at[slot, 1]),
                pltpu.make_async_copy(wd_hbm.at[layer, e], wd_buf.at[slot], sem.at[slot, 2]))

    @pl.when(i < na_ref[0])
    def _():
        @pl.when(first_ref[i] == 1)
        def _():
            e, slot = be_ref[i], slot_ref[i]

            @pl.when(i == 0)
            def _():
                for cp in fetch(e, slot):
                    cp.start()

            for cp in fetch(e, slot):
                cp.wait()
            wgb_ref[...] = wg_buf[slot].astype(BF16)
            wub_ref[...] = wu_buf[slot].astype(BF16)
            wdb_ref[...] = wd_buf[slot].astype(BF16)

            @pl.when(nxt_ref[i] >= 0)
            def _():
                for cp in fetch(nxt_ref[i], 1 - slot):
                    cp.start()

        xb = _unpack_rows(xb_ref[...]).astype(BF16)
        gate = jnp.dot(xb, wgb_ref[...], preferred_element_type=F32)
        up = jnp.dot(xb, wub_ref[...], preferred_element_type=F32)
        act = (jax.nn.silu(gate) * up).astype(BF16)
        yb_ref[...] = _pack_rows(jnp.dot(act, wdb_ref[...], preferred_element_type=F32))

    @pl.when(i >= na_ref[0])
    def _():
        yb_ref[...] = jnp.zeros_like(yb_ref)


def _experts(plan, xb, w_gate, w_up, w_down, layer, blk):
    rows = xb.shape[0]
    d, de = w_gate.shape[-2:]
    nb = rows // blk

    def row_map(i, be, first, slot, nxt, na):
        return (jnp.minimum(i, jnp.maximum(na[0] - 1, 0)), 0)

    hbm = pl.BlockSpec(memory_space=pl.ANY)
    return pl.pallas_call(
        functools.partial(_expert_kernel, layer=layer),
        grid_spec=pltpu.PrefetchScalarGridSpec(
            num_scalar_prefetch=5,
            grid=(nb,),
            in_specs=[pl.BlockSpec((blk, d // 2), row_map), hbm, hbm, hbm],
            out_specs=pl.BlockSpec((blk, d // 2), lambda i, *_: (i, 0)),
            scratch_shapes=[pltpu.VMEM((2, d, de), F32), pltpu.VMEM((2, d, de), F32), pltpu.VMEM((2, de, d), F32),
                            pltpu.VMEM((d, de), BF16), pltpu.VMEM((d, de), BF16), pltpu.VMEM((de, d), BF16),
                            pltpu.SemaphoreType.DMA((2, 3))]),
        out_shape=jax.ShapeDtypeStruct((rows, d // 2), I32),
        compiler_params=_params(("arbitrary",)),
        name="moe_experts",
    )(*plan, xb, w_gate, w_up, w_down)


def _sc_gather_rows(table, idx):
    b = idx.shape[0]
    d = table.shape[1]
    per_w = b // SC_WORKERS
    ch = SC_CHUNK_ROWS
    mesh = plsc.VectorSubcoreMesh(core_axis_name="c", subcore_axis_name="s")

    n_ch = per_w // ch
    assert per_w * SC_WORKERS == b and n_ch * ch == per_w and n_ch % 2 == 0

    def body(table_hbm, idx_hbm, out_hbm, idx_v, rows_v, gsem, wsem):
        wid = lax.axis_index("s") * SC_CORES + lax.axis_index("c")
        base = wid * per_w
        pltpu.sync_copy(idx_hbm.at[pl.ds(base, per_w)], idx_v)

        def write_back(c, buf):
            off = pl.multiple_of(c * ch, ch)
            return pltpu.make_async_copy(rows_v.at[buf], out_hbm.at[pl.ds(base + off, ch)], wsem.at[buf])

        @pl.loop(0, n_ch, step=2)
        def _(c0):
            for buf in range(2):
                c = c0 + buf

                @pl.when(c0 > 0)
                def _():
                    write_back(c - 2, buf).wait()

                off = pl.multiple_of(c * ch, ch)
                pltpu.async_copy(table_hbm.at[idx_v.at[pl.ds(off, ch)]], rows_v.at[buf], gsem).wait()
                write_back(c, buf).start()

        for buf in range(2):
            write_back(n_ch - 2 + buf, buf).wait()

    return pl.kernel(body, out_type=jax.ShapeDtypeStruct((b, d), table.dtype), mesh=mesh,
                     scratch_types=[pltpu.VMEM((per_w,), I32), pltpu.VMEM((2, ch, d), table.dtype),
                                    pltpu.SemaphoreType.DMA, pltpu.SemaphoreType.DMA((2,))],
                     name="sc_gather")(table, idx)


def _sc_scatter_rows(rows, idx, n_out):
    t, d = rows.shape
    top_k, _, n_ch, ch = idx.shape
    per_w = n_ch * ch
    assert per_w * SC_WORKERS == t and idx.shape[1] == SC_WORKERS
    mesh = plsc.VectorSubcoreMesh(core_axis_name="c", subcore_axis_name="s")

    assert n_ch % 2 == 0

    def body(rows_hbm, idx_hbm, out_hbm, idx_v, rows_v, lsem, ssem):
        wid = lax.axis_index("s") * SC_CORES + lax.axis_index("c")
        base = wid * per_w
        for k in range(top_k):
            pltpu.sync_copy(idx_hbm.at[k, wid], idx_v.at[k])

        def load(c, buf):
            off = pl.multiple_of(c * ch, ch)
            return pltpu.make_async_copy(rows_hbm.at[pl.ds(base + off, ch)], rows_v.at[buf], lsem.at[buf])

        load(0, 0).start()

        @pl.loop(0, n_ch, step=2)
        def _(c0):
            for buf in range(2):
                c = c0 + buf
                load(c, buf).wait()

                @pl.when(c + 1 < n_ch)
                def _():
                    load(c + 1, 1 - buf).start()

                scatters = [pltpu.async_copy(rows_v.at[buf], out_hbm.at[idx_v.at[k, c]], ssem)
                            for k in range(top_k)]
                for cp in scatters:
                    cp.wait()

    return pl.kernel(body, out_type=jax.ShapeDtypeStruct((n_out, d), rows.dtype), mesh=mesh,
                     scratch_types=[pltpu.VMEM((top_k, n_ch, ch), I32), pltpu.VMEM((2, ch, d), rows.dtype),
                                    pltpu.SemaphoreType.DMA((2,)), pltpu.SemaphoreType.DMA],
                     name="sc_scatter")(rows, idx)


def _combine_final_kernel(x_ref, rw_ref, y0_ref, y1_ref, gf_ref, o_ref):
    w = rw_ref[...]
    out = x_ref[...] + w[:, 0:1] * _unpack_rows(y0_ref[...]) + w[:, 1:2] * _unpack_rows(y1_ref[...])
    o_ref[...] = _rms(out, gf_ref[...])


def _combine_final(x2, rw, y2, g_final, seq, part, n_parts):
    t, d = x2.shape
    tm = min(ROW_TILE, seq)
    nt = t // tm // n_parts
    row = lambda w: pl.BlockSpec((tm, w), lambda i: (part * nt + i, 0))
    return pl.pallas_call(
        _combine_final_kernel,
        grid=(nt,),
        in_specs=[row(d), row(LANES), pl.BlockSpec((tm, d // 2), lambda i: (i, 0)),
                  pl.BlockSpec((tm, d // 2), lambda i: (nt + i, 0)), pl.BlockSpec((1, d), lambda i: (0, 0))],
        out_specs=row(d),
        out_shape=jax.ShapeDtypeStruct((t, d), F32),
        input_output_aliases={0: 0},
        compiler_params=_params(("parallel",)),
        name="moe_combine_final",
    )(x2, rw, y2, y2, g_final)


def _route_plan(ri, counts_f, blk, n_blocks):
    cnt = counts_f[0, EXPERT_LANE0:EXPERT_LANE0 + N_EXPERTS].astype(I32)
    nblk = (cnt + blk - 1) // blk
    cum = jnp.cumsum(nblk)
    pad_start = (cum - nblk) * blk
    e = ri[0:TOP_K]
    rank = ri[TOP_K:2 * TOP_K]
    onehot = e[:, :, None] == jnp.arange(N_EXPERTS, dtype=I32)[None, None, :]
    dest = rank + jnp.sum(jnp.where(onehot, pad_start[None, None, :], 0), axis=-1)
    blocks = jnp.arange(n_blocks, dtype=I32)
    block_expert = jnp.minimum(jnp.sum((cum[None, :] <= blocks[:, None]).astype(I32), axis=-1), N_EXPERTS - 1)
    n_active = cum[-1:].astype(I32)
    experts = jnp.arange(N_EXPERTS, dtype=I32)
    is_block_expert = block_expert[:, None] == experts[None, :]
    first = jnp.sum(jnp.where(is_block_expert, ((cum - nblk)[None, :] == blocks[:, None]).astype(I32), 0), axis=-1)
    used = (nblk > 0).astype(I32)
    slot_e = (jnp.cumsum(used) - used) % 2
    later = (experts[None, :] > experts[:, None]) & (nblk[None, :] > 0)
    next_e = jnp.min(jnp.where(later, experts[None, :], N_EXPERTS), axis=-1)
    next_e = jnp.where(next_e == N_EXPERTS, -1, next_e)
    slot = jnp.sum(jnp.where(is_block_expert, slot_e[None, :], 0), axis=-1)
    nxt = jnp.sum(jnp.where(is_block_expert, next_e[None, :], 0), axis=-1)
    plan = (block_expert, first.astype(I32), slot.astype(I32), nxt.astype(I32), n_active)
    return dest.astype(I32), plan


def kernel(x, mem, norm_mix, w_in, lambda_q1, lambda_k1, lambda_q2, lambda_k2, subln, conv_w, w_branch, w_o,
           norm_cross, norm_mem, w_cq, w_ckv, w_co, norm_ffn, w_router_group, w_router_expert, w_exp_gate,
           w_exp_up, w_exp_down, norm_final):
    batch, seq, d = x.shape
    depth = w_in.shape[0]
    n_mem = mem.shape[1]
    t = batch * seq
    dh = d // (2 * DA_HEADS)
    assert 2 * dh == LANES and d % LANES == 0 and w_in.shape[2] == 8 * d
    assert w_router_group.shape[2] == N_GROUPS and w_router_expert.shape[2] == N_EXPERTS
    blk = min(EXPERT_BLOCK, seq)
    n_blocks = (t * TOP_K) // blk + N_EXPERTS
    parts = COMBINE_PARTS if t % (COMBINE_PARTS * SC_WORKERS * SC_CHUNK_ROWS) == 0 else 1
    tokens_per_worker = t // SC_WORKERS

    tabs = _rotary_tables(seq, dh)
    x2 = x.reshape(t, d)
    kv = _mem_kv(mem.reshape(batch * n_mem, d), norm_mem.reshape(1, d), w_ckv.astype(BF16))
    w_r = jnp.concatenate([w_router_group, w_router_expert,
                           jnp.zeros((depth, d, LANES - N_GROUPS - N_EXPERTS), F32)], axis=-1).astype(BF16)

    w_in_bf, w_branch_bf, w_o_bf = w_in.astype(BF16), w_branch.astype(BF16), w_o.astype(BF16)
    w_cq_bf, w_co_bf = w_cq.astype(BF16), w_co.astype(BF16)

    moe_out = None
    for l in range(depth):
        lambda_init = 0.8 - 0.6 * math.exp(-0.3 * l)
        g_mix = norm_mix[l].reshape(1, d)
        if moe_out is None:
            proj = _inproj(x2, g_mix, w_in_bf, l, tabs, seq, dh)
        else:
            proj = None
            for p in range(parts):
                x2, proj = _inproj_combine(x2, moe_out[0], moe_out[1][p], g_mix, w_in_bf, l, tabs, proj, seq, dh,
                                           p, parts)
        y_attn = _attention(proj, lambda_q1[l].reshape(1, dh), lambda_k1[l].reshape(1, dh),
                            lambda_q2[l].reshape(1, dh), lambda_k2[l].reshape(1, dh),
                            subln[l].reshape(1, LANES), batch, seq, d, lambda_init)
        x2 = _mixer_out(x2, y_attn, proj, conv_w, w_branch_bf, w_o_bf, l, seq)
        x2, h_ffn, logits = _cross_attention(x2, norm_cross[l].reshape(1, d), w_cq_bf, kv, l, w_co_bf,
                                             norm_ffn[l].reshape(1, d), w_r, seq, n_mem)
        ri, rw, counts = _router(logits, seq)
        dest, plan = _route_plan(ri, counts, blk, n_blocks)
        scatter_idx = dest.reshape(TOP_K, SC_WORKERS, tokens_per_worker // SC_CHUNK_ROWS, SC_CHUNK_ROWS)
        xb = _sc_scatter_rows(h_ffn, scatter_idx, n_blocks * blk)
        yb = _experts(plan, xb, w_exp_gate, w_exp_up, w_exp_down, l, blk)
        tp = t // parts
        moe_out = (rw, [_sc_gather_rows(yb, dest[:, p * tp:(p + 1) * tp].reshape(-1)) for p in range(parts)])
    for p in range(parts):
        x2 = _combine_final(x2, moe_out[0], moe_out[1][p], norm_final.reshape(1, d), seq, p, parts)
    return x2.reshape(batch, seq, d)
```

```python
import functools
import math

import jax
import jax.numpy as jnp
from jax import lax
from jax.experimental import pallas as pl
from jax.experimental.pallas import tpu as pltpu
from jax.experimental.pallas import tpu_sc as plsc

EPS = 1e-6
DA_HEADS = 8
MEM_HEADS = 4
N_GROUPS = 4
EXPERTS_PER_GROUP = 8
N_EXPERTS = N_GROUPS * EXPERTS_PER_GROUP
TOP_K = 2
ROPE_THETA = 500000.0
LANES = 128
F32_SUBLANES = 8
BF16_SUBLANES = 16
EXPERT_LANE0 = N_GROUPS
ROW_TILE = 512
ROUTER_TILE = 2048
Q_TILE = 2048
EXPERT_BLOCK = 256
PROJ_ROW_CHUNK = 1024
PROJ_COMBINE_TILE = 2048
ATTN_SCORE_CHUNK = 128
ATTN_SOFTMAX_CHUNK = 128
COMBINE_PARTS = 1
SC_CORES = 2
SC_WORKERS = 32
SC_CHUNK_ROWS = 64
VMEM_LIMIT = 56 * 1024 * 1024

F32 = jnp.float32
BF16 = jnp.bfloat16
I32 = jnp.int32
NT_DIMS = (((1,), (1,)), ((), ()))


def _rms(x, g):
    return x * lax.rsqrt(jnp.mean(x * x, axis=-1, keepdims=True) + EPS) * g


def _pack_rows(x):
    half = x.shape[1] // 2
    lo = lax.bitcast_convert_type(x[:, :half].astype(jnp.bfloat16).astype(F32), jnp.uint32)
    hi = lax.bitcast_convert_type(x[:, half:].astype(jnp.bfloat16).astype(F32), jnp.uint32)
    return lax.bitcast_convert_type((hi & jnp.uint32(0xFFFF0000)) | (lo >> 16), I32)


def _unpack_rows(w):
    u = lax.bitcast_convert_type(w, jnp.uint32)
    lo = lax.bitcast_convert_type(u << 16, F32)
    hi = lax.bitcast_convert_type(u & jnp.uint32(0xFFFF0000), F32)
    return jnp.concatenate([lo, hi], axis=1)


def _params(sem, vmem=VMEM_LIMIT):
    return pltpu.CompilerParams(dimension_semantics=sem, vmem_limit_bytes=vmem)


def _inproj_kernel(x_ref, g_ref, w_ref, cos_ref, s1_ref, s2_ref, o_ref, h_ref, *, rc, half, q_scale):
    @pl.when(pl.program_id(1) == 0)
    def _():
        h_ref[...] = _rms(x_ref[...], g_ref[...]).astype(BF16)

    _proj_column_block(h_ref, w_ref, cos_ref, s1_ref, s2_ref, o_ref, rc, half, q_scale)


def _inproj_combine_kernel(x_ref, rw_ref, y0_ref, y1_ref, g_ref, w_ref, cos_ref, s1_ref, s2_ref, *rest,
                           rc, half, q_scale):
    xnew_ref, o_ref, h_ref = rest[-3:]

    @pl.when(pl.program_id(1) == 0)
    def _():
        w = rw_ref[...]
        x_new = x_ref[...] + w[:, 0:1] * _unpack_rows(y0_ref[...]) + w[:, 1:2] * _unpack_rows(y1_ref[...])
        xnew_ref[...] = x_new
        h_ref[...] = _rms(x_new, g_ref[...]).astype(BF16)

    _proj_column_block(h_ref, w_ref, cos_ref, s1_ref, s2_ref, o_ref, rc, half, q_scale)


def _proj_column_block(h_ref, w_ref, cos_ref, s1_ref, s2_ref, o_ref, rc, half, q_scale):
    j = pl.program_id(1)
    tm, tn = o_ref.shape

    def run(rot, scale):
        def body(r, carry):
            r0 = pl.multiple_of(r * rc, rc)
            acc = jnp.dot(h_ref[pl.ds(r0, rc), :], w_ref[...], preferred_element_type=F32)
            if not rot:
                o_ref[pl.ds(r0, rc), :] = acc.astype(o_ref.dtype)
                return carry
            c = cos_ref[pl.ds(r0, rc), :]
            s1 = s1_ref[pl.ds(r0, rc), :]
            s2 = s2_ref[pl.ds(r0, rc), :]
            for cc in range(tn // LANES):
                a = acc[:, cc * LANES:(cc + 1) * LANES]
                a = a * c + pltpu.roll(a, half, 1) * s1 + pltpu.roll(a, LANES - half, 1) * s2
                if scale != 1.0:
                    a = a * scale
                o_ref[pl.ds(r0, rc), cc * LANES:(cc + 1) * LANES] = a.astype(o_ref.dtype)
            return carry
        lax.fori_loop(0, tm // rc, body, 0)

    @pl.when(j == 0)
    def _():
        run(True, q_scale)

    @pl.when(j == 1)
    def _():
        run(True, 1.0)

    @pl.when(j >= 2)
    def _():
        run(False, 1.0)


def _inproj(x2, g, w_bf, layer, tabs, seq, dh):
    t, d = x2.shape
    n = w_bf.shape[2]
    tm, tn = seq, d
    rc = min(PROJ_ROW_CHUNK, tm)
    cos_t, s1_t, s2_t = tabs
    tab_spec = pl.BlockSpec((seq, LANES), lambda i, j: (0, 0))
    return pl.pallas_call(
        functools.partial(_inproj_kernel, rc=rc, half=dh // 8, q_scale=dh ** -0.5 * math.log2(math.e)),
        grid=(t // tm, n // tn),
        in_specs=[pl.BlockSpec((tm, d), lambda i, j: (i, 0)),
                  pl.BlockSpec((1, d), lambda i, j: (0, 0)),
                  pl.BlockSpec((None, d, tn), lambda i, j: (layer, 0, j)),
                  tab_spec, tab_spec, tab_spec],
        out_specs=pl.BlockSpec((tm, tn), lambda i, j: (i, j)),
        out_shape=jax.ShapeDtypeStruct((t, n), BF16),
        scratch_shapes=[pltpu.VMEM((tm, d), BF16)],
        compiler_params=_params(("parallel", "arbitrary")),
        name="inproj",
    )(x2, g, w_bf, cos_t, s1_t, s2_t)


def _inproj_combine(x2, rw, y2, g, w_bf, layer, tabs, proj_prev, seq, dh, part, n_parts):
    t, d = x2.shape
    n = w_bf.shape[2]
    tm, tn = min(PROJ_COMBINE_TILE, seq), d
    ntp = t // n_parts // tm
    row0 = part * ntp
    tiles_per_seq = seq // tm
    cos_t, s1_t, s2_t = tabs
    once = pl.Buffered(1)
    rows = lambda w: pl.BlockSpec((tm, w), lambda i, j: (row0 + i, 0), pipeline_mode=once)
    tab_spec = pl.BlockSpec((tm, LANES), lambda i, j: ((row0 + i) % tiles_per_seq, 0), pipeline_mode=once)
    in_specs = [rows(d), rows(LANES),
                pl.BlockSpec((tm, d // 2), lambda i, j: (i, 0), pipeline_mode=once),
                pl.BlockSpec((tm, d // 2), lambda i, j: (ntp + i, 0), pipeline_mode=once),
                pl.BlockSpec((1, d), lambda i, j: (0, 0)),
                pl.BlockSpec((None, d, tn), lambda i, j: (layer, 0, j)),
                tab_spec, tab_spec, tab_spec]
    args = [x2, rw, y2, y2, g, w_bf, cos_t, s1_t, s2_t]
    aliases = {0: 0}
    if proj_prev is not None:
        in_specs.append(pl.BlockSpec(memory_space=pl.ANY))
        args.append(proj_prev)
        aliases[len(args) - 1] = 1
    return pl.pallas_call(
        functools.partial(_inproj_combine_kernel, rc=min(PROJ_ROW_CHUNK, tm), half=dh // 8,
                          q_scale=dh ** -0.5 * math.log2(math.e)),
        grid=(ntp, n // tn),
        in_specs=in_specs,
        out_specs=[rows(d), pl.BlockSpec((tm, tn), lambda i, j: (row0 + i, j))],
        out_shape=[jax.ShapeDtypeStruct((t, d), F32), jax.ShapeDtypeStruct((t, n), BF16)],
        scratch_shapes=[pltpu.VMEM((tm, d), BF16)],
        input_output_aliases=aliases,
        compiler_params=_params(("parallel", "arbitrary")),
        name="inproj_combine",
    )(*args)


def _rotary_tables(seq, dh):
    rot = dh // 4
    half = rot // 2
    inv = jnp.float32(ROPE_THETA) ** (-jnp.arange(0, rot, 2, dtype=F32) / rot)
    ang = jnp.arange(seq, dtype=F32)[:, None] * inv[None, :]
    cos, sin = jnp.cos(ang), jnp.sin(ang)
    lane = jnp.arange(LANES) % dh
    idx = lane % half
    c_t = jnp.where(lane < rot, cos[:, idx], 1.0)
    s1_t = jnp.where((lane >= half) & (lane < rot), sin[:, idx], 0.0)
    s2_t = jnp.where(lane < half, -sin[:, idx], 0.0)
    return c_t.astype(F32), s1_t.astype(F32), s2_t.astype(F32)


def _attn_kernel(q_ref, k_ref, v_ref, lq1_ref, lk1_ref, lq2_ref, lk2_ref, sub_ref, o_ref,
                 s_ref, p_ref, kt_ref, *, dh, lambda_init, rc, rp):
    tq = q_ref.shape[0]
    n_chunks = tq // rc
    lane = lax.broadcasted_iota(I32, (1, LANES), 1)
    m1 = jnp.where(lane < dh, 1.0, 0.0).astype(BF16)
    m2 = jnp.where(lane >= dh, 1.0, 0.0).astype(BF16)
    lam = (jnp.exp(jnp.sum(lq1_ref[...] * lk1_ref[...], axis=-1, keepdims=True))
           - jnp.exp(jnp.sum(lq2_ref[...] * lk2_ref[...], axis=-1, keepdims=True)) + lambda_init)

    kt_ref[...] = k_ref[...].T

    def scores(c):
        qc = q_ref[c * rc:(c + 1) * rc, :]
        qq = jnp.concatenate([qc * m1, qc * m2], axis=0)
        s_ref[c % 2] = jnp.dot(qq, kt_ref[...], preferred_element_type=F32)

    def softmax_pv(c, j):
        r0 = j * rp
        s1 = s_ref[c % 2, r0:r0 + rp, :]
        s2 = s_ref[c % 2, rc + r0:rc + r0 + rp, :]
        e1 = jnp.exp2(s1 - jnp.max(s1, axis=-1, keepdims=True))
        e2 = jnp.exp2(s2 - jnp.max(s2, axis=-1, keepdims=True))
        l1 = jnp.sum(e1, axis=-1, keepdims=True)
        l2 = jnp.sum(e2, axis=-1, keepdims=True)
        slot = (c * (rc // rp) + j) % 2
        p_ref[slot] = (e1 - e2 * (lam * l1 / l2)).astype(BF16)
        od = jnp.dot(p_ref[slot], v_ref[...], preferred_element_type=F32) / l1
        od = _rms(od, sub_ref[...]) * (1.0 - lambda_init)
        o_ref[c * rc + r0:c * rc + r0 + rp, :] = od.astype(o_ref.dtype)

    scores(0)
    for c in range(n_chunks):
        if c + 1 < n_chunks:
            scores(c + 1)
        for j in range(rc // rp):
            softmax_pv(c, j)


def _attention(proj, lq1, lk1, lq2, lk2, sub, batch, seq, d, lambda_init):
    t = proj.shape[0]
    nh = d // LANES
    dh = LANES // 2
    tq = min(Q_TILE, seq)
    nq = seq // tq
    rc = min(ATTN_SCORE_CHUNK, tq)
    rp = min(ATTN_SOFTMAX_CHUNK, rc)
    vec = pl.BlockSpec((1, dh), lambda b, h, qi: (0, 0))
    return pl.pallas_call(
        functools.partial(_attn_kernel, dh=dh, lambda_init=lambda_init, rc=rc, rp=rp),
        grid=(batch, nh, nq),
        scratch_shapes=[pltpu.VMEM((2, 2 * rc, seq), F32), pltpu.VMEM((2, rp, seq), BF16),
                        pltpu.VMEM((LANES, seq), BF16)],
        in_specs=[pl.BlockSpec((tq, LANES), lambda b, h, qi: (b * nq + qi, h)),
                  pl.BlockSpec((seq, LANES), lambda b, h, qi: (b, nh + h)),
                  pl.BlockSpec((seq, LANES), lambda b, h, qi: (b, 2 * nh + h)),
                  vec, vec, vec, vec,
                  pl.BlockSpec((1, LANES), lambda b, h, qi: (0, 0))],
        out_specs=pl.BlockSpec((tq, LANES), lambda b, h, qi: (b * nq + qi, h)),
        out_shape=jax.ShapeDtypeStruct((t, d), BF16),
        compiler_params=_params(("parallel", "parallel", "arbitrary")),
        name="diff_attn",
    )(proj, proj, proj, lq1, lk1, lq2, lk2, sub)


def _mixout_kernel(x_ref, ya_ref, cb_ref, cc_ref, cx_ref, g0_ref, g1_ref, pc_ref, px_ref, nc_ref, nx_ref,
                   cw_ref, wb0_ref, wb1_ref, wo_ref, o_ref, *, tiles_per_seq):
    i = pl.program_id(0)
    tm = x_ref.shape[0]
    u = cc_ref[...].astype(F32) * cx_ref[...].astype(F32)
    hr = pc_ref.shape[0]
    u_prev = pc_ref[hr - 1:hr, :].astype(F32) * px_ref[hr - 1:hr, :].astype(F32)
    u_next = nc_ref[0:1, :].astype(F32) * nx_ref[0:1, :].astype(F32)
    pos = i % tiles_per_seq
    u_prev = jnp.where(pos == 0, 0.0, u_prev)
    u_next = jnp.where(pos == tiles_per_seq - 1, 0.0, u_next)
    row = lax.broadcasted_iota(I32, u.shape, 0)
    u_up = jnp.where(row == 0, u_prev, pltpu.roll(u, 1, 0))
    u_dn = jnp.where(row == tm - 1, u_next, pltpu.roll(u, tm - 1, 0))
    conv = cw_ref[0:1, :] * u_up + cw_ref[1:2, :] * u + cw_ref[2:3, :] * u_dn
    y_conv = (cb_ref[...].astype(F32) * conv).astype(BF16)
    br0 = jnp.dot(ya_ref[...], wb0_ref[...], preferred_element_type=F32)
    br1 = jnp.dot(y_conv, wb1_ref[...], preferred_element_type=F32)
    z = jax.nn.sigmoid(g0_ref[...].astype(F32)) * br0 + jax.nn.sigmoid(g1_ref[...].astype(F32)) * br1
    o_ref[...] = x_ref[...] + jnp.dot(z.astype(BF16), wo_ref[...], preferred_element_type=F32)


def _mixer_out(x2, y_attn, proj, conv_w, w_branch, w_o, layer, seq):
    t, d = x2.shape
    tm = min(ROW_TILE, seq)
    tps = seq // tm
    hr = BF16_SUBLANES
    nhb = t // hr
    row = lambda c: pl.BlockSpec((tm, d), lambda i: (i, c))
    prev = lambda c: pl.BlockSpec((hr, d), lambda i: (jnp.maximum(i * (tm // hr) - 1, 0), c))
    nxt = lambda c: pl.BlockSpec((hr, d), lambda i: (jnp.minimum((i + 1) * (tm // hr), nhb - 1), c))
    branch = lambda b: pl.BlockSpec((None, None, d, d), lambda i: (layer, b, 0, 0))
    return pl.pallas_call(
        functools.partial(_mixout_kernel, tiles_per_seq=tps),
        grid=(t // tm,),
        in_specs=[row(0), row(0), row(3), row(4), row(5), row(6), row(7),
                  prev(4), prev(5), nxt(4), nxt(5),
                  pl.BlockSpec((None,) + conv_w.shape[1:], lambda i: (layer, 0, 0)),
                  branch(0), branch(1), pl.BlockSpec((None, d, d), lambda i: (layer, 0, 0))],
        out_specs=row(0),
        out_shape=jax.ShapeDtypeStruct((t, d), F32),
        compiler_params=_params(("parallel",)),
        name="mixer_out",
    )(x2, y_attn, proj, proj, proj, proj, proj, proj, proj, proj, proj, conv_w, w_branch, w_branch, w_o)


def _memkv_kernel(m_ref, g_ref, w_ref, o_ref):
    h = _rms(m_ref[...], g_ref[...]).astype(BF16)
    o_ref[...] = jnp.dot(h, w_ref[...], preferred_element_type=F32).astype(o_ref.dtype)


def _mem_kv(mem2, g, w_bf):
    depth, d, n = w_bf.shape
    rows = mem2.shape[0]
    tn = min(n, 1024)
    return pl.pallas_call(
        _memkv_kernel,
        grid=(depth, n // tn),
        in_specs=[pl.BlockSpec((rows, d), lambda l, j: (0, 0)),
                  pl.BlockSpec((1, d), lambda l, j: (0, 0)),
                  pl.BlockSpec((None, d, tn), lambda l, j: (l, 0, j))],
        out_specs=pl.BlockSpec((None, rows, tn), lambda l, j: (l, 0, j)),
        out_shape=jax.ShapeDtypeStruct((depth, rows, n), BF16),
        compiler_params=_params(("parallel", "parallel")),
        name="mem_kv",
    )(mem2, g, w_bf)


def _cross_kernel(x_ref, g_ref, wq_ref, kv_ref, wo_ref, gf_ref, wr_ref, o_ref, hf_ref, lg_ref, *, heads):
    x = x_ref[...]
    d = x.shape[1]
    hd = d // heads
    h = _rms(x, g_ref[...]).astype(BF16)
    q = (jnp.dot(h, wq_ref[...], preferred_element_type=F32) * (hd ** -0.5)).astype(BF16)
    outs = []
    for hh in range(heads):
        qh = q[:, hh * hd:(hh + 1) * hd]
        kh = kv_ref[:, hh * hd:(hh + 1) * hd]
        vh = kv_ref[:, d + hh * hd:d + (hh + 1) * hd]
        s = lax.dot_general(qh, kh, NT_DIMS, preferred_element_type=F32)
        p = jnp.exp(s - jnp.max(s, axis=-1, keepdims=True))
        l = jnp.sum(p, axis=-1, keepdims=True)
        outs.append((jnp.dot(p.astype(BF16), vh, preferred_element_type=F32) / l).astype(BF16))
    o = jnp.concatenate(outs, axis=1)
    x_new = x + jnp.dot(o, wo_ref[...], preferred_element_type=F32)
    o_ref[...] = x_new
    h_ffn = _rms(x_new, gf_ref[...])
    hf_ref[...] = _pack_rows(h_ffn)
    lg_ref[...] = jnp.dot(h_ffn.astype(BF16), wr_ref[...], preferred_element_type=F32)


def _cross_attention(x2, g, wq, kv, layer, wo, g_ffn, w_r, seq, n_mem):
    t, d = x2.shape
    tm = min(ROW_TILE, seq)
    tps = seq // tm
    row = lambda w: pl.BlockSpec((tm, w), lambda i: (i, 0))
    const = lambda r, c: pl.BlockSpec((r, c), lambda i: (0, 0))
    of_layer = lambda r, c: pl.BlockSpec((None, r, c), lambda i: (layer, 0, 0))
    return pl.pallas_call(
        functools.partial(_cross_kernel, heads=MEM_HEADS),
        grid=(t // tm,),
        in_specs=[row(d), const(1, d), of_layer(d, d),
                  pl.BlockSpec((None, n_mem, 2 * d), lambda i: (layer, i // tps, 0)),
                  of_layer(d, d), const(1, d), of_layer(d, LANES)],
        out_specs=[row(d), row(d // 2), row(LANES)],
        out_shape=[jax.ShapeDtypeStruct((t, d), F32), jax.ShapeDtypeStruct((t, d // 2), I32),
                   jax.ShapeDtypeStruct((t, LANES), F32)],
        compiler_params=_params(("parallel",)),
        name="cross_attn",
    )(x2, g, wq, kv, wo, g_ffn, w_r)


def _router_kernel(lg_ref, ri_ref, rw_ref, cnt_ref, carry_ref, *, sub):
    @pl.when(pl.program_id(0) == 0)
    def _():
        carry_ref[...] = jnp.zeros_like(carry_ref)

    lg = lg_ref[...]
    tm = lg.shape[0]
    lane = lax.broadcasted_iota(I32, lg.shape, 1).astype(F32)
    neg = -jnp.inf

    def first_argmax(v):
        m = jnp.max(v, axis=-1, keepdims=True)
        return m, jnp.min(jnp.where(v == m, lane, float(LANES)), axis=-1, keepdims=True)

    gmask = lane < N_GROUPS
    gl = jnp.where(gmask, lg, neg)
    gmax, gidx = first_argmax(gl)
    g_w = 1.0 / jnp.sum(jnp.where(gmask, jnp.exp(gl - gmax), 0.0), axis=-1, keepdims=True)

    lo = EXPERT_LANE0 + gidx * EXPERTS_PER_GROUP
    emask = (lane >= lo) & (lane < lo + EXPERTS_PER_GROUP)
    el = jnp.where(emask, lg, neg)
    mx1, i1 = first_argmax(el)
    mx2, i2 = first_argmax(jnp.where(lane == i1, neg, el))
    esum = jnp.sum(jnp.where(emask, jnp.exp(el - mx1), 0.0), axis=-1, keepdims=True)
    p1 = 1.0 / esum
    p2 = jnp.exp(mx2 - mx1) / esum
    den = p1 + p2
    w1 = g_w * (p1 / den)
    w2 = g_w * (p2 / den)

    memb = jnp.where((lane == i1) | (lane == i2), 1.0, 0.0)
    r_i = lax.broadcasted_iota(I32, (sub, sub), 0)
    c_i = lax.broadcasted_iota(I32, (sub, sub), 1)
    ltri = jnp.where(c_i < r_i, 1.0, 0.0).astype(BF16)
    tots = []
    carry = carry_ref[...]
    for r0 in range(0, tm, sub):
        mb = memb[r0:r0 + sub]
        tots.append(jnp.dot(ltri, mb.astype(BF16), preferred_element_type=F32) + carry)
        carry = carry + jnp.sum(mb, axis=0, keepdims=True)
    carry_ref[...] = carry
    tot = jnp.concatenate(tots, axis=0)
    rank1 = jnp.sum(jnp.where(lane == i1, tot, 0.0), axis=-1, keepdims=True).astype(I32)
    rank2 = jnp.sum(jnp.where(lane == i2, tot, 0.0), axis=-1, keepdims=True).astype(I32)

    e1 = (i1 - EXPERT_LANE0).astype(I32)
    e2 = (i2 - EXPERT_LANE0).astype(I32)
    ri = jnp.where(lane == 0, e1, jnp.where(lane == 1, e2, jnp.where(lane == 2, rank1, jnp.where(lane == 3, rank2, 0))))
    ri_ref[...] = ri.T[:F32_SUBLANES]
    rw_ref[...] = jnp.where(lane == 0, w1, jnp.where(lane == 1, w2, 0.0))
    cnt_ref[...] = jnp.broadcast_to(carry, cnt_ref.shape)


def _router(lg, seq):
    t = lg.shape[0]
    tm = min(ROUTER_TILE, seq)
    row = pl.BlockSpec((tm, LANES), lambda i: (i, 0))
    return pl.pallas_call(
        functools.partial(_router_kernel, sub=min(ROW_TILE, tm)),
        grid=(t // tm,),
        in_specs=[row],
        out_specs=[pl.BlockSpec((F32_SUBLANES, tm), lambda i: (0, i)), row,
                   pl.BlockSpec((F32_SUBLANES, LANES), lambda i: (0, 0))],
        out_shape=[jax.ShapeDtypeStruct((F32_SUBLANES, t), I32),
                   jax.ShapeDtypeStruct((t, LANES), F32),
                   jax.ShapeDtypeStruct((F32_SUBLANES, LANES), F32)],
        scratch_shapes=[pltpu.VMEM((1, LANES), F32)],
        compiler_params=_params(("arbitrary",)),
        name="moe_router",
    )(lg)


def _expert_kernel(be_ref, first_ref, slot_ref, nxt_ref, na_ref, xb_ref, wg_hbm, wu_hbm, wd_hbm, yb_ref,
                   wg_buf, wu_buf, wd_buf, wgb_ref, wub_ref, wdb_ref, sem, *, layer):
    i = pl.program_id(0)

    def fetch(e, slot):
        return (pltpu.make_async_copy(wg_hbm.at[layer, e], wg_buf.at[slot], sem.at[slot, 0]),
                pltpu.make_async_copy(wu_hbm.at[layer, e], wu_buf.at[slot], sem.at[slot, 1]),
                pltpu.make_async_copy(wd_hbm.at[layer, e], wd_buf.at[slot], sem.at[slot, 2]))

    @pl.when(i < na_ref[0])
    def _():
        @pl.when(first_ref[i] == 1)
        def _():
            e, slot = be_ref[i], slot_ref[i]

            @pl.when(i == 0)
            def _():
                for cp in fetch(e, slot):
                    cp.start()

            for cp in fetch(e, slot):
                cp.wait()
            wgb_ref[...] = wg_buf[slot].astype(BF16)
            wub_ref[...] = wu_buf[slot].astype(BF16)
            wdb_ref[...] = wd_buf[slot].astype(BF16)

            @pl.when(nxt_ref[i] >= 0)
            def _():
                for cp in fetch(nxt_ref[i], 1 - slot):
                    cp.start()

        xb = _unpack_rows(xb_ref[...]).astype(BF16)
        gate = jnp.dot(xb, wgb_ref[...], preferred_element_type=F32)
        up = jnp.dot(xb, wub_ref[...], preferred_element_type=F32)
        act = (jax.nn.silu(gate) * up).astype(BF16)
        yb_ref[...] = _pack_rows(jnp.dot(act, wdb_ref[...], preferred_element_type=F32))

    @pl.when(i >= na_ref[0])
    def _():
        yb_ref[...] = jnp.zeros_like(yb_ref)


def _experts(plan, xb, w_gate, w_up, w_down, layer, blk):
    rows = xb.shape[0]
    d, de = w_gate.shape[-2:]
    nb = rows // blk

    def row_map(i, be, first, slot, nxt, na):
        return (jnp.minimum(i, jnp.maximum(na[0] - 1, 0)), 0)

    hbm = pl.BlockSpec(memory_space=pl.ANY)
    return pl.pallas_call(
        functools.partial(_expert_kernel, layer=layer),
        grid_spec=pltpu.PrefetchScalarGridSpec(
            num_scalar_prefetch=5,
            grid=(nb,),
            in_specs=[pl.BlockSpec((blk, d // 2), row_map), hbm, hbm, hbm],
            out_specs=pl.BlockSpec((blk, d // 2), lambda i, *_: (i, 0)),
            scratch_shapes=[pltpu.VMEM((2, d, de), F32), pltpu.VMEM((2, d, de), F32), pltpu.VMEM((2, de, d), F32),
                            pltpu.VMEM((d, de), BF16), pltpu.VMEM((d, de), BF16), pltpu.VMEM((de, d), BF16),
                            pltpu.SemaphoreType.DMA((2, 3))]),
        out_shape=jax.ShapeDtypeStruct((rows, d // 2), I32),
        compiler_params=_params(("arbitrary",)),
        name="moe_experts",
    )(*plan, xb, w_gate, w_up, w_down)


def _sc_gather_rows(table, idx):
    b = idx.shape[0]
    d = table.shape[1]
    per_w = b // SC_WORKERS
    ch = SC_CHUNK_ROWS
    mesh = plsc.VectorSubcoreMesh(core_axis_name="c", subcore_axis_name="s")

    n_ch = per_w // ch
    assert per_w * SC_WORKERS == b and n_ch * ch == per_w and n_ch % 2 == 0

    def body(table_hbm, idx_hbm, out_hbm, idx_v, rows_v, gsem, wsem):
        wid = lax.axis_index("s") * SC_CORES + lax.axis_index("c")
        base = wid * per_w
        pltpu.sync_copy(idx_hbm.at[pl.ds(base, per_w)], idx_v)

        def write_back(c, buf):
            off = pl.multiple_of(c * ch, ch)
            return pltpu.make_async_copy(rows_v.at[buf], out_hbm.at[pl.ds(base + off, ch)], wsem.at[buf])

        @pl.loop(0, n_ch, step=2)
        def _(c0):
            for buf in range(2):
                c = c0 + buf

                @pl.when(c0 > 0)
                def _():
                    write_back(c - 2, buf).wait()

                off = pl.multiple_of(c * ch, ch)
                pltpu.async_copy(table_hbm.at[idx_v.at[pl.ds(off, ch)]], rows_v.at[buf], gsem).wait()
                write_back(c, buf).start()

        for buf in range(2):
            write_back(n_ch - 2 + buf, buf).wait()

    return pl.kernel(body, out_type=jax.ShapeDtypeStruct((b, d), table.dtype), mesh=mesh,
                     scratch_types=[pltpu.VMEM((per_w,), I32), pltpu.VMEM((2, ch, d), table.dtype),
                                    pltpu.SemaphoreType.DMA, pltpu.SemaphoreType.DMA((2,))],
                     name="sc_gather")(table, idx)


def _sc_scatter_rows(rows, idx, n_out):
    t, d = rows.shape
    top_k, _, n_ch, ch = idx.shape
    per_w = n_ch * ch
    assert per_w * SC_WORKERS == t and idx.shape[1] == SC_WORKERS
    mesh = plsc.VectorSubcoreMesh(core_axis_name="c", subcore_axis_name="s")

    assert n_ch % 2 == 0

    def body(rows_hbm, idx_hbm, out_hbm, idx_v, rows_v, lsem, ssem):
        wid = lax.axis_index("s") * SC_CORES + lax.axis_index("c")
        base = wid * per_w
        for k in range(top_k):
            pltpu.sync_copy(idx_hbm.at[k, wid], idx_v.at[k])

        def load(c, buf):
            off = pl.multiple_of(c * ch, ch)
            return pltpu.make_async_copy(rows_hbm.at[pl.ds(base + off, ch)], rows_v.at[buf], lsem.at[buf])

        load(0, 0).start()

        @pl.loop(0, n_ch, step=2)
        def _(c0):
            for buf in range(2):
                c = c0 + buf
                load(c, buf).wait()

                @pl.when(c + 1 < n_ch)
                def _():
                    load(c + 1, 1 - buf).start()

                scatters = [pltpu.async_copy(rows_v.at[buf], out_hbm.at[idx_v.at[k, c]], ssem)
                            for k in range(top_k)]
                for cp in scatters:
                    cp.wait()

    return pl.kernel(body, out_type=jax.ShapeDtypeStruct((n_out, d), rows.dtype), mesh=mesh,
                     scratch_types=[pltpu.VMEM((top_k, n_ch, ch), I32), pltpu.VMEM((2, ch, d), rows.dtype),
                                    pltpu.SemaphoreType.DMA((2,)), pltpu.SemaphoreType.DMA],
                     name="sc_scatter")(rows, idx)


def _combine_final_kernel(x_ref, rw_ref, y0_ref, y1_ref, gf_ref, o_ref):
    w = rw_ref[...]
    out = x_ref[...] + w[:, 0:1] * _unpack_rows(y0_ref[...]) + w[:, 1:2] * _unpack_rows(y1_ref[...])
    o_ref[...] = _rms(out, gf_ref[...])


def _combine_final(x2, rw, y2, g_final, seq, part, n_parts):
    t, d = x2.shape
    tm = min(ROW_TILE, seq)
    nt = t // tm // n_parts
    row = lambda w: pl.BlockSpec((tm, w), lambda i: (part * nt + i, 0))
    return pl.pallas_call(
        _combine_final_kernel,
        grid=(nt,),
        in_specs=[row(d), row(LANES), pl.BlockSpec((tm, d // 2), lambda i: (i, 0)),
                  pl.BlockSpec((tm, d // 2), lambda i: (nt + i, 0)), pl.BlockSpec((1, d), lambda i: (0, 0))],
        out_specs=row(d),
        out_shape=jax.ShapeDtypeStruct((t, d), F32),
        input_output_aliases={0: 0},
        compiler_params=_params(("parallel",)),
        name="moe_combine_final",
    )(x2, rw, y2, y2, g_final)


def _route_plan(ri, counts_f, blk, n_blocks):
    cnt = counts_f[0, EXPERT_LANE0:EXPERT_LANE0 + N_EXPERTS].astype(I32)
    nblk = (cnt + blk - 1) // blk
    cum = jnp.cumsum(nblk)
    pad_start = (cum - nblk) * blk
    e = ri[0:TOP_K]
    rank = ri[TOP_K:2 * TOP_K]
    onehot = e[:, :, None] == jnp.arange(N_EXPERTS, dtype=I32)[None, None, :]
    dest = rank + jnp.sum(jnp.where(onehot, pad_start[None, None, :], 0), axis=-1)
    blocks = jnp.arange(n_blocks, dtype=I32)
    block_expert = jnp.minimum(jnp.sum((cum[None, :] <= blocks[:, None]).astype(I32), axis=-1), N_EXPERTS - 1)
    n_active = cum[-1:].astype(I32)
    experts = jnp.arange(N_EXPERTS, dtype=I32)
    is_block_expert = block_expert[:, None] == experts[None, :]
    first = jnp.sum(jnp.where(is_block_expert, ((cum - nblk)[None, :] == blocks[:, None]).astype(I32), 0), axis=-1)
    used = (nblk > 0).astype(I32)
    slot_e = (jnp.cumsum(used) - used) % 2
    later = (experts[None, :] > experts[:, None]) & (nblk[None, :] > 0)
    next_e = jnp.min(jnp.where(later, experts[None, :], N_EXPERTS), axis=-1)
    next_e = jnp.where(next_e == N_EXPERTS, -1, next_e)
    slot = jnp.sum(jnp.where(is_block_expert, slot_e[None, :], 0), axis=-1)
    nxt = jnp.sum(jnp.where(is_block_expert, next_e[None, :], 0), axis=-1)
    plan = (block_expert, first.astype(I32), slot.astype(I32), nxt.astype(I32), n_active)
    return dest.astype(I32), plan


def kernel(x, mem, norm_mix, w_in, lambda_q1, lambda_k1, lambda_q2, lambda_k2, subln, conv_w, w_branch, w_o,
           norm_cross, norm_mem, w_cq, w_ckv, w_co, norm_ffn, w_router_group, w_router_expert, w_exp_gate,
           w_exp_up, w_exp_down, norm_final):
    batch, seq, d = x.shape
    depth = w_in.shape[0]
    n_mem = mem.shape[1]
    t = batch * seq
    dh = d // (2 * DA_HEADS)
    assert 2 * dh == LANES and d % LANES == 0 and w_in.shape[2] == 8 * d
    assert w_router_group.shape[2] == N_GROUPS and w_router_expert.shape[2] == N_EXPERTS
    blk = min(EXPERT_BLOCK, seq)
    n_blocks = (t * TOP_K) // blk + N_EXPERTS
    parts = COMBINE_PARTS if t % (COMBINE_PARTS * SC_WORKERS * SC_CHUNK_ROWS) == 0 else 1
    tokens_per_worker = t // SC_WORKERS

    tabs = _rotary_tables(seq, dh)
    x2 = x.reshape(t, d)
    kv = _mem_kv(mem.reshape(batch * n_mem, d), norm_mem.reshape(1, d), w_ckv.astype(BF16))
    w_r = jnp.concatenate([w_router_group, w_router_expert,
                           jnp.zeros((depth, d, LANES - N_GROUPS - N_EXPERTS), F32)], axis=-1).astype(BF16)

    w_in_bf, w_branch_bf, w_o_bf = w_in.astype(BF16), w_branch.astype(BF16), w_o.astype(BF16)
    w_cq_bf, w_co_bf = w_cq.astype(BF16), w_co.astype(BF16)

    moe_out = None
    for l in range(depth):
        lambda_init = 0.8 - 0.6 * math.exp(-0.3 * l)
        g_mix = norm_mix[l].reshape(1, d)
        if moe_out is None:
            proj = _inproj(x2, g_mix, w_in_bf, l, tabs, seq, dh)
        else:
            proj = None
            for p in range(parts):
                x2, proj = _inproj_combine(x2, moe_out[0], moe_out[1][p], g_mix, w_in_bf, l, tabs, proj, seq, dh,
                                           p, parts)
        y_attn = _attention(proj, lambda_q1[l].reshape(1, dh), lambda_k1[l].reshape(1, dh),
                            lambda_q2[l].reshape(1, dh), lambda_k2[l].reshape(1, dh),
                            subln[l].reshape(1, LANES), batch, seq, d, lambda_init)
        x2 = _mixer_out(x2, y_attn, proj, conv_w, w_branch_bf, w_o_bf, l, seq)
        x2, h_ffn, logits = _cross_attention(x2, norm_cross[l].reshape(1, d), w_cq_bf, kv, l, w_co_bf,
                                             norm_ffn[l].reshape(1, d), w_r, seq, n_mem)
        ri, rw, counts = _router(logits, seq)
        dest, plan = _route_plan(ri, counts, blk, n_blocks)
        scatter_idx = dest.reshape(TOP_K, SC_WORKERS, tokens_per_worker // SC_CHUNK_ROWS, SC_CHUNK_ROWS)
        xb = _sc_scatter_rows(h_ffn, scatter_idx, n_blocks * blk)
        yb = _experts(plan, xb, w_exp_gate, w_exp_up, w_exp_down, l, blk)
        tp = t // parts
        moe_out = (rw, [_sc_gather_rows(yb, dest[:, p * tp:(p + 1) * tp].reshape(-1)) for p in range(parts)])
    for p in range(parts):
        x2 = _combine_final(x2, moe_out[0], moe_out[1][p], norm_final.reshape(1, d), seq, p, parts)
    return x2.reshape(batch, seq, d)
```

```python
import functools
import math

import jax
import jax.numpy as jnp
from jax import lax
from jax.experimental import pallas as pl
from jax.experimental.pallas import tpu as pltpu
from jax.experimental.pallas import tpu_sc as plsc

EPS = 1e-6
DA_HEADS = 8
MEM_HEADS = 4
N_GROUPS = 4
EXPERTS_PER_GROUP = 8
N_EXPERTS = N_GROUPS * EXPERTS_PER_GROUP
TOP_K = 2
ROPE_THETA = 500000.0
LANES = 128
F32_SUBLANES = 8
BF16_SUBLANES = 16
EXPERT_LANE0 = N_GROUPS
ROW_TILE = 512
CROSS_TILE = 1024
ROUTER_TILE = 2048
Q_TILE = 2048
EXPERT_BLOCK = 256
PROJ_ROW_CHUNK = 1024
PROJ_COMBINE_TILE = 2048
ATTN_SCORE_CHUNK = 128
ATTN_SOFTMAX_CHUNK = 128
COMBINE_PARTS = 2
SC_CORES = 2
SC_SUBCORES = 16
SC_WORKERS = SC_CORES * SC_SUBCORES
SC_CHUNK_ROWS = 64
VMEM_LIMIT = 56 * 1024 * 1024

F32 = jnp.float32
BF16 = jnp.bfloat16
I32 = jnp.int32
NT_DIMS = (((1,), (1,)), ((), ()))


def _rms(x, g):
    return x * lax.rsqrt(jnp.mean(x * x, axis=-1, keepdims=True) + EPS) * g


def _pack_rows(x):
    half = x.shape[1] // 2
    lo = lax.bitcast_convert_type(x[:, :half].astype(jnp.bfloat16).astype(F32), jnp.uint32)
    hi = lax.bitcast_convert_type(x[:, half:].astype(jnp.bfloat16).astype(F32), jnp.uint32)
    return lax.bitcast_convert_type((hi & jnp.uint32(0xFFFF0000)) | (lo >> 16), I32)


def _unpack_rows(w):
    u = lax.bitcast_convert_type(w, jnp.uint32)
    lo = lax.bitcast_convert_type(u << 16, F32)
    hi = lax.bitcast_convert_type(u & jnp.uint32(0xFFFF0000), F32)
    return jnp.concatenate([lo, hi], axis=1)


def _params(sem, vmem=VMEM_LIMIT):
    return pltpu.CompilerParams(dimension_semantics=sem, vmem_limit_bytes=vmem)


def _inproj_kernel(x_ref, g_ref, w_ref, cos_ref, s1_ref, s2_ref, o_ref, h_ref, *, rc, half, q_scale):
    @pl.when(pl.program_id(1) == 0)
    def _():
        h_ref[...] = _rms(x_ref[...], g_ref[...]).astype(BF16)

    _proj_column_block(h_ref, w_ref, cos_ref, s1_ref, s2_ref, o_ref, rc, half, q_scale)


def _inproj_combine_kernel(x_ref, rw_ref, y0_ref, y1_ref, g_ref, w_ref, cos_ref, s1_ref, s2_ref, *rest,
                           rc, half, q_scale):
    xnew_ref, o_ref, h_ref = rest[-3:]

    @pl.when(pl.program_id(1) == 0)
    def _():
        w = rw_ref[...]
        x_new = x_ref[...] + w[:, 0:1] * _unpack_rows(y0_ref[...]) + w[:, 1:2] * _unpack_rows(y1_ref[...])
        xnew_ref[...] = x_new
        h_ref[...] = _rms(x_new, g_ref[...]).astype(BF16)

    _proj_column_block(h_ref, w_ref, cos_ref, s1_ref, s2_ref, o_ref, rc, half, q_scale)


def _proj_column_block(h_ref, w_ref, cos_ref, s1_ref, s2_ref, o_ref, rc, half, q_scale):
    j = pl.program_id(1)
    tm, tn = o_ref.shape

    def run(rot, scale):
        def body(r, carry):
            r0 = pl.multiple_of(r * rc, rc)
            acc = jnp.dot(h_ref[pl.ds(r0, rc), :], w_ref[...], preferred_element_type=F32)
            if not rot:
                o_ref[pl.ds(r0, rc), :] = acc.astype(o_ref.dtype)
                return carry
            c = cos_ref[pl.ds(r0, rc), :]
            s1 = s1_ref[pl.ds(r0, rc), :]
            s2 = s2_ref[pl.ds(r0, rc), :]
            for cc in range(tn // LANES):
                a = acc[:, cc * LANES:(cc + 1) * LANES]
                a = a * c + pltpu.roll(a, half, 1) * s1 + pltpu.roll(a, LANES - half, 1) * s2
                if scale != 1.0:
                    a = a * scale
                o_ref[pl.ds(r0, rc), cc * LANES:(cc + 1) * LANES] = a.astype(o_ref.dtype)
            return carry
        lax.fori_loop(0, tm // rc, body, 0)

    @pl.when(j == 0)
    def _():
        run(True, q_scale)

    @pl.when(j == 1)
    def _():
        run(True, 1.0)

    @pl.when(j >= 2)
    def _():
        run(False, 1.0)


def _inproj(x2, g, w_bf, layer, tabs, seq, dh):
    t, d = x2.shape
    n = w_bf.shape[2]
    tm, tn = seq, d
    rc = min(PROJ_ROW_CHUNK, tm)
    cos_t, s1_t, s2_t = tabs
    tab_spec = pl.BlockSpec((seq, LANES), lambda i, j: (0, 0))
    return pl.pallas_call(
        functools.partial(_inproj_kernel, rc=rc, half=dh // 8, q_scale=dh ** -0.5 * math.log2(math.e)),
        grid=(t // tm, n // tn),
        in_specs=[pl.BlockSpec((tm, d), lambda i, j: (i, 0)),
                  pl.BlockSpec((1, d), lambda i, j: (0, 0)),
                  pl.BlockSpec((None, d, tn), lambda i, j: (layer, 0, j)),
                  tab_spec, tab_spec, tab_spec],
        out_specs=pl.BlockSpec((tm, tn), lambda i, j: (i, j)),
        out_shape=jax.ShapeDtypeStruct((t, n), BF16),
        scratch_shapes=[pltpu.VMEM((tm, d), BF16)],
        compiler_params=_params(("parallel", "arbitrary")),
        name="inproj",
    )(x2, g, w_bf, cos_t, s1_t, s2_t)


def _inproj_combine(x2, rw, y2, g, w_bf, layer, tabs, proj_prev, seq, dh, part, n_parts):
    t, d = x2.shape
    n = w_bf.shape[2]
    tm, tn = min(PROJ_COMBINE_TILE, seq), d
    ntp = t // n_parts // tm
    row0 = part * ntp
    tiles_per_seq = seq // tm
    cos_t, s1_t, s2_t = tabs
    once = pl.Buffered(1)
    rows = lambda w: pl.BlockSpec((tm, w), lambda i, j: (row0 + i, 0), pipeline_mode=once)
    tab_spec = pl.BlockSpec((tm, LANES), lambda i, j: ((row0 + i) % tiles_per_seq, 0), pipeline_mode=once)
    in_specs = [rows(d), rows(LANES),
                pl.BlockSpec((tm, d // 2), lambda i, j: (i, 0), pipeline_mode=once),
                pl.BlockSpec((tm, d // 2), lambda i, j: (ntp + i, 0), pipeline_mode=once),
                pl.BlockSpec((1, d), lambda i, j: (0, 0)),
                pl.BlockSpec((None, d, tn), lambda i, j: (layer, 0, j)),
                tab_spec, tab_spec, tab_spec]
    args = [x2, rw, y2, y2, g, w_bf, cos_t, s1_t, s2_t]
    aliases = {0: 0}
    if proj_prev is not None:
        in_specs.append(pl.BlockSpec(memory_space=pl.ANY))
        args.append(proj_prev)
        aliases[len(args) - 1] = 1
    return pl.pallas_call(
        functools.partial(_inproj_combine_kernel, rc=min(PROJ_ROW_CHUNK, tm), half=dh // 8,
                          q_scale=dh ** -0.5 * math.log2(math.e)),
        grid=(ntp, n // tn),
        in_specs=in_specs,
        out_specs=[rows(d), pl.BlockSpec((tm, tn), lambda i, j: (row0 + i, j))],
        out_shape=[jax.ShapeDtypeStruct((t, d), F32), jax.ShapeDtypeStruct((t, n), BF16)],
        scratch_shapes=[pltpu.VMEM((tm, d), BF16)],
        input_output_aliases=aliases,
        compiler_params=_params(("parallel", "arbitrary")),
        name="inproj_combine",
    )(*args)


def _rotary_tables(seq, dh):
    rot = dh // 4
    half = rot // 2
    inv = jnp.float32(ROPE_THETA) ** (-jnp.arange(0, rot, 2, dtype=F32) / rot)
    ang = jnp.arange(seq, dtype=F32)[:, None] * inv[None, :]
    cos, sin = jnp.cos(ang), jnp.sin(ang)
    lane = jnp.arange(LANES) % dh
    idx = lane % half
    c_t = jnp.where(lane < rot, cos[:, idx], 1.0)
    s1_t = jnp.where((lane >= half) & (lane < rot), sin[:, idx], 0.0)
    s2_t = jnp.where(lane < half, -sin[:, idx], 0.0)
    return c_t.astype(F32), s1_t.astype(F32), s2_t.astype(F32)


def _attn_kernel(q_ref, k_ref, v_ref, lq1_ref, lk1_ref, lq2_ref, lk2_ref, sub_ref, o_ref,
                 s_ref, p_ref, kt_ref, *, dh, lambda_init, rc, rp):
    tq = q_ref.shape[0]
    n_chunks = tq // rc
    lane = lax.broadcasted_iota(I32, (1, LANES), 1)
    m1 = jnp.where(lane < dh, 1.0, 0.0).astype(BF16)
    m2 = jnp.where(lane >= dh, 1.0, 0.0).astype(BF16)
    lam = (jnp.exp(jnp.sum(lq1_ref[...] * lk1_ref[...], axis=-1, keepdims=True))
           - jnp.exp(jnp.sum(lq2_ref[...] * lk2_ref[...], axis=-1, keepdims=True)) + lambda_init)

    kt_ref[...] = k_ref[...].T

    def scores(c):
        qc = q_ref[c * rc:(c + 1) * rc, :]
        qq = jnp.concatenate([qc * m1, qc * m2], axis=0)
        s_ref[c % 2] = jnp.dot(qq, kt_ref[...], preferred_element_type=F32)

    def softmax_pv(c, j):
        r0 = j * rp
        s1 = s_ref[c % 2, r0:r0 + rp, :]
        s2 = s_ref[c % 2, rc + r0:rc + r0 + rp, :]
        e1 = jnp.exp2(s1 - jnp.max(s1, axis=-1, keepdims=True))
        e2 = jnp.exp2(s2 - jnp.max(s2, axis=-1, keepdims=True))
        l1 = jnp.sum(e1, axis=-1, keepdims=True)
        l2 = jnp.sum(e2, axis=-1, keepdims=True)
        slot = (c * (rc // rp) + j) % 2
        p_ref[slot] = (e1 - e2 * (lam * l1 / l2)).astype(BF16)
        od = jnp.dot(p_ref[slot], v_ref[...], preferred_element_type=F32) / l1
        od = _rms(od, sub_ref[...]) * (1.0 - lambda_init)
        o_ref[c * rc + r0:c * rc + r0 + rp, :] = od.astype(o_ref.dtype)

    scores(0)
    for c in range(n_chunks):
        if c + 1 < n_chunks:
            scores(c + 1)
        for j in range(rc // rp):
            softmax_pv(c, j)


def _attention(proj, lq1, lk1, lq2, lk2, sub, batch, seq, d, lambda_init):
    t = proj.shape[0]
    nh = d // LANES
    dh = LANES // 2
    tq = min(Q_TILE, seq)
    nq = seq // tq
    rc = min(ATTN_SCORE_CHUNK, tq)
    rp = min(ATTN_SOFTMAX_CHUNK, rc)
    vec = pl.BlockSpec((1, dh), lambda b, h, qi: (0, 0))
    return pl.pallas_call(
        functools.partial(_attn_kernel, dh=dh, lambda_init=lambda_init, rc=rc, rp=rp),
        grid=(batch, nh, nq),
        scratch_shapes=[pltpu.VMEM((2, 2 * rc, seq), F32), pltpu.VMEM((2, rp, seq), BF16),
                        pltpu.VMEM((LANES, seq), BF16)],
        in_specs=[pl.BlockSpec((tq, LANES), lambda b, h, qi: (b * nq + qi, h)),
                  pl.BlockSpec((seq, LANES), lambda b, h, qi: (b, nh + h)),
                  pl.BlockSpec((seq, LANES), lambda b, h, qi: (b, 2 * nh + h)),
                  vec, vec, vec, vec,
                  pl.BlockSpec((1, LANES), lambda b, h, qi: (0, 0))],
        out_specs=pl.BlockSpec((tq, LANES), lambda b, h, qi: (b * nq + qi, h)),
        out_shape=jax.ShapeDtypeStruct((t, d), BF16),
        compiler_params=_params(("parallel", "parallel", "arbitrary")),
        name="diff_attn",
    )(proj, proj, proj, lq1, lk1, lq2, lk2, sub)


def _mixout_kernel(x_ref, ya_ref, cb_ref, cc_ref, cx_ref, g0_ref, g1_ref, pc_ref, px_ref, nc_ref, nx_ref,
                   cw_ref, wb0_ref, wb1_ref, wo_ref, o_ref, *, tiles_per_seq):
    i = pl.program_id(0)
    tm = x_ref.shape[0]
    u = cc_ref[...].astype(F32) * cx_ref[...].astype(F32)
    hr = pc_ref.shape[0]
    u_prev = pc_ref[hr - 1:hr, :].astype(F32) * px_ref[hr - 1:hr, :].astype(F32)
    u_next = nc_ref[0:1, :].astype(F32) * nx_ref[0:1, :].astype(F32)
    pos = i % tiles_per_seq
    u_prev = jnp.where(pos == 0, 0.0, u_prev)
    u_next = jnp.where(pos == tiles_per_seq - 1, 0.0, u_next)
    row = lax.broadcasted_iota(I32, u.shape, 0)
    u_up = jnp.where(row == 0, u_prev, pltpu.roll(u, 1, 0))
    u_dn = jnp.where(row == tm - 1, u_next, pltpu.roll(u, tm - 1, 0))
    conv = cw_ref[0:1, :] * u_up + cw_ref[1:2, :] * u + cw_ref[2:3, :] * u_dn
    y_conv = (cb_ref[...].astype(F32) * conv).astype(BF16)
    br0 = jnp.dot(ya_ref[...], wb0_ref[...], preferred_element_type=F32)
    br1 = jnp.dot(y_conv, wb1_ref[...], preferred_element_type=F32)
    z = jax.nn.sigmoid(g0_ref[...].astype(F32)) * br0 + jax.nn.sigmoid(g1_ref[...].astype(F32)) * br1
    o_ref[...] = x_ref[...] + jnp.dot(z.astype(BF16), wo_ref[...], preferred_element_type=F32)


def _mixer_out(x2, y_attn, proj, conv_w, w_branch, w_o, layer, seq):
    t, d = x2.shape
    tm = min(ROW_TILE, seq)
    tps = seq // tm
    hr = BF16_SUBLANES
    nhb = t // hr
    row = lambda c: pl.BlockSpec((tm, d), lambda i: (i, c))
    prev = lambda c: pl.BlockSpec((hr, d), lambda i: (jnp.maximum(i * (tm // hr) - 1, 0), c))
    nxt = lambda c: pl.BlockSpec((hr, d), lambda i: (jnp.minimum((i + 1) * (tm // hr), nhb - 1), c))
    branch = lambda b: pl.BlockSpec((None, None, d, d), lambda i: (layer, b, 0, 0))
    return pl.pallas_call(
        functools.partial(_mixout_kernel, tiles_per_seq=tps),
        grid=(t // tm,),
        in_specs=[row(0), row(0), row(3), row(4), row(5), row(6), row(7),
                  prev(4), prev(5), nxt(4), nxt(5),
                  pl.BlockSpec((None,) + conv_w.shape[1:], lambda i: (layer, 0, 0)),
                  branch(0), branch(1), pl.BlockSpec((None, d, d), lambda i: (layer, 0, 0))],
        out_specs=row(0),
        out_shape=jax.ShapeDtypeStruct((t, d), F32),
        compiler_params=_params(("parallel",)),
        name="mixer_out",
    )(x2, y_attn, proj, proj, proj, proj, proj, proj, proj, proj, proj, conv_w, w_branch, w_branch, w_o)


def _memkv_kernel(m_ref, g_ref, w_ref, o_ref, h_ref):
    @pl.when((pl.program_id(0) == 0) & (pl.program_id(1) == 0))
    def _():
        h_ref[...] = _rms(m_ref[...], g_ref[...]).astype(BF16)

    o_ref[...] = jnp.dot(h_ref[...], w_ref[...], preferred_element_type=F32).astype(o_ref.dtype)


def _mem_kv(mem2, g, w_bf):
    depth, d, n = w_bf.shape
    rows = mem2.shape[0]
    tn = min(n, 1024)
    return pl.pallas_call(
        _memkv_kernel,
        grid=(depth, n // tn),
        in_specs=[pl.BlockSpec((rows, d), lambda l, j: (0, 0)),
                  pl.BlockSpec((1, d), lambda l, j: (0, 0)),
                  pl.BlockSpec((None, d, tn), lambda l, j: (l, 0, j))],
        out_specs=pl.BlockSpec((None, rows, tn), lambda l, j: (l, 0, j)),
        out_shape=jax.ShapeDtypeStruct((depth, rows, n), BF16),
        scratch_shapes=[pltpu.VMEM((rows, d), BF16)],
        compiler_params=_params(("arbitrary", "arbitrary")),
        name="mem_kv",
    )(mem2, g, w_bf)


def _cross_kernel(x_ref, g_ref, wq_ref, kv_ref, wo_ref, gf_ref, wr_ref, o_ref, hf_ref, lg_ref, *, heads):
    x = x_ref[...]
    d = x.shape[1]
    hd = d // heads
    h = _rms(x, g_ref[...]).astype(BF16)
    q = (jnp.dot(h, wq_ref[...], preferred_element_type=F32) * (hd ** -0.5)).astype(BF16)
    outs = []
    for hh in range(heads):
        qh = q[:, hh * hd:(hh + 1) * hd]
        kh = kv_ref[:, hh * hd:(hh + 1) * hd]
        vh = kv_ref[:, d + hh * hd:d + (hh + 1) * hd]
        s = lax.dot_general(qh, kh, NT_DIMS, preferred_element_type=F32)
        p = jnp.exp(s - jnp.max(s, axis=-1, keepdims=True))
        l = jnp.sum(p, axis=-1, keepdims=True)
        outs.append((jnp.dot(p.astype(BF16), vh, preferred_element_type=F32) / l).astype(BF16))
    o = jnp.concatenate(outs, axis=1)
    x_new = x + jnp.dot(o, wo_ref[...], preferred_element_type=F32)
    o_ref[...] = x_new
    h_ffn = _rms(x_new, gf_ref[...])
    hf_ref[...] = _pack_rows(h_ffn)
    lg_ref[...] = jnp.dot(h_ffn.astype(BF16), wr_ref[...], preferred_element_type=F32)


def _cross_attention(x2, g, wq, kv, layer, wo, g_ffn, w_r, seq, n_mem):
    t, d = x2.shape
    tm = min(CROSS_TILE, seq)
    tps = seq // tm
    row = lambda w: pl.BlockSpec((tm, w), lambda i: (i, 0))
    const = lambda r, c: pl.BlockSpec((r, c), lambda i: (0, 0))
    of_layer = lambda r, c: pl.BlockSpec((None, r, c), lambda i: (layer, 0, 0))
    return pl.pallas_call(
        functools.partial(_cross_kernel, heads=MEM_HEADS),
        grid=(t // tm,),
        in_specs=[row(d), const(1, d), of_layer(d, d),
                  pl.BlockSpec((None, n_mem, 2 * d), lambda i: (layer, i // tps, 0)),
                  of_layer(d, d), const(1, d), of_layer(d, LANES)],
        out_specs=[row(d), row(d // 2), row(LANES)],
        out_shape=[jax.ShapeDtypeStruct((t, d), F32), jax.ShapeDtypeStruct((t, d // 2), I32),
                   jax.ShapeDtypeStruct((t, LANES), F32)],
        compiler_params=_params(("parallel",)),
        name="cross_attn",
    )(x2, g, wq, kv, wo, g_ffn, w_r)


def _router_kernel(lg_ref, ri_ref, rw_ref, cnt_ref, carry_ref, *, sub):
    @pl.when(pl.program_id(0) == 0)
    def _():
        carry_ref[...] = jnp.zeros_like(carry_ref)

    lg = lg_ref[...]
    tm = lg.shape[0]
    lane = lax.broadcasted_iota(I32, lg.shape, 1).astype(F32)
    neg = -jnp.inf

    def first_argmax(v):
        m = jnp.max(v, axis=-1, keepdims=True)
        return m, jnp.min(jnp.where(v == m, lane, float(LANES)), axis=-1, keepdims=True)

    gmask = lane < N_GROUPS
    gl = jnp.where(gmask, lg, neg)
    gmax, gidx = first_argmax(gl)
    g_w = 1.0 / jnp.sum(jnp.where(gmask, jnp.exp(gl - gmax), 0.0), axis=-1, keepdims=True)

    lo = EXPERT_LANE0 + gidx * EXPERTS_PER_GROUP
    emask = (lane >= lo) & (lane < lo + EXPERTS_PER_GROUP)
    el = jnp.where(emask, lg, neg)
    mx1, i1 = first_argmax(el)
    mx2, i2 = first_argmax(jnp.where(lane == i1, neg, el))
    esum = jnp.sum(jnp.where(emask, jnp.exp(el - mx1), 0.0), axis=-1, keepdims=True)
    p1 = 1.0 / esum
    p2 = jnp.exp(mx2 - mx1) / esum
    den = p1 + p2
    w1 = g_w * (p1 / den)
    w2 = g_w * (p2 / den)

    memb = jnp.where((lane == i1) | (lane == i2), 1.0, 0.0)
    r_i = lax.broadcasted_iota(I32, (sub, sub), 0)
    c_i = lax.broadcasted_iota(I32, (sub, sub), 1)
    ltri = jnp.where(c_i < r_i, 1.0, 0.0).astype(BF16)
    tots = []
    carry = carry_ref[...]
    for r0 in range(0, tm, sub):
        mb = memb[r0:r0 + sub]
        tots.append(jnp.dot(ltri, mb.astype(BF16), preferred_element_type=F32) + carry)
        carry = carry + jnp.sum(mb, axis=0, keepdims=True)
    carry_ref[...] = carry
    tot = jnp.concatenate(tots, axis=0)
    rank1 = jnp.sum(jnp.where(lane == i1, tot, 0.0), axis=-1, keepdims=True).astype(I32)
    rank2 = jnp.sum(jnp.where(lane == i2, tot, 0.0), axis=-1, keepdims=True).astype(I32)

    e1 = (i1 - EXPERT_LANE0).astype(I32)
    e2 = (i2 - EXPERT_LANE0).astype(I32)
    ri = jnp.where(lane == 0, e1, jnp.where(lane == 1, e2, jnp.where(lane == 2, rank1, jnp.where(lane == 3, rank2, 0))))
    ri_ref[...] = ri.T[:F32_SUBLANES]
    rw_ref[...] = jnp.where(lane == 0, w1, jnp.where(lane == 1, w2, 0.0))
    cnt_ref[...] = jnp.broadcast_to(carry, cnt_ref.shape)


def _router(lg, seq):
    t = lg.shape[0]
    tm = min(ROUTER_TILE, seq)
    row = pl.BlockSpec((tm, LANES), lambda i: (i, 0))
    return pl.pallas_call(
        functools.partial(_router_kernel, sub=min(ROW_TILE, tm)),
        grid=(t // tm,),
        in_specs=[row],
        out_specs=[pl.BlockSpec((F32_SUBLANES, tm), lambda i: (0, i)), row,
                   pl.BlockSpec((F32_SUBLANES, LANES), lambda i: (0, 0))],
        out_shape=[jax.ShapeDtypeStruct((F32_SUBLANES, t), I32),
                   jax.ShapeDtypeStruct((t, LANES), F32),
                   jax.ShapeDtypeStruct((F32_SUBLANES, LANES), F32)],
        scratch_shapes=[pltpu.VMEM((1, LANES), F32)],
        compiler_params=_params(("arbitrary",)),
        name="moe_router",
    )(lg)


def _expert_kernel(be_ref, first_ref, slot_ref, nxt_ref, na_ref, xb_ref, wg_hbm, wu_hbm, wd_hbm, yb_ref,
                   wg_buf, wu_buf, wd_buf, wgb_ref, wub_ref, wdb_ref, sem, *, layer):
    i = pl.program_id(0)

    def fetch(e, slot):
        return (pltpu.make_async_copy(wg_hbm.at[layer, e], wg_buf.at[slot], sem.at[slot, 0]),
                pltpu.make_async_copy(wu_hbm.at[layer, e], wu_buf.at[slot], sem.at[slot, 1]),
                pltpu.make_async_copy(wd_hbm.at[layer, e], wd_buf.at[slot], sem.at[slot, 2]))

    @pl.when(i < na_ref[0])
    def _():
        @pl.when(first_ref[i] == 1)
        def _():
            e, slot = be_ref[i], slot_ref[i]

            @pl.when(i == 0)
            def _():
                for cp in fetch(e, slot):
                    cp.start()

            for cp in fetch(e, slot):
                cp.wait()
            wgb_ref[...] = wg_buf[slot].astype(BF16)
            wub_ref[...] = wu_buf[slot].astype(BF16)
            wdb_ref[...] = wd_buf[slot].astype(BF16)

            @pl.when(nxt_ref[i] >= 0)
            def _():
                for cp in fetch(nxt_ref[i], 1 - slot):
                    cp.start()

        xb = _unpack_rows(xb_ref[...]).astype(BF16)
        gate = jnp.dot(xb, wgb_ref[...], preferred_element_type=F32)
        up = jnp.dot(xb, wub_ref[...], preferred_element_type=F32)
        act = (jax.nn.silu(gate) * up).astype(BF16)
        yb_ref[...] = _pack_rows(jnp.dot(act, wdb_ref[...], preferred_element_type=F32))

    @pl.when(i >= na_ref[0])
    def _():
        yb_ref[...] = jnp.zeros_like(yb_ref)


def _experts(plan, xb, w_gate, w_up, w_down, layer, blk):
    rows = xb.shape[0]
    d, de = w_gate.shape[-2:]
    nb = rows // blk

    def row_map(i, be, first, slot, nxt, na):
        return (jnp.minimum(i, jnp.maximum(na[0] - 1, 0)), 0)

    hbm = pl.BlockSpec(memory_space=pl.ANY)
    return pl.pallas_call(
        functools.partial(_expert_kernel, layer=layer),
        grid_spec=pltpu.PrefetchScalarGridSpec(
            num_scalar_prefetch=5,
            grid=(nb,),
            in_specs=[pl.BlockSpec((blk, d // 2), row_map), hbm, hbm, hbm],
            out_specs=pl.BlockSpec((blk, d // 2), lambda i, *_: (i, 0)),
            scratch_shapes=[pltpu.VMEM((2, d, de), F32), pltpu.VMEM((2, d, de), F32), pltpu.VMEM((2, de, d), F32),
                            pltpu.VMEM((d, de), BF16), pltpu.VMEM((d, de), BF16), pltpu.VMEM((de, d), BF16),
                            pltpu.SemaphoreType.DMA((2, 3))]),
        out_shape=jax.ShapeDtypeStruct((rows, d // 2), I32),
        compiler_params=_params(("arbitrary",)),
        name="moe_experts",
    )(*plan, xb, w_gate, w_up, w_down)


def _sc_gather_rows(table, idx):
    b = idx.shape[0]
    d = table.shape[1]
    per_w = b // SC_WORKERS
    ch = SC_CHUNK_ROWS
    mesh = plsc.VectorSubcoreMesh(core_axis_name="c", subcore_axis_name="s")

    n_ch = per_w // ch
    assert per_w * SC_WORKERS == b and n_ch * ch == per_w and n_ch % 2 == 0

    def body(table_hbm, idx_hbm, out_hbm, idx_v, rows_v, gsem, wsem):
        wid = lax.axis_index("s") * SC_CORES + lax.axis_index("c")
        base = wid * per_w
        pltpu.sync_copy(idx_hbm.at[pl.ds(base, per_w)], idx_v)

        def write_back(c, buf):
            off = pl.multiple_of(c * ch, ch)
            return pltpu.make_async_copy(rows_v.at[buf], out_hbm.at[pl.ds(base + off, ch)], wsem.at[buf])

        @pl.loop(0, n_ch, step=2)
        def _(c0):
            for buf in range(2):
                c = c0 + buf

                @pl.when(c0 > 0)
                def _():
                    write_back(c - 2, buf).wait()

                off = pl.multiple_of(c * ch, ch)
                pltpu.async_copy(table_hbm.at[idx_v.at[pl.ds(off, ch)]], rows_v.at[buf], gsem).wait()
                write_back(c, buf).start()

        for buf in range(2):
            write_back(n_ch - 2 + buf, buf).wait()

    return pl.kernel(body, out_type=jax.ShapeDtypeStruct((b, d), table.dtype), mesh=mesh,
                     scratch_types=[pltpu.VMEM((per_w,), I32), pltpu.VMEM((2, ch, d), table.dtype),
                                    pltpu.SemaphoreType.DMA, pltpu.SemaphoreType.DMA((2,))],
                     name="sc_gather")(table, idx)


def _sc_scatter_rows(rows, idx, n_out):
    t, d = rows.shape
    top_k, _, n_ch, ch = idx.shape
    per_w = n_ch * ch
    assert per_w * SC_WORKERS == t and idx.shape[1] == SC_WORKERS
    mesh = plsc.VectorSubcoreMesh(core_axis_name="c", subcore_axis_name="s")

    assert n_ch % 2 == 0

    def body(rows_hbm, idx_hbm, out_hbm, idx_v, rows_v, lsem, ssem):
        wid = lax.axis_index("s") * SC_CORES + lax.axis_index("c")
        base = wid * per_w
        for k in range(top_k):
            pltpu.sync_copy(idx_hbm.at[k, wid], idx_v.at[k])

        def load(c, buf):
            off = pl.multiple_of(c * ch, ch)
            return pltpu.make_async_copy(rows_hbm.at[pl.ds(base + off, ch)], rows_v.at[buf], lsem.at[buf])

        load(0, 0).start()

        @pl.loop(0, n_ch, step=2)
        def _(c0):
            for buf in range(2):
                c = c0 + buf
                load(c, buf).wait()

                @pl.when(c + 1 < n_ch)
                def _():
                    load(c + 1, 1 - buf).start()

                scatters = [pltpu.async_copy(rows_v.at[buf], out_hbm.at[idx_v.at[k, c]], ssem)
                            for k in range(top_k)]
                for cp in scatters:
                    cp.wait()

    return pl.kernel(body, out_type=jax.ShapeDtypeStruct((n_out, d), rows.dtype), mesh=mesh,
                     scratch_types=[pltpu.VMEM((top_k, n_ch, ch), I32), pltpu.VMEM((2, ch, d), rows.dtype),
                                    pltpu.SemaphoreType.DMA((2,)), pltpu.SemaphoreType.DMA],
                     name="sc_scatter")(rows, idx)


def _combine_final_kernel(x_ref, rw_ref, y0_ref, y1_ref, gf_ref, o_ref):
    w = rw_ref[...]
    out = x_ref[...] + w[:, 0:1] * _unpack_rows(y0_ref[...]) + w[:, 1:2] * _unpack_rows(y1_ref[...])
    o_ref[...] = _rms(out, gf_ref[...])


def _combine_final(x2, rw, y2, g_final, seq, part, n_parts):
    t, d = x2.shape
    tm = min(ROW_TILE, seq)
    nt = t // tm // n_parts
    row = lambda w: pl.BlockSpec((tm, w), lambda i: (part * nt + i, 0))
    return pl.pallas_call(
        _combine_final_kernel,
        grid=(nt,),
        in_specs=[row(d), row(LANES), pl.BlockSpec((tm, d // 2), lambda i: (i, 0)),
                  pl.BlockSpec((tm, d // 2), lambda i: (nt + i, 0)), pl.BlockSpec((1, d), lambda i: (0, 0))],
        out_specs=row(d),
        out_shape=jax.ShapeDtypeStruct((t, d), F32),
        input_output_aliases={0: 0},
        compiler_params=_params(("parallel",)),
        name="moe_combine_final",
    )(x2, rw, y2, y2, g_final)


def _route_plan(ri, counts_f, blk, n_blocks):
    cnt = counts_f[0, EXPERT_LANE0:EXPERT_LANE0 + N_EXPERTS].astype(I32)
    nblk = (cnt + blk - 1) // blk
    cum = jnp.cumsum(nblk)
    pad_start = (cum - nblk) * blk
    e = ri[0:TOP_K]
    rank = ri[TOP_K:2 * TOP_K]
    onehot = e[:, :, None] == jnp.arange(N_EXPERTS, dtype=I32)[None, None, :]
    dest = rank + jnp.sum(jnp.where(onehot, pad_start[None, None, :], 0), axis=-1)
    blocks = jnp.arange(n_blocks, dtype=I32)
    block_expert = jnp.minimum(jnp.sum((cum[None, :] <= blocks[:, None]).astype(I32), axis=-1), N_EXPERTS - 1)
    n_active = cum[-1:].astype(I32)
    experts = jnp.arange(N_EXPERTS, dtype=I32)
    is_block_expert = block_expert[:, None] == experts[None, :]
    first = jnp.sum(jnp.where(is_block_expert, ((cum - nblk)[None, :] == blocks[:, None]).astype(I32), 0), axis=-1)
    used = (nblk > 0).astype(I32)
    slot_e = (jnp.cumsum(used) - used) % 2
    later = (experts[None, :] > experts[:, None]) & (nblk[None, :] > 0)
    next_e = jnp.min(jnp.where(later, experts[None, :], N_EXPERTS), axis=-1)
    next_e = jnp.where(next_e == N_EXPERTS, -1, next_e)
    slot = jnp.sum(jnp.where(is_block_expert, slot_e[None, :], 0), axis=-1)
    nxt = jnp.sum(jnp.where(is_block_expert, next_e[None, :], 0), axis=-1)
    plan = (block_expert, first.astype(I32), slot.astype(I32), nxt.astype(I32), n_active)
    return dest.astype(I32), plan


def kernel(x, mem, norm_mix, w_in, lambda_q1, lambda_k1, lambda_q2, lambda_k2, subln, conv_w, w_branch, w_o,
           norm_cross, norm_mem, w_cq, w_ckv, w_co, norm_ffn, w_router_group, w_router_expert, w_exp_gate,
           w_exp_up, w_exp_down, norm_final):
    batch, seq, d = x.shape
    depth = w_in.shape[0]
    n_mem = mem.shape[1]
    t = batch * seq
    dh = d // (2 * DA_HEADS)
    assert 2 * dh == LANES and d % LANES == 0 and w_in.shape[2] == 8 * d
    assert w_router_group.shape[2] == N_GROUPS and w_router_expert.shape[2] == N_EXPERTS
    blk = min(EXPERT_BLOCK, seq)
    n_blocks = (t * TOP_K) // blk + N_EXPERTS
    parts = COMBINE_PARTS if t % (COMBINE_PARTS * SC_WORKERS * SC_CHUNK_ROWS) == 0 else 1
    tokens_per_worker = t // SC_WORKERS

    tabs = _rotary_tables(seq, dh)
    x2 = x.reshape(t, d)
    kv = _mem_kv(mem.reshape(batch * n_mem, d), norm_mem.reshape(1, d), w_ckv.astype(BF16))
    w_r = jnp.concatenate([w_router_group, w_router_expert,
                           jnp.zeros((depth, d, LANES - N_GROUPS - N_EXPERTS), F32)], axis=-1).astype(BF16)

    w_in_bf, w_branch_bf, w_o_bf = w_in.astype(BF16), w_branch.astype(BF16), w_o.astype(BF16)
    w_cq_bf, w_co_bf = w_cq.astype(BF16), w_co.astype(BF16)

    moe_out = None
    for l in range(depth):
        lambda_init = 0.8 - 0.6 * math.exp(-0.3 * l)
        g_mix = norm_mix[l].reshape(1, d)
        if moe_out is None:
            proj = _inproj(x2, g_mix, w_in_bf, l, tabs, seq, dh)
        else:
            proj = None
            for p in range(parts):
                x2, proj = _inproj_combine(x2, moe_out[0], moe_out[1][p], g_mix, w_in_bf, l, tabs, proj, seq, dh,
                                           p, parts)
        y_attn = _attention(proj, lambda_q1[l].reshape(1, dh), lambda_k1[l].reshape(1, dh),
                            lambda_q2[l].reshape(1, dh), lambda_k2[l].reshape(1, dh),
                            subln[l].reshape(1, LANES), batch, seq, d, lambda_init)
        x2 = _mixer_out(x2, y_attn, proj, conv_w, w_branch_bf, w_o_bf, l, seq)
        x2, h_ffn, logits = _cross_attention(x2, norm_cross[l].reshape(1, d), w_cq_bf, kv, l, w_co_bf,
                                             norm_ffn[l].reshape(1, d), w_r, seq, n_mem)
        ri, rw, counts = _router(logits, seq)
        dest, plan = _route_plan(ri, counts, blk, n_blocks)
        scatter_idx = dest.reshape(TOP_K, SC_WORKERS, tokens_per_worker // SC_CHUNK_ROWS, SC_CHUNK_ROWS)
        xb = _sc_scatter_rows(h_ffn, scatter_idx, n_blocks * blk)
        yb = _experts(plan, xb, w_exp_gate, w_exp_up, w_exp_down, l, blk)
        tp = t // parts
        moe_out = (rw, [_sc_gather_rows(yb, dest[:, p * tp:(p + 1) * tp].reshape(-1)) for p in range(parts)])
    for p in range(parts):
        x2 = _combine_final(x2, moe_out[0], moe_out[1][p], norm_final.reshape(1, d), seq, p, parts)
    return x2.reshape(batch, seq, d)
```

```python
import functools
import math

import jax
import jax.numpy as jnp
from jax import lax
from jax.experimental import pallas as pl
from jax.experimental.pallas import tpu as pltpu
from jax.experimental.pallas import tpu_sc as plsc

EPS = 1e-6
DA_HEADS = 8
MEM_HEADS = 4
N_GROUPS = 4
EXPERTS_PER_GROUP = 8
N_EXPERTS = N_GROUPS * EXPERTS_PER_GROUP
TOP_K = 2
ROPE_THETA = 500000.0
LANES = 128
F32_SUBLANES = 8
BF16_SUBLANES = 16
EXPERT_LANE0 = N_GROUPS
ROW_TILE = 512
CROSS_TILE = 1024
ROUTER_TILE = 2048
Q_TILE = 2048
EXPERT_BLOCK = 256
PROJ_ROW_CHUNK = 1024
PROJ_COMBINE_TILE = 2048
ATTN_SCORE_CHUNK = 128
ATTN_SOFTMAX_CHUNK = 128
COMBINE_PARTS = 2
SC_CORES = 2
SC_SUBCORES = 16
SC_WORKERS = SC_CORES * SC_SUBCORES
SC_CHUNK_ROWS = 64
VMEM_LIMIT = 56 * 1024 * 1024

F32 = jnp.float32
BF16 = jnp.bfloat16
I32 = jnp.int32
NT_DIMS = (((1,), (1,)), ((), ()))


def _rms(x, g):
    return x * lax.rsqrt(jnp.mean(x * x, axis=-1, keepdims=True) + EPS) * g


def _pack_rows(x):
    half = x.shape[1] // 2
    lo = lax.bitcast_convert_type(x[:, :half].astype(jnp.bfloat16).astype(F32), jnp.uint32)
    hi = lax.bitcast_convert_type(x[:, half:].astype(jnp.bfloat16).astype(F32), jnp.uint32)
    return lax.bitcast_convert_type((hi & jnp.uint32(0xFFFF0000)) | (lo >> 16), I32)


def _unpack_rows(w):
    u = lax.bitcast_convert_type(w, jnp.uint32)
    lo = lax.bitcast_convert_type(u << 16, F32)
    hi = lax.bitcast_convert_type(u & jnp.uint32(0xFFFF0000), F32)
    return jnp.concatenate([lo, hi], axis=1)


def _params(sem, vmem=VMEM_LIMIT):
    return pltpu.CompilerParams(dimension_semantics=sem, vmem_limit_bytes=vmem)


def _inproj_kernel(x_ref, g_ref, w_ref, cos_ref, s1_ref, s2_ref, o_ref, h_ref, *, rc, half, q_scale):
    @pl.when(pl.program_id(1) == 0)
    def _():
        h_ref[...] = _rms(x_ref[...], g_ref[...]).astype(BF16)

    _proj_column_block(h_ref, w_ref, cos_ref, s1_ref, s2_ref, o_ref, rc, half, q_scale)


def _inproj_combine_kernel(x_ref, rw_ref, y0_ref, y1_ref, g_ref, w_ref, cos_ref, s1_ref, s2_ref, *rest,
                           rc, half, q_scale):
    xnew_ref, o_ref, h_ref = rest[-3:]

    @pl.when(pl.program_id(1) == 0)
    def _():
        w = rw_ref[...]
        x_new = x_ref[...] + w[:, 0:1] * _unpack_rows(y0_ref[...]) + w[:, 1:2] * _unpack_rows(y1_ref[...])
        xnew_ref[...] = x_new
        h_ref[...] = _rms(x_new, g_ref[...]).astype(BF16)

    _proj_column_block(h_ref, w_ref, cos_ref, s1_ref, s2_ref, o_ref, rc, half, q_scale)


def _proj_column_block(h_ref, w_ref, cos_ref, s1_ref, s2_ref, o_ref, rc, half, q_scale):
    j = pl.program_id(1)
    tm, tn = o_ref.shape

    def run(rot, scale):
        def body(r, carry):
            r0 = pl.multiple_of(r * rc, rc)
            acc = jnp.dot(h_ref[pl.ds(r0, rc), :], w_ref[...], preferred_element_type=F32)
            if not rot:
                o_ref[pl.ds(r0, rc), :] = acc.astype(o_ref.dtype)
                return carry
            c = cos_ref[pl.ds(r0, rc), :]
            s1 = s1_ref[pl.ds(r0, rc), :]
            s2 = s2_ref[pl.ds(r0, rc), :]
            for cc in range(tn // LANES):
                a = acc[:, cc * LANES:(cc + 1) * LANES]
                a = a * c + pltpu.roll(a, half, 1) * s1 + pltpu.roll(a, LANES - half, 1) * s2
                if scale != 1.0:
                    a = a * scale
                o_ref[pl.ds(r0, rc), cc * LANES:(cc + 1) * LANES] = a.astype(o_ref.dtype)
            return carry
        lax.fori_loop(0, tm // rc, body, 0)

    @pl.when(j == 0)
    def _():
        run(True, q_scale)

    @pl.when(j == 1)
    def _():
        run(True, 1.0)

    @pl.when(j >= 2)
    def _():
        run(False, 1.0)


def _inproj(x2, g, w_bf, layer, tabs, seq, dh):
    t, d = x2.shape
    n = w_bf.shape[2]
    tm, tn = seq, d
    rc = min(PROJ_ROW_CHUNK, tm)
    cos_t, s1_t, s2_t = tabs
    tab_spec = pl.BlockSpec((seq, LANES), lambda i, j: (0, 0))
    return pl.pallas_call(
        functools.partial(_inproj_kernel, rc=rc, half=dh // 8, q_scale=dh ** -0.5 * math.log2(math.e)),
        grid=(t // tm, n // tn),
        in_specs=[pl.BlockSpec((tm, d), lambda i, j: (i, 0)),
                  pl.BlockSpec((1, d), lambda i, j: (0, 0)),
                  pl.BlockSpec((None, d, tn), lambda i, j: (layer, 0, j)),
                  tab_spec, tab_spec, tab_spec],
        out_specs=pl.BlockSpec((tm, tn), lambda i, j: (i, j)),
        out_shape=jax.ShapeDtypeStruct((t, n), BF16),
        scratch_shapes=[pltpu.VMEM((tm, d), BF16)],
        compiler_params=_params(("parallel", "arbitrary")),
        name="inproj",
    )(x2, g, w_bf, cos_t, s1_t, s2_t)


def _inproj_combine(x2, rw, y2, g, w_bf, layer, tabs, proj_prev, seq, dh, part, n_parts):
    t, d = x2.shape
    n = w_bf.shape[2]
    tm, tn = min(PROJ_COMBINE_TILE, seq), d
    ntp = t // n_parts // tm
    row0 = part * ntp
    tiles_per_seq = seq // tm
    cos_t, s1_t, s2_t = tabs
    once = pl.Buffered(1)
    rows = lambda w: pl.BlockSpec((tm, w), lambda i, j: (row0 + i, 0), pipeline_mode=once)
    tab_spec = pl.BlockSpec((tm, LANES), lambda i, j: ((row0 + i) % tiles_per_seq, 0), pipeline_mode=once)
    in_specs = [rows(d), rows(LANES),
                pl.BlockSpec((tm, d // 2), lambda i, j: (i, 0), pipeline_mode=once),
                pl.BlockSpec((tm, d // 2), lambda i, j: (ntp + i, 0), pipeline_mode=once),
                pl.BlockSpec((1, d), lambda i, j: (0, 0)),
                pl.BlockSpec((None, d, tn), lambda i, j: (layer, 0, j)),
                tab_spec, tab_spec, tab_spec]
    args = [x2, rw, y2, y2, g, w_bf, cos_t, s1_t, s2_t]
    aliases = {0: 0}
    if proj_prev is not None:
        in_specs.append(pl.BlockSpec(memory_space=pl.ANY))
        args.append(proj_prev)
        aliases[len(args) - 1] = 1
    return pl.pallas_call(
        functools.partial(_inproj_combine_kernel, rc=min(PROJ_ROW_CHUNK, tm), half=dh // 8,
                          q_scale=dh ** -0.5 * math.log2(math.e)),
        grid=(ntp, n // tn),
        in_specs=in_specs,
        out_specs=[rows(d), pl.BlockSpec((tm, tn), lambda i, j: (row0 + i, j))],
        out_shape=[jax.ShapeDtypeStruct((t, d), F32), jax.ShapeDtypeStruct((t, n), BF16)],
        scratch_shapes=[pltpu.VMEM((tm, d), BF16)],
        input_output_aliases=aliases,
        compiler_params=_params(("parallel", "arbitrary")),
        name="inproj_combine",
    )(*args)


def _rotary_tables(seq, dh):
    rot = dh // 4
    half = rot // 2
    inv = jnp.float32(ROPE_THETA) ** (-jnp.arange(0, rot, 2, dtype=F32) / rot)
    ang = jnp.arange(seq, dtype=F32)[:, None] * inv[None, :]
    cos, sin = jnp.cos(ang), jnp.sin(ang)
    lane = jnp.arange(LANES) % dh
    idx = lane % half
    c_t = jnp.where(lane < rot, cos[:, idx], 1.0)
    s1_t = jnp.where((lane >= half) & (lane < rot), sin[:, idx], 0.0)
    s2_t = jnp.where(lane < half, -sin[:, idx], 0.0)
    return c_t.astype(F32), s1_t.astype(F32), s2_t.astype(F32)


def _attn_kernel(q_ref, k_ref, v_ref, lq1_ref, lk1_ref, lq2_ref, lk2_ref, sub_ref, o_ref,
                 s_ref, p_ref, kt_ref, *, dh, lambda_init, rc, rp):
    tq = q_ref.shape[0]
    n_chunks = tq // rc
    lane = lax.broadcasted_iota(I32, (1, LANES), 1)
    m1 = jnp.where(lane < dh, 1.0, 0.0).astype(BF16)
    m2 = jnp.where(lane >= dh, 1.0, 0.0).astype(BF16)
    lam = (jnp.exp(jnp.sum(lq1_ref[...] * lk1_ref[...], axis=-1, keepdims=True))
           - jnp.exp(jnp.sum(lq2_ref[...] * lk2_ref[...], axis=-1, keepdims=True)) + lambda_init)

    kt_ref[...] = k_ref[...].T

    def scores(c):
        qc = q_ref[c * rc:(c + 1) * rc, :]
        qq = jnp.concatenate([qc * m1, qc * m2], axis=0)
        s_ref[c % 2] = jnp.dot(qq, kt_ref[...], preferred_element_type=F32)

    def softmax_pv(c, j):
        r0 = j * rp
        s1 = s_ref[c % 2, r0:r0 + rp, :]
        s2 = s_ref[c % 2, rc + r0:rc + r0 + rp, :]
        e1 = jnp.exp2(s1 - jnp.max(s1, axis=-1, keepdims=True))
        e2 = jnp.exp2(s2 - jnp.max(s2, axis=-1, keepdims=True))
        l1 = jnp.sum(e1, axis=-1, keepdims=True)
        l2 = jnp.sum(e2, axis=-1, keepdims=True)
        slot = (c * (rc // rp) + j) % 2
        p_ref[slot] = (e1 - e2 * (lam * l1 / l2)).astype(BF16)
        od = jnp.dot(p_ref[slot], v_ref[...], preferred_element_type=F32) / l1
        od = _rms(od, sub_ref[...]) * (1.0 - lambda_init)
        o_ref[c * rc + r0:c * rc + r0 + rp, :] = od.astype(o_ref.dtype)

    scores(0)
    for c in range(n_chunks):
        if c + 1 < n_chunks:
            scores(c + 1)
        for j in range(rc // rp):
            softmax_pv(c, j)


def _attention(proj, lq1, lk1, lq2, lk2, sub, batch, seq, d, lambda_init):
    t = proj.shape[0]
    nh = d // LANES
    dh = LANES // 2
    tq = min(Q_TILE, seq)
    nq = seq // tq
    rc = min(ATTN_SCORE_CHUNK, tq)
    rp = min(ATTN_SOFTMAX_CHUNK, rc)
    vec = pl.BlockSpec((1, dh), lambda b, h, qi: (0, 0))
    return pl.pallas_call(
        functools.partial(_attn_kernel, dh=dh, lambda_init=lambda_init, rc=rc, rp=rp),
        grid=(batch, nh, nq),
        scratch_shapes=[pltpu.VMEM((2, 2 * rc, seq), F32), pltpu.VMEM((2, rp, seq), BF16),
                        pltpu.VMEM((LANES, seq), BF16)],
        in_specs=[pl.BlockSpec((tq, LANES), lambda b, h, qi: (b * nq + qi, h)),
                  pl.BlockSpec((seq, LANES), lambda b, h, qi: (b, nh + h)),
                  pl.BlockSpec((seq, LANES), lambda b, h, qi: (b, 2 * nh + h)),
                  vec, vec, vec, vec,
                  pl.BlockSpec((1, LANES), lambda b, h, qi: (0, 0))],
        out_specs=pl.BlockSpec((tq, LANES), lambda b, h, qi: (b * nq + qi, h)),
        out_shape=jax.ShapeDtypeStruct((t, d), BF16),
        compiler_params=_params(("parallel", "parallel", "arbitrary")),
        name="diff_attn",
    )(proj, proj, proj, lq1, lk1, lq2, lk2, sub)


def _mixout_kernel(x_ref, ya_ref, cb_ref, cc_ref, cx_ref, g0_ref, g1_ref, pc_ref, px_ref, nc_ref, nx_ref,
                   cw_ref, wb0_ref, wb1_ref, wo_ref, o_ref, *, tiles_per_seq):
    i = pl.program_id(0)
    tm = x_ref.shape[0]
    u = cc_ref[...].astype(F32) * cx_ref[...].astype(F32)
    hr = pc_ref.shape[0]
    u_prev = pc_ref[hr - 1:hr, :].astype(F32) * px_ref[hr - 1:hr, :].astype(F32)
    u_next = nc_ref[0:1, :].astype(F32) * nx_ref[0:1, :].astype(F32)
    pos = i % tiles_per_seq
    u_prev = jnp.where(pos == 0, 0.0, u_prev)
    u_next = jnp.where(pos == tiles_per_seq - 1, 0.0, u_next)
    row = lax.broadcasted_iota(I32, u.shape, 0)
    u_up = jnp.where(row == 0, u_prev, pltpu.roll(u, 1, 0))
    u_dn = jnp.where(row == tm - 1, u_next, pltpu.roll(u, tm - 1, 0))
    conv = cw_ref[0:1, :] * u_up + cw_ref[1:2, :] * u + cw_ref[2:3, :] * u_dn
    y_conv = (cb_ref[...].astype(F32) * conv).astype(BF16)
    br0 = jnp.dot(ya_ref[...], wb0_ref[...], preferred_element_type=F32)
    br1 = jnp.dot(y_conv, wb1_ref[...], preferred_element_type=F32)
    z = jax.nn.sigmoid(g0_ref[...].astype(F32)) * br0 + jax.nn.sigmoid(g1_ref[...].astype(F32)) * br1
    o_ref[...] = x_ref[...] + jnp.dot(z.astype(BF16), wo_ref[...], preferred_element_type=F32)


def _mixer_out(x2, y_attn, proj, conv_w, w_branch, w_o, layer, seq):
    t, d = x2.shape
    tm = min(ROW_TILE, seq)
    tps = seq // tm
    hr = BF16_SUBLANES
    nhb = t // hr
    row = lambda c: pl.BlockSpec((tm, d), lambda i: (i, c))
    prev = lambda c: pl.BlockSpec((hr, d), lambda i: (jnp.maximum(i * (tm // hr) - 1, 0), c))
    nxt = lambda c: pl.BlockSpec((hr, d), lambda i: (jnp.minimum((i + 1) * (tm // hr), nhb - 1), c))
    branch = lambda b: pl.BlockSpec((None, None, d, d), lambda i: (layer, b, 0, 0))
    return pl.pallas_call(
        functools.partial(_mixout_kernel, tiles_per_seq=tps),
        grid=(t // tm,),
        in_specs=[row(0), row(0), row(3), row(4), row(5), row(6), row(7),
                  prev(4), prev(5), nxt(4), nxt(5),
                  pl.BlockSpec((None,) + conv_w.shape[1:], lambda i: (layer, 0, 0)),
                  branch(0), branch(1), pl.BlockSpec((None, d, d), lambda i: (layer, 0, 0))],
        out_specs=row(0),
        out_shape=jax.ShapeDtypeStruct((t, d), F32),
        compiler_params=_params(("parallel",)),
        name="mixer_out",
    )(x2, y_attn, proj, proj, proj, proj, proj, proj, proj, proj, proj, conv_w, w_branch, w_branch, w_o)


def _memkv_kernel(m_ref, g_ref, w_ref, o_ref, h_ref):
    @pl.when((pl.program_id(0) == 0) & (pl.program_id(1) == 0))
    def _():
        h_ref[...] = _rms(m_ref[...], g_ref[...]).astype(BF16)

    o_ref[...] = jnp.dot(h_ref[...], w_ref[...], preferred_element_type=F32).astype(o_ref.dtype)


def _mem_kv(mem2, g, w_bf):
    depth, d, n = w_bf.shape
    rows = mem2.shape[0]
    tn = min(n, 1024)
    return pl.pallas_call(
        _memkv_kernel,
        grid=(depth, n // tn),
        in_specs=[pl.BlockSpec((rows, d), lambda l, j: (0, 0)),
                  pl.BlockSpec((1, d), lambda l, j: (0, 0)),
                  pl.BlockSpec((None, d, tn), lambda l, j: (l, 0, j))],
        out_specs=pl.BlockSpec((None, rows, tn), lambda l, j: (l, 0, j)),
        out_shape=jax.ShapeDtypeStruct((depth, rows, n), BF16),
        scratch_shapes=[pltpu.VMEM((rows, d), BF16)],
        compiler_params=_params(("arbitrary", "arbitrary")),
        name="mem_kv",
    )(mem2, g, w_bf)


def _cross_kernel(x_ref, g_ref, wq_ref, kv_ref, wo_ref, gf_ref, wr_ref, o_ref, hf_ref, lg_ref, *, heads):
    x = x_ref[...]
    d = x.shape[1]
    hd = d // heads
    h = _rms(x, g_ref[...]).astype(BF16)
    q = (jnp.dot(h, wq_ref[...], preferred_element_type=F32) * (hd ** -0.5)).astype(BF16)
    outs = []
    for hh in range(heads):
        qh = q[:, hh * hd:(hh + 1) * hd]
        kh = kv_ref[:, hh * hd:(hh + 1) * hd]
        vh = kv_ref[:, d + hh * hd:d + (hh + 1) * hd]
        s = lax.dot_general(qh, kh, NT_DIMS, preferred_element_type=F32)
        p = jnp.exp(s - jnp.max(s, axis=-1, keepdims=True))
        l = jnp.sum(p, axis=-1, keepdims=True)
        outs.append((jnp.dot(p.astype(BF16), vh, preferred_element_type=F32) / l).astype(BF16))
    o = jnp.concatenate(outs, axis=1)
    x_new = x + jnp.dot(o, wo_ref[...], preferred_element_type=F32)
    o_ref[...] = x_new
    h_ffn = _rms(x_new, gf_ref[...])
    hf_ref[...] = _pack_rows(h_ffn)
    lg_ref[...] = jnp.dot(h_ffn.astype(BF16), wr_ref[...], preferred_element_type=F32)


def _cross_attention(x2, g, wq, kv, kv_layer, wo, g_ffn, w_r, layer, seq, n_mem):
    t, d = x2.shape
    tm = min(CROSS_TILE, seq)
    tps = seq // tm
    row = lambda w: pl.BlockSpec((tm, w), lambda i: (i, 0))
    const = lambda r, c: pl.BlockSpec((r, c), lambda i: (0, 0))
    of_layer = lambda r, c: pl.BlockSpec((None, r, c), lambda i: (layer, 0, 0))
    return pl.pallas_call(
        functools.partial(_cross_kernel, heads=MEM_HEADS),
        grid=(t // tm,),
        in_specs=[row(d), const(1, d), of_layer(d, d),
                  pl.BlockSpec((None, n_mem, 2 * d), lambda i: (kv_layer, i // tps, 0)),
                  of_layer(d, d), const(1, d), of_layer(d, LANES)],
        out_specs=[row(d), row(d // 2), row(LANES)],
        out_shape=[jax.ShapeDtypeStruct((t, d), F32), jax.ShapeDtypeStruct((t, d // 2), I32),
                   jax.ShapeDtypeStruct((t, LANES), F32)],
        compiler_params=_params(("parallel",)),
        name="cross_attn",
    )(x2, g, wq, kv, wo, g_ffn, w_r)


def _router_kernel(lg_ref, ri_ref, rw_ref, cnt_ref, carry_ref, *, sub):
    @pl.when(pl.program_id(0) == 0)
    def _():
        carry_ref[...] = jnp.zeros_like(carry_ref)

    lg = lg_ref[...]
    tm = lg.shape[0]
    lane = lax.broadcasted_iota(I32, lg.shape, 1).astype(F32)
    neg = -jnp.inf

    def first_argmax(v):
        m = jnp.max(v, axis=-1, keepdims=True)
        return m, jnp.min(jnp.where(v == m, lane, float(LANES)), axis=-1, keepdims=True)

    gmask = lane < N_GROUPS
    gl = jnp.where(gmask, lg, neg)
    gmax, gidx = first_argmax(gl)
    g_w = 1.0 / jnp.sum(jnp.where(gmask, jnp.exp(gl - gmax), 0.0), axis=-1, keepdims=True)

    lo = EXPERT_LANE0 + gidx * EXPERTS_PER_GROUP
    emask = (lane >= lo) & (lane < lo + EXPERTS_PER_GROUP)
    el = jnp.where(emask, lg, neg)
    mx1, i1 = first_argmax(el)
    mx2, i2 = first_argmax(jnp.where(lane == i1, neg, el))
    esum = jnp.sum(jnp.where(emask, jnp.exp(el - mx1), 0.0), axis=-1, keepdims=True)
    p1 = 1.0 / esum
    p2 = jnp.exp(mx2 - mx1) / esum
    den = p1 + p2
    w1 = g_w * (p1 / den)
    w2 = g_w * (p2 / den)

    memb = jnp.where((lane == i1) | (lane == i2), 1.0, 0.0)
    r_i = lax.broadcasted_iota(I32, (sub, sub), 0)
    c_i = lax.broadcasted_iota(I32, (sub, sub), 1)
    ltri = jnp.where(c_i < r_i, 1.0, 0.0).astype(BF16)
    tots = []
    carry = carry_ref[...]
    for r0 in range(0, tm, sub):
        mb = memb[r0:r0 + sub]
        tots.append(jnp.dot(ltri, mb.astype(BF16), preferred_element_type=F32) + carry)
        carry = carry + jnp.sum(mb, axis=0, keepdims=True)
    carry_ref[...] = carry
    tot = jnp.concatenate(tots, axis=0)
    rank1 = jnp.sum(jnp.where(lane == i1, tot, 0.0), axis=-1, keepdims=True).astype(I32)
    rank2 = jnp.sum(jnp.where(lane == i2, tot, 0.0), axis=-1, keepdims=True).astype(I32)

    e1 = (i1 - EXPERT_LANE0).astype(I32)
    e2 = (i2 - EXPERT_LANE0).astype(I32)
    ri = jnp.where(lane == 0, e1, jnp.where(lane == 1, e2, jnp.where(lane == 2, rank1, jnp.where(lane == 3, rank2, 0))))
    ri_ref[...] = ri.T[:F32_SUBLANES]
    rw_ref[...] = jnp.where(lane == 0, w1, jnp.where(lane == 1, w2, 0.0))
    cnt_ref[...] = jnp.broadcast_to(carry, cnt_ref.shape)


def _router(lg, seq):
    t = lg.shape[0]
    tm = min(ROUTER_TILE, seq)
    row = pl.BlockSpec((tm, LANES), lambda i: (i, 0))
    return pl.pallas_call(
        functools.partial(_router_kernel, sub=min(ROW_TILE, tm)),
        grid=(t // tm,),
        in_specs=[row],
        out_specs=[pl.BlockSpec((F32_SUBLANES, tm), lambda i: (0, i)), row,
                   pl.BlockSpec((F32_SUBLANES, LANES), lambda i: (0, 0))],
        out_shape=[jax.ShapeDtypeStruct((F32_SUBLANES, t), I32),
                   jax.ShapeDtypeStruct((t, LANES), F32),
                   jax.ShapeDtypeStruct((F32_SUBLANES, LANES), F32)],
        scratch_shapes=[pltpu.VMEM((1, LANES), F32)],
        compiler_params=_params(("arbitrary",)),
        name="moe_router",
    )(lg)


def _expert_kernel(be_ref, first_ref, slot_ref, nxt_ref, na_ref, xb_ref, wg_hbm, wu_hbm, wd_hbm, yb_ref,
                   wg_buf, wu_buf, wd_buf, wgb_ref, wub_ref, wdb_ref, sem, *, layer):
    i = pl.program_id(0)

    def fetch(e, slot):
        return (pltpu.make_async_copy(wg_hbm.at[layer, e], wg_buf.at[slot], sem.at[slot, 0]),
                pltpu.make_async_copy(wu_hbm.at[layer, e], wu_buf.at[slot], sem.at[slot, 1]),
                pltpu.make_async_copy(wd_hbm.at[layer, e], wd_buf.at[slot], sem.at[slot, 2]))

    @pl.when(i < na_ref[0])
    def _():
        @pl.when(first_ref[i] == 1)
        def _():
            e, slot = be_ref[i], slot_ref[i]

            @pl.when(i == 0)
            def _():
                for cp in fetch(e, slot):
                    cp.start()

            for cp in fetch(e, slot):
                cp.wait()
            wgb_ref[...] = wg_buf[slot].astype(BF16)
            wub_ref[...] = wu_buf[slot].astype(BF16)
            wdb_ref[...] = wd_buf[slot].astype(BF16)

            @pl.when(nxt_ref[i] >= 0)
            def _():
                for cp in fetch(nxt_ref[i], 1 - slot):
                    cp.start()

        xb = _unpack_rows(xb_ref[...]).astype(BF16)
        gate = jnp.dot(xb, wgb_ref[...], preferred_element_type=F32)
        up = jnp.dot(xb, wub_ref[...], preferred_element_type=F32)
        act = (jax.nn.silu(gate) * up).astype(BF16)
        yb_ref[...] = _pack_rows(jnp.dot(act, wdb_ref[...], preferred_element_type=F32))

    @pl.when(i >= na_ref[0])
    def _():
        yb_ref[...] = jnp.zeros_like(yb_ref)


def _experts(plan, xb, w_gate, w_up, w_down, layer, blk):
    rows = xb.shape[0]
    d, de = w_gate.shape[-2:]
    nb = rows // blk

    def row_map(i, be, first, slot, nxt, na):
        return (jnp.minimum(i, jnp.maximum(na[0] - 1, 0)), 0)

    hbm = pl.BlockSpec(memory_space=pl.ANY)
    return pl.pallas_call(
        functools.partial(_expert_kernel, layer=layer),
        grid_spec=pltpu.PrefetchScalarGridSpec(
            num_scalar_prefetch=5,
            grid=(nb,),
            in_specs=[pl.BlockSpec((blk, d // 2), row_map), hbm, hbm, hbm],
            out_specs=pl.BlockSpec((blk, d // 2), lambda i, *_: (i, 0)),
            scratch_shapes=[pltpu.VMEM((2, d, de), F32), pltpu.VMEM((2, d, de), F32), pltpu.VMEM((2, de, d), F32),
                            pltpu.VMEM((d, de), BF16), pltpu.VMEM((d, de), BF16), pltpu.VMEM((de, d), BF16),
                            pltpu.SemaphoreType.DMA((2, 3))]),
        out_shape=jax.ShapeDtypeStruct((rows, d // 2), I32),
        compiler_params=_params(("arbitrary",)),
        name="moe_experts",
    )(*plan, xb, w_gate, w_up, w_down)


def _sc_gather_rows(table, idx):
    b = idx.shape[0]
    d = table.shape[1]
    per_w = b // SC_WORKERS
    ch = SC_CHUNK_ROWS
    mesh = plsc.VectorSubcoreMesh(core_axis_name="c", subcore_axis_name="s")

    n_ch = per_w // ch
    assert per_w * SC_WORKERS == b and n_ch * ch == per_w and n_ch % 2 == 0

    def body(table_hbm, idx_hbm, out_hbm, idx_v, rows_v, gsem, wsem):
        wid = lax.axis_index("s") * SC_CORES + lax.axis_index("c")
        base = wid * per_w
        pltpu.sync_copy(idx_hbm.at[pl.ds(base, per_w)], idx_v)

        def write_back(c, buf):
            off = pl.multiple_of(c * ch, ch)
            return pltpu.make_async_copy(rows_v.at[buf], out_hbm.at[pl.ds(base + off, ch)], wsem.at[buf])

        @pl.loop(0, n_ch, step=2)
        def _(c0):
            for buf in range(2):
                c = c0 + buf

                @pl.when(c0 > 0)
                def _():
                    write_back(c - 2, buf).wait()

                off = pl.multiple_of(c * ch, ch)
                pltpu.async_copy(table_hbm.at[idx_v.at[pl.ds(off, ch)]], rows_v.at[buf], gsem).wait()
                write_back(c, buf).start()

        for buf in range(2):
            write_back(n_ch - 2 + buf, buf).wait()

    return pl.kernel(body, out_type=jax.ShapeDtypeStruct((b, d), table.dtype), mesh=mesh,
                     scratch_types=[pltpu.VMEM((per_w,), I32), pltpu.VMEM((2, ch, d), table.dtype),
                                    pltpu.SemaphoreType.DMA, pltpu.SemaphoreType.DMA((2,))],
                     name="sc_gather")(table, idx)


def _sc_scatter_rows(rows, idx, n_out):
    t, d = rows.shape
    top_k, _, n_ch, ch = idx.shape
    per_w = n_ch * ch
    assert per_w * SC_WORKERS == t and idx.shape[1] == SC_WORKERS
    mesh = plsc.VectorSubcoreMesh(core_axis_name="c", subcore_axis_name="s")

    assert n_ch % 2 == 0

    def body(rows_hbm, idx_hbm, out_hbm, idx_v, rows_v, lsem, ssem):
        wid = lax.axis_index("s") * SC_CORES + lax.axis_index("c")
        base = wid * per_w
        for k in range(top_k):
            pltpu.sync_copy(idx_hbm.at[k, wid], idx_v.at[k])

        def load(c, buf):
            off = pl.multiple_of(c * ch, ch)
            return pltpu.make_async_copy(rows_hbm.at[pl.ds(base + off, ch)], rows_v.at[buf], lsem.at[buf])

        load(0, 0).start()

        @pl.loop(0, n_ch, step=2)
        def _(c0):
            for buf in range(2):
                c = c0 + buf
                load(c, buf).wait()

                @pl.when(c + 1 < n_ch)
                def _():
                    load(c + 1, 1 - buf).start()

                scatters = [pltpu.async_copy(rows_v.at[buf], out_hbm.at[idx_v.at[k, c]], ssem)
                            for k in range(top_k)]
                for cp in scatters:
                    cp.wait()

    return pl.kernel(body, out_type=jax.ShapeDtypeStruct((n_out, d), rows.dtype), mesh=mesh,
                     scratch_types=[pltpu.VMEM((top_k, n_ch, ch), I32), pltpu.VMEM((2, ch, d), rows.dtype),
                                    pltpu.SemaphoreType.DMA((2,)), pltpu.SemaphoreType.DMA],
                     name="sc_scatter")(rows, idx)


def _combine_final_kernel(x_ref, rw_ref, y0_ref, y1_ref, gf_ref, o_ref):
    w = rw_ref[...]
    out = x_ref[...] + w[:, 0:1] * _unpack_rows(y0_ref[...]) + w[:, 1:2] * _unpack_rows(y1_ref[...])
    o_ref[...] = _rms(out, gf_ref[...])


def _combine_final(x2, rw, y2, g_final, seq, part, n_parts):
    t, d = x2.shape
    tm = min(ROW_TILE, seq)
    nt = t // tm // n_parts
    row = lambda w: pl.BlockSpec((tm, w), lambda i: (part * nt + i, 0))
    return pl.pallas_call(
        _combine_final_kernel,
        grid=(nt,),
        in_specs=[row(d), row(LANES), pl.BlockSpec((tm, d // 2), lambda i: (i, 0)),
                  pl.BlockSpec((tm, d // 2), lambda i: (nt + i, 0)), pl.BlockSpec((1, d), lambda i: (0, 0))],
        out_specs=row(d),
        out_shape=jax.ShapeDtypeStruct((t, d), F32),
        input_output_aliases={0: 0},
        compiler_params=_params(("parallel",)),
        name="moe_combine_final",
    )(x2, rw, y2, y2, g_final)


def _route_plan(ri, counts_f, blk, n_blocks):
    cnt = counts_f[0, EXPERT_LANE0:EXPERT_LANE0 + N_EXPERTS].astype(I32)
    nblk = (cnt + blk - 1) // blk
    cum = jnp.cumsum(nblk)
    pad_start = (cum - nblk) * blk
    e = ri[0:TOP_K]
    rank = ri[TOP_K:2 * TOP_K]
    onehot = e[:, :, None] == jnp.arange(N_EXPERTS, dtype=I32)[None, None, :]
    dest = rank + jnp.sum(jnp.where(onehot, pad_start[None, None, :], 0), axis=-1)
    blocks = jnp.arange(n_blocks, dtype=I32)
    block_expert = jnp.minimum(jnp.sum((cum[None, :] <= blocks[:, None]).astype(I32), axis=-1), N_EXPERTS - 1)
    n_active = cum[-1:].astype(I32)
    experts = jnp.arange(N_EXPERTS, dtype=I32)
    is_block_expert = block_expert[:, None] == experts[None, :]
    first = jnp.sum(jnp.where(is_block_expert, ((cum - nblk)[None, :] == blocks[:, None]).astype(I32), 0), axis=-1)
    used = (nblk > 0).astype(I32)
    slot_e = (jnp.cumsum(used) - used) % 2
    later = (experts[None, :] > experts[:, None]) & (nblk[None, :] > 0)
    next_e = jnp.min(jnp.where(later, experts[None, :], N_EXPERTS), axis=-1)
    next_e = jnp.where(next_e == N_EXPERTS, -1, next_e)
    slot = jnp.sum(jnp.where(is_block_expert, slot_e[None, :], 0), axis=-1)
    nxt = jnp.sum(jnp.where(is_block_expert, next_e[None, :], 0), axis=-1)
    plan = (block_expert, first.astype(I32), slot.astype(I32), nxt.astype(I32), n_active)
    return dest.astype(I32), plan


def kernel(x, mem, norm_mix, w_in, lambda_q1, lambda_k1, lambda_q2, lambda_k2, subln, conv_w, w_branch, w_o,
           norm_cross, norm_mem, w_cq, w_ckv, w_co, norm_ffn, w_router_group, w_router_expert, w_exp_gate,
           w_exp_up, w_exp_down, norm_final):
    batch, seq, d = x.shape
    depth = w_in.shape[0]
    n_mem = mem.shape[1]
    t = batch * seq
    dh = d // (2 * DA_HEADS)
    assert 2 * dh == LANES and d % LANES == 0 and w_in.shape[2] == 8 * d
    assert w_router_group.shape[2] == N_GROUPS and w_router_expert.shape[2] == N_EXPERTS
    blk = min(EXPERT_BLOCK, seq)
    n_blocks = (t * TOP_K) // blk + N_EXPERTS
    parts = COMBINE_PARTS if t % (COMBINE_PARTS * SC_WORKERS * SC_CHUNK_ROWS) == 0 else 1
    tokens_per_worker = t // SC_WORKERS

    tabs = _rotary_tables(seq, dh)
    x2 = x.reshape(t, d)
    kv = _mem_kv(mem.reshape(batch * n_mem, d), norm_mem.reshape(1, d), w_ckv.astype(BF16))
    w_r = jnp.concatenate([w_router_group, w_router_expert,
                           jnp.zeros((depth, d, LANES - N_GROUPS - N_EXPERTS), F32)], axis=-1).astype(BF16)

    def dense_weights(layer):
        return tuple(w[layer:layer + 1] for w in (w_in, w_branch, w_o, w_cq, w_co))

    weights = tuple(w.astype(BF16) for w in dense_weights(0))
    moe_out = None
    for l in range(depth):
        lambda_init = 0.8 - 0.6 * math.exp(-0.3 * l)
        g_mix = norm_mix[l].reshape(1, d)
        w_in_bf, w_branch_bf, w_o_bf, w_cq_bf, w_co_bf = weights
        if moe_out is None:
            proj = _inproj(x2, g_mix, w_in_bf, 0, tabs, seq, dh)
        else:
            proj = None
            for p in range(parts):
                x2, proj = _inproj_combine(x2, moe_out[0], moe_out[1][p], g_mix, w_in_bf, 0, tabs, proj, seq, dh,
                                           p, parts)
        y_attn = _attention(proj, lambda_q1[l].reshape(1, dh), lambda_k1[l].reshape(1, dh),
                            lambda_q2[l].reshape(1, dh), lambda_k2[l].reshape(1, dh),
                            subln[l].reshape(1, LANES), batch, seq, d, lambda_init)
        x2 = _mixer_out(x2, y_attn, proj, conv_w[l:l + 1], w_branch_bf, w_o_bf, 0, seq)
        x2, h_ffn, logits = _cross_attention(x2, norm_cross[l].reshape(1, d), w_cq_bf, kv, l, w_co_bf,
                                             norm_ffn[l].reshape(1, d), w_r[l:l + 1], 0, seq, n_mem)
        ri, rw, counts = _router(logits, seq)
        dest, plan = _route_plan(ri, counts, blk, n_blocks)
        if l + 1 < depth:
            nxt, dest = lax.optimization_barrier((dense_weights(l + 1), dest))
            weights = tuple(w.astype(BF16) for w in nxt)
        scatter_idx = dest.reshape(TOP_K, SC_WORKERS, tokens_per_worker // SC_CHUNK_ROWS, SC_CHUNK_ROWS)
        xb = _sc_scatter_rows(h_ffn, scatter_idx, n_blocks * blk)
        yb = _experts(plan, xb, w_exp_gate, w_exp_up, w_exp_down, l, blk)
        tp = t // parts
        moe_out = (rw, [_sc_gather_rows(yb, dest[:, p * tp:(p + 1) * tp].reshape(-1)) for p in range(parts)])
    for p in range(parts):
        x2 = _combine_final(x2, moe_out[0], moe_out[1][p], norm_final.reshape(1, d), seq, p, parts)
    return x2.reshape(batch, seq, d)
```

```python
import functools
import math

import jax
import jax.numpy as jnp
from jax import lax
from jax.experimental import pallas as pl
from jax.experimental.pallas import tpu as pltpu
from jax.experimental.pallas import tpu_sc as plsc

EPS = 1e-6
DA_HEADS = 8
MEM_HEADS = 4
N_GROUPS = 4
EXPERTS_PER_GROUP = 8
N_EXPERTS = N_GROUPS * EXPERTS_PER_GROUP
TOP_K = 2
ROPE_THETA = 500000.0
LANES = 128
F32_SUBLANES = 8
BF16_SUBLANES = 16
EXPERT_LANE0 = N_GROUPS
ROW_TILE = 512
CROSS_TILE = 1024
ROUTER_TILE = 2048
Q_TILE = 2048
EXPERT_BLOCK = 256
CAST_ROWS = 128
PROJ_ROW_CHUNK = 1024
PROJ_COMBINE_TILE = 2048
ATTN_SCORE_CHUNK = 128
ATTN_SOFTMAX_CHUNK = 128
COMBINE_PARTS = 2
SC_CORES = 2
SC_SUBCORES = 16
SC_WORKERS = SC_CORES * SC_SUBCORES
SC_CHUNK_ROWS = 64
VMEM_LIMIT = 56 * 1024 * 1024

F32 = jnp.float32
BF16 = jnp.bfloat16
I32 = jnp.int32
NT_DIMS = (((1,), (1,)), ((), ()))


def _rms(x, g):
    return x * lax.rsqrt(jnp.mean(x * x, axis=-1, keepdims=True) + EPS) * g


def _pack_rows(x):
    half = x.shape[1] // 2
    lo = lax.bitcast_convert_type(x[:, :half].astype(jnp.bfloat16).astype(F32), jnp.uint32)
    hi = lax.bitcast_convert_type(x[:, half:].astype(jnp.bfloat16).astype(F32), jnp.uint32)
    return lax.bitcast_convert_type((hi & jnp.uint32(0xFFFF0000)) | (lo >> 16), I32)


def _unpack_rows(w):
    u = lax.bitcast_convert_type(w, jnp.uint32)
    lo = lax.bitcast_convert_type(u << 16, F32)
    hi = lax.bitcast_convert_type(u & jnp.uint32(0xFFFF0000), F32)
    return jnp.concatenate([lo, hi], axis=1)


def _params(sem, vmem=VMEM_LIMIT):
    return pltpu.CompilerParams(dimension_semantics=sem, vmem_limit_bytes=vmem)


def _inproj_kernel(x_ref, g_ref, w_ref, cos_ref, s1_ref, s2_ref, o_ref, h_ref, *, rc, half, q_scale):
    @pl.when(pl.program_id(1) == 0)
    def _():
        h_ref[...] = _rms(x_ref[...], g_ref[...]).astype(BF16)

    _proj_column_block(h_ref, w_ref, cos_ref, s1_ref, s2_ref, o_ref, rc, half, q_scale)


def _inproj_combine_kernel(x_ref, rw_ref, y0_ref, y1_ref, g_ref, w_ref, cos_ref, s1_ref, s2_ref, *rest,
                           rc, half, q_scale):
    xnew_ref, o_ref, h_ref = rest[-3:]

    @pl.when(pl.program_id(1) == 0)
    def _():
        w = rw_ref[...]
        x_new = x_ref[...] + w[:, 0:1] * _unpack_rows(y0_ref[...]) + w[:, 1:2] * _unpack_rows(y1_ref[...])
        xnew_ref[...] = x_new
        h_ref[...] = _rms(x_new, g_ref[...]).astype(BF16)

    _proj_column_block(h_ref, w_ref, cos_ref, s1_ref, s2_ref, o_ref, rc, half, q_scale)


def _proj_column_block(h_ref, w_ref, cos_ref, s1_ref, s2_ref, o_ref, rc, half, q_scale):
    j = pl.program_id(1)
    tm, tn = o_ref.shape

    def run(rot, scale):
        def body(r, carry):
            r0 = pl.multiple_of(r * rc, rc)
            acc = jnp.dot(h_ref[pl.ds(r0, rc), :], w_ref[...], preferred_element_type=F32)
            if not rot:
                o_ref[pl.ds(r0, rc), :] = acc.astype(o_ref.dtype)
                return carry
            c = cos_ref[pl.ds(r0, rc), :]
            s1 = s1_ref[pl.ds(r0, rc), :]
            s2 = s2_ref[pl.ds(r0, rc), :]
            for cc in range(tn // LANES):
                a = acc[:, cc * LANES:(cc + 1) * LANES]
                a = a * c + pltpu.roll(a, half, 1) * s1 + pltpu.roll(a, LANES - half, 1) * s2
                if scale != 1.0:
                    a = a * scale
                o_ref[pl.ds(r0, rc), cc * LANES:(cc + 1) * LANES] = a.astype(o_ref.dtype)
            return carry
        lax.fori_loop(0, tm // rc, body, 0)

    @pl.when(j == 0)
    def _():
        run(True, q_scale)

    @pl.when(j == 1)
    def _():
        run(True, 1.0)

    @pl.when(j >= 2)
    def _():
        run(False, 1.0)


def _inproj(x2, g, w_bf, layer, tabs, seq, dh):
    t, d = x2.shape
    n = w_bf.shape[2]
    tm, tn = seq, d
    rc = min(PROJ_ROW_CHUNK, tm)
    cos_t, s1_t, s2_t = tabs
    tab_spec = pl.BlockSpec((seq, LANES), lambda i, j: (0, 0))
    return pl.pallas_call(
        functools.partial(_inproj_kernel, rc=rc, half=dh // 8, q_scale=dh ** -0.5 * math.log2(math.e)),
        grid=(t // tm, n // tn),
        in_specs=[pl.BlockSpec((tm, d), lambda i, j: (i, 0)),
                  pl.BlockSpec((1, d), lambda i, j: (0, 0)),
                  pl.BlockSpec((None, d, tn), lambda i, j: (layer, 0, j)),
                  tab_spec, tab_spec, tab_spec],
        out_specs=pl.BlockSpec((tm, tn), lambda i, j: (i, j)),
        out_shape=jax.ShapeDtypeStruct((t, n), BF16),
        scratch_shapes=[pltpu.VMEM((tm, d), BF16)],
        compiler_params=_params(("parallel", "arbitrary")),
        name="inproj",
    )(x2, g, w_bf, cos_t, s1_t, s2_t)


def _inproj_combine(x2, rw, y2, g, w_bf, layer, tabs, proj_prev, seq, dh, part, n_parts):
    t, d = x2.shape
    n = w_bf.shape[2]
    tm, tn = min(PROJ_COMBINE_TILE, seq), d
    ntp = t // n_parts // tm
    row0 = part * ntp
    tiles_per_seq = seq // tm
    cos_t, s1_t, s2_t = tabs
    once = pl.Buffered(1)
    rows = lambda w: pl.BlockSpec((tm, w), lambda i, j: (row0 + i, 0), pipeline_mode=once)
    tab_spec = pl.BlockSpec((tm, LANES), lambda i, j: ((row0 + i) % tiles_per_seq, 0), pipeline_mode=once)
    in_specs = [rows(d), rows(LANES),
                pl.BlockSpec((tm, d // 2), lambda i, j: (i, 0), pipeline_mode=once),
                pl.BlockSpec((tm, d // 2), lambda i, j: (ntp + i, 0), pipeline_mode=once),
                pl.BlockSpec((1, d), lambda i, j: (0, 0)),
                pl.BlockSpec((None, d, tn), lambda i, j: (layer, 0, j)),
                tab_spec, tab_spec, tab_spec]
    args = [x2, rw, y2, y2, g, w_bf, cos_t, s1_t, s2_t]
    aliases = {0: 0}
    if proj_prev is not None:
        in_specs.append(pl.BlockSpec(memory_space=pl.ANY))
        args.append(proj_prev)
        aliases[len(args) - 1] = 1
    return pl.pallas_call(
        functools.partial(_inproj_combine_kernel, rc=min(PROJ_ROW_CHUNK, tm), half=dh // 8,
                          q_scale=dh ** -0.5 * math.log2(math.e)),
        grid=(ntp, n // tn),
        in_specs=in_specs,
        out_specs=[rows(d), pl.BlockSpec((tm, tn), lambda i, j: (row0 + i, j))],
        out_shape=[jax.ShapeDtypeStruct((t, d), F32), jax.ShapeDtypeStruct((t, n), BF16)],
        scratch_shapes=[pltpu.VMEM((tm, d), BF16)],
        input_output_aliases=aliases,
        compiler_params=_params(("parallel", "arbitrary")),
        name="inproj_combine",
    )(*args)


def _rotary_tables(seq, dh):
    rot = dh // 4
    half = rot // 2
    inv = jnp.float32(ROPE_THETA) ** (-jnp.arange(0, rot, 2, dtype=F32) / rot)
    ang = jnp.arange(seq, dtype=F32)[:, None] * inv[None, :]
    cos, sin = jnp.cos(ang), jnp.sin(ang)
    lane = jnp.arange(LANES) % dh
    idx = lane % half
    c_t = jnp.where(lane < rot, cos[:, idx], 1.0)
    s1_t = jnp.where((lane >= half) & (lane < rot), sin[:, idx], 0.0)
    s2_t = jnp.where(lane < half, -sin[:, idx], 0.0)
    return c_t.astype(F32), s1_t.astype(F32), s2_t.astype(F32)


def _attn_kernel(q_ref, k_ref, v_ref, lq1_ref, lk1_ref, lq2_ref, lk2_ref, sub_ref, o_ref,
                 s_ref, p_ref, kt_ref, *, dh, lambda_init, rc, rp):
    tq = q_ref.shape[0]
    n_chunks = tq // rc
    lane = lax.broadcasted_iota(I32, (1, LANES), 1)
    m1 = jnp.where(lane < dh, 1.0, 0.0).astype(BF16)
    m2 = jnp.where(lane >= dh, 1.0, 0.0).astype(BF16)
    lam = (jnp.exp(jnp.sum(lq1_ref[...] * lk1_ref[...], axis=-1, keepdims=True))
           - jnp.exp(jnp.sum(lq2_ref[...] * lk2_ref[...], axis=-1, keepdims=True)) + lambda_init)

    kt_ref[...] = k_ref[...].T

    def scores(c):
        qc = q_ref[c * rc:(c + 1) * rc, :]
        qq = jnp.concatenate([qc * m1, qc * m2], axis=0)
        s_ref[c % 2] = jnp.dot(qq, kt_ref[...], preferred_element_type=F32)

    def softmax_pv(c, j):
        r0 = j * rp
        s1 = s_ref[c % 2, r0:r0 + rp, :]
        s2 = s_ref[c % 2, rc + r0:rc + r0 + rp, :]
        e1 = jnp.exp2(s1 - jnp.max(s1, axis=-1, keepdims=True))
        e2 = jnp.exp2(s2 - jnp.max(s2, axis=-1, keepdims=True))
        l1 = jnp.sum(e1, axis=-1, keepdims=True)
        l2 = jnp.sum(e2, axis=-1, keepdims=True)
        slot = (c * (rc // rp) + j) % 2
        p_ref[slot] = (e1 - e2 * (lam * l1 / l2)).astype(BF16)
        od = jnp.dot(p_ref[slot], v_ref[...], preferred_element_type=F32) / l1
        od = _rms(od, sub_ref[...]) * (1.0 - lambda_init)
        o_ref[c * rc + r0:c * rc + r0 + rp, :] = od.astype(o_ref.dtype)

    scores(0)
    for c in range(n_chunks):
        if c + 1 < n_chunks:
            scores(c + 1)
        for j in range(rc // rp):
            softmax_pv(c, j)


def _attention(proj, lq1, lk1, lq2, lk2, sub, batch, seq, d, lambda_init):
    t = proj.shape[0]
    nh = d // LANES
    dh = LANES // 2
    tq = min(Q_TILE, seq)
    nq = seq // tq
    rc = min(ATTN_SCORE_CHUNK, tq)
    rp = min(ATTN_SOFTMAX_CHUNK, rc)
    vec = pl.BlockSpec((1, dh), lambda b, h, qi: (0, 0))
    return pl.pallas_call(
        functools.partial(_attn_kernel, dh=dh, lambda_init=lambda_init, rc=rc, rp=rp),
        grid=(batch, nh, nq),
        scratch_shapes=[pltpu.VMEM((2, 2 * rc, seq), F32), pltpu.VMEM((2, rp, seq), BF16),
                        pltpu.VMEM((LANES, seq), BF16)],
        in_specs=[pl.BlockSpec((tq, LANES), lambda b, h, qi: (b * nq + qi, h)),
                  pl.BlockSpec((seq, LANES), lambda b, h, qi: (b, nh + h)),
                  pl.BlockSpec((seq, LANES), lambda b, h, qi: (b, 2 * nh + h)),
                  vec, vec, vec, vec,
                  pl.BlockSpec((1, LANES), lambda b, h, qi: (0, 0))],
        out_specs=pl.BlockSpec((tq, LANES), lambda b, h, qi: (b * nq + qi, h)),
        out_shape=jax.ShapeDtypeStruct((t, d), BF16),
        compiler_params=_params(("parallel", "parallel", "arbitrary")),
        name="diff_attn",
    )(proj, proj, proj, lq1, lk1, lq2, lk2, sub)


def _mixout_kernel(x_ref, ya_ref, cb_ref, cc_ref, cx_ref, g0_ref, g1_ref, pc_ref, px_ref, nc_ref, nx_ref,
                   cw_ref, wb0_ref, wb1_ref, wo_ref, o_ref, *, tiles_per_seq):
    i = pl.program_id(0)
    tm = x_ref.shape[0]
    u = cc_ref[...].astype(F32) * cx_ref[...].astype(F32)
    hr = pc_ref.shape[0]
    u_prev = pc_ref[hr - 1:hr, :].astype(F32) * px_ref[hr - 1:hr, :].astype(F32)
    u_next = nc_ref[0:1, :].astype(F32) * nx_ref[0:1, :].astype(F32)
    pos = i % tiles_per_seq
    u_prev = jnp.where(pos == 0, 0.0, u_prev)
    u_next = jnp.where(pos == tiles_per_seq - 1, 0.0, u_next)
    row = lax.broadcasted_iota(I32, u.shape, 0)
    u_up = jnp.where(row == 0, u_prev, pltpu.roll(u, 1, 0))
    u_dn = jnp.where(row == tm - 1, u_next, pltpu.roll(u, tm - 1, 0))
    conv = cw_ref[0:1, :] * u_up + cw_ref[1:2, :] * u + cw_ref[2:3, :] * u_dn
    y_conv = (cb_ref[...].astype(F32) * conv).astype(BF16)
    br0 = jnp.dot(ya_ref[...], wb0_ref[...], preferred_element_type=F32)
    br1 = jnp.dot(y_conv, wb1_ref[...], preferred_element_type=F32)
    z = jax.nn.sigmoid(g0_ref[...].astype(F32)) * br0 + jax.nn.sigmoid(g1_ref[...].astype(F32)) * br1
    o_ref[...] = x_ref[...] + jnp.dot(z.astype(BF16), wo_ref[...], preferred_element_type=F32)


def _mixer_out(x2, y_attn, proj, conv_w, w_branch, w_o, layer, seq):
    t, d = x2.shape
    tm = min(ROW_TILE, seq)
    tps = seq // tm
    hr = BF16_SUBLANES
    nhb = t // hr
    row = lambda c: pl.BlockSpec((tm, d), lambda i: (i, c))
    prev = lambda c: pl.BlockSpec((hr, d), lambda i: (jnp.maximum(i * (tm // hr) - 1, 0), c))
    nxt = lambda c: pl.BlockSpec((hr, d), lambda i: (jnp.minimum((i + 1) * (tm // hr), nhb - 1), c))
    branch = lambda b: pl.BlockSpec((None, None, d, d), lambda i: (layer, b, 0, 0))
    return pl.pallas_call(
        functools.partial(_mixout_kernel, tiles_per_seq=tps),
        grid=(t // tm,),
        in_specs=[row(0), row(0), row(3), row(4), row(5), row(6), row(7),
                  prev(4), prev(5), nxt(4), nxt(5),
                  pl.BlockSpec((None,) + conv_w.shape[1:], lambda i: (layer, 0, 0)),
                  branch(0), branch(1), pl.BlockSpec((None, d, d), lambda i: (layer, 0, 0))],
        out_specs=row(0),
        out_shape=jax.ShapeDtypeStruct((t, d), F32),
        compiler_params=_params(("parallel",)),
        name="mixer_out",
    )(x2, y_attn, proj, proj, proj, proj, proj, proj, proj, proj, proj, conv_w, w_branch, w_branch, w_o)


def _memkv_kernel(m_ref, g_ref, w_ref, o_ref, h_ref):
    @pl.when((pl.program_id(0) == 0) & (pl.program_id(1) == 0))
    def _():
        h_ref[...] = _rms(m_ref[...], g_ref[...]).astype(BF16)

    o_ref[...] = jnp.dot(h_ref[...], w_ref[...], preferred_element_type=F32).astype(o_ref.dtype)


def _mem_kv(mem2, g, w_bf):
    depth, d, n = w_bf.shape
    rows = mem2.shape[0]
    tn = min(n, 1024)
    return pl.pallas_call(
        _memkv_kernel,
        grid=(depth, n // tn),
        in_specs=[pl.BlockSpec((rows, d), lambda l, j: (0, 0)),
                  pl.BlockSpec((1, d), lambda l, j: (0, 0)),
                  pl.BlockSpec((None, d, tn), lambda l, j: (l, 0, j))],
        out_specs=pl.BlockSpec((None, rows, tn), lambda l, j: (l, 0, j)),
        out_shape=jax.ShapeDtypeStruct((depth, rows, n), BF16),
        scratch_shapes=[pltpu.VMEM((rows, d), BF16)],
        compiler_params=_params(("arbitrary", "arbitrary")),
        name="mem_kv",
    )(mem2, g, w_bf)


def _cross_kernel(x_ref, g_ref, wq_ref, kv_ref, wo_ref, gf_ref, wr_ref, o_ref, hf_ref, lg_ref, *, heads):
    x = x_ref[...]
    d = x.shape[1]
    hd = d // heads
    h = _rms(x, g_ref[...]).astype(BF16)
    q = (jnp.dot(h, wq_ref[...], preferred_element_type=F32) * (hd ** -0.5)).astype(BF16)
    outs = []
    for hh in range(heads):
        qh = q[:, hh * hd:(hh + 1) * hd]
        kh = kv_ref[:, hh * hd:(hh + 1) * hd]
        vh = kv_ref[:, d + hh * hd:d + (hh + 1) * hd]
        s = lax.dot_general(qh, kh, NT_DIMS, preferred_element_type=F32)
        p = jnp.exp(s - jnp.max(s, axis=-1, keepdims=True))
        l = jnp.sum(p, axis=-1, keepdims=True)
        outs.append((jnp.dot(p.astype(BF16), vh, preferred_element_type=F32) / l).astype(BF16))
    o = jnp.concatenate(outs, axis=1)
    x_new = x + jnp.dot(o, wo_ref[...], preferred_element_type=F32)
    o_ref[...] = x_new
    h_ffn = _rms(x_new, gf_ref[...])
    hf_ref[...] = _pack_rows(h_ffn)
    lg_ref[...] = jnp.dot(h_ffn.astype(BF16), wr_ref[...], preferred_element_type=F32)


def _cross_attention(x2, g, wq, kv, layer, wo, g_ffn, w_r, seq, n_mem):
    t, d = x2.shape
    tm = min(CROSS_TILE, seq)
    tps = seq // tm
    row = lambda w: pl.BlockSpec((tm, w), lambda i: (i, 0))
    const = lambda r, c: pl.BlockSpec((r, c), lambda i: (0, 0))
    of_layer = lambda r, c: pl.BlockSpec((None, r, c), lambda i: (layer, 0, 0))
    return pl.pallas_call(
        functools.partial(_cross_kernel, heads=MEM_HEADS),
        grid=(t // tm,),
        in_specs=[row(d), const(1, d), of_layer(d, d),
                  pl.BlockSpec((None, n_mem, 2 * d), lambda i: (layer, i // tps, 0)),
                  of_layer(d, d), const(1, d), of_layer(d, LANES)],
        out_specs=[row(d), row(d // 2), row(LANES)],
        out_shape=[jax.ShapeDtypeStruct((t, d), F32), jax.ShapeDtypeStruct((t, d // 2), I32),
                   jax.ShapeDtypeStruct((t, LANES), F32)],
        compiler_params=_params(("parallel",)),
        name="cross_attn",
    )(x2, g, wq, kv, wo, g_ffn, w_r)


def _router_kernel(lg_ref, ri_ref, rw_ref, cnt_ref, carry_ref, *, sub):
    @pl.when(pl.program_id(0) == 0)
    def _():
        carry_ref[...] = jnp.zeros_like(carry_ref)

    lg = lg_ref[...]
    tm = lg.shape[0]
    lane = lax.broadcasted_iota(I32, lg.shape, 1).astype(F32)
    neg = -jnp.inf

    def first_argmax(v):
        m = jnp.max(v, axis=-1, keepdims=True)
        return m, jnp.min(jnp.where(v == m, lane, float(LANES)), axis=-1, keepdims=True)

    gmask = lane < N_GROUPS
    gl = jnp.where(gmask, lg, neg)
    gmax, gidx = first_argmax(gl)
    g_w = 1.0 / jnp.sum(jnp.where(gmask, jnp.exp(gl - gmax), 0.0), axis=-1, keepdims=True)

    lo = EXPERT_LANE0 + gidx * EXPERTS_PER_GROUP
    emask = (lane >= lo) & (lane < lo + EXPERTS_PER_GROUP)
    el = jnp.where(emask, lg, neg)
    mx1, i1 = first_argmax(el)
    mx2, i2 = first_argmax(jnp.where(lane == i1, neg, el))
    esum = jnp.sum(jnp.where(emask, jnp.exp(el - mx1), 0.0), axis=-1, keepdims=True)
    p1 = 1.0 / esum
    p2 = jnp.exp(mx2 - mx1) / esum
    den = p1 + p2
    w1 = g_w * (p1 / den)
    w2 = g_w * (p2 / den)

    memb = jnp.where((lane == i1) | (lane == i2), 1.0, 0.0)
    r_i = lax.broadcasted_iota(I32, (sub, sub), 0)
    c_i = lax.broadcasted_iota(I32, (sub, sub), 1)
    ltri = jnp.where(c_i < r_i, 1.0, 0.0).astype(BF16)
    tots = []
    carry = carry_ref[...]
    for r0 in range(0, tm, sub):
        mb = memb[r0:r0 + sub]
        tots.append(jnp.dot(ltri, mb.astype(BF16), preferred_element_type=F32) + carry)
        carry = carry + jnp.sum(mb, axis=0, keepdims=True)
    carry_ref[...] = carry
    tot = jnp.concatenate(tots, axis=0)
    rank1 = jnp.sum(jnp.where(lane == i1, tot, 0.0), axis=-1, keepdims=True).astype(I32)
    rank2 = jnp.sum(jnp.where(lane == i2, tot, 0.0), axis=-1, keepdims=True).astype(I32)

    e1 = (i1 - EXPERT_LANE0).astype(I32)
    e2 = (i2 - EXPERT_LANE0).astype(I32)
    ri = jnp.where(lane == 0, e1, jnp.where(lane == 1, e2, jnp.where(lane == 2, rank1, jnp.where(lane == 3, rank2, 0))))
    ri_ref[...] = ri.T[:F32_SUBLANES]
    rw_ref[...] = jnp.where(lane == 0, w1, jnp.where(lane == 1, w2, 0.0))
    cnt_ref[...] = jnp.broadcast_to(carry, cnt_ref.shape)


def _router(lg, seq):
    t = lg.shape[0]
    tm = min(ROUTER_TILE, seq)
    row = pl.BlockSpec((tm, LANES), lambda i: (i, 0))
    return pl.pallas_call(
        functools.partial(_router_kernel, sub=min(ROW_TILE, tm)),
        grid=(t // tm,),
        in_specs=[row],
        out_specs=[pl.BlockSpec((F32_SUBLANES, tm), lambda i: (0, i)), row,
                   pl.BlockSpec((F32_SUBLANES, LANES), lambda i: (0, 0))],
        out_shape=[jax.ShapeDtypeStruct((F32_SUBLANES, t), I32),
                   jax.ShapeDtypeStruct((t, LANES), F32),
                   jax.ShapeDtypeStruct((F32_SUBLANES, LANES), F32)],
        scratch_shapes=[pltpu.VMEM((1, LANES), F32)],
        compiler_params=_params(("arbitrary",)),
        name="moe_router",
    )(lg)


def _expert_kernel(be_ref, first_ref, slot_ref, nxt_ref, na_ref, xb_ref, wg_hbm, wu_hbm, wd_hbm, yb_ref,
                   wg_buf, wu_buf, wd_buf, wgb_ref, wub_ref, wdb_ref, sem, *, layer):
    i = pl.program_id(0)

    def fetch(e, slot):
        return (pltpu.make_async_copy(wg_hbm.at[layer, e], wg_buf.at[slot], sem.at[slot, 0]),
                pltpu.make_async_copy(wu_hbm.at[layer, e], wu_buf.at[slot], sem.at[slot, 1]),
                pltpu.make_async_copy(wd_hbm.at[layer, e], wd_buf.at[slot], sem.at[slot, 2]))

    @pl.when(i < na_ref[0])
    def _():
        @pl.when(first_ref[i] == 1)
        def _():
            e, slot = be_ref[i], slot_ref[i]

            @pl.when(i == 0)
            def _():
                for cp in fetch(e, slot):
                    cp.start()

            for cp in fetch(e, slot):
                cp.wait()
            def cast(dst, src):
                def body(r, carry):
                    r0 = pl.multiple_of(r * CAST_ROWS, CAST_ROWS)
                    dst[pl.ds(r0, CAST_ROWS), :] = src[slot, pl.ds(r0, CAST_ROWS), :].astype(BF16)
                    return carry
                lax.fori_loop(0, src.shape[1] // CAST_ROWS, body, 0)

            cast(wgb_ref, wg_buf)
            cast(wub_ref, wu_buf)
            cast(wdb_ref, wd_buf)

            @pl.when(nxt_ref[i] >= 0)
            def _():
                for cp in fetch(nxt_ref[i], 1 - slot):
                    cp.start()

        xb = _unpack_rows(xb_ref[...]).astype(BF16)
        gate = jnp.dot(xb, wgb_ref[...], preferred_element_type=F32)
        up = jnp.dot(xb, wub_ref[...], preferred_element_type=F32)
        act = (jax.nn.silu(gate) * up).astype(BF16)
        yb_ref[...] = _pack_rows(jnp.dot(act, wdb_ref[...], preferred_element_type=F32))

    @pl.when(i >= na_ref[0])
    def _():
        yb_ref[...] = jnp.zeros_like(yb_ref)


def _experts(plan, xb, w_gate, w_up, w_down, layer, blk):
    rows = xb.shape[0]
    d, de = w_gate.shape[-2:]
    nb = rows // blk

    def row_map(i, be, first, slot, nxt, na):
        return (jnp.minimum(i, jnp.maximum(na[0] - 1, 0)), 0)

    hbm = pl.BlockSpec(memory_space=pl.ANY)
    return pl.pallas_call(
        functools.partial(_expert_kernel, layer=layer),
        grid_spec=pltpu.PrefetchScalarGridSpec(
            num_scalar_prefetch=5,
            grid=(nb,),
            in_specs=[pl.BlockSpec((blk, d // 2), row_map), hbm, hbm, hbm],
            out_specs=pl.BlockSpec((blk, d // 2), lambda i, *_: (i, 0)),
            scratch_shapes=[pltpu.VMEM((2, d, de), F32), pltpu.VMEM((2, d, de), F32), pltpu.VMEM((2, de, d), F32),
                            pltpu.VMEM((d, de), BF16), pltpu.VMEM((d, de), BF16), pltpu.VMEM((de, d), BF16),
                            pltpu.SemaphoreType.DMA((2, 3))]),
        out_shape=jax.ShapeDtypeStruct((rows, d // 2), I32),
        compiler_params=_params(("arbitrary",)),
        name="moe_experts",
    )(*plan, xb, w_gate, w_up, w_down)


def _sc_gather_rows(table, idx):
    b = idx.shape[0]
    d = table.shape[1]
    per_w = b // SC_WORKERS
    ch = SC_CHUNK_ROWS
    mesh = plsc.VectorSubcoreMesh(core_axis_name="c", subcore_axis_name="s")

    n_ch = per_w // ch
    assert per_w * SC_WORKERS == b and n_ch * ch == per_w and n_ch % 2 == 0

    def body(table_hbm, idx_hbm, out_hbm, idx_v, rows_v, gsem, wsem):
        wid = lax.axis_index("s") * SC_CORES + lax.axis_index("c")
        base = wid * per_w
        pltpu.sync_copy(idx_hbm.at[pl.ds(base, per_w)], idx_v)

        def write_back(c, buf):
            off = pl.multiple_of(c * ch, ch)
            return pltpu.make_async_copy(rows_v.at[buf], out_hbm.at[pl.ds(base + off, ch)], wsem.at[buf])

        @pl.loop(0, n_ch, step=2)
        def _(c0):
            for buf in range(2):
                c = c0 + buf

                @pl.when(c0 > 0)
                def _():
                    write_back(c - 2, buf).wait()

                off = pl.multiple_of(c * ch, ch)
                pltpu.async_copy(table_hbm.at[idx_v.at[pl.ds(off, ch)]], rows_v.at[buf], gsem).wait()
                write_back(c, buf).start()

        for buf in range(2):
            write_back(n_ch - 2 + buf, buf).wait()

    return pl.kernel(body, out_type=jax.ShapeDtypeStruct((b, d), table.dtype), mesh=mesh,
                     scratch_types=[pltpu.VMEM((per_w,), I32), pltpu.VMEM((2, ch, d), table.dtype),
                                    pltpu.SemaphoreType.DMA, pltpu.SemaphoreType.DMA((2,))],
                     name="sc_gather")(table, idx)


def _sc_scatter_rows(rows, idx, n_out):
    t, d = rows.shape
    top_k, _, n_ch, ch = idx.shape
    per_w = n_ch * ch
    assert per_w * SC_WORKERS == t and idx.shape[1] == SC_WORKERS
    mesh = plsc.VectorSubcoreMesh(core_axis_name="c", subcore_axis_name="s")

    assert n_ch % 2 == 0

    def body(rows_hbm, idx_hbm, out_hbm, idx_v, rows_v, lsem, ssem):
        wid = lax.axis_index("s") * SC_CORES + lax.axis_index("c")
        base = wid * per_w
        for k in range(top_k):
            pltpu.sync_copy(idx_hbm.at[k, wid], idx_v.at[k])

        def load(c, buf):
            off = pl.multiple_of(c * ch, ch)
            return pltpu.make_async_copy(rows_hbm.at[pl.ds(base + off, ch)], rows_v.at[buf], lsem.at[buf])

        load(0, 0).start()

        @pl.loop(0, n_ch, step=2)
        def _(c0):
            for buf in range(2):
                c = c0 + buf
                load(c, buf).wait()

                @pl.when(c + 1 < n_ch)
                def _():
                    load(c + 1, 1 - buf).start()

                scatters = [pltpu.async_copy(rows_v.at[buf], out_hbm.at[idx_v.at[k, c]], ssem)
                            for k in range(top_k)]
                for cp in scatters:
                    cp.wait()

    return pl.kernel(body, out_type=jax.ShapeDtypeStruct((n_out, d), rows.dtype), mesh=mesh,
                     scratch_types=[pltpu.VMEM((top_k, n_ch, ch), I32), pltpu.VMEM((2, ch, d), rows.dtype),
                                    pltpu.SemaphoreType.DMA((2,)), pltpu.SemaphoreType.DMA],
                     name="sc_scatter")(rows, idx)


def _combine_final_kernel(x_ref, rw_ref, y0_ref, y1_ref, gf_ref, o_ref):
    w = rw_ref[...]
    out = x_ref[...] + w[:, 0:1] * _unpack_rows(y0_ref[...]) + w[:, 1:2] * _unpack_rows(y1_ref[...])
    o_ref[...] = _rms(out, gf_ref[...])


def _combine_final(x2, rw, y2, g_final, seq, part, n_parts):
    t, d = x2.shape
    tm = min(ROW_TILE, seq)
    nt = t // tm // n_parts
    row = lambda w: pl.BlockSpec((tm, w), lambda i: (part * nt + i, 0))
    return pl.pallas_call(
        _combine_final_kernel,
        grid=(nt,),
        in_specs=[row(d), row(LANES), pl.BlockSpec((tm, d // 2), lambda i: (i, 0)),
                  pl.BlockSpec((tm, d // 2), lambda i: (nt + i, 0)), pl.BlockSpec((1, d), lambda i: (0, 0))],
        out_specs=row(d),
        out_shape=jax.ShapeDtypeStruct((t, d), F32),
        input_output_aliases={0: 0},
        compiler_params=_params(("parallel",)),
        name="moe_combine_final",
    )(x2, rw, y2, y2, g_final)


def _route_plan(ri, counts_f, blk, n_blocks):
    cnt = counts_f[0, EXPERT_LANE0:EXPERT_LANE0 + N_EXPERTS].astype(I32)
    nblk = (cnt + blk - 1) // blk
    cum = jnp.cumsum(nblk)
    pad_start = (cum - nblk) * blk
    e = ri[0:TOP_K]
    rank = ri[TOP_K:2 * TOP_K]
    onehot = e[:, :, None] == jnp.arange(N_EXPERTS, dtype=I32)[None, None, :]
    dest = rank + jnp.sum(jnp.where(onehot, pad_start[None, None, :], 0), axis=-1)
    blocks = jnp.arange(n_blocks, dtype=I32)
    block_expert = jnp.minimum(jnp.sum((cum[None, :] <= blocks[:, None]).astype(I32), axis=-1), N_EXPERTS - 1)
    n_active = cum[-1:].astype(I32)
    experts = jnp.arange(N_EXPERTS, dtype=I32)
    is_block_expert = block_expert[:, None] == experts[None, :]
    first = jnp.sum(jnp.where(is_block_expert, ((cum - nblk)[None, :] == blocks[:, None]).astype(I32), 0), axis=-1)
    used = (nblk > 0).astype(I32)
    slot_e = (jnp.cumsum(used) - used) % 2
    later = (experts[None, :] > experts[:, None]) & (nblk[None, :] > 0)
    next_e = jnp.min(jnp.where(later, experts[None, :], N_EXPERTS), axis=-1)
    next_e = jnp.where(next_e == N_EXPERTS, -1, next_e)
    slot = jnp.sum(jnp.where(is_block_expert, slot_e[None, :], 0), axis=-1)
    nxt = jnp.sum(jnp.where(is_block_expert, next_e[None, :], 0), axis=-1)
    plan = (block_expert, first.astype(I32), slot.astype(I32), nxt.astype(I32), n_active)
    return dest.astype(I32), plan


def kernel(x, mem, norm_mix, w_in, lambda_q1, lambda_k1, lambda_q2, lambda_k2, subln, conv_w, w_branch, w_o,
           norm_cross, norm_mem, w_cq, w_ckv, w_co, norm_ffn, w_router_group, w_router_expert, w_exp_gate,
           w_exp_up, w_exp_down, norm_final):
    batch, seq, d = x.shape
    depth = w_in.shape[0]
    n_mem = mem.shape[1]
    t = batch * seq
    dh = d // (2 * DA_HEADS)
    assert 2 * dh == LANES and d % LANES == 0 and w_in.shape[2] == 8 * d
    assert w_router_group.shape[2] == N_GROUPS and w_router_expert.shape[2] == N_EXPERTS
    blk = min(EXPERT_BLOCK, seq)
    n_blocks = (t * TOP_K) // blk + N_EXPERTS
    parts = COMBINE_PARTS if t % (COMBINE_PARTS * SC_WORKERS * SC_CHUNK_ROWS) == 0 else 1
    tokens_per_worker = t // SC_WORKERS

    tabs = _rotary_tables(seq, dh)
    x2 = x.reshape(t, d)
    kv = _mem_kv(mem.reshape(batch * n_mem, d), norm_mem.reshape(1, d), w_ckv.astype(BF16))
    w_r = jnp.concatenate([w_router_group, w_router_expert,
                           jnp.zeros((depth, d, LANES - N_GROUPS - N_EXPERTS), F32)], axis=-1).astype(BF16)

    w_in_bf, w_branch_bf, w_o_bf = w_in.astype(BF16), w_branch.astype(BF16), w_o.astype(BF16)
    w_cq_bf, w_co_bf = w_cq.astype(BF16), w_co.astype(BF16)

    moe_out = None
    for l in range(depth):
        lambda_init = 0.8 - 0.6 * math.exp(-0.3 * l)
        g_mix = norm_mix[l].reshape(1, d)
        if moe_out is None:
            proj = _inproj(x2, g_mix, w_in_bf, l, tabs, seq, dh)
        else:
            proj = None
            for p in range(parts):
                x2, proj = _inproj_combine(x2, moe_out[0], moe_out[1][p], g_mix, w_in_bf, l, tabs, proj, seq, dh,
                                           p, parts)
        y_attn = _attention(proj, lambda_q1[l].reshape(1, dh), lambda_k1[l].reshape(1, dh),
                            lambda_q2[l].reshape(1, dh), lambda_k2[l].reshape(1, dh),
                            subln[l].reshape(1, LANES), batch, seq, d, lambda_init)
        x2 = _mixer_out(x2, y_attn, proj, conv_w, w_branch_bf, w_o_bf, l, seq)
        x2, h_ffn, logits = _cross_attention(x2, norm_cross[l].reshape(1, d), w_cq_bf, kv, l, w_co_bf,
                                             norm_ffn[l].reshape(1, d), w_r, seq, n_mem)
        ri, rw, counts = _router(logits, seq)
        dest, plan = _route_plan(ri, counts, blk, n_blocks)
        scatter_idx = dest.reshape(TOP_K, SC_WORKERS, tokens_per_worker // SC_CHUNK_ROWS, SC_CHUNK_ROWS)
        xb = _sc_scatter_rows(h_ffn, scatter_idx, n_blocks * blk)
        yb = _experts(plan, xb, w_exp_gate, w_exp_up, w_exp_down, l, blk)
        tp = t // parts
        moe_out = (rw, [_sc_gather_rows(yb, dest[:, p * tp:(p + 1) * tp].reshape(-1)) for p in range(parts)])
    for p in range(parts):
        x2 = _combine_final(x2, moe_out[0], moe_out[1][p], norm_final.reshape(1, d), seq, p, parts)
    return x2.reshape(batch, seq, d)
```

```python
import functools
import math

import jax
import jax.numpy as jnp
from jax import lax
from jax.experimental import pallas as pl
from jax.experimental.pallas import tpu as pltpu
from jax.experimental.pallas import tpu_sc as plsc

EPS = 1e-6
DA_HEADS = 8
MEM_HEADS = 4
N_GROUPS = 4
EXPERTS_PER_GROUP = 8
N_EXPERTS = N_GROUPS * EXPERTS_PER_GROUP
TOP_K = 2
ROPE_THETA = 500000.0
LANES = 128
F32_SUBLANES = 8
BF16_SUBLANES = 16
EXPERT_LANE0 = N_GROUPS
ROW_TILE = 512
CROSS_TILE = 1024
ROUTER_TILE = 2048
Q_TILE = 2048
EXPERT_BLOCK = 256
PROJ_ROW_CHUNK = 1024
PROJ_COMBINE_TILE = 2048
ATTN_SCORE_CHUNK = 128
ATTN_SOFTMAX_CHUNK = 128
COMBINE_PARTS = 2
SC_CORES = 2
SC_SUBCORES = 16
SC_WORKERS = SC_CORES * SC_SUBCORES
SC_CHUNK_ROWS = 64
VMEM_LIMIT = 56 * 1024 * 1024

F32 = jnp.float32
BF16 = jnp.bfloat16
I32 = jnp.int32
NT_DIMS = (((1,), (1,)), ((), ()))


def _rms(x, g):
    return x * lax.rsqrt(jnp.mean(x * x, axis=-1, keepdims=True) + EPS) * g


def _pack_rows(x):
    half = x.shape[1] // 2
    lo = lax.bitcast_convert_type(x[:, :half].astype(jnp.bfloat16).astype(F32), jnp.uint32)
    hi = lax.bitcast_convert_type(x[:, half:].astype(jnp.bfloat16).astype(F32), jnp.uint32)
    return lax.bitcast_convert_type((hi & jnp.uint32(0xFFFF0000)) | (lo >> 16), I32)


def _unpack_rows(w):
    u = lax.bitcast_convert_type(w, jnp.uint32)
    lo = lax.bitcast_convert_type(u << 16, F32)
    hi = lax.bitcast_convert_type(u & jnp.uint32(0xFFFF0000), F32)
    return jnp.concatenate([lo, hi], axis=1)


def _params(sem, vmem=VMEM_LIMIT):
    return pltpu.CompilerParams(dimension_semantics=sem, vmem_limit_bytes=vmem)


def _inproj_kernel(x_ref, g_ref, w_ref, cos_ref, s1_ref, s2_ref, o_ref, h_ref, *, rc, half, q_scale):
    @pl.when(pl.program_id(1) == 0)
    def _():
        h_ref[...] = _rms(x_ref[...], g_ref[...]).astype(BF16)

    _proj_column_block(h_ref, w_ref, cos_ref, s1_ref, s2_ref, o_ref, rc, half, q_scale)


def _inproj_combine_kernel(x_ref, rw_ref, y0_ref, y1_ref, g_ref, w_ref, cos_ref, s1_ref, s2_ref, *rest,
                           rc, half, q_scale):
    xnew_ref, o_ref, h_ref = rest[-3:]

    @pl.when(pl.program_id(1) == 0)
    def _():
        w = rw_ref[...]
        x_new = x_ref[...] + w[:, 0:1] * _unpack_rows(y0_ref[...]) + w[:, 1:2] * _unpack_rows(y1_ref[...])
        xnew_ref[...] = x_new
        h_ref[...] = _rms(x_new, g_ref[...]).astype(BF16)

    _proj_column_block(h_ref, w_ref, cos_ref, s1_ref, s2_ref, o_ref, rc, half, q_scale)


def _proj_column_block(h_ref, w_ref, cos_ref, s1_ref, s2_ref, o_ref, rc, half, q_scale):
    j = pl.program_id(1)
    tm, tn = o_ref.shape

    def run(rot, scale):
        def body(r, carry):
            r0 = pl.multiple_of(r * rc, rc)
            acc = jnp.dot(h_ref[pl.ds(r0, rc), :], w_ref[...], preferred_element_type=F32)
            if not rot:
                o_ref[pl.ds(r0, rc), :] = acc.astype(o_ref.dtype)
                return carry
            c = cos_ref[pl.ds(r0, rc), :]
            s1 = s1_ref[pl.ds(r0, rc), :]
            s2 = s2_ref[pl.ds(r0, rc), :]
            for cc in range(tn // LANES):
                a = acc[:, cc * LANES:(cc + 1) * LANES]
                a = a * c + pltpu.roll(a, half, 1) * s1 + pltpu.roll(a, LANES - half, 1) * s2
                if scale != 1.0:
                    a = a * scale
                o_ref[pl.ds(r0, rc), cc * LANES:(cc + 1) * LANES] = a.astype(o_ref.dtype)
            return carry
        lax.fori_loop(0, tm // rc, body, 0)

    @pl.when(j == 0)
    def _():
        run(True, q_scale)

    @pl.when(j == 1)
    def _():
        run(True, 1.0)

    @pl.when(j >= 2)
    def _():
        run(False, 1.0)


def _inproj(x2, g, w_bf, layer, tabs, seq, dh):
    t, d = x2.shape
    n = w_bf.shape[2]
    tm, tn = seq, d
    rc = min(PROJ_ROW_CHUNK, tm)
    cos_t, s1_t, s2_t = tabs
    tab_spec = pl.BlockSpec((seq, LANES), lambda i, j: (0, 0))
    return pl.pallas_call(
        functools.partial(_inproj_kernel, rc=rc, half=dh // 8, q_scale=dh ** -0.5 * math.log2(math.e)),
        grid=(t // tm, n // tn),
        in_specs=[pl.BlockSpec((tm, d), lambda i, j: (i, 0)),
                  pl.BlockSpec((1, d), lambda i, j: (0, 0)),
                  pl.BlockSpec((None, d, tn), lambda i, j: (layer, 0, j)),
                  tab_spec, tab_spec, tab_spec],
        out_specs=pl.BlockSpec((tm, tn), lambda i, j: (i, j)),
        out_shape=jax.ShapeDtypeStruct((t, n), BF16),
        scratch_shapes=[pltpu.VMEM((tm, d), BF16)],
        compiler_params=_params(("parallel", "arbitrary")),
        name="inproj",
    )(x2, g, w_bf, cos_t, s1_t, s2_t)


def _inproj_combine(x2, rw, y2, g, w_bf, layer, tabs, proj_prev, seq, dh, part, n_parts):
    t, d = x2.shape
    n = w_bf.shape[2]
    tm, tn = min(PROJ_COMBINE_TILE, seq), d
    ntp = t // n_parts // tm
    row0 = part * ntp
    tiles_per_seq = seq // tm
    cos_t, s1_t, s2_t = tabs
    once = pl.Buffered(1)
    rows = lambda w: pl.BlockSpec((tm, w), lambda i, j: (row0 + i, 0), pipeline_mode=once)
    tab_spec = pl.BlockSpec((tm, LANES), lambda i, j: ((row0 + i) % tiles_per_seq, 0), pipeline_mode=once)
    in_specs = [rows(d), rows(LANES),
                pl.BlockSpec((tm, d // 2), lambda i, j: (i, 0), pipeline_mode=once),
                pl.BlockSpec((tm, d // 2), lambda i, j: (ntp + i, 0), pipeline_mode=once),
                pl.BlockSpec((1, d), lambda i, j: (0, 0)),
                pl.BlockSpec((None, d, tn), lambda i, j: (layer, 0, j)),
                tab_spec, tab_spec, tab_spec]
    args = [x2, rw, y2, y2, g, w_bf, cos_t, s1_t, s2_t]
    aliases = {0: 0}
    if proj_prev is not None:
        in_specs.append(pl.BlockSpec(memory_space=pl.ANY))
        args.append(proj_prev)
        aliases[len(args) - 1] = 1
    return pl.pallas_call(
        functools.partial(_inproj_combine_kernel, rc=min(PROJ_ROW_CHUNK, tm), half=dh // 8,
                          q_scale=dh ** -0.5 * math.log2(math.e)),
        grid=(ntp, n // tn),
        in_specs=in_specs,
        out_specs=[rows(d), pl.BlockSpec((tm, tn), lambda i, j: (row0 + i, j))],
        out_shape=[jax.ShapeDtypeStruct((t, d), F32), jax.ShapeDtypeStruct((t, n), BF16)],
        scratch_shapes=[pltpu.VMEM((tm, d), BF16)],
        input_output_aliases=aliases,
        compiler_params=_params(("parallel", "arbitrary")),
        name="inproj_combine",
    )(*args)


def _rotary_tables(seq, dh):
    rot = dh // 4
    half = rot // 2
    inv = jnp.float32(ROPE_THETA) ** (-jnp.arange(0, rot, 2, dtype=F32) / rot)
    ang = jnp.arange(seq, dtype=F32)[:, None] * inv[None, :]
    cos, sin = jnp.cos(ang), jnp.sin(ang)
    lane = jnp.arange(LANES) % dh
    idx = lane % half
    c_t = jnp.where(lane < rot, cos[:, idx], 1.0)
    s1_t = jnp.where((lane >= half) & (lane < rot), sin[:, idx], 0.0)
    s2_t = jnp.where(lane < half, -sin[:, idx], 0.0)
    return c_t.astype(F32), s1_t.astype(F32), s2_t.astype(F32)


def _attn_kernel(q_ref, k_ref, v_ref, lq1_ref, lk1_ref, lq2_ref, lk2_ref, sub_ref, o_ref,
                 s_ref, p_ref, kt_ref, *, dh, lambda_init, rc, rp):
    tq = q_ref.shape[0]
    n_chunks = tq // rc
    lane = lax.broadcasted_iota(I32, (1, LANES), 1)
    m1 = jnp.where(lane < dh, 1.0, 0.0).astype(BF16)
    m2 = jnp.where(lane >= dh, 1.0, 0.0).astype(BF16)
    lam = (jnp.exp(jnp.sum(lq1_ref[...] * lk1_ref[...], axis=-1, keepdims=True))
           - jnp.exp(jnp.sum(lq2_ref[...] * lk2_ref[...], axis=-1, keepdims=True)) + lambda_init)

    kt_ref[...] = k_ref[...].T

    def scores(c):
        qc = q_ref[c * rc:(c + 1) * rc, :]
        qq = jnp.concatenate([qc * m1, qc * m2], axis=0)
        s_ref[c % 2] = jnp.dot(qq, kt_ref[...], preferred_element_type=F32)

    def softmax_pv(c, j):
        r0 = j * rp
        s1 = s_ref[c % 2, r0:r0 + rp, :]
        s2 = s_ref[c % 2, rc + r0:rc + r0 + rp, :]
        e1 = jnp.exp2(s1 - jnp.max(s1, axis=-1, keepdims=True))
        e2 = jnp.exp2(s2 - jnp.max(s2, axis=-1, keepdims=True))
        l1 = jnp.sum(e1, axis=-1, keepdims=True)
        l2 = jnp.sum(e2, axis=-1, keepdims=True)
        slot = (c * (rc // rp) + j) % 2
        p_ref[slot] = (e1 - e2 * (lam * l1 / l2)).astype(BF16)
        od = jnp.dot(p_ref[slot], v_ref[...], preferred_element_type=F32) / l1
        od = _rms(od, sub_ref[...]) * (1.0 - lambda_init)
        o_ref[c * rc + r0:c * rc + r0 + rp, :] = od.astype(o_ref.dtype)

    scores(0)
    for c in range(n_chunks):
        if c + 1 < n_chunks:
            scores(c + 1)
        for j in range(rc // rp):
            softmax_pv(c, j)


def _attention(proj, lq1, lk1, lq2, lk2, sub, batch, seq, d, lambda_init):
    t = proj.shape[0]
    nh = d // LANES
    dh = LANES // 2
    tq = min(Q_TILE, seq)
    nq = seq // tq
    rc = min(ATTN_SCORE_CHUNK, tq)
    rp = min(ATTN_SOFTMAX_CHUNK, rc)
    vec = pl.BlockSpec((1, dh), lambda b, h, qi: (0, 0))
    return pl.pallas_call(
        functools.partial(_attn_kernel, dh=dh, lambda_init=lambda_init, rc=rc, rp=rp),
        grid=(batch, nh, nq),
        scratch_shapes=[pltpu.VMEM((2, 2 * rc, seq), F32), pltpu.VMEM((2, rp, seq), BF16),
                        pltpu.VMEM((LANES, seq), BF16)],
        in_specs=[pl.BlockSpec((tq, LANES), lambda b, h, qi: (b * nq + qi, h)),
                  pl.BlockSpec((seq, LANES), lambda b, h, qi: (b, nh + h)),
                  pl.BlockSpec((seq, LANES), lambda b, h, qi: (b, 2 * nh + h)),
                  vec, vec, vec, vec,
                  pl.BlockSpec((1, LANES), lambda b, h, qi: (0, 0))],
        out_specs=pl.BlockSpec((tq, LANES), lambda b, h, qi: (b * nq + qi, h)),
        out_shape=jax.ShapeDtypeStruct((t, d), BF16),
        compiler_params=_params(("parallel", "parallel", "arbitrary")),
        name="diff_attn",
    )(proj, proj, proj, lq1, lk1, lq2, lk2, sub)


def _mixout_kernel(x_ref, ya_ref, cb_ref, cc_ref, cx_ref, g0_ref, g1_ref, pc_ref, px_ref, nc_ref, nx_ref,
                   cw_ref, wb0_ref, wb1_ref, wo_ref, o_ref, *, tiles_per_seq):
    i = pl.program_id(0)
    tm = x_ref.shape[0]
    u = cc_ref[...].astype(F32) * cx_ref[...].astype(F32)
    hr = pc_ref.shape[0]
    u_prev = pc_ref[hr - 1:hr, :].astype(F32) * px_ref[hr - 1:hr, :].astype(F32)
    u_next = nc_ref[0:1, :].astype(F32) * nx_ref[0:1, :].astype(F32)
    pos = i % tiles_per_seq
    u_prev = jnp.where(pos == 0, 0.0, u_prev)
    u_next = jnp.where(pos == tiles_per_seq - 1, 0.0, u_next)
    row = lax.broadcasted_iota(I32, u.shape, 0)
    u_up = jnp.where(row == 0, u_prev, pltpu.roll(u, 1, 0))
    u_dn = jnp.where(row == tm - 1, u_next, pltpu.roll(u, tm - 1, 0))
    conv = cw_ref[0:1, :] * u_up + cw_ref[1:2, :] * u + cw_ref[2:3, :] * u_dn
    y_conv = (cb_ref[...].astype(F32) * conv).astype(BF16)
    br0 = jnp.dot(ya_ref[...], wb0_ref[...], preferred_element_type=F32)
    br1 = jnp.dot(y_conv, wb1_ref[...], preferred_element_type=F32)
    z = jax.nn.sigmoid(g0_ref[...].astype(F32)) * br0 + jax.nn.sigmoid(g1_ref[...].astype(F32)) * br1
    o_ref[...] = x_ref[...] + jnp.dot(z.astype(BF16), wo_ref[...], preferred_element_type=F32)


def _mixer_out(x2, y_attn, proj, conv_w, w_branch, w_o, layer, seq):
    t, d = x2.shape
    tm = min(ROW_TILE, seq)
    tps = seq // tm
    hr = BF16_SUBLANES
    nhb = t // hr
    row = lambda c: pl.BlockSpec((tm, d), lambda i: (i, c))
    prev = lambda c: pl.BlockSpec((hr, d), lambda i: (jnp.maximum(i * (tm // hr) - 1, 0), c))
    nxt = lambda c: pl.BlockSpec((hr, d), lambda i: (jnp.minimum((i + 1) * (tm // hr), nhb - 1), c))
    branch = lambda b: pl.BlockSpec((None, None, d, d), lambda i: (layer, b, 0, 0))
    return pl.pallas_call(
        functools.partial(_mixout_kernel, tiles_per_seq=tps),
        grid=(t // tm,),
        in_specs=[row(0), row(0), row(3), row(4), row(5), row(6), row(7),
                  prev(4), prev(5), nxt(4), nxt(5),
                  pl.BlockSpec((None,) + conv_w.shape[1:], lambda i: (layer, 0, 0)),
                  branch(0), branch(1), pl.BlockSpec((None, d, d), lambda i: (layer, 0, 0))],
        out_specs=row(0),
        out_shape=jax.ShapeDtypeStruct((t, d), F32),
        compiler_params=_params(("parallel",)),
        name="mixer_out",
    )(x2, y_attn, proj, proj, proj, proj, proj, proj, proj, proj, proj, conv_w, w_branch, w_branch, w_o)


def _memkv_kernel(m_ref, g_ref, w_ref, o_ref, h_ref):
    @pl.when((pl.program_id(0) == 0) & (pl.program_id(1) == 0))
    def _():
        h_ref[...] = _rms(m_ref[...], g_ref[...]).astype(BF16)

    o_ref[...] = jnp.dot(h_ref[...], w_ref[...], preferred_element_type=F32).astype(o_ref.dtype)


def _mem_kv(mem2, g, w_bf):
    depth, d, n = w_bf.shape
    rows = mem2.shape[0]
    tn = min(n, 1024)
    return pl.pallas_call(
        _memkv_kernel,
        grid=(depth, n // tn),
        in_specs=[pl.BlockSpec((rows, d), lambda l, j: (0, 0)),
                  pl.BlockSpec((1, d), lambda l, j: (0, 0)),
                  pl.BlockSpec((None, d, tn), lambda l, j: (l, 0, j))],
        out_specs=pl.BlockSpec((None, rows, tn), lambda l, j: (l, 0, j)),
        out_shape=jax.ShapeDtypeStruct((depth, rows, n), BF16),
        scratch_shapes=[pltpu.VMEM((rows, d), BF16)],
        compiler_params=_params(("arbitrary", "arbitrary")),
        name="mem_kv",
    )(mem2, g, w_bf)


def _cross_kernel(x_ref, g_ref, wq_ref, kv_ref, wo_ref, gf_ref, wr_ref, o_ref, hf_ref, lg_ref, *, heads):
    x = x_ref[...]
    d = x.shape[1]
    hd = d // heads
    h = _rms(x, g_ref[...]).astype(BF16)
    q = (jnp.dot(h, wq_ref[...], preferred_element_type=F32) * (hd ** -0.5)).astype(BF16)
    outs = []
    for hh in range(heads):
        qh = q[:, hh * hd:(hh + 1) * hd]
        kh = kv_ref[:, hh * hd:(hh + 1) * hd]
        vh = kv_ref[:, d + hh * hd:d + (hh + 1) * hd]
        s = lax.dot_general(qh, kh, NT_DIMS, preferred_element_type=F32)
        p = jnp.exp(s - jnp.max(s, axis=-1, keepdims=True))
        l = jnp.sum(p, axis=-1, keepdims=True)
        outs.append((jnp.dot(p.astype(BF16), vh, preferred_element_type=F32) / l).astype(BF16))
    o = jnp.concatenate(outs, axis=1)
    x_new = x + jnp.dot(o, wo_ref[...], preferred_element_type=F32)
    o_ref[...] = x_new
    h_ffn = _rms(x_new, gf_ref[...])
    hf_ref[...] = _pack_rows(h_ffn)
    lg_ref[...] = jnp.dot(h_ffn.astype(BF16), wr_ref[...], preferred_element_type=F32)


def _cross_attention(x2, g, wq, kv, layer, wo, g_ffn, w_r, seq, n_mem):
    t, d = x2.shape
    tm = min(CROSS_TILE, seq)
    tps = seq // tm
    row = lambda w: pl.BlockSpec((tm, w), lambda i: (i, 0))
    const = lambda r, c: pl.BlockSpec((r, c), lambda i: (0, 0))
    of_layer = lambda r, c: pl.BlockSpec((None, r, c), lambda i: (layer, 0, 0))
    return pl.pallas_call(
        functools.partial(_cross_kernel, heads=MEM_HEADS),
        grid=(t // tm,),
        in_specs=[row(d), const(1, d), of_layer(d, d),
                  pl.BlockSpec((None, n_mem, 2 * d), lambda i: (0, i // tps, 0)),
                  of_layer(d, d), const(1, d), of_layer(d, LANES)],
        out_specs=[row(d), row(d // 2), row(LANES)],
        out_shape=[jax.ShapeDtypeStruct((t, d), F32), jax.ShapeDtypeStruct((t, d // 2), I32),
                   jax.ShapeDtypeStruct((t, LANES), F32)],
        compiler_params=_params(("parallel",)),
        name="cross_attn",
    )(x2, g, wq, kv, wo, g_ffn, w_r)


def _router_kernel(lg_ref, ri_ref, rw_ref, cnt_ref, carry_ref, *, sub):
    @pl.when(pl.program_id(0) == 0)
    def _():
        carry_ref[...] = jnp.zeros_like(carry_ref)

    lg = lg_ref[...]
    tm = lg.shape[0]
    lane = lax.broadcasted_iota(I32, lg.shape, 1).astype(F32)
    neg = -jnp.inf

    def first_argmax(v):
        m = jnp.max(v, axis=-1, keepdims=True)
        return m, jnp.min(jnp.where(v == m, lane, float(LANES)), axis=-1, keepdims=True)

    gmask = lane < N_GROUPS
    gl = jnp.where(gmask, lg, neg)
    gmax, gidx = first_argmax(gl)
    g_w = 1.0 / jnp.sum(jnp.where(gmask, jnp.exp(gl - gmax), 0.0), axis=-1, keepdims=True)

    lo = EXPERT_LANE0 + gidx * EXPERTS_PER_GROUP
    emask = (lane >= lo) & (lane < lo + EXPERTS_PER_GROUP)
    el = jnp.where(emask, lg, neg)
    mx1, i1 = first_argmax(el)
    mx2, i2 = first_argmax(jnp.where(lane == i1, neg, el))
    esum = jnp.sum(jnp.where(emask, jnp.exp(el - mx1), 0.0), axis=-1, keepdims=True)
    p1 = 1.0 / esum
    p2 = jnp.exp(mx2 - mx1) / esum
    den = p1 + p2
    w1 = g_w * (p1 / den)
    w2 = g_w * (p2 / den)

    memb = jnp.where((lane == i1) | (lane == i2), 1.0, 0.0)
    r_i = lax.broadcasted_iota(I32, (sub, sub), 0)
    c_i = lax.broadcasted_iota(I32, (sub, sub), 1)
    ltri = jnp.where(c_i < r_i, 1.0, 0.0).astype(BF16)
    tots = []
    carry = carry_ref[...]
    for r0 in range(0, tm, sub):
        mb = memb[r0:r0 + sub]
        tots.append(jnp.dot(ltri, mb.astype(BF16), preferred_element_type=F32) + carry)
        carry = carry + jnp.sum(mb, axis=0, keepdims=True)
    carry_ref[...] = carry
    tot = jnp.concatenate(tots, axis=0)
    rank1 = jnp.sum(jnp.where(lane == i1, tot, 0.0), axis=-1, keepdims=True).astype(I32)
    rank2 = jnp.sum(jnp.where(lane == i2, tot, 0.0), axis=-1, keepdims=True).astype(I32)

    e1 = (i1 - EXPERT_LANE0).astype(I32)
    e2 = (i2 - EXPERT_LANE0).astype(I32)
    ri = jnp.where(lane == 0, e1, jnp.where(lane == 1, e2, jnp.where(lane == 2, rank1, jnp.where(lane == 3, rank2, 0))))
    ri_ref[...] = ri.T[:F32_SUBLANES]
    rw_ref[...] = jnp.where(lane == 0, w1, jnp.where(lane == 1, w2, 0.0))
    cnt_ref[...] = jnp.broadcast_to(carry, cnt_ref.shape)


def _router(lg, seq):
    t = lg.shape[0]
    tm = min(ROUTER_TILE, seq)
    row = pl.BlockSpec((tm, LANES), lambda i: (i, 0))
    return pl.pallas_call(
        functools.partial(_router_kernel, sub=min(ROW_TILE, tm)),
        grid=(t // tm,),
        in_specs=[row],
        out_specs=[pl.BlockSpec((F32_SUBLANES, tm), lambda i: (0, i)), row,
                   pl.BlockSpec((F32_SUBLANES, LANES), lambda i: (0, 0))],
        out_shape=[jax.ShapeDtypeStruct((F32_SUBLANES, t), I32),
                   jax.ShapeDtypeStruct((t, LANES), F32),
                   jax.ShapeDtypeStruct((F32_SUBLANES, LANES), F32)],
        scratch_shapes=[pltpu.VMEM((1, LANES), F32)],
        compiler_params=_params(("arbitrary",)),
        name="moe_router",
    )(lg)


def _expert_kernel(be_ref, first_ref, slot_ref, nxt_ref, na_ref, xb_ref, wg_hbm, wu_hbm, wd_hbm, yb_ref,
                   wg_buf, wu_buf, wd_buf, wgb_ref, wub_ref, wdb_ref, sem, *, layer):
    i = pl.program_id(0)

    def fetch(e, slot):
        return (pltpu.make_async_copy(wg_hbm.at[layer, e], wg_buf.at[slot], sem.at[slot, 0]),
                pltpu.make_async_copy(wu_hbm.at[layer, e], wu_buf.at[slot], sem.at[slot, 1]),
                pltpu.make_async_copy(wd_hbm.at[layer, e], wd_buf.at[slot], sem.at[slot, 2]))

    @pl.when(i < na_ref[0])
    def _():
        @pl.when(first_ref[i] == 1)
        def _():
            e, slot = be_ref[i], slot_ref[i]

            @pl.when(i == 0)
            def _():
                for cp in fetch(e, slot):
                    cp.start()

            for cp in fetch(e, slot):
                cp.wait()
            wgb_ref[...] = wg_buf[slot].astype(BF16)
            wub_ref[...] = wu_buf[slot].astype(BF16)
            wdb_ref[...] = wd_buf[slot].astype(BF16)

            @pl.when(nxt_ref[i] >= 0)
            def _():
                for cp in fetch(nxt_ref[i], 1 - slot):
                    cp.start()

        xb = _unpack_rows(xb_ref[...]).astype(BF16)
        gate = jnp.dot(xb, wgb_ref[...], preferred_element_type=F32)
        up = jnp.dot(xb, wub_ref[...], preferred_element_type=F32)
        act = (jax.nn.silu(gate) * up).astype(BF16)
        yb_ref[...] = _pack_rows(jnp.dot(act, wdb_ref[...], preferred_element_type=F32))

    @pl.when(i >= na_ref[0])
    def _():
        yb_ref[...] = jnp.zeros_like(yb_ref)


def _experts(plan, xb, w_gate, w_up, w_down, layer, blk):
    rows = xb.shape[0]
    d, de = w_gate.shape[-2:]
    nb = rows // blk

    def row_map(i, be, first, slot, nxt, na):
        return (jnp.minimum(i, jnp.maximum(na[0] - 1, 0)), 0)

    hbm = pl.BlockSpec(memory_space=pl.ANY)
    return pl.pallas_call(
        functools.partial(_expert_kernel, layer=layer),
        grid_spec=pltpu.PrefetchScalarGridSpec(
            num_scalar_prefetch=5,
            grid=(nb,),
            in_specs=[pl.BlockSpec((blk, d // 2), row_map), hbm, hbm, hbm],
            out_specs=pl.BlockSpec((blk, d // 2), lambda i, *_: (i, 0)),
            scratch_shapes=[pltpu.VMEM((2, d, de), F32), pltpu.VMEM((2, d, de), F32), pltpu.VMEM((2, de, d), F32),
                            pltpu.VMEM((d, de), BF16), pltpu.VMEM((d, de), BF16), pltpu.VMEM((de, d), BF16),
                            pltpu.SemaphoreType.DMA((2, 3))]),
        out_shape=jax.ShapeDtypeStruct((rows, d // 2), I32),
        compiler_params=_params(("arbitrary",)),
        name="moe_experts",
    )(*plan, xb, w_gate, w_up, w_down)


def _sc_gather_rows(table, idx):
    b = idx.shape[0]
    d = table.shape[1]
    per_w = b // SC_WORKERS
    ch = SC_CHUNK_ROWS
    mesh = plsc.VectorSubcoreMesh(core_axis_name="c", subcore_axis_name="s")

    n_ch = per_w // ch
    assert per_w * SC_WORKERS == b and n_ch * ch == per_w and n_ch % 2 == 0

    def body(table_hbm, idx_hbm, out_hbm, idx_v, rows_v, gsem, wsem):
        wid = lax.axis_index("s") * SC_CORES + lax.axis_index("c")
        base = wid * per_w
        pltpu.sync_copy(idx_hbm.at[pl.ds(base, per_w)], idx_v)

        def write_back(c, buf):
            off = pl.multiple_of(c * ch, ch)
            return pltpu.make_async_copy(rows_v.at[buf], out_hbm.at[pl.ds(base + off, ch)], wsem.at[buf])

        @pl.loop(0, n_ch, step=2)
        def _(c0):
            for buf in range(2):
                c = c0 + buf

                @pl.when(c0 > 0)
                def _():
                    write_back(c - 2, buf).wait()

                off = pl.multiple_of(c * ch, ch)
                pltpu.async_copy(table_hbm.at[idx_v.at[pl.ds(off, ch)]], rows_v.at[buf], gsem).wait()
                write_back(c, buf).start()

        for buf in range(2):
            write_back(n_ch - 2 + buf, buf).wait()

    return pl.kernel(body, out_type=jax.ShapeDtypeStruct((b, d), table.dtype), mesh=mesh,
                     scratch_types=[pltpu.VMEM((per_w,), I32), pltpu.VMEM((2, ch, d), table.dtype),
                                    pltpu.SemaphoreType.DMA, pltpu.SemaphoreType.DMA((2,))],
                     name="sc_gather")(table, idx)


def _sc_scatter_rows(rows, idx, n_out):
    t, d = rows.shape
    top_k, _, n_ch, ch = idx.shape
    per_w = n_ch * ch
    assert per_w * SC_WORKERS == t and idx.shape[1] == SC_WORKERS
    mesh = plsc.VectorSubcoreMesh(core_axis_name="c", subcore_axis_name="s")

    assert n_ch % 2 == 0

    def body(rows_hbm, idx_hbm, out_hbm, idx_v, rows_v, lsem, ssem):
        wid = lax.axis_index("s") * SC_CORES + lax.axis_index("c")
        base = wid * per_w
        for k in range(top_k):
            pltpu.sync_copy(idx_hbm.at[k, wid], idx_v.at[k])

        def load(c, buf):
            off = pl.multiple_of(c * ch, ch)
            return pltpu.make_async_copy(rows_hbm.at[pl.ds(base + off, ch)], rows_v.at[buf], lsem.at[buf])

        load(0, 0).start()

        @pl.loop(0, n_ch, step=2)
        def _(c0):
            for buf in range(2):
                c = c0 + buf
                load(c, buf).wait()

                @pl.when(c + 1 < n_ch)
                def _():
                    load(c + 1, 1 - buf).start()

                scatters = [pltpu.async_copy(rows_v.at[buf], out_hbm.at[idx_v.at[k, c]], ssem)
                            for k in range(top_k)]
                for cp in scatters:
                    cp.wait()

    return pl.kernel(body, out_type=jax.ShapeDtypeStruct((n_out, d), rows.dtype), mesh=mesh,
                     scratch_types=[pltpu.VMEM((top_k, n_ch, ch), I32), pltpu.VMEM((2, ch, d), rows.dtype),
                                    pltpu.SemaphoreType.DMA((2,)), pltpu.SemaphoreType.DMA],
                     name="sc_scatter")(rows, idx)


def _combine_final_kernel(x_ref, rw_ref, y0_ref, y1_ref, gf_ref, o_ref):
    w = rw_ref[...]
    out = x_ref[...] + w[:, 0:1] * _unpack_rows(y0_ref[...]) + w[:, 1:2] * _unpack_rows(y1_ref[...])
    o_ref[...] = _rms(out, gf_ref[...])


def _combine_final(x2, rw, y2, g_final, seq, part, n_parts):
    t, d = x2.shape
    tm = min(ROW_TILE, seq)
    nt = t // tm // n_parts
    row = lambda w: pl.BlockSpec((tm, w), lambda i: (part * nt + i, 0))
    return pl.pallas_call(
        _combine_final_kernel,
        grid=(nt,),
        in_specs=[row(d), row(LANES), pl.BlockSpec((tm, d // 2), lambda i: (i, 0)),
                  pl.BlockSpec((tm, d // 2), lambda i: (nt + i, 0)), pl.BlockSpec((1, d), lambda i: (0, 0))],
        out_specs=row(d),
        out_shape=jax.ShapeDtypeStruct((t, d), F32),
        input_output_aliases={0: 0},
        compiler_params=_params(("parallel",)),
        name="moe_combine_final",
    )(x2, rw, y2, y2, g_final)


def _route_plan(ri, counts_f, blk, n_blocks):
    cnt = counts_f[0, EXPERT_LANE0:EXPERT_LANE0 + N_EXPERTS].astype(I32)
    nblk = (cnt + blk - 1) // blk
    cum = jnp.cumsum(nblk)
    pad_start = (cum - nblk) * blk
    e = ri[0:TOP_K]
    rank = ri[TOP_K:2 * TOP_K]
    onehot = e[:, :, None] == jnp.arange(N_EXPERTS, dtype=I32)[None, None, :]
    dest = rank + jnp.sum(jnp.where(onehot, pad_start[None, None, :], 0), axis=-1)
    blocks = jnp.arange(n_blocks, dtype=I32)
    block_expert = jnp.minimum(jnp.sum((cum[None, :] <= blocks[:, None]).astype(I32), axis=-1), N_EXPERTS - 1)
    n_active = cum[-1:].astype(I32)
    experts = jnp.arange(N_EXPERTS, dtype=I32)
    is_block_expert = block_expert[:, None] == experts[None, :]
    first = jnp.sum(jnp.where(is_block_expert, ((cum - nblk)[None, :] == blocks[:, None]).astype(I32), 0), axis=-1)
    used = (nblk > 0).astype(I32)
    slot_e = (jnp.cumsum(used) - used) % 2
    later = (experts[None, :] > experts[:, None]) & (nblk[None, :] > 0)
    next_e = jnp.min(jnp.where(later, experts[None, :], N_EXPERTS), axis=-1)
    next_e = jnp.where(next_e == N_EXPERTS, -1, next_e)
    slot = jnp.sum(jnp.where(is_block_expert, slot_e[None, :], 0), axis=-1)
    nxt = jnp.sum(jnp.where(is_block_expert, next_e[None, :], 0), axis=-1)
    plan = (block_expert, first.astype(I32), slot.astype(I32), nxt.astype(I32), n_active)
    return dest.astype(I32), plan


def kernel(x, mem, norm_mix, w_in, lambda_q1, lambda_k1, lambda_q2, lambda_k2, subln, conv_w, w_branch, w_o,
           norm_cross, norm_mem, w_cq, w_ckv, w_co, norm_ffn, w_router_group, w_router_expert, w_exp_gate,
           w_exp_up, w_exp_down, norm_final):
    batch, seq, d = x.shape
    depth = w_in.shape[0]
    n_mem = mem.shape[1]
    t = batch * seq
    dh = d // (2 * DA_HEADS)
    assert 2 * dh == LANES and d % LANES == 0 and w_in.shape[2] == 8 * d
    assert w_router_group.shape[2] == N_GROUPS and w_router_expert.shape[2] == N_EXPERTS
    blk = min(EXPERT_BLOCK, seq)
    n_blocks = (t * TOP_K) // blk + N_EXPERTS
    parts = COMBINE_PARTS if t % (COMBINE_PARTS * SC_WORKERS * SC_CHUNK_ROWS) == 0 else 1
    tokens_per_worker = t // SC_WORKERS

    tabs = _rotary_tables(seq, dh)
    x2 = x.reshape(t, d)
    mem2, g_mem, w_ckv_bf = mem.reshape(batch * n_mem, d), norm_mem.reshape(1, d), w_ckv.astype(BF16)
    kv = _mem_kv(mem2, g_mem, w_ckv_bf[0:1])
    w_r =jnp.concatenate([w_router_group, w_router_expert,
                           jnp.zeros((depth, d, LANES - N_GROUPS - N_EXPERTS), F32)], axis=-1).astype(BF16)

    w_in_bf, w_branch_bf, w_o_bf = w_in.astype(BF16), w_branch.astype(BF16), w_o.astype(BF16)
    w_cq_bf, w_co_bf = w_cq.astype(BF16), w_co.astype(BF16)

    moe_out = None
    for l in range(depth):
        lambda_init = 0.8 - 0.6 * math.exp(-0.3 * l)
        g_mix = norm_mix[l].reshape(1, d)
        if moe_out is None:
            proj = _inproj(x2, g_mix, w_in_bf, l, tabs, seq, dh)
        else:
            proj = None
            for p in range(parts):
                x2, proj = _inproj_combine(x2, moe_out[0], moe_out[1][p], g_mix, w_in_bf, l, tabs, proj, seq, dh,
                                           p, parts)
        y_attn = _attention(proj, lambda_q1[l].reshape(1, dh), lambda_k1[l].reshape(1, dh),
                            lambda_q2[l].reshape(1, dh), lambda_k2[l].reshape(1, dh),
                            subln[l].reshape(1, LANES), batch, seq, d, lambda_init)
        x2 = _mixer_out(x2, y_attn, proj, conv_w, w_branch_bf, w_o_bf, l, seq)
        x2, h_ffn, logits = _cross_attention(x2, norm_cross[l].reshape(1, d), w_cq_bf, kv, l, w_co_bf,
                                             norm_ffn[l].reshape(1, d), w_r, seq, n_mem)
        ri, rw, counts = _router(logits, seq)
        dest, plan = _route_plan(ri, counts, blk, n_blocks)
        if l + 1 < depth:
            mem_next, dest = lax.optimization_barrier((mem2, dest))
            kv = _mem_kv(mem_next, g_mem, w_ckv_bf[l + 1:l + 2])
        scatter_idx = dest.reshape(TOP_K, SC_WORKERS, tokens_per_worker // SC_CHUNK_ROWS, SC_CHUNK_ROWS)
        xb = _sc_scatter_rows(h_ffn, scatter_idx, n_blocks * blk)
        yb = _experts(plan, xb, w_exp_gate, w_exp_up, w_exp_down, l, blk)
        tp = t // parts
        moe_out = (rw, [_sc_gather_rows(yb, dest[:, p * tp:(p + 1) * tp].reshape(-1)) for p in range(parts)])
    for p in range(parts):
        x2 = _combine_final(x2, moe_out[0], moe_out[1][p], norm_final.reshape(1, d), seq, p, parts)
    return x2.reshape(batch, seq, d)
```
